```python
import jax, jax.numpy as jnp
from jax import lax
import numpy as np

D_MODEL = 1024
BATCH = 16
SEQ = 4096
DEPTH = 1

HEAD_DIM = 64
DILATED_GROUPS = ((128, 1), (512, 4), (2048, 16))
N_ATT_GROUPS = len(DILATED_GROUPS)
HEADS_PER_GROUP = 4
ATT_HEADS = N_ATT_GROUPS * HEADS_PER_GROUP
ATT_WIDTH = ATT_HEADS * HEAD_DIM
ATT_OUT_WIDTH = HEADS_PER_GROUP * HEAD_DIM
BAND_BLOCK = 128
ROPE_THETA = 10000.0

GMLP_CHUNK = 128
GMLP_GROUPS = 4
GMLP_WIDTH = 512
GMLP_GROUP_WIDTH = GMLP_WIDTH // GMLP_GROUPS

IN_COLS = 3 * ATT_WIDTH + 2 * GMLP_WIDTH + 2 * D_MODEL

N_MEM = 256
XATTN_HEADS = 4
XATTN_HEAD_DIM = D_MODEL // XATTN_HEADS

N_EXPERT_GROUPS = 4
EXPERTS_PER_GROUP = 8
N_EXPERTS = N_EXPERT_GROUPS * EXPERTS_PER_GROUP
TOP_K_INNER = 2
EXPERT_FF = D_MODEL // 2
MOE_BLOCK = 128

RMS_EPS = 1e-6
LN_EPS = 1e-5
NEG_INF = -1e30

kernel_name = "hybrid_dilated_gmlp_hmoe_block"


def rmsnorm(x, g):
    xf = x.astype(jnp.float32)
    y = xf * lax.rsqrt(jnp.mean(xf * xf, axis=-1, keepdims=True) + RMS_EPS)
    return (y * g.astype(jnp.float32)).astype(x.dtype)


def layernorm(x, g, b):
    xf = x.astype(jnp.float32)
    mu = jnp.mean(xf, axis=-1, keepdims=True)
    var = jnp.mean(jnp.square(xf - mu), axis=-1, keepdims=True)
    y = (xf - mu) * lax.rsqrt(var + LN_EPS)
    return (y * g.astype(jnp.float32) + b.astype(jnp.float32)).astype(x.dtype)


def rope(x, positions):
    half = x.shape[-1] // 2
    inv_freq = ROPE_THETA ** (-jnp.arange(half, dtype=jnp.float32) / half)
    ang = positions.astype(jnp.float32)[..., None] * inv_freq
    cos = jnp.cos(ang)[:, :, None, :]
    sin = jnp.sin(ang)[:, :, None, :]
    xf = x.astype(jnp.float32)
    x1, x2 = xf[..., :half], xf[..., half:]
    return jnp.concatenate([x1 * cos - x2 * sin, x2 * cos + x1 * sin], axis=-1).astype(x.dtype)


def dilated_window_attention(q, k, v, window, dilation):
    B, S, H, Dh = q.shape
    span = window // dilation
    L = S // dilation
    nb = -(-L // BAND_BLOCK)
    Lp = nb * BAND_BLOCK

    def to_strided(t):
        t = t.reshape(B, L, dilation, H, Dh).transpose(0, 2, 1, 3, 4)
        t = jnp.pad(t, ((0, 0), (0, 0), (0, Lp - L), (0, 0), (0, 0)))
        return t.reshape(B, dilation, nb, BAND_BLOCK, H, Dh)

    def with_prev(t):
        prev = jnp.pad(t, ((0, 0), (0, 0), (1, 0), (0, 0), (0, 0), (0, 0)))[:, :, :-1]
        return jnp.concatenate([prev, t], axis=3)

    qb = to_strided(q)
    kk = with_prev(to_strided(k))
    vv = with_prev(to_strided(v))

    s = jnp.einsum('brnqhd,brnkhd->brnhqk', qb, kk,
                   preferred_element_type=jnp.float32) * (Dh ** -0.5)
    qi = jnp.arange(BAND_BLOCK)[:, None]
    kc = jnp.arange(2 * BAND_BLOCK)[None, :]
    dist = qi + BAND_BLOCK - kc
    band = (dist >= 0) & (dist <= span)
    blk = jnp.arange(nb)[:, None, None]
    mask = band[None] & ((blk > 0) | (kc[None] >= BAND_BLOCK))
    s = jnp.where(mask[None, None, :, None], s, NEG_INF)

    m = jnp.max(s, axis=-1, keepdims=True)
    p = jnp.exp(s - m)
    den = jnp.sum(p, axis=-1, keepdims=True)
    lse = (m + jnp.log(den))[..., 0]
    o = jnp.einsum('brnhqk,brnkhd->brnqhd', p / den, vv.astype(jnp.float32))

    o = o.reshape(B, dilation, Lp, H, Dh)[:, :, :L].transpose(0, 2, 1, 3, 4).reshape(B, S, H, Dh)
    lse = lse.transpose(0, 1, 2, 4, 3).reshape(B, dilation, Lp, H)[:, :, :L]
    lse = lse.transpose(0, 2, 1, 3).reshape(B, S, H)
    return o, lse


def dilated_mixture(q, k, v):
    outs, lses = [], []
    for g, (window, dilation) in enumerate(DILATED_GROUPS):
        sl = slice(g * HEADS_PER_GROUP, (g + 1) * HEADS_PER_GROUP)
        o, l = dilated_window_attention(q[:, :, sl], k[:, :, sl], v[:, :, sl], window, dilation)
        outs.append(o)
        lses.append(l)
    wts = jax.nn.softmax(jnp.stack(lses, axis=0), axis=0)
    o = jnp.sum(wts[..., None] * jnp.stack(outs, axis=0), axis=0)
    B, S = o.shape[:2]
    return o.reshape(B, S, ATT_OUT_WIDTH)


def chunked_spatial_gating(zb, w_spatial, b_spatial, v_norm_g, v_norm_b):
    B, S, _ = zb.shape
    z = jax.nn.gelu(zb)
    u, v = z[..., :GMLP_WIDTH], z[..., GMLP_WIDTH:]
    v = layernorm(v, v_norm_g, v_norm_b)
    causal = jnp.tril(jnp.ones((GMLP_CHUNK, GMLP_CHUNK), dtype=bool))
    ws = jnp.where(causal[None], w_spatial, 0.0).astype(v.dtype)
    vc = v.reshape(B, S // GMLP_CHUNK, GMLP_CHUNK, GMLP_GROUPS, GMLP_GROUP_WIDTH)
    mixed = jnp.einsum('gts,bnsgc->bntgc', ws, vc) + b_spatial.T[:, :, None].astype(v.dtype)
    return u * mixed.reshape(B, S, GMLP_WIDTH)


def hybrid_mixer(xn, positions, w_in, b_gates, w_spatial, b_spatial, v_norm_g, v_norm_b,
                 w_out_a, w_out_b, w_out):
    B, S, D = xn.shape
    proj = xn @ w_in
    cuts = np.cumsum([ATT_WIDTH, ATT_WIDTH, ATT_WIDTH, 2 * GMLP_WIDTH]).tolist()
    q, k, v, zb, gate_logits = jnp.split(proj, cuts, axis=-1)
    q = rope(q.reshape(B, S, ATT_HEADS, HEAD_DIM), positions)
    k = rope(k.reshape(B, S, ATT_HEADS, HEAD_DIM), positions)
    v = v.reshape(B, S, ATT_HEADS, HEAD_DIM)
    y_a = dilated_mixture(q, k, v).astype(xn.dtype)
    y_b = chunked_spatial_gating(zb, w_spatial, b_spatial, v_norm_g, v_norm_b)
    gates = jax.nn.sigmoid(gate_logits + b_gates)
    g_a, g_b = gates[..., :D], gates[..., D:]
    merged = g_a * (y_a @ w_out_a) + g_b * (y_b @ w_out_b)
    return merged @ w_out


def memory_cross_attention(hn, mn, w_q, w_kv, w_o):
    B, S, D = hn.shape
    M = mn.shape[1]
    q = (hn @ w_q).reshape(B, S, XATTN_HEADS, XATTN_HEAD_DIM)
    kv = (mn @ w_kv).reshape(B, M, 2, XATTN_HEADS, XATTN_HEAD_DIM)
    k, v = kv[:, :, 0], kv[:, :, 1]
    s = jnp.einsum('bshd,bmhd->bhsm', q, k,
                   preferred_element_type=jnp.float32) * (XATTN_HEAD_DIM ** -0.5)
    p = jax.nn.softmax(s, axis=-1)
    o = jnp.einsum('bhsm,bmhd->bshd', p.astype(v.dtype), v).reshape(B, S, D)
    return o @ w_o


def hierarchical_moe(hn, w_router_grp, b_router_grp, w_router_exp, b_router_exp,
                     w_gate_e, w_up_e, w_down_e):
    B, S, D = hn.shape
    T = B * S
    xt = hn.reshape(T, D)
    grp_logits = (xt @ w_router_grp).astype(jnp.float32) + b_router_grp.astype(jnp.float32)
    grp_prob = jax.nn.softmax(grp_logits, axis=-1)
    grp = jnp.argmax(grp_logits, axis=-1).astype(jnp.int32)
    grp_gate = jnp.take_along_axis(grp_prob, grp[:, None], axis=1)[:, 0]
    exp_logits = ((xt @ w_router_exp).astype(jnp.float32) + b_router_exp.astype(jnp.float32))
    exp_logits = exp_logits.reshape(T, N_EXPERT_GROUPS, EXPERTS_PER_GROUP)
    in_grp = jnp.take_along_axis(exp_logits, grp[:, None, None], axis=1)[:, 0]
    top_v, top_i = lax.top_k(in_grp, TOP_K_INNER)
    gate = grp_gate[:, None] * jax.nn.softmax(top_v, axis=-1)

    A = T * TOP_K_INNER
    eid = (grp[:, None] * EXPERTS_PER_GROUP + top_i.astype(jnp.int32)).reshape(A)
    wgt = gate.reshape(A)
    e_sorted, a_sorted = lax.sort((eid, jnp.arange(A, dtype=jnp.int32)), num_keys=1, is_stable=True)
    tok_sorted = a_sorted // TOP_K_INNER
    w_sorted = wgt[a_sorted]

    counts = jax.ops.segment_sum(jnp.ones((A,), jnp.int32), eid, num_segments=N_EXPERTS)
    padded = ((counts + MOE_BLOCK - 1) // MOE_BLOCK) * MOE_BLOCK
    start = jnp.cumsum(counts) - counts
    pend = jnp.cumsum(padded)
    pstart = pend - padded
    dest = pstart[e_sorted] + (jnp.arange(A, dtype=jnp.int32) - start[e_sorted])
    P = A + N_EXPERTS * MOE_BLOCK
    n_blocks = P // MOE_BLOCK
    tok_buf = jnp.full((P,), T, jnp.int32).at[dest].set(tok_sorted)
    w_buf = jnp.zeros((P,), jnp.float32).at[dest].set(w_sorted)
    blk_start = jnp.arange(n_blocks, dtype=jnp.int32) * MOE_BLOCK
    blk_expert = jnp.minimum(jnp.searchsorted(pend, blk_start, side='right'),
                             N_EXPERTS - 1).astype(jnp.int32)

    def expert_block(args):
        idx, e = args
        xb = jnp.take(xt, idx, axis=0, mode='fill', fill_value=0)
        hb = jax.nn.silu(xb @ w_gate_e[e]) * (xb @ w_up_e[e])
        return hb @ w_down_e[e]

    yb = lax.map(expert_block, (tok_buf.reshape(n_blocks, MOE_BLOCK), blk_expert))
    yb = yb.reshape(P, D) * w_buf[:, None].astype(yb.dtype)
    y = jnp.zeros((T, D), yb.dtype).at[tok_buf].add(yb, mode='drop')
    return y.reshape(B, S, D)


def setup_inputs(seed: int = 0) -> dict:
    key = jax.random.key(seed)
    ks = iter(jax.random.split(key, 40))
    f32 = jnp.float32
    L, D = DEPTH, D_MODEL

    def nrm(shape, scale):
        return jax.random.normal(next(ks), shape, f32) * scale

    def gain(shape):
        return 1.0 + nrm(shape, 0.02)

    x = nrm((BATCH, SEQ, D), 1.0)
    mem = nrm((BATCH, N_MEM, D), 1.0)
    positions = (jnp.arange(SEQ, dtype=jnp.int32)[None, :]
                 + jax.random.randint(next(ks), (BATCH, 1), 0, 512, dtype=jnp.int32))
    return {
        "x": x,
        "mem": mem,
        "positions": positions,
        "mix_norm_g": gain((L, D)),
        "w_in": nrm((L, D, IN_COLS), D ** -0.5),
        "b_gates": nrm((L, 2 * D), 0.1),
        "w_spatial": nrm((L, GMLP_GROUPS, GMLP_CHUNK, GMLP_CHUNK), GMLP_CHUNK ** -0.5),
        "b_spatial": 1.0 + nrm((L, GMLP_GROUPS, GMLP_CHUNK), 0.02),
        "v_norm_g": gain((L, GMLP_WIDTH)),
        "v_norm_b": nrm((L, GMLP_WIDTH), 0.02),
        "w_out_a": nrm((L, ATT_OUT_WIDTH, D), ATT_OUT_WIDTH ** -0.5),
        "w_out_b": nrm((L, GMLP_WIDTH, D), GMLP_WIDTH ** -0.5),
        "w_out": nrm((L, D, D), D ** -0.5),
        "xattn_norm_g": gain((L, D)),
        "mem_norm_g": gain((L, D)),
        "w_q_x": nrm((L, D, D), D ** -0.5),
        "w_kv_x": nrm((L, D, 2 * D), D ** -0.5),
        "w_o_x": nrm((L, D, D), D ** -0.5),
        "moe_norm_g": gain((L, D)),
        "w_router_grp": nrm((L, D, N_EXPERT_GROUPS), D ** -0.5),
        "b_router_grp": nrm((L, N_EXPERT_GROUPS), 0.01),
        "w_router_exp": nrm((L, D, N_EXPERTS), D ** -0.5),
        "b_router_exp": nrm((L, N_EXPERTS), 0.01),
        "w_gate_e": nrm((L, N_EXPERTS, D, EXPERT_FF), D ** -0.5),
        "w_up_e": nrm((L, N_EXPERTS, D, EXPERT_FF), D ** -0.5),
        "w_down_e": nrm((L, N_EXPERTS, EXPERT_FF, D), EXPERT_FF ** -0.5),
        "final_norm_g": gain((D,)),
    }


def reference(x, mem, positions, mix_norm_g, w_in, b_gates, w_spatial, b_spatial, v_norm_g,
              v_norm_b, w_out_a, w_out_b, w_out, xattn_norm_g, mem_norm_g, w_q_x, w_kv_x,
              w_o_x, moe_norm_g, w_router_grp, b_router_grp, w_router_exp, b_router_exp,
              w_gate_e, w_up_e, w_down_e, final_norm_g):
    h = x
    for layer in range(DEPTH):
        h = h + hybrid_mixer(rmsnorm(h, mix_norm_g[layer]), positions, w_in[layer],
                             b_gates[layer], w_spatial[layer], b_spatial[layer],
                             v_norm_g[layer], v_norm_b[layer], w_out_a[layer],
                             w_out_b[layer], w_out[layer])
        h = h + memory_cross_attention(rmsnorm(h, xattn_norm_g[layer]),
                                       rmsnorm(mem, mem_norm_g[layer]),
                                       w_q_x[layer], w_kv_x[layer], w_o_x[layer])
        h = h + hierarchical_moe(rmsnorm(h, moe_norm_g[layer]), w_router_grp[layer],
                                 b_router_grp[layer], w_router_exp[layer], b_router_exp[layer],
                                 w_gate_e[layer], w_up_e[layer], w_down_e[layer])
    return rmsnorm(h, final_norm_g)
```

```python
import functools
import math

import jax
import jax.numpy as jnp
from jax import lax
from jax.experimental import pallas as pl
from jax.experimental.pallas import tpu as pltpu

F32 = jnp.float32
BF16 = jnp.bfloat16
I32 = jnp.int32

D_MODEL = 1024
BATCH = 16
SEQ = 4096
TOKENS = BATCH * SEQ

HEAD_DIM = 64
DILATIONS = (1, 4, 16)
HEADS_PER_GROUP = 4
GROUP_WIDTH = HEADS_PER_GROUP * HEAD_DIM
ATT_WIDTH = len(DILATIONS) * GROUP_WIDTH
BAND_BLOCK = 128
ROPE_THETA = 10000.0

GMLP_CHUNK = 128
GMLP_GROUPS = 4
GMLP_WIDTH = 512

N_MEM = 256
XATTN_HEADS = 4
XATTN_HEAD_DIM = D_MODEL // XATTN_HEADS

N_EXPERT_GROUPS = 4
EXPERTS_PER_GROUP = 8
N_EXPERTS = 32
TOP_K = 2
EXPERT_FF = 512

RMS_EPS = 1e-6
LN_EPS = 1e-5
NEG_INF = -1e30

LANES = 128

COL_U = 3 * ATT_WIDTH
COL_V = COL_U + GMLP_WIDTH
COL_GA = COL_V + GMLP_WIDTH
COL_GB = COL_GA + D_MODEL

ROW_BLOCK = 256
ASSIGN = TOKENS * TOP_K
PADDED_ROWS = ASSIGN + N_EXPERTS * ROW_BLOCK
N_ROW_BLOCKS = PADDED_ROWS // ROW_BLOCK
META_LANES = ((N_ROW_BLOCKS + LANES - 1) // LANES) * LANES

TM_PROJ = 512
TM_POST = 512
TL_ROUTE = 2048
SCAN_CHUNK = 256
TM_DISPATCH = 512
TM_COMBINE = 256

VMEM_LIMIT = 56 * 1024 * 1024


def _rms(x, g):
    return x * lax.rsqrt(jnp.mean(x * x, axis=-1, keepdims=True) + RMS_EPS) * g


def _resident(shape):
    nd = len(shape)
    return pl.BlockSpec(shape, lambda *_: (0,) * nd, pipeline_mode=pl.Buffered(1))


def _memkv_kernel(mem_ref, g_ref, w_ref, kv_ref):
    mn = _rms(mem_ref[...], g_ref[...]).astype(BF16)
    kv_ref[...] = jnp.dot(mn, w_ref[...], preferred_element_type=F32).astype(BF16)


def _memkv(mem2d, g, w_kv):
    rows = mem2d.shape[0]
    tm = 512
    return pl.pallas_call(
        _memkv_kernel,
        grid=(rows // tm,),
        in_specs=[
            pl.BlockSpec((tm, D_MODEL), lambda i: (i, 0)),
            _resident((1, D_MODEL)),
            _resident((D_MODEL, 2 * D_MODEL)),
        ],
        out_specs=pl.BlockSpec((tm, 2 * D_MODEL), lambda i: (i, 0)),
        out_shape=jax.ShapeDtypeStruct((rows, 2 * D_MODEL), BF16),
        compiler_params=pltpu.CompilerParams(
            dimension_semantics=("parallel",), vmem_limit_bytes=VMEM_LIMIT),
        name="memkv",
    )(mem2d, g, w_kv)


def _inproj_kernel(x_ref, pos_ref, invf_ref, phase_ref, g_ref, w_ref, bg_ref, wsp_ref,
                   bsp_ref, lng_ref, lnb_ref, wob_ref,
                   qkv0_ref, qkv1_ref, qkv2_ref, ga_ref, mb_ref, scr_ref, yb_ref):
    tm = TM_PROJ
    xn = _rms(x_ref[...], g_ref[...]).astype(BF16)

    lane = lax.broadcasted_iota(I32, (tm, LANES), 1)
    upper = (lane & 32) != 0
    t1 = jnp.sin(pos_ref[...] * invf_ref[...] + phase_ref[...])
    cosf = jnp.where(upper, t1, pltpu.roll(t1, 96, 1))
    sinf = jnp.where(upper, pltpu.roll(t1, 32, 1), -t1)

    def rope(res):
        outs = []
        for c in range(GROUP_WIDTH // LANES):
            xt = res[:, c * LANES:(c + 1) * LANES]
            rot = jnp.where(upper, pltpu.roll(xt, 32, 1), pltpu.roll(xt, 96, 1))
            outs.append(xt * cosf + rot * sinf)
        return jnp.concatenate(outs, axis=1)

    out_refs = (qkv0_ref, qkv1_ref, qkv2_ref)
    for gi, dil in enumerate(DILATIONS):
        for which in range(3):
            c0 = which * ATT_WIDTH + gi * GROUP_WIDTH
            res = jnp.dot(xn, w_ref[:, c0:c0 + GROUP_WIDTH], preferred_element_type=F32)
            if which < 2:
                res = rope(res)
            if which == 0:
                res = res * (HEAD_DIM ** -0.5)
            if dil == 1:
                out_refs[gi][0, which, 0] = res.astype(BF16)
            else:
                rows = tm // dil
                for c in range(GROUP_WIDTH // LANES):
                    slot = ((gi - 1) * 3 + which) * (GROUP_WIDTH // LANES) + c
                    scr_ref[slot] = res[:, c * LANES:(c + 1) * LANES]
                    for r in range(dil):
                        out_refs[gi][0, which, r, :, c * LANES:(c + 1) * LANES] = (
                            scr_ref[slot, pl.ds(r, rows, stride=dil), :].astype(BF16))

    zu = jax.nn.gelu(jnp.dot(xn, w_ref[:, COL_U:COL_V], preferred_element_type=F32))
    zv = jax.nn.gelu(jnp.dot(xn, w_ref[:, COL_V:COL_GA], preferred_element_type=F32))
    mu = jnp.mean(zv, axis=-1, keepdims=True)
    zc = zv - mu
    var = jnp.mean(zc * zc, axis=-1, keepdims=True)
    vn = (zc * lax.rsqrt(var + LN_EPS) * lng_ref[...] + lnb_ref[...]).astype(BF16)
    tri_r = lax.broadcasted_iota(I32, (GMLP_CHUNK, GMLP_CHUNK), 0)
    tri_c = lax.broadcasted_iota(I32, (GMLP_CHUNK, GMLP_CHUNK), 1)
    causal = tri_r >= tri_c
    n_chunks = tm // GMLP_CHUNK
    gw = GMLP_WIDTH // GMLP_GROUPS
    for g in range(GMLP_GROUPS):
        wsg = jnp.where(causal, wsp_ref[g], 0.0).astype(BF16)
        vcat = jnp.concatenate(
            [vn[c * GMLP_CHUNK:(c + 1) * GMLP_CHUNK, g * gw:(g + 1) * gw] for c in range(n_chunks)],
            axis=1)
        mixed = jnp.dot(wsg, vcat, preferred_element_type=F32) + bsp_ref[:, g:g + 1]
        for c in range(n_chunks):
            u_blk = zu[c * GMLP_CHUNK:(c + 1) * GMLP_CHUNK, g * gw:(g + 1) * gw]
            yb_ref[c * GMLP_CHUNK:(c + 1) * GMLP_CHUNK, g * gw:(g + 1) * gw] = (
                u_blk * mixed[:, c * gw:(c + 1) * gw]).astype(BF16)

    gate_a = jax.nn.sigmoid(
        jnp.dot(xn, w_ref[:, COL_GA:COL_GB], preferred_element_type=F32) + bg_ref[:, :D_MODEL])
    ga_ref[...] = gate_a.astype(BF16)
    gate_b = jax.nn.sigmoid(
        jnp.dot(xn, w_ref[:, COL_GB:COL_GB + D_MODEL], preferred_element_type=F32) + bg_ref[:, D_MODEL:])
    mb_ref[...] = (gate_b * jnp.dot(yb_ref[...], wob_ref[...], preferred_element_type=F32)).astype(BF16)


def _inproj(x2d, pos_col, invf, phase, g, w_in, b_gates, w_spatial, b_spatial_t, ln_g, ln_b, w_out_b):
    tm = TM_PROJ
    nt = SEQ // tm
    in_cols = w_in.shape[1]
    qkv_shapes = [jax.ShapeDtypeStruct((BATCH, 3, d, SEQ // d, GROUP_WIDTH), BF16) for d in DILATIONS]
    qkv_specs = [
        pl.BlockSpec((1, 3, d, tm // d, GROUP_WIDTH), lambda i: (i // nt, 0, 0, i % nt, 0))
        for d in DILATIONS
    ]
    tok_spec = pl.BlockSpec((tm, D_MODEL), lambda i: (i, 0))
    return pl.pallas_call(
        _inproj_kernel,
        grid=(TOKENS // tm,),
        in_specs=[
            tok_spec,
            pl.BlockSpec((tm, 1), lambda i: (i, 0)),
            _resident((1, LANES)),
            _resident((1, LANES)),
            _resident((1, D_MODEL)),
            _resident((D_MODEL, in_cols)),
            _resident((1, 2 * D_MODEL)),
            _resident((GMLP_GROUPS, GMLP_CHUNK, GMLP_CHUNK)),
            _resident((GMLP_CHUNK, GMLP_GROUPS)),
            _resident((1, GMLP_WIDTH)),
            _resident((1, GMLP_WIDTH)),
            _resident((GMLP_WIDTH, D_MODEL)),
        ],
        out_specs=qkv_specs + [tok_spec, tok_spec],
        out_shape=qkv_shapes + [jax.ShapeDtypeStruct((TOKENS, D_MODEL), BF16)] * 2,
        scratch_shapes=[
            pltpu.VMEM((6 * (GROUP_WIDTH // LANES), tm, LANES), F32),
            pltpu.VMEM((tm, GMLP_WIDTH), BF16),
        ],
        compiler_params=pltpu.CompilerParams(
            dimension_semantics=("parallel",), vmem_limit_bytes=VMEM_LIMIT),
        name="inproj",
    )(x2d, pos_col, invf, phase, g, w_in, b_gates, w_spatial, b_spatial_t, ln_g, ln_b, w_out_b)


def _attn_kernel(qkv0_ref, qkv1_ref, qkv2_ref, y_ref, acc_ref, lse_ref):
    blk = BAND_BLOCK
    lane_row = lax.broadcasted_iota(I32, (1, LANES), 1)
    head0_b = jnp.where(lane_row < HEAD_DIM, 1.0, 0.0).astype(BF16)
    head1_b = jnp.where(lane_row < HEAD_DIM, 0.0, 1.0).astype(BF16)
    head0 = lax.broadcasted_iota(I32, (blk, LANES), 1) < HEAD_DIM
    qi = lax.broadcasted_iota(I32, (2 * blk, 2 * blk), 0) & (blk - 1)
    kc = lax.broadcasted_iota(I32, (2 * blk, 2 * blk), 1)
    base = qi - kc

    for gi, (ref, dil) in enumerate(zip((qkv0_ref, qkv1_ref, qkv2_ref), DILATIONS)):
        seq_len = SEQ // dil
        nb = seq_len // blk
        nb_shift = nb.bit_length() - 1

        def body(i, carry, ref=ref, dil=dil, nb=nb, nb_shift=nb_shift, gi=gi):
            r = lax.shift_right_logical(i, nb_shift)
            n = i & (nb - 1)
            q0 = pl.multiple_of(n * blk, blk)
            w0 = pl.multiple_of(jnp.maximum(n - 1, 0) * blk, blk)
            off = n * blk - w0
            q = ref[0, 0, r, pl.ds(q0, blk), :]
            k = ref[0, 1, r, pl.ds(w0, 2 * blk), :]
            v = ref[0, 2, r, pl.ds(w0, 2 * blk), :]
            q2 = jnp.concatenate([q * head0_b, q * head1_b], axis=0)
            s = lax.dot_general(q2, k, (((1,), (1,)), ((), ())), preferred_element_type=F32)
            dist = base + off
            valid = (dist >= 0) & (dist <= blk)
            s = jnp.where(valid, s, NEG_INF)
            m = jnp.max(s, axis=-1, keepdims=True)
            p = jnp.exp(s - m)
            den = jnp.sum(p, axis=-1, keepdims=True)
            o2 = jnp.dot(p.astype(BF16), v, preferred_element_type=F32) / den
            lse2 = m + jnp.log(den)
            o = jnp.where(head0, o2[:blk], o2[blk:])
            l = jnp.where(head0, lse2[:blk], lse2[blk:])
            if gi == 0:
                acc_ref[pl.ds(q0, blk), :] = o
                lse_ref[pl.ds(q0, blk), :] = l
            else:
                idx = pl.ds(n * (blk * dil) + r, blk, stride=dil)
                l_old = lse_ref[idx, :]
                a_old = acc_ref[idx, :]
                mx = jnp.maximum(l_old, l)
                e_old = jnp.exp(l_old - mx)
                e_new = jnp.exp(l - mx)
                tot = e_old + e_new
                acc_ref[idx, :] = (a_old * e_old + o * e_new) / tot
                lse_ref[idx, :] = mx + jnp.log(tot)
            return carry

        lax.fori_loop(0, dil * nb, body, 0)

    y_ref[0] = acc_ref[...].astype(BF16)


def _attention(qkv):
    in_specs = [
        pl.BlockSpec((1, 3, d, SEQ // d, LANES), lambda b, h: (b, 0, 0, 0, h)) for d in DILATIONS
    ]
    return pl.pallas_call(
        _attn_kernel,
        grid=(BATCH, GROUP_WIDTH // LANES),
        in_specs=in_specs,
        out_specs=pl.BlockSpec((1, SEQ, LANES), lambda b, h: (b, 0, h)),
        out_shape=jax.ShapeDtypeStruct((BATCH, SEQ, GROUP_WIDTH), BF16),
        scratch_shapes=[pltpu.VMEM((SEQ, LANES), F32), pltpu.VMEM((SEQ, LANES), F32)],
        compiler_params=pltpu.CompilerParams(
            dimension_semantics=("parallel", "parallel"), vmem_limit_bytes=VMEM_LIMIT),
        name="attn",
    )(*qkv)


def _post_kernel(ya_ref, ga_ref, mb_ref, x_ref, k_ref, v_ref, woa_ref, wo_ref, xg_ref, wq_ref,
                 wox_ref, mg_ref, wr_ref, br_ref, h_ref, hn_ref, route_ref, o_scr):
    tm = TM_POST
    t = jnp.dot(ya_ref[...], woa_ref[...], preferred_element_type=F32)
    merged = (ga_ref[...].astype(F32) * t + mb_ref[...].astype(F32)).astype(BF16)
    h1 = x_ref[...] + jnp.dot(merged, wo_ref[...], preferred_element_type=F32)

    hn = _rms(h1, xg_ref[...]).astype(BF16)
    q = (jnp.dot(hn, wq_ref[...], preferred_element_type=F32) * (XATTN_HEAD_DIM ** -0.5)).astype(BF16)
    hd = XATTN_HEAD_DIM
    for h in range(XATTN_HEADS):
        s = lax.dot_general(q[:, h * hd:(h + 1) * hd], k_ref[:, h * hd:(h + 1) * hd],
                            (((1,), (1,)), ((), ())), preferred_element_type=F32)
        m = jnp.max(s, axis=-1, keepdims=True)
        p = jnp.exp(s - m)
        den = jnp.sum(p, axis=-1, keepdims=True)
        oh = jnp.dot(p.astype(BF16), v_ref[:, h * hd:(h + 1) * hd], preferred_element_type=F32) / den
        o_scr[:, h * hd:(h + 1) * hd] = oh.astype(BF16)
    h2 = h1 + jnp.dot(o_scr[...], wox_ref[...], preferred_element_type=F32)
    h_ref[...] = h2

    hn2 = _rms(h2, mg_ref[...])
    hn_ref[...] = hn2

    logits = jnp.dot(hn2.astype(BF16), wr_ref[...], preferred_element_type=F32) + br_ref[...]
    li = lax.broadcasted_iota(I32, (tm, LANES), 1)
    lif = li.astype(F32)
    grp_of_lane = lax.shift_right_logical(li, 3).astype(F32)
    is_grp = (li >= N_EXPERTS) & (li < N_EXPERTS + N_EXPERT_GROUPS)
    gl = jnp.where(is_grp, logits, -jnp.inf)
    gmax = jnp.max(gl, axis=-1, keepdims=True)
    grp = jnp.min(jnp.where(gl == gmax, lif - N_EXPERTS, float(LANES)), axis=-1, keepdims=True)
    gsum = jnp.sum(jnp.where(is_grp, jnp.exp(logits - gmax), 0.0), axis=-1, keepdims=True)
    grp_gate = 1.0 / gsum
    in_grp = grp_of_lane == grp
    el = jnp.where(in_grp, logits, -jnp.inf)
    v1 = jnp.max(el, axis=-1, keepdims=True)
    i1 = jnp.min(jnp.where(el == v1, lif, float(LANES)), axis=-1, keepdims=True)
    el2 = jnp.where(lif == i1, -jnp.inf, el)
    v2 = jnp.max(el2, axis=-1, keepdims=True)
    i2 = jnp.min(jnp.where(el2 == v2, lif, float(LANES)), axis=-1, keepdims=True)
    tt = jnp.exp(v2 - v1)
    w1 = grp_gate / (1.0 + tt)
    w2 = grp_gate * tt / (1.0 + tt)
    l8 = lax.broadcasted_iota(I32, (tm, 8), 1)
    route = jnp.where(l8 == 0, i1,
                      jnp.where(l8 == 1, i2,
                                jnp.where(l8 == 2, w1, jnp.where(l8 == 3, w2, 0.0))))
    route_ref[...] = route


def _post(ya, ga, mb, x2d, kv, w_out_a, w_out, xg, w_q, w_o, mg, w_r, b_r):
    tm = TM_POST
    nt = SEQ // tm
    tok = lambda w: pl.BlockSpec((tm, w), lambda i: (i, 0))
    return pl.pallas_call(
        _post_kernel,
        grid=(TOKENS // tm,),
        in_specs=[
            tok(GROUP_WIDTH), tok(D_MODEL), tok(D_MODEL), tok(D_MODEL),
            pl.BlockSpec((N_MEM, D_MODEL), lambda i: (i // nt, 0)),
            pl.BlockSpec((N_MEM, D_MODEL), lambda i: (i // nt, 1)),
            _resident((GROUP_WIDTH, D_MODEL)),
            _resident((D_MODEL, D_MODEL)),
            _resident((1, D_MODEL)),
            _resident((D_MODEL, D_MODEL)),
            _resident((D_MODEL, D_MODEL)),
            _resident((1, D_MODEL)),
            _resident((D_MODEL, LANES)),
            _resident((1, LANES)),
        ],
        out_specs=[tok(D_MODEL), tok(D_MODEL), pl.BlockSpec((tm, 8), lambda i: (i, 0))],
        out_shape=[
            jax.ShapeDtypeStruct((TOKENS, D_MODEL), F32),
            jax.ShapeDtypeStruct((TOKENS, D_MODEL), F32),
            jax.ShapeDtypeStruct((TOKENS, 8), F32),
        ],
        scratch_shapes=[pltpu.VMEM((tm, D_MODEL), BF16)],
        compiler_params=pltpu.CompilerParams(
            dimension_semantics=("parallel",), vmem_limit_bytes=VMEM_LIMIT),
        name="post",
    )(ya, ga, mb, x2d, kv, kv, w_out_a, w_out, xg, w_q, w_o, mg, w_r, b_r)


def _route_kernel(rt_ref, dest_ref, meta_ref, rank_scr, carry_ref, pstart_ref):
    pss = pl.program_id(0)
    i = pl.program_id(1)
    tl = TL_ROUTE
    ch = SCAN_CHUNK
    ei = lax.broadcasted_iota(I32, (N_EXPERTS, ch), 0).astype(F32)

    @pl.when((pss == 0) & (i == 0))
    def _():
        carry_ref[...] = jnp.zeros_like(carry_ref)

    @pl.when(pss == 0)
    def _():
        ur = lax.broadcasted_iota(I32, (ch, ch), 0)
        uc = lax.broadcasted_iota(I32, (ch, ch), 1)
        upper = jnp.where(ur < uc, 1.0, 0.0).astype(BF16)
        for c in range(tl // ch):
            e1 = rt_ref[0:1, c * ch:(c + 1) * ch]
            e2 = rt_ref[1:2, c * ch:(c + 1) * ch]
            oh1 = e1 == ei
            oh2 = e2 == ei
            oh = jnp.where(oh1 | oh2, 1.0, 0.0)
            cnt = jnp.dot(oh.astype(BF16), upper, preferred_element_type=F32) + carry_ref[:, 0:1]
            rank1 = jnp.sum(jnp.where(oh1, cnt, 0.0), axis=0, keepdims=True)
            rank2 = jnp.sum(jnp.where(oh2, cnt, 0.0), axis=0, keepdims=True)
            col = pl.multiple_of(i * tl + c * ch, ch)
            rank_scr[0:1, pl.ds(col, ch)] = rank1
            rank_scr[1:2, pl.ds(col, ch)] = rank2
            carry_ref[...] = carry_ref[...] + jnp.sum(oh, axis=1, keepdims=True)

    @pl.when((pss == 1) & (i == 0))
    def _():
        counts = carry_ref[...].astype(I32)
        padded = lax.shift_left(lax.shift_right_logical(counts + (ROW_BLOCK - 1),
                                                        int(math.log2(ROW_BLOCK))),
                                int(math.log2(ROW_BLOCK)))
        row = lax.broadcasted_iota(I32, (N_EXPERTS, LANES), 0)
        lane = lax.broadcasted_iota(I32, (N_EXPERTS, LANES), 1)
        pend = padded
        sh = 1
        while sh < N_EXPERTS:
            pend = pend + jnp.where(row >= sh, pltpu.roll(pend, sh, 0), 0)
            sh *= 2
        pstart = pend - padded
        pstart_ref[...] = pstart
        diag = row == lane

        def as_row(x):
            return jnp.sum(jnp.where(diag, x, 0), axis=0, keepdims=True)

        blk_lane = lax.broadcasted_iota(I32, (N_EXPERTS, META_LANES), 1) * ROW_BLOCK
        blk_exp = jnp.sum(jnp.where(pend[:, 0:1] <= blk_lane, 1, 0), axis=0, keepdims=True)
        blk_exp = jnp.minimum(blk_exp, N_EXPERTS - 1)
        meta_ref[...] = jnp.zeros_like(meta_ref)
        meta_ref[0:1, :] = blk_exp
        meta_ref[1:2, 0:LANES] = as_row(counts)
        meta_ref[2:3, 0:LANES] = as_row(pstart)
        meta_ref[3:4, 0:LANES] = as_row(pend)

    @pl.when(pss == 1)
    def _():
        ps = pstart_ref[:, 0:1].astype(F32)
        for c in range(tl // ch):
            e1 = rt_ref[0:1, c * ch:(c + 1) * ch]
            e2 = rt_ref[1:2, c * ch:(c + 1) * ch]
            col = pl.multiple_of(i * tl + c * ch, ch)
            d1 = rank_scr[0:1, pl.ds(col, ch)] + jnp.sum(jnp.where(e1 == ei, ps, 0.0), axis=0, keepdims=True)
            d2 = rank_scr[1:2, pl.ds(col, ch)] + jnp.sum(jnp.where(e2 == ei, ps, 0.0), axis=0, keepdims=True)
            dest_ref[0:1, c * ch:(c + 1) * ch] = d1.astype(I32)
            dest_ref[1:2, c * ch:(c + 1) * ch] = d2.astype(I32)


def _route(rt):
    tl = TL_ROUTE
    return pl.pallas_call(
        _route_kernel,
        grid=(2, TOKENS // tl),
        in_specs=[pl.BlockSpec((8, tl), lambda p, i: (0, i))],
        out_specs=[
            pl.BlockSpec((2, tl), lambda p, i: (0, i * p)),
            pl.BlockSpec((8, META_LANES), lambda p, i: (0, 0)),
        ],
        out_shape=[
            jax.ShapeDtypeStruct((2, TOKENS), I32),
            jax.ShapeDtypeStruct((8, META_LANES), I32),
        ],
        scratch_shapes=[
            pltpu.VMEM((2, TOKENS), F32),
            pltpu.VMEM((N_EXPERTS, LANES), F32),
            pltpu.VMEM((N_EXPERTS, LANES), I32),
        ],
        compiler_params=pltpu.CompilerParams(
            dimension_semantics=("arbitrary", "arbitrary"), vmem_limit_bytes=VMEM_LIMIT),
        name="route",
    )(rt)


def _dispatch_kernel(dest_ref, meta_ref, hn_ref, xs_ref, zrow_ref, sem, zsem):
    i = pl.program_id(0)
    tm = TM_DISPATCH

    def zero_copy(row):
        return pltpu.make_async_copy(zrow_ref.at[pl.ds(0, 1)], xs_ref.at[pl.ds(row, 1)], zsem)

    @pl.when(i == 0)
    def _():
        zrow_ref[...] = jnp.zeros_like(zrow_ref)
        for e in range(N_EXPERTS):
            lo = meta_ref[2, e] + meta_ref[1, e]
            hi = meta_ref[3, e]

            def start(r, c):
                zero_copy(r).start()
                return c

            def wait(r, c):
                zero_copy(r).wait()
                return c

            lax.fori_loop(lo, hi, start, 0)
            lax.fori_loop(lo, hi, wait, 0)

    def row_copy(t, k):
        return pltpu.make_async_copy(hn_ref.at[pl.ds(t, 1)], xs_ref.at[pl.ds(dest_ref[k, t], 1)], sem)

    def start(t, c):
        row_copy(t, 0).start()
        row_copy(t, 1).start()
        return c

    def wait(t, c):
        row_copy(t, 0).wait()
        row_copy(t, 1).wait()
        return c

    lax.fori_loop(0, tm, start, 0, unroll=8)
    lax.fori_loop(0, tm, wait, 0, unroll=8)


def _dispatch(dest, meta, hn2):
    tm = TM_DISPATCH
    return pl.pallas_call(
        _dispatch_kernel,
        grid=(TOKENS // tm,),
        in_specs=[
            pl.BlockSpec((2, tm), lambda i: (0, i), memory_space=pltpu.SMEM),
            pl.BlockSpec(memory_space=pltpu.SMEM),
            pl.BlockSpec((tm, D_MODEL), lambda i: (i, 0)),
        ],
        out_specs=pl.BlockSpec(memory_space=pl.ANY),
        out_shape=jax.ShapeDtypeStruct((PADDED_ROWS, D_MODEL), F32),
        scratch_shapes=[
            pltpu.VMEM((8, D_MODEL), F32),
            pltpu.SemaphoreType.DMA(()),
            pltpu.SemaphoreType.DMA(()),
        ],
        compiler_params=pltpu.CompilerParams(
            dimension_semantics=("arbitrary",), vmem_limit_bytes=VMEM_LIMIT),
        name="dispatch",
    )(dest, meta, hn2)


def _experts_kernel(be_ref, nv_ref, xs_ref, wg_ref, wu_ref, wd_ref, yb_ref):
    j = pl.program_id(0)

    @pl.when(j < nv_ref[0])
    def _():
        xb = xs_ref[...].astype(BF16)
        a = jnp.dot(xb, wg_ref[0], preferred_element_type=F32)
        b = jnp.dot(xb, wu_ref[0], preferred_element_type=F32)
        hb = (jax.nn.silu(a) * b).astype(BF16)
        yb_ref[...] = jnp.dot(hb, wd_ref[0], preferred_element_type=F32)


def _experts(blk_expert, n_valid, xs, w_gate, w_up, w_down):
    def row_map(j, be, nv):
        return (jnp.minimum(j, nv[0] - 1), 0)

    def w_map(j, be, nv):
        return (be[jnp.minimum(j, nv[0] - 1)], 0, 0)

    grid_spec = pltpu.PrefetchScalarGridSpec(
        num_scalar_prefetch=2,
        grid=(N_ROW_BLOCKS,),
        in_specs=[
            pl.BlockSpec((ROW_BLOCK, D_MODEL), row_map),
            pl.BlockSpec((1, D_MODEL, EXPERT_FF), w_map),
            pl.BlockSpec((1, D_MODEL, EXPERT_FF), w_map),
            pl.BlockSpec((1, EXPERT_FF, D_MODEL), w_map),
        ],
        out_specs=pl.BlockSpec((ROW_BLOCK, D_MODEL), row_map),
    )
    return pl.pallas_call(
        _experts_kernel,
        grid_spec=grid_spec,
        out_shape=jax.ShapeDtypeStruct((PADDED_ROWS, D_MODEL), F32),
        compiler_params=pltpu.CompilerParams(
            dimension_semantics=("arbitrary",), vmem_limit_bytes=VMEM_LIMIT),
        name="experts",
    )(blk_expert, n_valid, xs, w_gate, w_up, w_down)


def _combine_kernel(dest_ref, route_ref, h_ref, g_ref, yb_ref, out_ref, buf_ref, sem):
    tm = TM_COMBINE

    def row_copy(t, k):
        return pltpu.make_async_copy(yb_ref.at[pl.ds(dest_ref[k, t], 1)], buf_ref.at[k, pl.ds(t, 1)], sem)

    def start(t, c):
        row_copy(t, 0).start()
        row_copy(t, 1).start()
        return c

    def wait(t, c):
        row_copy(t, 0).wait()
        row_copy(t, 1).wait()
        return c

    lax.fori_loop(0, tm, start, 0, unroll=8)
    lax.fori_loop(0, tm, wait, 0, unroll=8)
    w1 = route_ref[:, 2:3]
    w2 = route_ref[:, 3:4]
    h3 = h_ref[...] + (buf_ref[0] * w1 + buf_ref[1] * w2)
    out_ref[...] = _rms(h3, g_ref[...])


def _combine(dest, route, h2, g, yb):
    tm = TM_COMBINE
    return pl.pallas_call(
        _combine_kernel,
        grid=(TOKENS // tm,),
        in_specs=[
            pl.BlockSpec((2, tm), lambda i: (0, i), memory_space=pltpu.SMEM),
            pl.BlockSpec((tm, 8), lambda i: (i, 0)),
            pl.BlockSpec((tm, D_MODEL), lambda i: (i, 0)),
            _resident((1, D_MODEL)),
            pl.BlockSpec(memory_space=pl.ANY),
        ],
        out_specs=pl.BlockSpec((tm, D_MODEL), lambda i: (i, 0)),
        out_shape=jax.ShapeDtypeStruct((TOKENS, D_MODEL), F32),
        scratch_shapes=[pltpu.VMEM((2, tm, D_MODEL), F32), pltpu.SemaphoreType.DMA(())],
        compiler_params=pltpu.CompilerParams(
            dimension_semantics=("arbitrary",), vmem_limit_bytes=VMEM_LIMIT),
        name="combine",
    )(dest, route, h2, g, yb)


def kernel(x, mem, positions, mix_norm_g, w_in, b_gates, w_spatial, b_spatial, v_norm_g, v_norm_b,
           w_out_a, w_out_b, w_out, xattn_norm_g, mem_norm_g, w_q_x, w_kv_x, w_o_x, moe_norm_g,
           w_router_grp, b_router_grp, w_router_exp, b_router_exp, w_gate_e, w_up_e, w_down_e,
           final_norm_g):
    assert x.shape == (BATCH, SEQ, D_MODEL) and mem.shape == (BATCH, N_MEM, D_MODEL)
    assert mix_norm_g.shape[0] == 1, "single layer"
    x2d = x.reshape(TOKENS, D_MODEL)
    pos_col = positions.reshape(TOKENS, 1).astype(F32)
    half = HEAD_DIM // 2
    inv_freq = ROPE_THETA ** (-jnp.arange(half, dtype=F32) / half)
    invf = jnp.tile(inv_freq, LANES // half).reshape(1, LANES)
    phase = jnp.tile(jnp.concatenate([jnp.zeros((half,), F32), jnp.full((half,), math.pi / 2, F32)]),
                     LANES // HEAD_DIM).reshape(1, LANES)

    kv = _memkv(mem.reshape(BATCH * N_MEM, D_MODEL), mem_norm_g[0].reshape(1, D_MODEL),
                w_kv_x[0].astype(BF16))

    qkv0, qkv1, qkv2, ga, mb = _inproj(
        x2d, pos_col, invf, phase, mix_norm_g[0].reshape(1, D_MODEL), w_in[0].astype(BF16),
        b_gates[0].reshape(1, 2 * D_MODEL), w_spatial[0], b_spatial[0].T,
        v_norm_g[0].reshape(1, GMLP_WIDTH), v_norm_b[0].reshape(1, GMLP_WIDTH),
        w_out_b[0].astype(BF16))

    ya = _attention((qkv0, qkv1, qkv2)).reshape(TOKENS, GROUP_WIDTH)

    pad = LANES - N_EXPERTS - N_EXPERT_GROUPS
    w_r = jnp.concatenate([w_router_exp[0], w_router_grp[0], jnp.zeros((D_MODEL, pad), F32)], axis=1)
    b_r = jnp.concatenate([b_router_exp[0], b_router_grp[0], jnp.zeros((pad,), F32)]).reshape(1, LANES)
    h2, hn2, route = _post(
        ya, ga, mb, x2d, kv, w_out_a[0].astype(BF16), w_out[0].astype(BF16),
        xattn_norm_g[0].reshape(1, D_MODEL), w_q_x[0].astype(BF16), w_o_x[0].astype(BF16),
        moe_norm_g[0].reshape(1, D_MODEL), w_r.astype(BF16), b_r)

    dest, meta = _route(route.T)
    xs = _dispatch(dest, meta, hn2)
    blk_expert = meta[0, :N_ROW_BLOCKS]
    n_valid = (meta[3, N_EXPERTS - 1:N_EXPERTS] // ROW_BLOCK).astype(I32)
    yb = _experts(blk_expert, n_valid, xs, w_gate_e[0].astype(BF16), w_up_e[0].astype(BF16),
                  w_down_e[0].astype(BF16))
    out = _combine(dest, route, h2, final_norm_g.reshape(1, D_MODEL), yb)
    return out.reshape(BATCH, SEQ, D_MODEL)
```

```python
import functools
import math

import jax
import jax.numpy as jnp
from jax import lax
from jax.experimental import pallas as pl
from jax.experimental.pallas import tpu as pltpu

F32 = jnp.float32
BF16 = jnp.bfloat16
I32 = jnp.int32

D_MODEL = 1024
BATCH = 16
SEQ = 4096
TOKENS = BATCH * SEQ

HEAD_DIM = 64
DILATIONS = (1, 4, 16)
HEADS_PER_GROUP = 4
GROUP_WIDTH = HEADS_PER_GROUP * HEAD_DIM
ATT_WIDTH = len(DILATIONS) * GROUP_WIDTH
BAND_BLOCK = 128
ROPE_THETA = 10000.0

GMLP_CHUNK = 128
GMLP_GROUPS = 4
GMLP_WIDTH = 512

N_MEM = 256
XATTN_HEADS = 4
XATTN_HEAD_DIM = D_MODEL // XATTN_HEADS

N_EXPERT_GROUPS = 4
EXPERTS_PER_GROUP = 8
N_EXPERTS = 32
TOP_K = 2
EXPERT_FF = 512

RMS_EPS = 1e-6
LN_EPS = 1e-5
NEG_INF = -1e30

LANES = 128

COL_U = 3 * ATT_WIDTH
COL_V = COL_U + GMLP_WIDTH
COL_GA = COL_V + GMLP_WIDTH
COL_GB = COL_GA + D_MODEL

ROW_BLOCK = 512
ASSIGN = TOKENS * TOP_K
PADDED_ROWS = ASSIGN + N_EXPERTS * ROW_BLOCK
N_ROW_BLOCKS = PADDED_ROWS // ROW_BLOCK
META_LANES = ((N_ROW_BLOCKS + LANES - 1) // LANES) * LANES

TM_PROJ = 512
TM_POST = 512
TL_ROUTE = 2048
SCAN_CHUNK = 256
TM_DISPATCH = 512
TM_COMBINE = 256
ATTN_UNROLL = 4

VMEM_LIMIT = 56 * 1024 * 1024


def _rms(x, g):
    return x * lax.rsqrt(jnp.mean(x * x, axis=-1, keepdims=True) + RMS_EPS) * g


def _resident(shape):
    nd = len(shape)
    return pl.BlockSpec(shape, lambda *_: (0,) * nd, pipeline_mode=pl.Buffered(1))


def _memkv_kernel(mem_ref, g_ref, w_ref, kv_ref):
    mn = _rms(mem_ref[...], g_ref[...]).astype(BF16)
    kv_ref[...] = jnp.dot(mn, w_ref[...], preferred_element_type=F32).astype(BF16)


def _memkv(mem2d, g, w_kv):
    rows = mem2d.shape[0]
    tm = 512
    return pl.pallas_call(
        _memkv_kernel,
        grid=(rows // tm,),
        in_specs=[
            pl.BlockSpec((tm, D_MODEL), lambda i: (i, 0)),
            _resident((1, D_MODEL)),
            _resident((D_MODEL, 2 * D_MODEL)),
        ],
        out_specs=pl.BlockSpec((tm, 2 * D_MODEL), lambda i: (i, 0)),
        out_shape=jax.ShapeDtypeStruct((rows, 2 * D_MODEL), BF16),
        compiler_params=pltpu.CompilerParams(
            dimension_semantics=("parallel",), vmem_limit_bytes=VMEM_LIMIT),
        name="memkv",
    )(mem2d, g, w_kv)


def _inproj_kernel(x_ref, pos_ref, invf_ref, phase_ref, g_ref, w_ref, bg_ref, wsp_ref,
                   bsp_ref, lng_ref, lnb_ref, wob_ref,
                   qkv0_ref, qkv1_ref, qkv2_ref, ga_ref, mb_ref, scr_ref, yb_ref):
    tm = TM_PROJ
    xn = _rms(x_ref[...], g_ref[...]).astype(BF16)

    lane = lax.broadcasted_iota(I32, (tm, LANES), 1)
    upper = (lane & 32) != 0
    t1 = jnp.sin(pos_ref[...] * invf_ref[...] + phase_ref[...])
    cosf = jnp.where(upper, t1, pltpu.roll(t1, 96, 1))
    sinf = jnp.where(upper, pltpu.roll(t1, 32, 1), -t1)

    def rope(res):
        outs = []
        for c in range(GROUP_WIDTH // LANES):
            xt = res[:, c * LANES:(c + 1) * LANES]
            rot = jnp.where(upper, pltpu.roll(xt, 32, 1), pltpu.roll(xt, 96, 1))
            outs.append(xt * cosf + rot * sinf)
        return jnp.concatenate(outs, axis=1)

    out_refs = (qkv0_ref, qkv1_ref, qkv2_ref)
    for gi, dil in enumerate(DILATIONS):
        for which in range(3):
            c0 = which * ATT_WIDTH + gi * GROUP_WIDTH
            res = jnp.dot(xn, w_ref[:, c0:c0 + GROUP_WIDTH], preferred_element_type=F32)
            if which < 2:
                res = rope(res)
            if which == 0:
                res = res * (HEAD_DIM ** -0.5)
            if dil == 1:
                out_refs[gi][0, which, 0] = res.astype(BF16)
            else:
                rows = tm // dil
                for c in range(GROUP_WIDTH // LANES):
                    slot = ((gi - 1) * 3 + which) * (GROUP_WIDTH // LANES) + c
                    scr_ref[slot] = res[:, c * LANES:(c + 1) * LANES]
                    for r in range(dil):
                        out_refs[gi][0, which, r, :, c * LANES:(c + 1) * LANES] = (
                            scr_ref[slot, pl.ds(r, rows, stride=dil), :].astype(BF16))

    zu = jax.nn.gelu(jnp.dot(xn, w_ref[:, COL_U:COL_V], preferred_element_type=F32))
    zv = jax.nn.gelu(jnp.dot(xn, w_ref[:, COL_V:COL_GA], preferred_element_type=F32))
    mu = jnp.mean(zv, axis=-1, keepdims=True)
    zc = zv - mu
    var = jnp.mean(zc * zc, axis=-1, keepdims=True)
    vn = (zc * lax.rsqrt(var + LN_EPS) * lng_ref[...] + lnb_ref[...]).astype(BF16)
    tri_r = lax.broadcasted_iota(I32, (GMLP_CHUNK, GMLP_CHUNK), 0)
    tri_c = lax.broadcasted_iota(I32, (GMLP_CHUNK, GMLP_CHUNK), 1)
    causal = tri_r >= tri_c
    n_chunks = tm // GMLP_CHUNK
    gw = GMLP_WIDTH // GMLP_GROUPS
    for g in range(GMLP_GROUPS):
        wsg = jnp.where(causal, wsp_ref[g], 0.0).astype(BF16)
        vcat = jnp.concatenate(
            [vn[c * GMLP_CHUNK:(c + 1) * GMLP_CHUNK, g * gw:(g + 1) * gw] for c in range(n_chunks)],
            axis=1)
        mixed = jnp.dot(wsg, vcat, preferred_element_type=F32) + bsp_ref[:, g:g + 1]
        for c in range(n_chunks):
            u_blk = zu[c * GMLP_CHUNK:(c + 1) * GMLP_CHUNK, g * gw:(g + 1) * gw]
            yb_ref[c * GMLP_CHUNK:(c + 1) * GMLP_CHUNK, g * gw:(g + 1) * gw] = (
                u_blk * mixed[:, c * gw:(c + 1) * gw]).astype(BF16)

    gate_a = jax.nn.sigmoid(
        jnp.dot(xn, w_ref[:, COL_GA:COL_GB], preferred_element_type=F32) + bg_ref[:, :D_MODEL])
    ga_ref[...] = gate_a.astype(BF16)
    gate_b = jax.nn.sigmoid(
        jnp.dot(xn, w_ref[:, COL_GB:COL_GB + D_MODEL], preferred_element_type=F32) + bg_ref[:, D_MODEL:])
    mb_ref[...] = (gate_b * jnp.dot(yb_ref[...], wob_ref[...], preferred_element_type=F32)).astype(BF16)


def _inproj(x2d, pos_col, invf, phase, g, w_in, b_gates, w_spatial, b_spatial_t, ln_g, ln_b, w_out_b):
    tm = TM_PROJ
    nt = SEQ // tm
    in_cols = w_in.shape[1]
    qkv_shapes = [jax.ShapeDtypeStruct((BATCH, 3, d, SEQ // d, GROUP_WIDTH), BF16) for d in DILATIONS]
    qkv_specs = [
        pl.BlockSpec((1, 3, d, tm // d, GROUP_WIDTH), lambda i: (i // nt, 0, 0, i % nt, 0))
        for d in DILATIONS
    ]
    tok_spec = pl.BlockSpec((tm, D_MODEL), lambda i: (i, 0))
    return pl.pallas_call(
        _inproj_kernel,
        grid=(TOKENS // tm,),
        in_specs=[
            tok_spec,
            pl.BlockSpec((tm, 1), lambda i: (i, 0)),
            _resident((1, LANES)),
            _resident((1, LANES)),
            _resident((1, D_MODEL)),
            _resident((D_MODEL, in_cols)),
            _resident((1, 2 * D_MODEL)),
            _resident((GMLP_GROUPS, GMLP_CHUNK, GMLP_CHUNK)),
            _resident((GMLP_CHUNK, GMLP_GROUPS)),
            _resident((1, GMLP_WIDTH)),
            _resident((1, GMLP_WIDTH)),
            _resident((GMLP_WIDTH, D_MODEL)),
        ],
        out_specs=qkv_specs + [tok_spec, tok_spec],
        out_shape=qkv_shapes + [jax.ShapeDtypeStruct((TOKENS, D_MODEL), BF16)] * 2,
        scratch_shapes=[
            pltpu.VMEM((6 * (GROUP_WIDTH // LANES), tm, LANES), F32),
            pltpu.VMEM((tm, GMLP_WIDTH), BF16),
        ],
        compiler_params=pltpu.CompilerParams(
            dimension_semantics=("parallel",), vmem_limit_bytes=VMEM_LIMIT),
        name="inproj",
    )(x2d, pos_col, invf, phase, g, w_in, b_gates, w_spatial, b_spatial_t, ln_g, ln_b, w_out_b)


def _attn_kernel(qkv0_ref, qkv1_ref, qkv2_ref, y_ref, acc_ref, m_ref, z_ref, bias_ref):
    blk = BAND_BLOCK
    lane_row = lax.broadcasted_iota(I32, (1, LANES), 1)
    head0_b = jnp.where(lane_row < HEAD_DIM, 1.0, 0.0).astype(BF16)
    head1_b = jnp.where(lane_row < HEAD_DIM, 0.0, 1.0).astype(BF16)
    head0 = lax.broadcasted_iota(I32, (blk, LANES), 1) < HEAD_DIM
    ones_b = jnp.ones((2 * blk, LANES), BF16)

    qi = lax.broadcasted_iota(I32, (2 * blk, 2 * blk), 0) & (blk - 1)
    kc = lax.broadcasted_iota(I32, (2 * blk, 2 * blk), 1)
    for slot, off in enumerate((0, blk)):
        dist = qi + off - kc
        bias_ref[slot] = jnp.where((dist >= 0) & (dist <= blk), 0.0, NEG_INF)

    for gi, (ref, dil) in enumerate(zip((qkv0_ref, qkv1_ref, qkv2_ref), DILATIONS)):
        seq_len = SEQ // dil
        nb = seq_len // blk
        nb_shift = nb.bit_length() - 1

        def body(i, carry, ref=ref, dil=dil, nb=nb, nb_shift=nb_shift, gi=gi):
            r = lax.shift_right_logical(i, nb_shift)
            n = i & (nb - 1)
            q0 = pl.multiple_of(n * blk, blk)
            w0 = pl.multiple_of(jnp.maximum(n - 1, 0) * blk, blk)
            q = ref[0, 0, r, pl.ds(q0, blk), :]
            k = ref[0, 1, r, pl.ds(w0, 2 * blk), :]
            v = ref[0, 2, r, pl.ds(w0, 2 * blk), :]
            q2 = jnp.concatenate([q * head0_b, q * head1_b], axis=0)
            s = lax.dot_general(q2, k, (((1,), (1,)), ((), ())), preferred_element_type=F32)
            s = s + bias_ref[jnp.minimum(n, 1)]
            m2 = jnp.max(s, axis=-1, keepdims=True)
            p = jnp.exp(s - m2)
            v_ext = jnp.concatenate([v, ones_b], axis=1)
            o2 = jnp.dot(p.astype(BF16), v_ext, preferred_element_type=F32)
            o = jnp.where(head0, o2[:blk, :LANES], o2[blk:, :LANES])
            den = jnp.where(head0, o2[:blk, LANES:], o2[blk:, LANES:])
            m = jnp.where(head0, m2[:blk], m2[blk:])
            if gi == 0:
                acc_ref[pl.ds(q0, blk), :] = o
                m_ref[pl.ds(q0, blk), :] = m
                z_ref[pl.ds(q0, blk), :] = den
            else:
                idx = pl.ds(n * (blk * dil) + r, blk, stride=dil)
                m_old = m_ref[idx, :]
                m_new = jnp.maximum(m_old, m)
                e_old = jnp.exp(m_old - m_new)
                e_new = jnp.exp(m - m_new)
                acc_ref[idx, :] = acc_ref[idx, :] * e_old + o * e_new
                z_ref[idx, :] = z_ref[idx, :] * e_old + den * e_new
                m_ref[idx, :] = m_new
            return carry

        lax.fori_loop(0, dil * nb, body, 0, unroll=ATTN_UNROLL)

    y_ref[0] = (acc_ref[...] / z_ref[...]).astype(BF16)


def _attention(qkv):
    in_specs = [
        pl.BlockSpec((1, 3, d, SEQ // d, LANES), lambda b, h: (b, 0, 0, 0, h)) for d in DILATIONS
    ]
    return pl.pallas_call(
        _attn_kernel,
        grid=(BATCH, GROUP_WIDTH // LANES),
        in_specs=in_specs,
        out_specs=pl.BlockSpec((1, SEQ, LANES), lambda b, h: (b, 0, h)),
        out_shape=jax.ShapeDtypeStruct((BATCH, SEQ, GROUP_WIDTH), BF16),
        scratch_shapes=[
            pltpu.VMEM((SEQ, LANES), F32), pltpu.VMEM((SEQ, LANES), F32), pltpu.VMEM((SEQ, LANES), F32),
            pltpu.VMEM((2, 2 * BAND_BLOCK, 2 * BAND_BLOCK), F32),
        ],
        compiler_params=pltpu.CompilerParams(
            dimension_semantics=("parallel", "parallel"), vmem_limit_bytes=VMEM_LIMIT),
        name="attn",
    )(*qkv)


def _post_kernel(ya_ref, ga_ref, mb_ref, x_ref, k_ref, v_ref, woa_ref, wo_ref, xg_ref, wq_ref,
                 wox_ref, mg_ref, wr_ref, br_ref, h_ref, hn_ref, route_ref, o_scr):
    tm = TM_POST
    t = jnp.dot(ya_ref[...], woa_ref[...], preferred_element_type=F32)
    merged = (ga_ref[...].astype(F32) * t + mb_ref[...].astype(F32)).astype(BF16)
    h1 = x_ref[...] + jnp.dot(merged, wo_ref[...], preferred_element_type=F32)

    hn = _rms(h1, xg_ref[...]).astype(BF16)
    q = (jnp.dot(hn, wq_ref[...], preferred_element_type=F32) * (XATTN_HEAD_DIM ** -0.5)).astype(BF16)
    hd = XATTN_HEAD_DIM
    for h in range(XATTN_HEADS):
        s = lax.dot_general(q[:, h * hd:(h + 1) * hd], k_ref[:, h * hd:(h + 1) * hd],
                            (((1,), (1,)), ((), ())), preferred_element_type=F32)
        m = jnp.max(s, axis=-1, keepdims=True)
        p = jnp.exp(s - m)
        den = jnp.sum(p, axis=-1, keepdims=True)
        oh = jnp.dot(p.astype(BF16), v_ref[:, h * hd:(h + 1) * hd], preferred_element_type=F32) / den
        o_scr[:, h * hd:(h + 1) * hd] = oh.astype(BF16)
    h2 = h1 + jnp.dot(o_scr[...], wox_ref[...], preferred_element_type=F32)
    h_ref[...] = h2

    hn2 = _rms(h2, mg_ref[...])
    hn_ref[...] = hn2

    logits = jnp.dot(hn2.astype(BF16), wr_ref[...], preferred_element_type=F32) + br_ref[...]
    li = lax.broadcasted_iota(I32, (tm, LANES), 1)
    lif = li.astype(F32)
    grp_of_lane = lax.shift_right_logical(li, 3).astype(F32)
    is_grp = (li >= N_EXPERTS) & (li < N_EXPERTS + N_EXPERT_GROUPS)
    gl = jnp.where(is_grp, logits, -jnp.inf)
    gmax = jnp.max(gl, axis=-1, keepdims=True)
    grp = jnp.min(jnp.where(gl == gmax, lif - N_EXPERTS, float(LANES)), axis=-1, keepdims=True)
    gsum = jnp.sum(jnp.where(is_grp, jnp.exp(logits - gmax), 0.0), axis=-1, keepdims=True)
    grp_gate = 1.0 / gsum
    in_grp = grp_of_lane == grp
    el = jnp.where(in_grp, logits, -jnp.inf)
    v1 = jnp.max(el, axis=-1, keepdims=True)
    i1 = jnp.min(jnp.where(el == v1, lif, float(LANES)), axis=-1, keepdims=True)
    el2 = jnp.where(lif == i1, -jnp.inf, el)
    v2 = jnp.max(el2, axis=-1, keepdims=True)
    i2 = jnp.min(jnp.where(el2 == v2, lif, float(LANES)), axis=-1, keepdims=True)
    tt = jnp.exp(v2 - v1)
    w1 = grp_gate / (1.0 + tt)
    w2 = grp_gate * tt / (1.0 + tt)
    l8 = lax.broadcasted_iota(I32, (tm, 8), 1)
    route = jnp.where(l8 == 0, i1,
                      jnp.where(l8 == 1, i2,
                                jnp.where(l8 == 2, w1, jnp.where(l8 == 3, w2, 0.0))))
    route_ref[...] = route


def _post(ya, ga, mb, x2d, kv, w_out_a, w_out, xg, w_q, w_o, mg, w_r, b_r):
    tm = TM_POST
    nt = SEQ // tm
    tok = lambda w: pl.BlockSpec((tm, w), lambda i: (i, 0))
    return pl.pallas_call(
        _post_kernel,
        grid=(TOKENS // tm,),
        in_specs=[
            tok(GROUP_WIDTH), tok(D_MODEL), tok(D_MODEL), tok(D_MODEL),
            pl.BlockSpec((N_MEM, D_MODEL), lambda i: (i // nt, 0)),
            pl.BlockSpec((N_MEM, D_MODEL), lambda i: (i // nt, 1)),
            _resident((GROUP_WIDTH, D_MODEL)),
            _resident((D_MODEL, D_MODEL)),
            _resident((1, D_MODEL)),
            _resident((D_MODEL, D_MODEL)),
            _resident((D_MODEL, D_MODEL)),
            _resident((1, D_MODEL)),
            _resident((D_MODEL, LANES)),
            _resident((1, LANES)),
        ],
        out_specs=[tok(D_MODEL), tok(D_MODEL), pl.BlockSpec((tm, 8), lambda i: (i, 0))],
        out_shape=[
            jax.ShapeDtypeStruct((TOKENS, D_MODEL), F32),
            jax.ShapeDtypeStruct((TOKENS, D_MODEL), F32),
            jax.ShapeDtypeStruct((TOKENS, 8), F32),
        ],
        scratch_shapes=[pltpu.VMEM((tm, D_MODEL), BF16)],
        compiler_params=pltpu.CompilerParams(
            dimension_semantics=("parallel",), vmem_limit_bytes=VMEM_LIMIT),
        name="post",
    )(ya, ga, mb, x2d, kv, kv, w_out_a, w_out, xg, w_q, w_o, mg, w_r, b_r)


def _route_kernel(rt_ref, dest_ref, meta_ref, rank_scr, carry_ref, pstart_ref):
    pss = pl.program_id(0)
    i = pl.program_id(1)
    tl = TL_ROUTE
    ch = SCAN_CHUNK
    ei = lax.broadcasted_iota(I32, (N_EXPERTS, ch), 0).astype(F32)

    @pl.when((pss == 0) & (i == 0))
    def _():
        carry_ref[...] = jnp.zeros_like(carry_ref)

    @pl.when(pss == 0)
    def _():
        ur = lax.broadcasted_iota(I32, (ch, ch), 0)
        uc = lax.broadcasted_iota(I32, (ch, ch), 1)
        upper = jnp.where(ur < uc, 1.0, 0.0).astype(BF16)
        for c in range(tl // ch):
            e1 = rt_ref[0:1, c * ch:(c + 1) * ch]
            e2 = rt_ref[1:2, c * ch:(c + 1) * ch]
            oh1 = e1 == ei
            oh2 = e2 == ei
            oh = jnp.where(oh1 | oh2, 1.0, 0.0)
            cnt = jnp.dot(oh.astype(BF16), upper, preferred_element_type=F32) + carry_ref[:, 0:1]
            rank1 = jnp.sum(jnp.where(oh1, cnt, 0.0), axis=0, keepdims=True)
            rank2 = jnp.sum(jnp.where(oh2, cnt, 0.0), axis=0, keepdims=True)
            col = pl.multiple_of(i * tl + c * ch, ch)
            rank_scr[0:1, pl.ds(col, ch)] = rank1
            rank_scr[1:2, pl.ds(col, ch)] = rank2
            carry_ref[...] = carry_ref[...] + jnp.sum(oh, axis=1, keepdims=True)

    @pl.when((pss == 1) & (i == 0))
    def _():
        counts = carry_ref[...].astype(I32)
        padded = lax.shift_left(lax.shift_right_logical(counts + (ROW_BLOCK - 1),
                                                        int(math.log2(ROW_BLOCK))),
                                int(math.log2(ROW_BLOCK)))
        row = lax.broadcasted_iota(I32, (N_EXPERTS, LANES), 0)
        lane = lax.broadcasted_iota(I32, (N_EXPERTS, LANES), 1)
        pend = padded
        sh = 1
        while sh < N_EXPERTS:
            pend = pend + jnp.where(row >= sh, pltpu.roll(pend, sh, 0), 0)
            sh *= 2
        pstart = pend - padded
        pstart_ref[...] = pstart
        diag = row == lane

        def as_row(x):
            return jnp.sum(jnp.where(diag, x, 0), axis=0, keepdims=True)

        blk_lane = lax.broadcasted_iota(I32, (N_EXPERTS, META_LANES), 1) * ROW_BLOCK
        blk_exp = jnp.sum(jnp.where(pend[:, 0:1] <= blk_lane, 1, 0), axis=0, keepdims=True)
        blk_exp = jnp.minimum(blk_exp, N_EXPERTS - 1)
        meta_ref[...] = jnp.zeros_like(meta_ref)
        meta_ref[0:1, :] = blk_exp
        meta_ref[1:2, 0:LANES] = as_row(counts)
        meta_ref[2:3, 0:LANES] = as_row(pstart)
        meta_ref[3:4, 0:LANES] = as_row(pend)

    @pl.when(pss == 1)
    def _():
        ps = pstart_ref[:, 0:1].astype(F32)
        for c in range(tl // ch):
            e1 = rt_ref[0:1, c * ch:(c + 1) * ch]
            e2 = rt_ref[1:2, c * ch:(c + 1) * ch]
            col = pl.multiple_of(i * tl + c * ch, ch)
            d1 = rank_scr[0:1, pl.ds(col, ch)] + jnp.sum(jnp.where(e1 == ei, ps, 0.0), axis=0, keepdims=True)
            d2 = rank_scr[1:2, pl.ds(col, ch)] + jnp.sum(jnp.where(e2 == ei, ps, 0.0), axis=0, keepdims=True)
            dest_ref[0:1, c * ch:(c + 1) * ch] = d1.astype(I32)
            dest_ref[1:2, c * ch:(c + 1) * ch] = d2.astype(I32)


def _route(rt):
    tl = TL_ROUTE
    return pl.pallas_call(
        _route_kernel,
        grid=(2, TOKENS // tl),
        in_specs=[pl.BlockSpec((8, tl), lambda p, i: (0, i))],
        out_specs=[
            pl.BlockSpec((2, tl), lambda p, i: (0, i * p)),
            pl.BlockSpec((8, META_LANES), lambda p, i: (0, 0)),
        ],
        out_shape=[
            jax.ShapeDtypeStruct((2, TOKENS), I32),
            jax.ShapeDtypeStruct((8, META_LANES), I32),
        ],
        scratch_shapes=[
            pltpu.VMEM((2, TOKENS), F32),
            pltpu.VMEM((N_EXPERTS, LANES), F32),
            pltpu.VMEM((N_EXPERTS, LANES), I32),
        ],
        compiler_params=pltpu.CompilerParams(
            dimension_semantics=("arbitrary", "arbitrary"), vmem_limit_bytes=VMEM_LIMIT),
        name="route",
    )(rt)


def _dispatch_kernel(dest_ref, meta_ref, hn_ref, xs_ref, zrow_ref, sem, zsem):
    i = pl.program_id(0)
    tm = TM_DISPATCH

    def zero_copy(row):
        return pltpu.make_async_copy(zrow_ref.at[pl.ds(0, 1)], xs_ref.at[pl.ds(row, 1)], zsem)

    def zero_block_copy(b):
        row = pl.multiple_of(b * ROW_BLOCK, ROW_BLOCK)
        return pltpu.make_async_copy(zrow_ref, xs_ref.at[pl.ds(row, ROW_BLOCK)], zsem)

    @pl.when(i == 0)
    def _():
        zrow_ref[...] = jnp.zeros_like(zrow_ref)
        first_free = lax.shift_right_logical(meta_ref[3, N_EXPERTS - 1], int(math.log2(ROW_BLOCK)))

        def start_blk(b, c):
            zero_block_copy(b).start()
            return c

        def wait_blk(b, c):
            zero_block_copy(b).wait()
            return c

        lax.fori_loop(first_free, N_ROW_BLOCKS, start_blk, 0)
        lax.fori_loop(first_free, N_ROW_BLOCKS, wait_blk, 0)
        for e in range(N_EXPERTS):
            lo = meta_ref[2, e] + meta_ref[1, e]
            hi = meta_ref[3, e]

            def start(r, c):
                zero_copy(r).start()
                return c

            def wait(r, c):
                zero_copy(r).wait()
                return c

            lax.fori_loop(lo, hi, start, 0)
            lax.fori_loop(lo, hi, wait, 0)

    def row_copy(t, k):
        return pltpu.make_async_copy(hn_ref.at[pl.ds(t, 1)], xs_ref.at[pl.ds(dest_ref[k, t], 1)], sem)

    def start(t, c):
        row_copy(t, 0).start()
        row_copy(t, 1).start()
        return c

    def wait(t, c):
        row_copy(t, 0).wait()
        row_copy(t, 1).wait()
        return c

    lax.fori_loop(0, tm, start, 0, unroll=8)
    lax.fori_loop(0, tm, wait, 0, unroll=8)


def _dispatch(dest, meta, hn2):
    tm = TM_DISPATCH
    return pl.pallas_call(
        _dispatch_kernel,
        grid=(TOKENS // tm,),
        in_specs=[
            pl.BlockSpec((2, tm), lambda i: (0, i), memory_space=pltpu.SMEM),
            pl.BlockSpec(memory_space=pltpu.SMEM),
            pl.BlockSpec((tm, D_MODEL), lambda i: (i, 0)),
        ],
        out_specs=pl.BlockSpec(memory_space=pl.ANY),
        out_shape=jax.ShapeDtypeStruct((PADDED_ROWS, D_MODEL), F32),
        scratch_shapes=[
            pltpu.VMEM((ROW_BLOCK, D_MODEL), F32),
            pltpu.SemaphoreType.DMA(()),
            pltpu.SemaphoreType.DMA(()),
        ],
        compiler_params=pltpu.CompilerParams(
            dimension_semantics=("arbitrary",), vmem_limit_bytes=VMEM_LIMIT),
        name="dispatch",
    )(dest, meta, hn2)


def _experts_kernel(be_ref, nv_ref, xs_ref, wg_ref, wu_ref, wd_ref, yb_ref):
    j = pl.program_id(0)

    @pl.when(j < nv_ref[0])
    def _():
        xb = xs_ref[...].astype(BF16)
        a = jnp.dot(xb, wg_ref[0], preferred_element_type=F32)
        b = jnp.dot(xb, wu_ref[0], preferred_element_type=F32)
        hb = (jax.nn.silu(a) * b).astype(BF16)
        yb_ref[...] = jnp.dot(hb, wd_ref[0], preferred_element_type=F32)

    @pl.when(j >= nv_ref[0])
    def _():
        yb_ref[...] = jnp.zeros_like(yb_ref)


def _experts(blk_expert, n_valid, xs, w_gate, w_up, w_down):
    def row_map(j, be, nv):
        return (jnp.minimum(j, nv[0] - 1), 0)

    def out_map(j, be, nv):
        return (j, 0)

    def w_map(j, be, nv):
        return (be[jnp.minimum(j, nv[0] - 1)], 0, 0)

    grid_spec = pltpu.PrefetchScalarGridSpec(
        num_scalar_prefetch=2,
        grid=(N_ROW_BLOCKS,),
        in_specs=[
            pl.BlockSpec((ROW_BLOCK, D_MODEL), row_map),
            pl.BlockSpec((1, D_MODEL, EXPERT_FF), w_map),
            pl.BlockSpec((1, D_MODEL, EXPERT_FF), w_map),
            pl.BlockSpec((1, EXPERT_FF, D_MODEL), w_map),
        ],
        out_specs=pl.BlockSpec((ROW_BLOCK, D_MODEL), out_map),
    )
    return pl.pallas_call(
        _experts_kernel,
        grid_spec=grid_spec,
        out_shape=jax.ShapeDtypeStruct((PADDED_ROWS, D_MODEL), F32),
        compiler_params=pltpu.CompilerParams(
            dimension_semantics=("arbitrary",), vmem_limit_bytes=VMEM_LIMIT),
        name="experts",
    )(blk_expert, n_valid, xs, w_gate, w_up, w_down)


def _combine_kernel(dest_ref, route_ref, h_ref, g_ref, yb_ref, out_ref, buf_ref, sem):
    tm = TM_COMBINE

    def row_copy(t, k):
        return pltpu.make_async_copy(yb_ref.at[pl.ds(dest_ref[k, t], 1)], buf_ref.at[k, pl.ds(t, 1)], sem)

    def start(t, c):
        row_copy(t, 0).start()
        row_copy(t, 1).start()
        return c

    def wait(t, c):
        row_copy(t, 0).wait()
        row_copy(t, 1).wait()
        return c

    lax.fori_loop(0, tm, start, 0, unroll=8)
    lax.fori_loop(0, tm, wait, 0, unroll=8)
    w1 = route_ref[:, 2:3]
    w2 = route_ref[:, 3:4]
    h3 = h_ref[...] + (buf_ref[0] * w1 + buf_ref[1] * w2)
    out_ref[...] = _rms(h3, g_ref[...])


def _combine(dest, route, h2, g, yb):
    tm = TM_COMBINE
    return pl.pallas_call(
        _combine_kernel,
        grid=(TOKENS // tm,),
        in_specs=[
            pl.BlockSpec((2, tm), lambda i: (0, i), memory_space=pltpu.SMEM),
            pl.BlockSpec((tm, 8), lambda i: (i, 0)),
            pl.BlockSpec((tm, D_MODEL), lambda i: (i, 0)),
            _resident((1, D_MODEL)),
            pl.BlockSpec(memory_space=pl.ANY),
        ],
        out_specs=pl.BlockSpec((tm, D_MODEL), lambda i: (i, 0)),
        out_shape=jax.ShapeDtypeStruct((TOKENS, D_MODEL), F32),
        scratch_shapes=[pltpu.VMEM((2, tm, D_MODEL), F32), pltpu.SemaphoreType.DMA(())],
        compiler_params=pltpu.CompilerParams(
            dimension_semantics=("arbitrary",), vmem_limit_bytes=VMEM_LIMIT),
        name="combine",
    )(dest, route, h2, g, yb)


def kernel(x, mem, positions, mix_norm_g, w_in, b_gates, w_spatial, b_spatial, v_norm_g, v_norm_b,
           w_out_a, w_out_b, w_out, xattn_norm_g, mem_norm_g, w_q_x, w_kv_x, w_o_x, moe_norm_g,
           w_router_grp, b_router_grp, w_router_exp, b_router_exp, w_gate_e, w_up_e, w_down_e,
           final_norm_g):
    assert x.shape == (BATCH, SEQ, D_MODEL) and mem.shape == (BATCH, N_MEM, D_MODEL)
    assert mix_norm_g.shape[0] == 1, "single layer"
    x2d = x.reshape(TOKENS, D_MODEL)
    pos_col = positions.reshape(TOKENS, 1).astype(F32)
    half = HEAD_DIM // 2
    inv_freq = ROPE_THETA ** (-jnp.arange(half, dtype=F32) / half)
    invf = jnp.tile(inv_freq, LANES // half).reshape(1, LANES)
    phase = jnp.tile(jnp.concatenate([jnp.zeros((half,), F32), jnp.full((half,), math.pi / 2, F32)]),
                     LANES // HEAD_DIM).reshape(1, LANES)

    kv = _memkv(mem.reshape(BATCH * N_MEM, D_MODEL), mem_norm_g[0].reshape(1, D_MODEL),
                w_kv_x[0].astype(BF16))

    qkv0, qkv1, qkv2, ga, mb = _inproj(
        x2d, pos_col, invf, phase, mix_norm_g[0].reshape(1, D_MODEL), w_in[0].astype(BF16),
        b_gates[0].reshape(1, 2 * D_MODEL), w_spatial[0], b_spatial[0].T,
        v_norm_g[0].reshape(1, GMLP_WIDTH), v_norm_b[0].reshape(1, GMLP_WIDTH),
        w_out_b[0].astype(BF16))

    ya = _attention((qkv0, qkv1, qkv2)).reshape(TOKENS, GROUP_WIDTH)

    pad = LANES - N_EXPERTS - N_EXPERT_GROUPS
    w_r = jnp.concatenate([w_router_exp[0], w_router_grp[0], jnp.zeros((D_MODEL, pad), F32)], axis=1)
    b_r = jnp.concatenate([b_router_exp[0], b_router_grp[0], jnp.zeros((pad,), F32)]).reshape(1, LANES)
    h2, hn2, route = _post(
        ya, ga, mb, x2d, kv, w_out_a[0].astype(BF16), w_out[0].astype(BF16),
        xattn_norm_g[0].reshape(1, D_MODEL), w_q_x[0].astype(BF16), w_o_x[0].astype(BF16),
        moe_norm_g[0].reshape(1, D_MODEL), w_r.astype(BF16), b_r)

    dest, meta = _route(route.T)
    xs = _dispatch(dest, meta, hn2)
    blk_expert = meta[0, :N_ROW_BLOCKS]
    n_valid = (meta[3, N_EXPERTS - 1:N_EXPERTS] // ROW_BLOCK).astype(I32)
    yb = _experts(blk_expert, n_valid, xs, w_gate_e[0].astype(BF16), w_up_e[0].astype(BF16),
                  w_down_e[0].astype(BF16))
    out = _combine(dest, route, h2, final_norm_g.reshape(1, D_MODEL), yb)
    return out.reshape(BATCH, SEQ, D_MODEL)
```

```python
import functools
import math

import jax
import jax.numpy as jnp
from jax import lax
from jax.experimental import pallas as pl
from jax.experimental.pallas import tpu as pltpu
from jax.experimental.pallas import tpu_sc as plsc

F32 = jnp.float32
BF16 = jnp.bfloat16
I32 = jnp.int32

D_MODEL = 1024
BATCH = 16
SEQ = 4096
TOKENS = BATCH * SEQ

HEAD_DIM = 64
DILATIONS = (1, 4, 16)
HEADS_PER_GROUP = 4
GROUP_WIDTH = HEADS_PER_GROUP * HEAD_DIM
ATT_WIDTH = len(DILATIONS) * GROUP_WIDTH
BAND_BLOCK = 128
ROPE_THETA = 10000.0

GMLP_CHUNK = 128
GMLP_GROUPS = 4
GMLP_WIDTH = 512

N_MEM = 256
XATTN_HEADS = 4
XATTN_HEAD_DIM = D_MODEL // XATTN_HEADS

N_EXPERT_GROUPS = 4
EXPERTS_PER_GROUP = 8
N_EXPERTS = 32
TOP_K = 2
EXPERT_FF = 512

RMS_EPS = 1e-6
LN_EPS = 1e-5
NEG_INF = -1e30

LANES = 128

COL_U = 3 * ATT_WIDTH
COL_V = COL_U + GMLP_WIDTH
COL_GA = COL_V + GMLP_WIDTH
COL_GB = COL_GA + D_MODEL

ROW_BLOCK = 512
ASSIGN = TOKENS * TOP_K
PADDED_ROWS = ASSIGN + N_EXPERTS * ROW_BLOCK
N_ROW_BLOCKS = PADDED_ROWS // ROW_BLOCK
META_LANES = ((N_ROW_BLOCKS + LANES - 1) // LANES) * LANES

TM_PROJ = 512
TM_POST = 512
TL_ROUTE = 2048
SCAN_CHUNK = 256
TM_COMBINE = 512
ATTN_UNROLL = 4

VMEM_LIMIT = 56 * 1024 * 1024


def _rms(x, g):
    return x * lax.rsqrt(jnp.mean(x * x, axis=-1, keepdims=True) + RMS_EPS) * g


def _resident(shape):
    nd = len(shape)
    return pl.BlockSpec(shape, lambda *_: (0,) * nd, pipeline_mode=pl.Buffered(1))


def _memkv_kernel(mem_ref, g_ref, w_ref, kv_ref):
    mn = _rms(mem_ref[...], g_ref[...]).astype(BF16)
    kv_ref[...] = jnp.dot(mn, w_ref[...], preferred_element_type=F32).astype(BF16)


def _memkv(mem2d, g, w_kv):
    rows = mem2d.shape[0]
    tm = 512
    return pl.pallas_call(
        _memkv_kernel,
        grid=(rows // tm,),
        in_specs=[
            pl.BlockSpec((tm, D_MODEL), lambda i: (i, 0)),
            _resident((1, D_MODEL)),
            _resident((D_MODEL, 2 * D_MODEL)),
        ],
        out_specs=pl.BlockSpec((tm, 2 * D_MODEL), lambda i: (i, 0)),
        out_shape=jax.ShapeDtypeStruct((rows, 2 * D_MODEL), BF16),
        compiler_params=pltpu.CompilerParams(
            dimension_semantics=("parallel",), vmem_limit_bytes=VMEM_LIMIT),
        name="memkv",
    )(mem2d, g, w_kv)


def _inproj_kernel(x_ref, pos_ref, invf_ref, phase_ref, g_ref, w_ref, bg_ref, wsp_ref,
                   bsp_ref, lng_ref, lnb_ref, wob_ref,
                   qkv0_ref, qkv1_ref, qkv2_ref, ga_ref, mb_ref, scr_ref, yb_ref):
    tm = TM_PROJ
    xn = _rms(x_ref[...], g_ref[...]).astype(BF16)

    lane = lax.broadcasted_iota(I32, (tm, LANES), 1)
    upper = (lane & 32) != 0
    t1 = jnp.sin(pos_ref[...] * invf_ref[...] + phase_ref[...])
    cosf = jnp.where(upper, t1, pltpu.roll(t1, 96, 1))
    sinf = jnp.where(upper, pltpu.roll(t1, 32, 1), -t1)

    def rope(res):
        outs = []
        for c in range(GROUP_WIDTH // LANES):
            xt = res[:, c * LANES:(c + 1) * LANES]
            rot = jnp.where(upper, pltpu.roll(xt, 32, 1), pltpu.roll(xt, 96, 1))
            outs.append(xt * cosf + rot * sinf)
        return jnp.concatenate(outs, axis=1)

    out_refs = (qkv0_ref, qkv1_ref, qkv2_ref)
    for gi, dil in enumerate(DILATIONS):
        for which in range(3):
            c0 = which * ATT_WIDTH + gi * GROUP_WIDTH
            res = jnp.dot(xn, w_ref[:, c0:c0 + GROUP_WIDTH], preferred_element_type=F32)
            if which < 2:
                res = rope(res)
            if which == 0:
                res = res * (HEAD_DIM ** -0.5)
            if dil == 1:
                out_refs[gi][0, which, 0] = res.astype(BF16)
            else:
                rows = tm // dil
                for c in range(GROUP_WIDTH // LANES):
                    slot = ((gi - 1) * 3 + which) * (GROUP_WIDTH // LANES) + c
                    scr_ref[slot] = res[:, c * LANES:(c + 1) * LANES]
                    for r in range(dil):
                        out_refs[gi][0, which, r, :, c * LANES:(c + 1) * LANES] = (
                            scr_ref[slot, pl.ds(r, rows, stride=dil), :].astype(BF16))

    zu = jax.nn.gelu(jnp.dot(xn, w_ref[:, COL_U:COL_V], preferred_element_type=F32))
    zv = jax.nn.gelu(jnp.dot(xn, w_ref[:, COL_V:COL_GA], preferred_element_type=F32))
    mu = jnp.mean(zv, axis=-1, keepdims=True)
    zc = zv - mu
    var = jnp.mean(zc * zc, axis=-1, keepdims=True)
    vn = (zc * lax.rsqrt(var + LN_EPS) * lng_ref[...] + lnb_ref[...]).astype(BF16)
    tri_r = lax.broadcasted_iota(I32, (GMLP_CHUNK, GMLP_CHUNK), 0)
    tri_c = lax.broadcasted_iota(I32, (GMLP_CHUNK, GMLP_CHUNK), 1)
    causal = tri_r >= tri_c
    n_chunks = tm // GMLP_CHUNK
    gw = GMLP_WIDTH // GMLP_GROUPS
    for g in range(GMLP_GROUPS):
        wsg = jnp.where(causal, wsp_ref[g], 0.0).astype(BF16)
        vcat = jnp.concatenate(
            [vn[c * GMLP_CHUNK:(c + 1) * GMLP_CHUNK, g * gw:(g + 1) * gw] for c in range(n_chunks)],
            axis=1)
        mixed = jnp.dot(wsg, vcat, preferred_element_type=F32) + bsp_ref[:, g:g + 1]
        for c in range(n_chunks):
            u_blk = zu[c * GMLP_CHUNK:(c + 1) * GMLP_CHUNK, g * gw:(g + 1) * gw]
            yb_ref[c * GMLP_CHUNK:(c + 1) * GMLP_CHUNK, g * gw:(g + 1) * gw] = (
                u_blk * mixed[:, c * gw:(c + 1) * gw]).astype(BF16)

    gate_a = jax.nn.sigmoid(
        jnp.dot(xn, w_ref[:, COL_GA:COL_GB], preferred_element_type=F32) + bg_ref[:, :D_MODEL])
    ga_ref[...] = gate_a.astype(BF16)
    gate_b = jax.nn.sigmoid(
        jnp.dot(xn, w_ref[:, COL_GB:COL_GB + D_MODEL], preferred_element_type=F32) + bg_ref[:, D_MODEL:])
    mb_ref[...] = (gate_b * jnp.dot(yb_ref[...], wob_ref[...], preferred_element_type=F32)).astype(BF16)


def _inproj(x2d, pos_col, invf, phase, g, w_in, b_gates, w_spatial, b_spatial_t, ln_g, ln_b, w_out_b):
    tm = TM_PROJ
    nt = SEQ // tm
    in_cols = w_in.shape[1]
    qkv_shapes = [jax.ShapeDtypeStruct((BATCH, 3, d, SEQ // d, GROUP_WIDTH), BF16) for d in DILATIONS]
    qkv_specs = [
        pl.BlockSpec((1, 3, d, tm // d, GROUP_WIDTH), lambda i: (i // nt, 0, 0, i % nt, 0))
        for d in DILATIONS
    ]
    tok_spec = pl.BlockSpec((tm, D_MODEL), lambda i: (i, 0))
    return pl.pallas_call(
        _inproj_kernel,
        grid=(TOKENS // tm,),
        in_specs=[
            tok_spec,
            pl.BlockSpec((tm, 1), lambda i: (i, 0)),
            _resident((1, LANES)),
            _resident((1, LANES)),
            _resident((1, D_MODEL)),
            _resident((D_MODEL, in_cols)),
            _resident((1, 2 * D_MODEL)),
            _resident((GMLP_GROUPS, GMLP_CHUNK, GMLP_CHUNK)),
            _resident((GMLP_CHUNK, GMLP_GROUPS)),
            _resident((1, GMLP_WIDTH)),
            _resident((1, GMLP_WIDTH)),
            _resident((GMLP_WIDTH, D_MODEL)),
        ],
        out_specs=qkv_specs + [tok_spec, tok_spec],
        out_shape=qkv_shapes + [jax.ShapeDtypeStruct((TOKENS, D_MODEL), BF16)] * 2,
        scratch_shapes=[
            pltpu.VMEM((6 * (GROUP_WIDTH // LANES), tm, LANES), F32),
            pltpu.VMEM((tm, GMLP_WIDTH), BF16),
        ],
        compiler_params=pltpu.CompilerParams(
            dimension_semantics=("parallel",), vmem_limit_bytes=VMEM_LIMIT),
        name="inproj",
    )(x2d, pos_col, invf, phase, g, w_in, b_gates, w_spatial, b_spatial_t, ln_g, ln_b, w_out_b)


def _attn_kernel(qkv0_ref, qkv1_ref, qkv2_ref, y_ref, acc_ref, m_ref, z_ref, bias_ref):
    blk = BAND_BLOCK
    lane_row = lax.broadcasted_iota(I32, (1, LANES), 1)
    head0_b = jnp.where(lane_row < HEAD_DIM, 1.0, 0.0).astype(BF16)
    head1_b = jnp.where(lane_row < HEAD_DIM, 0.0, 1.0).astype(BF16)
    head0 = lax.broadcasted_iota(I32, (blk, LANES), 1) < HEAD_DIM
    ones_b = jnp.ones((2 * blk, LANES), BF16)

    qi = lax.broadcasted_iota(I32, (2 * blk, 2 * blk), 0) & (blk - 1)
    kc = lax.broadcasted_iota(I32, (2 * blk, 2 * blk), 1)
    for slot, off in enumerate((0, blk)):
        dist = qi + off - kc
        bias_ref[slot] = jnp.where((dist >= 0) & (dist <= blk), 0.0, NEG_INF)

    for gi, (ref, dil) in enumerate(zip((qkv0_ref, qkv1_ref, qkv2_ref), DILATIONS)):
        seq_len = SEQ // dil
        nb = seq_len // blk
        nb_shift = nb.bit_length() - 1

        def body(i, carry, ref=ref, dil=dil, nb=nb, nb_shift=nb_shift, gi=gi):
            r = lax.shift_right_logical(i, nb_shift)
            n = i & (nb - 1)
            q0 = pl.multiple_of(n * blk, blk)
            w0 = pl.multiple_of(jnp.maximum(n - 1, 0) * blk, blk)
            q = ref[0, 0, r, pl.ds(q0, blk), :]
            k = ref[0, 1, r, pl.ds(w0, 2 * blk), :]
            v = ref[0, 2, r, pl.ds(w0, 2 * blk), :]
            q2 = jnp.concatenate([q * head0_b, q * head1_b], axis=0)
            s = lax.dot_general(q2, k, (((1,), (1,)), ((), ())), preferred_element_type=F32)
            s = s + bias_ref[jnp.minimum(n, 1)]
            m2 = jnp.max(s, axis=-1, keepdims=True)
            p = jnp.exp(s - m2)
            v_ext = jnp.concatenate([v, ones_b], axis=1)
            o2 = jnp.dot(p.astype(BF16), v_ext, preferred_element_type=F32)
            o = jnp.where(head0, o2[:blk, :LANES], o2[blk:, :LANES])
            den = jnp.where(head0, o2[:blk, LANES:], o2[blk:, LANES:])
            m = jnp.where(head0, m2[:blk], m2[blk:])
            if gi == 0:
                acc_ref[pl.ds(q0, blk), :] = o
                m_ref[pl.ds(q0, blk), :] = m
                z_ref[pl.ds(q0, blk), :] = den
            else:
                idx = pl.ds(n * (blk * dil) + r, blk, stride=dil)
                m_old = m_ref[idx, :]
                m_new = jnp.maximum(m_old, m)
                e_old = jnp.exp(m_old - m_new)
                e_new = jnp.exp(m - m_new)
                acc_ref[idx, :] = acc_ref[idx, :] * e_old + o * e_new
                z_ref[idx, :] = z_ref[idx, :] * e_old + den * e_new
                m_ref[idx, :] = m_new
            return carry

        lax.fori_loop(0, dil * nb, body, 0, unroll=ATTN_UNROLL)

    y_ref[0] = (acc_ref[...] / z_ref[...]).astype(BF16)


def _attention(qkv):
    in_specs = [
        pl.BlockSpec((1, 3, d, SEQ // d, LANES), lambda b, h: (b, 0, 0, 0, h)) for d in DILATIONS
    ]
    return pl.pallas_call(
        _attn_kernel,
        grid=(BATCH, GROUP_WIDTH // LANES),
        in_specs=in_specs,
        out_specs=pl.BlockSpec((1, SEQ, LANES), lambda b, h: (b, 0, h)),
        out_shape=jax.ShapeDtypeStruct((BATCH, SEQ, GROUP_WIDTH), BF16),
        scratch_shapes=[
            pltpu.VMEM((SEQ, LANES), F32), pltpu.VMEM((SEQ, LANES), F32), pltpu.VMEM((SEQ, LANES), F32),
            pltpu.VMEM((2, 2 * BAND_BLOCK, 2 * BAND_BLOCK), F32),
        ],
        compiler_params=pltpu.CompilerParams(
            dimension_semantics=("parallel", "parallel"), vmem_limit_bytes=VMEM_LIMIT),
        name="attn",
    )(*qkv)


def _post_kernel(ya_ref, ga_ref, mb_ref, x_ref, k_ref, v_ref, woa_ref, wo_ref, xg_ref, wq_ref,
                 wox_ref, mg_ref, wr_ref, br_ref, h_ref, hn_ref, route_ref, o_scr):
    tm = TM_POST
    t = jnp.dot(ya_ref[...], woa_ref[...], preferred_element_type=F32)
    merged = (ga_ref[...].astype(F32) * t + mb_ref[...].astype(F32)).astype(BF16)
    h1 = x_ref[...] + jnp.dot(merged, wo_ref[...], preferred_element_type=F32)

    hn = _rms(h1, xg_ref[...]).astype(BF16)
    q = (jnp.dot(hn, wq_ref[...], preferred_element_type=F32) * (XATTN_HEAD_DIM ** -0.5)).astype(BF16)
    hd = XATTN_HEAD_DIM
    for h in range(XATTN_HEADS):
        s = lax.dot_general(q[:, h * hd:(h + 1) * hd], k_ref[:, h * hd:(h + 1) * hd],
                            (((1,), (1,)), ((), ())), preferred_element_type=F32)
        m = jnp.max(s, axis=-1, keepdims=True)
        p = jnp.exp(s - m)
        den = jnp.sum(p, axis=-1, keepdims=True)
        oh = jnp.dot(p.astype(BF16), v_ref[:, h * hd:(h + 1) * hd], preferred_element_type=F32) / den
        o_scr[:, h * hd:(h + 1) * hd] = oh.astype(BF16)
    h2 = h1 + jnp.dot(o_scr[...], wox_ref[...], preferred_element_type=F32)
    h_ref[...] = h2

    hn2 = _rms(h2, mg_ref[...])
    hn_ref[...] = hn2

    logits = jnp.dot(hn2.astype(BF16), wr_ref[...], preferred_element_type=F32) + br_ref[...]
    li = lax.broadcasted_iota(I32, (tm, LANES), 1)
    lif = li.astype(F32)
    grp_of_lane = lax.shift_right_logical(li, 3).astype(F32)
    is_grp = (li >= N_EXPERTS) & (li < N_EXPERTS + N_EXPERT_GROUPS)
    gl = jnp.where(is_grp, logits, -jnp.inf)
    gmax = jnp.max(gl, axis=-1, keepdims=True)
    grp = jnp.min(jnp.where(gl == gmax, lif - N_EXPERTS, float(LANES)), axis=-1, keepdims=True)
    gsum = jnp.sum(jnp.where(is_grp, jnp.exp(logits - gmax), 0.0), axis=-1, keepdims=True)
    grp_gate = 1.0 / gsum
    in_grp = grp_of_lane == grp
    el = jnp.where(in_grp, logits, -jnp.inf)
    v1 = jnp.max(el, axis=-1, keepdims=True)
    i1 = jnp.min(jnp.where(el == v1, lif, float(LANES)), axis=-1, keepdims=True)
    el2 = jnp.where(lif == i1, -jnp.inf, el)
    v2 = jnp.max(el2, axis=-1, keepdims=True)
    i2 = jnp.min(jnp.where(el2 == v2, lif, float(LANES)), axis=-1, keepdims=True)
    tt = jnp.exp(v2 - v1)
    w1 = grp_gate / (1.0 + tt)
    w2 = grp_gate * tt / (1.0 + tt)
    l8 = lax.broadcasted_iota(I32, (tm, 8), 1)
    route = jnp.where(l8 == 0, i1,
                      jnp.where(l8 == 1, i2,
                                jnp.where(l8 == 2, w1, jnp.where(l8 == 3, w2, 0.0))))
    route_ref[...] = route


def _post(ya, ga, mb, x2d, kv, w_out_a, w_out, xg, w_q, w_o, mg, w_r, b_r):
    tm = TM_POST
    nt = SEQ // tm
    tok = lambda w: pl.BlockSpec((tm, w), lambda i: (i, 0))
    return pl.pallas_call(
        _post_kernel,
        grid=(TOKENS // tm,),
        in_specs=[
            tok(GROUP_WIDTH), tok(D_MODEL), tok(D_MODEL), tok(D_MODEL),
            pl.BlockSpec((N_MEM, D_MODEL), lambda i: (i // nt, 0)),
            pl.BlockSpec((N_MEM, D_MODEL), lambda i: (i // nt, 1)),
            _resident((GROUP_WIDTH, D_MODEL)),
            _resident((D_MODEL, D_MODEL)),
            _resident((1, D_MODEL)),
            _resident((D_MODEL, D_MODEL)),
            _resident((D_MODEL, D_MODEL)),
            _resident((1, D_MODEL)),
            _resident((D_MODEL, LANES)),
            _resident((1, LANES)),
        ],
        out_specs=[tok(D_MODEL), tok(D_MODEL), pl.BlockSpec((tm, 8), lambda i: (i, 0))],
        out_shape=[
            jax.ShapeDtypeStruct((TOKENS, D_MODEL), F32),
            jax.ShapeDtypeStruct((TOKENS, D_MODEL), F32),
            jax.ShapeDtypeStruct((TOKENS, 8), F32),
        ],
        scratch_shapes=[pltpu.VMEM((tm, D_MODEL), BF16)],
        compiler_params=pltpu.CompilerParams(
            dimension_semantics=("parallel",), vmem_limit_bytes=VMEM_LIMIT),
        name="post",
    )(ya, ga, mb, x2d, kv, kv, w_out_a, w_out, xg, w_q, w_o, mg, w_r, b_r)


def _route_kernel(rt_ref, dest_ref, meta_ref, rank_scr, carry_ref, pstart_ref):
    pss = pl.program_id(0)
    i = pl.program_id(1)
    tl = TL_ROUTE
    ch = SCAN_CHUNK
    ei = lax.broadcasted_iota(I32, (N_EXPERTS, ch), 0).astype(F32)

    @pl.when((pss == 0) & (i == 0))
    def _():
        carry_ref[...] = jnp.zeros_like(carry_ref)

    @pl.when(pss == 0)
    def _():
        ur = lax.broadcasted_iota(I32, (ch, ch), 0)
        uc = lax.broadcasted_iota(I32, (ch, ch), 1)
        upper = jnp.where(ur < uc, 1.0, 0.0).astype(BF16)
        for c in range(tl // ch):
            e1 = rt_ref[0:1, c * ch:(c + 1) * ch]
            e2 = rt_ref[1:2, c * ch:(c + 1) * ch]
            oh1 = e1 == ei
            oh2 = e2 == ei
            oh = jnp.where(oh1 | oh2, 1.0, 0.0)
            cnt = jnp.dot(oh.astype(BF16), upper, preferred_element_type=F32) + carry_ref[:, 0:1]
            rank1 = jnp.sum(jnp.where(oh1, cnt, 0.0), axis=0, keepdims=True)
            rank2 = jnp.sum(jnp.where(oh2, cnt, 0.0), axis=0, keepdims=True)
            col = pl.multiple_of(i * tl + c * ch, ch)
            rank_scr[0:1, pl.ds(col, ch)] = rank1
            rank_scr[1:2, pl.ds(col, ch)] = rank2
            carry_ref[...] = carry_ref[...] + jnp.sum(oh, axis=1, keepdims=True)

    @pl.when((pss == 1) & (i == 0))
    def _():
        counts = carry_ref[...].astype(I32)
        padded = lax.shift_left(lax.shift_right_logical(counts + (ROW_BLOCK - 1),
                                                        int(math.log2(ROW_BLOCK))),
                                int(math.log2(ROW_BLOCK)))
        row = lax.broadcasted_iota(I32, (N_EXPERTS, LANES), 0)
        lane = lax.broadcasted_iota(I32, (N_EXPERTS, LANES), 1)
        pend = padded
        sh = 1
        while sh < N_EXPERTS:
            pend = pend + jnp.where(row >= sh, pltpu.roll(pend, sh, 0), 0)
            sh *= 2
        pstart = pend - padded
        pstart_ref[...] = pstart
        diag = row == lane

        def as_row(x):
            return jnp.sum(jnp.where(diag, x, 0), axis=0, keepdims=True)

        blk_lane = lax.broadcasted_iota(I32, (N_EXPERTS, META_LANES), 1) * ROW_BLOCK
        blk_exp = jnp.sum(jnp.where(pend[:, 0:1] <= blk_lane, 1, 0), axis=0, keepdims=True)
        blk_exp = jnp.minimum(blk_exp, N_EXPERTS - 1)
        meta_ref[...] = jnp.zeros_like(meta_ref)
        meta_ref[0:1, :] = blk_exp
        meta_ref[1:2, 0:LANES] = as_row(counts)
        meta_ref[2:3, 0:LANES] = as_row(pstart)
        meta_ref[3:4, 0:LANES] = as_row(pend)

    @pl.when(pss == 1)
    def _():
        ps = pstart_ref[:, 0:1].astype(F32)
        for c in range(tl // ch):
            e1 = rt_ref[0:1, c * ch:(c + 1) * ch]
            e2 = rt_ref[1:2, c * ch:(c + 1) * ch]
            col = pl.multiple_of(i * tl + c * ch, ch)
            d1 = rank_scr[0:1, pl.ds(col, ch)] + jnp.sum(jnp.where(e1 == ei, ps, 0.0), axis=0, keepdims=True)
            d2 = rank_scr[1:2, pl.ds(col, ch)] + jnp.sum(jnp.where(e2 == ei, ps, 0.0), axis=0, keepdims=True)
            dest_ref[0:1, c * ch:(c + 1) * ch] = d1.astype(I32)
            dest_ref[1:2, c * ch:(c + 1) * ch] = d2.astype(I32)


def _route(rt):
    tl = TL_ROUTE
    return pl.pallas_call(
        _route_kernel,
        grid=(2, TOKENS // tl),
        in_specs=[pl.BlockSpec((8, tl), lambda p, i: (0, i))],
        out_specs=[
            pl.BlockSpec((2, tl), lambda p, i: (0, i * p)),
            pl.BlockSpec((8, META_LANES), lambda p, i: (0, 0)),
        ],
        out_shape=[
            jax.ShapeDtypeStruct((2, TOKENS), I32),
            jax.ShapeDtypeStruct((8, META_LANES), I32),
        ],
        scratch_shapes=[
            pltpu.VMEM((2, TOKENS), F32),
            pltpu.VMEM((N_EXPERTS, LANES), F32),
            pltpu.VMEM((N_EXPERTS, LANES), I32),
        ],
        compiler_params=pltpu.CompilerParams(
            dimension_semantics=("arbitrary", "arbitrary"), vmem_limit_bytes=VMEM_LIMIT),
        name="route",
    )(rt)


SC_CORES = 2
SC_SUBCORES = 16
SC_WORKERS = SC_CORES * SC_SUBCORES
SC_LANES = 16
SC_WINDOW = 64
SC_SCAN_CHUNK = 4096


def _sc_dispatch(hn, dest0, dest1):
    n_tok, width = hn.shape
    rows_per_w = PADDED_ROWS // SC_WORKERS
    assert rows_per_w % SC_WINDOW == 0 and n_tok % SC_SCAN_CHUNK == 0
    mesh = plsc.VectorSubcoreMesh(core_axis_name="c", subcore_axis_name="s")

    @functools.partial(
        pl.kernel, mesh=mesh,
        out_type=jax.ShapeDtypeStruct((PADDED_ROWS, width), hn.dtype),
        scratch_types=[
            pltpu.VMEM((rows_per_w,), I32),
            pltpu.VMEM((SC_SCAN_CHUNK,), I32),
            pltpu.VMEM((SC_WINDOW, width), hn.dtype),
            pltpu.SemaphoreType.DMA,
        ],
        compiler_params=pltpu.CompilerParams(needs_layout_passes=False),
        name="sc_dispatch",
    )
    def dispatch(hn_hbm, d0_hbm, d1_hbm, xs_hbm, tok_v, dchunk_v, rows_v, sem):
        wid = lax.axis_index("s") * SC_CORES + lax.axis_index("c")
        lo = wid * rows_per_w
        lane = lax.iota(I32, SC_LANES)

        @pl.loop(0, rows_per_w // SC_LANES)
        def _(i):
            tok_v[pl.ds(i * SC_LANES, SC_LANES)] = (lo + i * SC_LANES + lane) & (n_tok - 1)

        for d_hbm in (d0_hbm, d1_hbm):
            @pl.loop(0, n_tok // SC_SCAN_CHUNK)
            def _(c, d_hbm=d_hbm):
                pltpu.sync_copy(d_hbm.at[pl.ds(c * SC_SCAN_CHUNK, SC_SCAN_CHUNK)], dchunk_v)

                @pl.loop(0, SC_SCAN_CHUNK // SC_LANES)
                def _(v):
                    local = dchunk_v[pl.ds(v * SC_LANES, SC_LANES)] - lo
                    mine = (local >= 0) & (local < rows_per_w)
                    tok = c * SC_SCAN_CHUNK + v * SC_LANES + lane
                    plsc.store_scatter(tok_v, [jnp.where(mine, local, 0)], tok, mask=mine)

        @pl.loop(0, rows_per_w // SC_WINDOW)
        def _(j):
            off = j * SC_WINDOW
            pltpu.async_copy(hn_hbm.at[tok_v.at[pl.ds(off, SC_WINDOW)]], rows_v, sem).wait()
            pltpu.sync_copy(rows_v, xs_hbm.at[pl.ds(lo + off, SC_WINDOW)])

    return dispatch(hn, dest0, dest1)


def _experts_kernel(be_ref, nv_ref, xs_ref, wg_ref, wu_ref, wd_ref, yb_ref):
    j = pl.program_id(0)

    @pl.when(j < nv_ref[0])
    def _():
        xb = xs_ref[...].astype(BF16)
        a = jnp.dot(xb, wg_ref[0], preferred_element_type=F32)
        b = jnp.dot(xb, wu_ref[0], preferred_element_type=F32)
        hb = (jax.nn.silu(a) * b).astype(BF16)
        yb_ref[...] = jnp.dot(hb, wd_ref[0], preferred_element_type=F32)

    @pl.when(j >= nv_ref[0])
    def _():
        yb_ref[...] = jnp.zeros_like(yb_ref)


def _experts(blk_expert, n_valid, xs, w_gate, w_up, w_down):
    def row_map(j, be, nv):
        return (jnp.minimum(j, nv[0] - 1), 0)

    def out_map(j, be, nv):
        return (j, 0)

    def w_map(j, be, nv):
        return (be[jnp.minimum(j, nv[0] - 1)], 0, 0)

    grid_spec = pltpu.PrefetchScalarGridSpec(
        num_scalar_prefetch=2,
        grid=(N_ROW_BLOCKS,),
        in_specs=[
            pl.BlockSpec((ROW_BLOCK, D_MODEL), row_map),
            pl.BlockSpec((1, D_MODEL, EXPERT_FF), w_map),
            pl.BlockSpec((1, D_MODEL, EXPERT_FF), w_map),
            pl.BlockSpec((1, EXPERT_FF, D_MODEL), w_map),
        ],
        out_specs=pl.BlockSpec((ROW_BLOCK, D_MODEL), out_map),
    )
    return pl.pallas_call(
        _experts_kernel,
        grid_spec=grid_spec,
        out_shape=jax.ShapeDtypeStruct((PADDED_ROWS, D_MODEL), F32),
        compiler_params=pltpu.CompilerParams(
            dimension_semantics=("arbitrary",), vmem_limit_bytes=VMEM_LIMIT),
        name="experts",
    )(blk_expert, n_valid, xs, w_gate, w_up, w_down)


def _sc_gather_pair(table, idx0, idx1):
    n = idx0.shape[0]
    width = table.shape[1]
    per_w = n // SC_WORKERS
    mesh = plsc.VectorSubcoreMesh(core_axis_name="c", subcore_axis_name="s")
    out = jax.ShapeDtypeStruct((n, width), table.dtype)

    @functools.partial(
        pl.kernel, mesh=mesh, out_type=(out, out),
        scratch_types=[
            pltpu.VMEM((per_w,), I32),
            pltpu.VMEM((SC_WINDOW, width), table.dtype),
            pltpu.SemaphoreType.DMA,
        ],
        name="sc_gather",
    )
    def gather(table_hbm, idx0_hbm, idx1_hbm, out0_hbm, out1_hbm, idx_v, rows_v, sem):
        wid = lax.axis_index("s") * SC_CORES + lax.axis_index("c")
        base = wid * per_w
        for idx_hbm, out_hbm in ((idx0_hbm, out0_hbm), (idx1_hbm, out1_hbm)):
            pltpu.sync_copy(idx_hbm.at[pl.ds(base, per_w)], idx_v)

            @pl.loop(0, per_w // SC_WINDOW)
            def _(j, out_hbm=out_hbm):
                off = j * SC_WINDOW
                pltpu.async_copy(table_hbm.at[idx_v.at[pl.ds(off, SC_WINDOW)]], rows_v, sem).wait()
                pltpu.sync_copy(rows_v, out_hbm.at[pl.ds(base + off, SC_WINDOW)])

    return gather(table, idx0, idx1)


def _combine_kernel(route_ref, h_ref, g_ref, y0_ref, y1_ref, out_ref):
    w1 = route_ref[:, 2:3]
    w2 = route_ref[:, 3:4]
    h3 = h_ref[...] + (y0_ref[...] * w1 + y1_ref[...] * w2)
    out_ref[...] = _rms(h3, g_ref[...])


def _combine(route, h2, g, y0, y1):
    tm = TM_COMBINE
    tok = pl.BlockSpec((tm, D_MODEL), lambda i: (i, 0))
    return pl.pallas_call(
        _combine_kernel,
        grid=(TOKENS // tm,),
        in_specs=[pl.BlockSpec((tm, 8), lambda i: (i, 0)), tok, _resident((1, D_MODEL)), tok, tok],
        out_specs=tok,
        out_shape=jax.ShapeDtypeStruct((TOKENS, D_MODEL), F32),
        compiler_params=pltpu.CompilerParams(
            dimension_semantics=("parallel",), vmem_limit_bytes=VMEM_LIMIT),
        name="combine",
    )(route, h2, g, y0, y1)


def kernel(x, mem, positions, mix_norm_g, w_in, b_gates, w_spatial, b_spatial, v_norm_g, v_norm_b,
           w_out_a, w_out_b, w_out, xattn_norm_g, mem_norm_g, w_q_x, w_kv_x, w_o_x, moe_norm_g,
           w_router_grp, b_router_grp, w_router_exp, b_router_exp, w_gate_e, w_up_e, w_down_e,
           final_norm_g):
    assert x.shape == (BATCH, SEQ, D_MODEL) and mem.shape == (BATCH, N_MEM, D_MODEL)
    assert mix_norm_g.shape[0] == 1, "single layer"
    x2d = x.reshape(TOKENS, D_MODEL)
    pos_col = positions.reshape(TOKENS, 1).astype(F32)
    half = HEAD_DIM // 2
    inv_freq = ROPE_THETA ** (-jnp.arange(half, dtype=F32) / half)
    invf = jnp.tile(inv_freq, LANES // half).reshape(1, LANES)
    phase = jnp.tile(jnp.concatenate([jnp.zeros((half,), F32), jnp.full((half,), math.pi / 2, F32)]),
                     LANES // HEAD_DIM).reshape(1, LANES)

    kv = _memkv(mem.reshape(BATCH * N_MEM, D_MODEL), mem_norm_g[0].reshape(1, D_MODEL),
                w_kv_x[0].astype(BF16))

    qkv0, qkv1, qkv2, ga, mb = _inproj(
        x2d, pos_col, invf, phase, mix_norm_g[0].reshape(1, D_MODEL), w_in[0].astype(BF16),
        b_gates[0].reshape(1, 2 * D_MODEL), w_spatial[0], b_spatial[0].T,
        v_norm_g[0].reshape(1, GMLP_WIDTH), v_norm_b[0].reshape(1, GMLP_WIDTH),
        w_out_b[0].astype(BF16))

    ya = _attention((qkv0, qkv1, qkv2)).reshape(TOKENS, GROUP_WIDTH)

    pad = LANES - N_EXPERTS - N_EXPERT_GROUPS
    w_r = jnp.concatenate([w_router_exp[0], w_router_grp[0], jnp.zeros((D_MODEL, pad), F32)], axis=1)
    b_r = jnp.concatenate([b_router_exp[0], b_router_grp[0], jnp.zeros((pad,), F32)]).reshape(1, LANES)
    h2, hn2, route = _post(
        ya, ga, mb, x2d, kv, w_out_a[0].astype(BF16), w_out[0].astype(BF16),
        xattn_norm_g[0].reshape(1, D_MODEL), w_q_x[0].astype(BF16), w_o_x[0].astype(BF16),
        moe_norm_g[0].reshape(1, D_MODEL), w_r.astype(BF16), b_r)

    dest, meta = _route(route.T)
    d0, d1 = dest[0], dest[1]
    xs = _sc_dispatch(hn2, d0, d1)
    blk_expert = meta[0, :N_ROW_BLOCKS]
    n_valid = (meta[3, N_EXPERTS - 1:N_EXPERTS] // ROW_BLOCK).astype(I32)
    yb = _experts(blk_expert, n_valid, xs, w_gate_e[0].astype(BF16), w_up_e[0].astype(BF16),
                  w_down_e[0].astype(BF16))
    y0, y1 = _sc_gather_pair(yb, d0, d1)
    out = _combine(route, h2, final_norm_g.reshape(1, D_MODEL), y0, y1)
    return out.reshape(BATCH, SEQ, D_MODEL)
```

```python
import functools
import math

import jax
import jax.numpy as jnp
from jax import lax
from jax.experimental import pallas as pl
from jax.experimental.pallas import tpu as pltpu
from jax.experimental.pallas import tpu_sc as plsc

F32 = jnp.float32
BF16 = jnp.bfloat16
I32 = jnp.int32

D_MODEL = 1024
BATCH = 16
SEQ = 4096
TOKENS = BATCH * SEQ

HEAD_DIM = 64
DILATIONS = (1, 4, 16)
HEADS_PER_GROUP = 4
GROUP_WIDTH = HEADS_PER_GROUP * HEAD_DIM
ATT_WIDTH = len(DILATIONS) * GROUP_WIDTH
BAND_BLOCK = 128
ROPE_THETA = 10000.0

GMLP_CHUNK = 128
GMLP_GROUPS = 4
GMLP_WIDTH = 512

N_MEM = 256
XATTN_HEADS = 4
XATTN_HEAD_DIM = D_MODEL // XATTN_HEADS

N_EXPERT_GROUPS = 4
EXPERTS_PER_GROUP = 8
N_EXPERTS = 32
TOP_K = 2
EXPERT_FF = 512

RMS_EPS = 1e-6
LN_EPS = 1e-5
NEG_INF = -1e30

LANES = 128

COL_U = 3 * ATT_WIDTH
COL_V = COL_U + GMLP_WIDTH
COL_GA = COL_V + GMLP_WIDTH
COL_GB = COL_GA + D_MODEL

ROW_BLOCK = 512
ASSIGN = TOKENS * TOP_K
PADDED_ROWS = ASSIGN + N_EXPERTS * ROW_BLOCK
N_ROW_BLOCKS = PADDED_ROWS // ROW_BLOCK
META_LANES = ((N_ROW_BLOCKS + LANES - 1) // LANES) * LANES

TM_PROJ = 512
TM_POST = 512
TL_ROUTE = 2048
SCAN_CHUNK = 256
TM_COMBINE = 512
ATTN_UNROLL = 4

VMEM_LIMIT = 56 * 1024 * 1024


def _rms(x, g):
    return x * lax.rsqrt(jnp.mean(x * x, axis=-1, keepdims=True) + RMS_EPS) * g


HALF_MODEL = D_MODEL // 2


def _pack_row_halves(x):
    return pltpu.pack_elementwise([x[:, :HALF_MODEL], x[:, HALF_MODEL:]], packed_dtype=BF16)


def _unpack_row_halves(p):
    lo = pltpu.unpack_elementwise(p, index=0, packed_dtype=BF16, unpacked_dtype=F32)
    hi = pltpu.unpack_elementwise(p, index=1, packed_dtype=BF16, unpacked_dtype=F32)
    return lo, hi


def _resident(shape):
    nd = len(shape)
    return pl.BlockSpec(shape, lambda *_: (0,) * nd, pipeline_mode=pl.Buffered(1))


def _memkv_kernel(mem_ref, g_ref, w_ref, kv_ref):
    mn = _rms(mem_ref[...], g_ref[...]).astype(BF16)
    kv_ref[...] = jnp.dot(mn, w_ref[...], preferred_element_type=F32).astype(BF16)


def _memkv(mem2d, g, w_kv):
    rows = mem2d.shape[0]
    tm = 512
    return pl.pallas_call(
        _memkv_kernel,
        grid=(rows // tm,),
        in_specs=[
            pl.BlockSpec((tm, D_MODEL), lambda i: (i, 0)),
            _resident((1, D_MODEL)),
            _resident((D_MODEL, 2 * D_MODEL)),
        ],
        out_specs=pl.BlockSpec((tm, 2 * D_MODEL), lambda i: (i, 0)),
        out_shape=jax.ShapeDtypeStruct((rows, 2 * D_MODEL), BF16),
        compiler_params=pltpu.CompilerParams(
            dimension_semantics=("parallel",), vmem_limit_bytes=VMEM_LIMIT),
        name="memkv",
    )(mem2d, g, w_kv)


def _inproj_kernel(x_ref, pos_ref, invf_ref, phase_ref, g_ref, w_ref, bg_ref, wsp_ref,
                   bsp_ref, lng_ref, lnb_ref, wob_ref,
                   qkv0_ref, qkv1_ref, qkv2_ref, ga_ref, mb_ref, scr_ref, yb_ref):
    tm = TM_PROJ
    xn = _rms(x_ref[...], g_ref[...]).astype(BF16)

    lane = lax.broadcasted_iota(I32, (tm, LANES), 1)
    upper = (lane & 32) != 0
    t1 = jnp.sin(pos_ref[...] * invf_ref[...] + phase_ref[...])
    cosf = jnp.where(upper, t1, pltpu.roll(t1, 96, 1))
    sinf = jnp.where(upper, pltpu.roll(t1, 32, 1), -t1)

    def rope(res):
        outs = []
        for c in range(GROUP_WIDTH // LANES):
            xt = res[:, c * LANES:(c + 1) * LANES]
            rot = jnp.where(upper, pltpu.roll(xt, 32, 1), pltpu.roll(xt, 96, 1))
            outs.append(xt * cosf + rot * sinf)
        return jnp.concatenate(outs, axis=1)

    zu_raw = jnp.dot(xn, w_ref[:, COL_U:COL_V], preferred_element_type=F32)
    zv_raw = jnp.dot(xn, w_ref[:, COL_V:COL_GA], preferred_element_type=F32)

    out_refs = (qkv0_ref, qkv1_ref, qkv2_ref)

    def project_group(gi):
        dil = DILATIONS[gi]
        for which in range(3):
            c0 = which * ATT_WIDTH + gi * GROUP_WIDTH
            res = jnp.dot(xn, w_ref[:, c0:c0 + GROUP_WIDTH], preferred_element_type=F32)
            if which < 2:
                res = rope(res)
            if which == 0:
                res = res * (HEAD_DIM ** -0.5)
            if dil == 1:
                out_refs[gi][0, which, 0] = res.astype(BF16)
            else:
                rows = tm // dil
                for c in range(GROUP_WIDTH // LANES):
                    slot = ((gi - 1) * 3 + which) * (GROUP_WIDTH // LANES) + c
                    scr_ref[slot] = res[:, c * LANES:(c + 1) * LANES]
                    for r in range(dil):
                        out_refs[gi][0, which, r, :, c * LANES:(c + 1) * LANES] = (
                            scr_ref[slot, pl.ds(r, rows, stride=dil), :].astype(BF16))

    for gi in range(len(DILATIONS)):
        project_group(gi)

    zu = jax.nn.gelu(zu_raw)
    zv = jax.nn.gelu(zv_raw)
    mu = jnp.mean(zv, axis=-1, keepdims=True)
    zc = zv - mu
    var = jnp.mean(zc * zc, axis=-1, keepdims=True)
    vn = (zc * lax.rsqrt(var + LN_EPS) * lng_ref[...] + lnb_ref[...]).astype(BF16)
    tri_r = lax.broadcasted_iota(I32, (GMLP_CHUNK, GMLP_CHUNK), 0)
    tri_c = lax.broadcasted_iota(I32, (GMLP_CHUNK, GMLP_CHUNK), 1)
    causal = tri_r >= tri_c
    n_chunks = tm // GMLP_CHUNK
    gw = GMLP_WIDTH // GMLP_GROUPS
    for g in range(GMLP_GROUPS):
        wsg = jnp.where(causal, wsp_ref[g], 0.0).astype(BF16)
        vcat = jnp.concatenate(
            [vn[c * GMLP_CHUNK:(c + 1) * GMLP_CHUNK, g * gw:(g + 1) * gw] for c in range(n_chunks)],
            axis=1)
        mixed = jnp.dot(wsg, vcat, preferred_element_type=F32) + bsp_ref[:, g:g + 1]
        for c in range(n_chunks):
            u_blk = zu[c * GMLP_CHUNK:(c + 1) * GMLP_CHUNK, g * gw:(g + 1) * gw]
            yb_ref[c * GMLP_CHUNK:(c + 1) * GMLP_CHUNK, g * gw:(g + 1) * gw] = (
                u_blk * mixed[:, c * gw:(c + 1) * gw]).astype(BF16)

    gate_a = jax.nn.sigmoid(
        jnp.dot(xn, w_ref[:, COL_GA:COL_GB], preferred_element_type=F32) + bg_ref[:, :D_MODEL])
    ga_ref[...] = gate_a.astype(BF16)
    gate_b = jax.nn.sigmoid(
        jnp.dot(xn, w_ref[:, COL_GB:COL_GB + D_MODEL], preferred_element_type=F32) + bg_ref[:, D_MODEL:])
    mb_ref[...] = (gate_b * jnp.dot(yb_ref[...], wob_ref[...], preferred_element_type=F32)).astype(BF16)


def _inproj(x2d, pos_col, invf, phase, g, w_in, b_gates, w_spatial, b_spatial_t, ln_g, ln_b, w_out_b):
    tm = TM_PROJ
    nt = SEQ // tm
    in_cols = w_in.shape[1]
    qkv_shapes = [jax.ShapeDtypeStruct((BATCH, 3, d, SEQ // d, GROUP_WIDTH), BF16) for d in DILATIONS]
    qkv_specs = [
        pl.BlockSpec((1, 3, d, tm // d, GROUP_WIDTH), lambda i: (i // nt, 0, 0, i % nt, 0))
        for d in DILATIONS
    ]
    tok_spec = pl.BlockSpec((tm, D_MODEL), lambda i: (i, 0))
    return pl.pallas_call(
        _inproj_kernel,
        grid=(TOKENS // tm,),
        in_specs=[
            tok_spec,
            pl.BlockSpec((tm, 1), lambda i: (i, 0)),
            _resident((1, LANES)),
            _resident((1, LANES)),
            _resident((1, D_MODEL)),
            _resident((D_MODEL, in_cols)),
            _resident((1, 2 * D_MODEL)),
            _resident((GMLP_GROUPS, GMLP_CHUNK, GMLP_CHUNK)),
            _resident((GMLP_CHUNK, GMLP_GROUPS)),
            _resident((1, GMLP_WIDTH)),
            _resident((1, GMLP_WIDTH)),
            _resident((GMLP_WIDTH, D_MODEL)),
        ],
        out_specs=qkv_specs + [tok_spec, tok_spec],
        out_shape=qkv_shapes + [jax.ShapeDtypeStruct((TOKENS, D_MODEL), BF16)] * 2,
        scratch_shapes=[
            pltpu.VMEM((6 * (GROUP_WIDTH // LANES), tm, LANES), F32),
            pltpu.VMEM((tm, GMLP_WIDTH), BF16),
        ],
        compiler_params=pltpu.CompilerParams(
            dimension_semantics=("parallel",), vmem_limit_bytes=VMEM_LIMIT),
        name="inproj",
    )(x2d, pos_col, invf, phase, g, w_in, b_gates, w_spatial, b_spatial_t, ln_g, ln_b, w_out_b)


def _attn_kernel(qkv0_ref, qkv1_ref, qkv2_ref, y_ref, acc_ref, m_ref, z_ref, bias_ref):
    blk = BAND_BLOCK
    lane_row = lax.broadcasted_iota(I32, (1, LANES), 1)
    head0_b = jnp.where(lane_row < HEAD_DIM, 1.0, 0.0).astype(BF16)
    head1_b = jnp.where(lane_row < HEAD_DIM, 0.0, 1.0).astype(BF16)
    head0 = lax.broadcasted_iota(I32, (blk, LANES), 1) < HEAD_DIM
    ones_b = jnp.ones((2 * blk, LANES), BF16)

    qi = lax.broadcasted_iota(I32, (2 * blk, 2 * blk), 0) & (blk - 1)
    kc = lax.broadcasted_iota(I32, (2 * blk, 2 * blk), 1)
    for slot, off in enumerate((0, blk)):
        dist = qi + off - kc
        bias_ref[slot] = jnp.where((dist >= 0) & (dist <= blk), 0.0, NEG_INF)

    for gi, (ref, dil) in enumerate(zip((qkv0_ref, qkv1_ref, qkv2_ref), DILATIONS)):
        seq_len = SEQ // dil
        nb = seq_len // blk
        nb_shift = nb.bit_length() - 1

        def body(i, carry, ref=ref, dil=dil, nb=nb, nb_shift=nb_shift, gi=gi):
            r = lax.shift_right_logical(i, nb_shift)
            n = i & (nb - 1)
            q0 = pl.multiple_of(n * blk, blk)
            w0 = pl.multiple_of(jnp.maximum(n - 1, 0) * blk, blk)
            q = ref[0, 0, r, pl.ds(q0, blk), :]
            k = ref[0, 1, r, pl.ds(w0, 2 * blk), :]
            v = ref[0, 2, r, pl.ds(w0, 2 * blk), :]
            q2 = jnp.concatenate([q * head0_b, q * head1_b], axis=0)
            s = lax.dot_general(q2, k, (((1,), (1,)), ((), ())), preferred_element_type=F32)
            s = s + bias_ref[jnp.minimum(n, 1)]
            m2 = jnp.max(s, axis=-1, keepdims=True)
            p = jnp.exp(s - m2)
            v_ext = jnp.concatenate([v, ones_b], axis=1)
            o2 = jnp.dot(p.astype(BF16), v_ext, preferred_element_type=F32)
            o = jnp.where(head0, o2[:blk, :LANES], o2[blk:, :LANES])
            den = jnp.where(head0, o2[:blk, LANES:], o2[blk:, LANES:])
            m = jnp.where(head0, m2[:blk], m2[blk:])
            if gi == 0:
                acc_ref[pl.ds(q0, blk), :] = o
                m_ref[pl.ds(q0, blk), :] = m
                z_ref[pl.ds(q0, blk), :] = den
            else:
                idx = pl.ds(n * (blk * dil) + r, blk, stride=dil)
                m_old = m_ref[idx, :]
                m_new = jnp.maximum(m_old, m)
                e_old = jnp.exp(m_old - m_new)
                e_new = jnp.exp(m - m_new)
                acc_ref[idx, :] = acc_ref[idx, :] * e_old + o * e_new
                z_ref[idx, :] = z_ref[idx, :] * e_old + den * e_new
                m_ref[idx, :] = m_new
            return carry

        lax.fori_loop(0, dil * nb, body, 0, unroll=ATTN_UNROLL)

    y_ref[0] = (acc_ref[...] / z_ref[...]).astype(BF16)


def _attention(qkv):
    in_specs = [
        pl.BlockSpec((1, 3, d, SEQ // d, LANES), lambda b, h: (b, 0, 0, 0, h)) for d in DILATIONS
    ]
    return pl.pallas_call(
        _attn_kernel,
        grid=(BATCH, GROUP_WIDTH // LANES),
        in_specs=in_specs,
        out_specs=pl.BlockSpec((1, SEQ, LANES), lambda b, h: (b, 0, h)),
        out_shape=jax.ShapeDtypeStruct((BATCH, SEQ, GROUP_WIDTH), BF16),
        scratch_shapes=[
            pltpu.VMEM((SEQ, LANES), F32), pltpu.VMEM((SEQ, LANES), F32), pltpu.VMEM((SEQ, LANES), F32),
            pltpu.VMEM((2, 2 * BAND_BLOCK, 2 * BAND_BLOCK), F32),
        ],
        compiler_params=pltpu.CompilerParams(
            dimension_semantics=("parallel", "parallel"), vmem_limit_bytes=VMEM_LIMIT),
        name="attn",
    )(*qkv)


def _post_kernel(ya_ref, ga_ref, mb_ref, x_ref, k_ref, v_ref, woa_ref, wo_ref, xg_ref, wq_ref,
                 wox_ref, mg_ref, wr_ref, br_ref, h_ref, hn_ref, route_ref, o_scr):
    tm = TM_POST
    t = jnp.dot(ya_ref[...], woa_ref[...], preferred_element_type=F32)
    merged = (ga_ref[...].astype(F32) * t + mb_ref[...].astype(F32)).astype(BF16)
    h1 = x_ref[...] + jnp.dot(merged, wo_ref[...], preferred_element_type=F32)

    hn = _rms(h1, xg_ref[...]).astype(BF16)
    q = (jnp.dot(hn, wq_ref[...], preferred_element_type=F32) * (XATTN_HEAD_DIM ** -0.5)).astype(BF16)
    hd = XATTN_HEAD_DIM
    for h in range(XATTN_HEADS):
        s = lax.dot_general(q[:, h * hd:(h + 1) * hd], k_ref[:, h * hd:(h + 1) * hd],
                            (((1,), (1,)), ((), ())), preferred_element_type=F32)
        m = jnp.max(s, axis=-1, keepdims=True)
        p = jnp.exp(s - m)
        den = jnp.sum(p, axis=-1, keepdims=True)
        oh = jnp.dot(p.astype(BF16), v_ref[:, h * hd:(h + 1) * hd], preferred_element_type=F32) / den
        o_scr[:, h * hd:(h + 1) * hd] = oh.astype(BF16)
    h2 = h1 + jnp.dot(o_scr[...], wox_ref[...], preferred_element_type=F32)
    h_ref[...] = h2

    hn2 = _rms(h2, mg_ref[...])
    hn_ref[...] = _pack_row_halves(hn2)

    logits = jnp.dot(hn2.astype(BF16), wr_ref[...], preferred_element_type=F32) + br_ref[...]
    li = lax.broadcasted_iota(I32, (tm, LANES), 1)
    lif = li.astype(F32)
    grp_of_lane = lax.shift_right_logical(li, 3).astype(F32)
    is_grp = (li >= N_EXPERTS) & (li < N_EXPERTS + N_EXPERT_GROUPS)
    gl = jnp.where(is_grp, logits, -jnp.inf)
    gmax = jnp.max(gl, axis=-1, keepdims=True)
    grp = jnp.min(jnp.where(gl == gmax, lif - N_EXPERTS, float(LANES)), axis=-1, keepdims=True)
    gsum = jnp.sum(jnp.where(is_grp, jnp.exp(logits - gmax), 0.0), axis=-1, keepdims=True)
    grp_gate = 1.0 / gsum
    in_grp = grp_of_lane == grp
    el = jnp.where(in_grp, logits, -jnp.inf)
    v1 = jnp.max(el, axis=-1, keepdims=True)
    i1 = jnp.min(jnp.where(el == v1, lif, float(LANES)), axis=-1, keepdims=True)
    el2 = jnp.where(lif == i1, -jnp.inf, el)
    v2 = jnp.max(el2, axis=-1, keepdims=True)
    i2 = jnp.min(jnp.where(el2 == v2, lif, float(LANES)), axis=-1, keepdims=True)
    tt = jnp.exp(v2 - v1)
    w1 = grp_gate / (1.0 + tt)
    w2 = grp_gate * tt / (1.0 + tt)
    l8 = lax.broadcasted_iota(I32, (tm, 8), 1)
    route = jnp.where(l8 == 0, i1,
                      jnp.where(l8 == 1, i2,
                                jnp.where(l8 == 2, w1, jnp.where(l8 == 3, w2, 0.0))))
    route_ref[...] = route


def _post(ya, ga, mb, x2d, kv, w_out_a, w_out, xg, w_q, w_o, mg, w_r, b_r):
    tm = TM_POST
    nt = SEQ // tm
    tok = lambda w: pl.BlockSpec((tm, w), lambda i: (i, 0))
    return pl.pallas_call(
        _post_kernel,
        grid=(TOKENS // tm,),
        in_specs=[
            tok(GROUP_WIDTH), tok(D_MODEL), tok(D_MODEL), tok(D_MODEL),
            pl.BlockSpec((N_MEM, D_MODEL), lambda i: (i // nt, 0)),
            pl.BlockSpec((N_MEM, D_MODEL), lambda i: (i // nt, 1)),
            _resident((GROUP_WIDTH, D_MODEL)),
            _resident((D_MODEL, D_MODEL)),
            _resident((1, D_MODEL)),
            _resident((D_MODEL, D_MODEL)),
            _resident((D_MODEL, D_MODEL)),
            _resident((1, D_MODEL)),
            _resident((D_MODEL, LANES)),
            _resident((1, LANES)),
        ],
        out_specs=[tok(D_MODEL), tok(HALF_MODEL), pl.BlockSpec((tm, 8), lambda i: (i, 0))],
        out_shape=[
            jax.ShapeDtypeStruct((TOKENS, D_MODEL), F32),
            jax.ShapeDtypeStruct((TOKENS, HALF_MODEL), I32),
            jax.ShapeDtypeStruct((TOKENS, 8), F32),
        ],
        scratch_shapes=[pltpu.VMEM((tm, D_MODEL), BF16)],
        compiler_params=pltpu.CompilerParams(
            dimension_semantics=("parallel",), vmem_limit_bytes=VMEM_LIMIT),
        name="post",
    )(ya, ga, mb, x2d, kv, kv, w_out_a, w_out, xg, w_q, w_o, mg, w_r, b_r)


def _route_kernel(rt_ref, dest_ref, meta_ref, rank_scr, carry_ref, pstart_ref):
    pss = pl.program_id(0)
    i = pl.program_id(1)
    tl = TL_ROUTE
    ch = SCAN_CHUNK
    ei = lax.broadcasted_iota(I32, (N_EXPERTS, ch), 0).astype(F32)

    @pl.when((pss == 0) & (i == 0))
    def _():
        carry_ref[...] = jnp.zeros_like(carry_ref)

    @pl.when(pss == 0)
    def _():
        ur = lax.broadcasted_iota(I32, (ch, ch), 0)
        uc = lax.broadcasted_iota(I32, (ch, ch), 1)
        upper = jnp.where(ur < uc, 1.0, 0.0).astype(BF16)
        for c in range(tl // ch):
            e1 = rt_ref[0:1, c * ch:(c + 1) * ch]
            e2 = rt_ref[1:2, c * ch:(c + 1) * ch]
            oh1 = e1 == ei
            oh2 = e2 == ei
            oh = jnp.where(oh1 | oh2, 1.0, 0.0)
            cnt = jnp.dot(oh.astype(BF16), upper, preferred_element_type=F32) + carry_ref[:, 0:1]
            rank1 = jnp.sum(jnp.where(oh1, cnt, 0.0), axis=0, keepdims=True)
            rank2 = jnp.sum(jnp.where(oh2, cnt, 0.0), axis=0, keepdims=True)
            col = pl.multiple_of(i * tl + c * ch, ch)
            rank_scr[0:1, pl.ds(col, ch)] = rank1
            rank_scr[1:2, pl.ds(col, ch)] = rank2
            carry_ref[...] = carry_ref[...] + jnp.sum(oh, axis=1, keepdims=True)

    @pl.when((pss == 1) & (i == 0))
    def _():
        counts = carry_ref[...].astype(I32)
        padded = lax.shift_left(lax.shift_right_logical(counts + (ROW_BLOCK - 1),
                                                        int(math.log2(ROW_BLOCK))),
                                int(math.log2(ROW_BLOCK)))
        row = lax.broadcasted_iota(I32, (N_EXPERTS, LANES), 0)
        lane = lax.broadcasted_iota(I32, (N_EXPERTS, LANES), 1)
        pend = padded
        sh = 1
        while sh < N_EXPERTS:
            pend = pend + jnp.where(row >= sh, pltpu.roll(pend, sh, 0), 0)
            sh *= 2
        pstart = pend - padded
        pstart_ref[...] = pstart
        diag = row == lane

        def as_row(x):
            return jnp.sum(jnp.where(diag, x, 0), axis=0, keepdims=True)

        blk_lane = lax.broadcasted_iota(I32, (N_EXPERTS, META_LANES), 1) * ROW_BLOCK
        blk_exp = jnp.sum(jnp.where(pend[:, 0:1] <= blk_lane, 1, 0), axis=0, keepdims=True)
        blk_exp = jnp.minimum(blk_exp, N_EXPERTS - 1)
        meta_ref[...] = jnp.zeros_like(meta_ref)
        meta_ref[0:1, :] = blk_exp
        meta_ref[1:2, 0:LANES] = as_row(counts)
        meta_ref[2:3, 0:LANES] = as_row(pstart)
        meta_ref[3:4, 0:LANES] = as_row(pend)

    @pl.when(pss == 1)
    def _():
        ps = pstart_ref[:, 0:1].astype(F32)
        for c in range(tl // ch):
            e1 = rt_ref[0:1, c * ch:(c + 1) * ch]
            e2 = rt_ref[1:2, c * ch:(c + 1) * ch]
            col = pl.multiple_of(i * tl + c * ch, ch)
            d1 = rank_scr[0:1, pl.ds(col, ch)] + jnp.sum(jnp.where(e1 == ei, ps, 0.0), axis=0, keepdims=True)
            d2 = rank_scr[1:2, pl.ds(col, ch)] + jnp.sum(jnp.where(e2 == ei, ps, 0.0), axis=0, keepdims=True)
            dest_ref[0:1, c * ch:(c + 1) * ch] = d1.astype(I32)
            dest_ref[1:2, c * ch:(c + 1) * ch] = d2.astype(I32)


def _route(rt):
    tl = TL_ROUTE
    return pl.pallas_call(
        _route_kernel,
        grid=(2, TOKENS // tl),
        in_specs=[pl.BlockSpec((8, tl), lambda p, i: (0, i))],
        out_specs=[
            pl.BlockSpec((2, tl), lambda p, i: (0, i * p)),
            pl.BlockSpec((8, META_LANES), lambda p, i: (0, 0)),
        ],
        out_shape=[
            jax.ShapeDtypeStruct((2, TOKENS), I32),
            jax.ShapeDtypeStruct((8, META_LANES), I32),
        ],
        scratch_shapes=[
            pltpu.VMEM((2, TOKENS), F32),
            pltpu.VMEM((N_EXPERTS, LANES), F32),
            pltpu.VMEM((N_EXPERTS, LANES), I32),
        ],
        compiler_params=pltpu.CompilerParams(
            dimension_semantics=("arbitrary", "arbitrary"), vmem_limit_bytes=VMEM_LIMIT),
        name="route",
    )(rt)


SC_CORES = 2
SC_SUBCORES = 16
SC_WORKERS = SC_CORES * SC_SUBCORES
SC_LANES = 16
SC_WINDOW = 64
SC_SCAN_CHUNK = 4096


def _sc_move_rows(table_hbm, idx_v, out_hbm, out_base, n_windows, rows_v, gsem, wsems):
    assert n_windows % 2 == 0 and n_windows >= 2

    def gather(j, b):
        idx = idx_v.at[pl.ds(j * SC_WINDOW, SC_WINDOW)]
        pltpu.async_copy(table_hbm.at[idx], rows_v.at[b], gsem).wait()

    def write(j, b):
        dst = out_hbm.at[pl.ds(out_base + j * SC_WINDOW, SC_WINDOW)]
        return pltpu.make_async_copy(rows_v.at[b], dst, wsems.at[b])

    for b in range(2):
        gather(b, b)
        write(b, b).start()

    @pl.loop(2, n_windows, step=2)
    def _(j):
        for b in range(2):
            write(j - 2 + b, b).wait()
            gather(j + b, b)
            write(j + b, b).start()

    for b in range(2):
        write(n_windows - 2 + b, b).wait()


def _sc_dispatch(hn, dest0, dest1):
    n_tok, width = hn.shape
    rows_per_w = PADDED_ROWS // SC_WORKERS
    assert rows_per_w % SC_WINDOW == 0 and n_tok % SC_SCAN_CHUNK == 0
    mesh = plsc.VectorSubcoreMesh(core_axis_name="c", subcore_axis_name="s")

    @functools.partial(
        pl.kernel, mesh=mesh,
        out_type=jax.ShapeDtypeStruct((PADDED_ROWS, width), hn.dtype),
        scratch_types=[
            pltpu.VMEM((rows_per_w,), I32),
            pltpu.VMEM((SC_SCAN_CHUNK,), I32),
            pltpu.VMEM((2, SC_WINDOW, width), hn.dtype),
            pltpu.SemaphoreType.DMA,
            pltpu.SemaphoreType.DMA((2,)),
        ],
        compiler_params=pltpu.CompilerParams(needs_layout_passes=False),
        name="sc_dispatch",
    )
    def dispatch(hn_hbm, d0_hbm, d1_hbm, xs_hbm, tok_v, dchunk_v, rows_v, gsem, wsems):
        wid = lax.axis_index("s") * SC_CORES + lax.axis_index("c")
        lo = wid * rows_per_w
        lane = lax.iota(I32, SC_LANES)

        @pl.loop(0, rows_per_w // SC_LANES)
        def _(i):
            tok_v[pl.ds(i * SC_LANES, SC_LANES)] = (lo + i * SC_LANES + lane) & (n_tok - 1)

        for d_hbm in (d0_hbm, d1_hbm):
            @pl.loop(0, n_tok // SC_SCAN_CHUNK)
            def _(c, d_hbm=d_hbm):
                pltpu.sync_copy(d_hbm.at[pl.ds(c * SC_SCAN_CHUNK, SC_SCAN_CHUNK)], dchunk_v)

                @pl.loop(0, SC_SCAN_CHUNK // SC_LANES)
                def _(v):
                    local = dchunk_v[pl.ds(v * SC_LANES, SC_LANES)] - lo
                    mine = (local >= 0) & (local < rows_per_w)
                    tok = c * SC_SCAN_CHUNK + v * SC_LANES + lane
                    plsc.store_scatter(tok_v, [jnp.where(mine, local, 0)], tok, mask=mine)

        _sc_move_rows(hn_hbm, tok_v, xs_hbm, lo, rows_per_w // SC_WINDOW, rows_v, gsem, wsems)

    return dispatch(hn, dest0, dest1)


def _experts_kernel(be_ref, nv_ref, xs_ref, wg_ref, wu_ref, wd_ref, yb_ref):
    j = pl.program_id(0)

    @pl.when(j < nv_ref[0])
    def _():
        lo, hi = _unpack_row_halves(xs_ref[...])
        xb = jnp.concatenate([lo.astype(BF16), hi.astype(BF16)], axis=1)
        a = jnp.dot(xb, wg_ref[0].astype(BF16), preferred_element_type=F32)
        b = jnp.dot(xb, wu_ref[0].astype(BF16), preferred_element_type=F32)
        hb = (jax.nn.silu(a) * b).astype(BF16)
        yb_ref[...] = _pack_row_halves(
            jnp.dot(hb, wd_ref[0].astype(BF16), preferred_element_type=F32))

    @pl.when(j >= nv_ref[0])
    def _():
        yb_ref[...] = jnp.zeros_like(yb_ref)


def _experts(blk_expert, n_valid, xs, w_gate, w_up, w_down):
    def row_map(j, be, nv):
        return (jnp.minimum(j, nv[0] - 1), 0)

    def out_map(j, be, nv):
        return (j, 0)

    def w_map(j, be, nv):
        return (be[jnp.minimum(j, nv[0] - 1)], 0, 0)

    grid_spec = pltpu.PrefetchScalarGridSpec(
        num_scalar_prefetch=2,
        grid=(N_ROW_BLOCKS,),
        in_specs=[
            pl.BlockSpec((ROW_BLOCK, HALF_MODEL), row_map),
            pl.BlockSpec((1, D_MODEL, EXPERT_FF), w_map),
            pl.BlockSpec((1, D_MODEL, EXPERT_FF), w_map),
            pl.BlockSpec((1, EXPERT_FF, D_MODEL), w_map),
        ],
        out_specs=pl.BlockSpec((ROW_BLOCK, HALF_MODEL), out_map),
    )
    return pl.pallas_call(
        _experts_kernel,
        grid_spec=grid_spec,
        out_shape=jax.ShapeDtypeStruct((PADDED_ROWS, HALF_MODEL), I32),
        compiler_params=pltpu.CompilerParams(
            dimension_semantics=("arbitrary",), vmem_limit_bytes=VMEM_LIMIT),
        name="experts",
    )(blk_expert, n_valid, xs, w_gate, w_up, w_down)


def _sc_gather_pair(table, idx0, idx1):
    n = idx0.shape[0]
    width = table.shape[1]
    per_w = n // SC_WORKERS
    mesh = plsc.VectorSubcoreMesh(core_axis_name="c", subcore_axis_name="s")
    out = jax.ShapeDtypeStruct((n, width), table.dtype)

    @functools.partial(
        pl.kernel, mesh=mesh, out_type=(out, out),
        scratch_types=[
            pltpu.VMEM((per_w,), I32),
            pltpu.VMEM((2, SC_WINDOW, width), table.dtype),
            pltpu.SemaphoreType.DMA,
            pltpu.SemaphoreType.DMA((2,)),
        ],
        name="sc_gather",
    )
    def gather(table_hbm, idx0_hbm, idx1_hbm, out0_hbm, out1_hbm, idx_v, rows_v, gsem, wsems):
        wid = lax.axis_index("s") * SC_CORES + lax.axis_index("c")
        base = wid * per_w
        for idx_hbm, out_hbm in ((idx0_hbm, out0_hbm), (idx1_hbm, out1_hbm)):
            pltpu.sync_copy(idx_hbm.at[pl.ds(base, per_w)], idx_v)
            _sc_move_rows(table_hbm, idx_v, out_hbm, base, per_w // SC_WINDOW, rows_v, gsem, wsems)

    return gather(table, idx0, idx1)


def _combine_kernel(route_ref, h_ref, g_ref, y0_ref, y1_ref, out_ref):
    w1 = route_ref[:, 2:3]
    w2 = route_ref[:, 3:4]
    lo0, hi0 = _unpack_row_halves(y0_ref[...])
    lo1, hi1 = _unpack_row_halves(y1_ref[...])
    y = jnp.concatenate([lo0 * w1 + lo1 * w2, hi0 * w1 + hi1 * w2], axis=1)
    out_ref[...] = _rms(h_ref[...] + y, g_ref[...])


def _combine(route, h2, g, y0, y1):
    tm = TM_COMBINE
    tok = pl.BlockSpec((tm, D_MODEL), lambda i: (i, 0))
    packed = pl.BlockSpec((tm, HALF_MODEL), lambda i: (i, 0))
    return pl.pallas_call(
        _combine_kernel,
        grid=(TOKENS // tm,),
        in_specs=[pl.BlockSpec((tm, 8), lambda i: (i, 0)), tok, _resident((1, D_MODEL)), packed, packed],
        out_specs=tok,
        out_shape=jax.ShapeDtypeStruct((TOKENS, D_MODEL), F32),
        compiler_params=pltpu.CompilerParams(
            dimension_semantics=("parallel",), vmem_limit_bytes=VMEM_LIMIT),
        name="combine",
    )(route, h2, g, y0, y1)


def kernel(x, mem, positions, mix_norm_g, w_in, b_gates, w_spatial, b_spatial, v_norm_g, v_norm_b,
           w_out_a, w_out_b, w_out, xattn_norm_g, mem_norm_g, w_q_x, w_kv_x, w_o_x, moe_norm_g,
           w_router_grp, b_router_grp, w_router_exp, b_router_exp, w_gate_e, w_up_e, w_down_e,
           final_norm_g):
    assert x.shape == (BATCH, SEQ, D_MODEL) and mem.shape == (BATCH, N_MEM, D_MODEL)
    assert mix_norm_g.shape[0] == 1, "single layer"
    x2d = x.reshape(TOKENS, D_MODEL)
    pos_col = positions.reshape(TOKENS, 1).astype(F32)
    half = HEAD_DIM // 2
    inv_freq = ROPE_THETA ** (-jnp.arange(half, dtype=F32) / half)
    invf = jnp.tile(inv_freq, LANES // half).reshape(1, LANES)
    phase = jnp.tile(jnp.concatenate([jnp.zeros((half,), F32), jnp.full((half,), math.pi / 2, F32)]),
                     LANES // HEAD_DIM).reshape(1, LANES)

    kv = _memkv(mem.reshape(BATCH * N_MEM, D_MODEL), mem_norm_g[0].reshape(1, D_MODEL),
                w_kv_x[0].astype(BF16))

    qkv0, qkv1, qkv2, ga, mb = _inproj(
        x2d, pos_col, invf, phase, mix_norm_g[0].reshape(1, D_MODEL), w_in[0].astype(BF16),
        b_gates[0].reshape(1, 2 * D_MODEL), w_spatial[0], b_spatial[0].T,
        v_norm_g[0].reshape(1, GMLP_WIDTH), v_norm_b[0].reshape(1, GMLP_WIDTH),
        w_out_b[0].astype(BF16))

    ya = _attention((qkv0, qkv1, qkv2)).reshape(TOKENS, GROUP_WIDTH)

    pad = LANES - N_EXPERTS - N_EXPERT_GROUPS
    w_r = jnp.concatenate([w_router_exp[0], w_router_grp[0], jnp.zeros((D_MODEL, pad), F32)], axis=1)
    b_r = jnp.concatenate([b_router_exp[0], b_router_grp[0], jnp.zeros((pad,), F32)]).reshape(1, LANES)
    h2, hn2, route = _post(
        ya, ga, mb, x2d, kv, w_out_a[0].astype(BF16), w_out[0].astype(BF16),
        xattn_norm_g[0].reshape(1, D_MODEL), w_q_x[0].astype(BF16), w_o_x[0].astype(BF16),
        moe_norm_g[0].reshape(1, D_MODEL), w_r.astype(BF16), b_r)

    dest, meta = _route(route.T)
    d0, d1 = dest[0], dest[1]
    xs = _sc_dispatch(hn2, d0, d1)
    blk_expert = meta[0, :N_ROW_BLOCKS]
    n_valid = (meta[3, N_EXPERTS - 1:N_EXPERTS] // ROW_BLOCK).astype(I32)
    yb = _experts(blk_expert, n_valid, xs, w_gate_e[0], w_up_e[0], w_down_e[0])
    y0, y1 = _sc_gather_pair(yb, d0, d1)
    out = _combine(route, h2, final_norm_g.reshape(1, D_MODEL), y0, y1)
    return out.reshape(BATCH, SEQ, D_MODEL)
```

```python
import functools
import math

import jax
import jax.numpy as jnp
from jax import lax
from jax.experimental import pallas as pl
from jax.experimental.pallas import tpu as pltpu
from jax.experimental.pallas import tpu_sc as plsc

F32 = jnp.float32
BF16 = jnp.bfloat16
I32 = jnp.int32

D_MODEL = 1024
BATCH = 16
SEQ = 4096
TOKENS = BATCH * SEQ

HEAD_DIM = 64
DILATIONS = (1, 4, 16)
HEADS_PER_GROUP = 4
GROUP_WIDTH = HEADS_PER_GROUP * HEAD_DIM
ATT_WIDTH = len(DILATIONS) * GROUP_WIDTH
BAND_BLOCK = 128
ROPE_THETA = 10000.0

GMLP_CHUNK = 128
GMLP_GROUPS = 4
GMLP_WIDTH = 512

N_MEM = 256
XATTN_HEADS = 4
XATTN_HEAD_DIM = D_MODEL // XATTN_HEADS

N_EXPERT_GROUPS = 4
EXPERTS_PER_GROUP = 8
N_EXPERTS = 32
TOP_K = 2
EXPERT_FF = 512

RMS_EPS = 1e-6
LN_EPS = 1e-5
NEG_INF = -1e30

LANES = 128

COL_U = 3 * ATT_WIDTH
COL_V = COL_U + GMLP_WIDTH
COL_GA = COL_V + GMLP_WIDTH
COL_GB = COL_GA + D_MODEL

ROW_BLOCK = 512
ASSIGN = TOKENS * TOP_K
PADDED_ROWS = ASSIGN + N_EXPERTS * ROW_BLOCK
N_ROW_BLOCKS = PADDED_ROWS // ROW_BLOCK
META_LANES = ((N_ROW_BLOCKS + LANES - 1) // LANES) * LANES

TM_PROJ = 512
TM_POST = 1024
POST_SUB = 512
TL_ROUTE = 2048
SCAN_CHUNK = 256
TM_COMBINE = 512
ATTN_UNROLL = 8

VMEM_LIMIT = 56 * 1024 * 1024


def _rms(x, g):
    return x * lax.rsqrt(jnp.mean(x * x, axis=-1, keepdims=True) + RMS_EPS) * g


HALF_MODEL = D_MODEL // 2


def _pack_row_halves(x):
    return pltpu.pack_elementwise([x[:, :HALF_MODEL], x[:, HALF_MODEL:]], packed_dtype=BF16)


def _unpack_row_halves(p):
    lo = pltpu.unpack_elementwise(p, index=0, packed_dtype=BF16, unpacked_dtype=F32)
    hi = pltpu.unpack_elementwise(p, index=1, packed_dtype=BF16, unpacked_dtype=F32)
    return lo, hi


def _resident(shape):
    nd = len(shape)
    return pl.BlockSpec(shape, lambda *_: (0,) * nd, pipeline_mode=pl.Buffered(1))


def _memkv_kernel(mem_ref, g_ref, w_ref, kv_ref):
    mn = _rms(mem_ref[...], g_ref[...]).astype(BF16)
    kv_ref[...] = jnp.dot(mn, w_ref[...], preferred_element_type=F32).astype(BF16)


def _memkv(mem2d, g, w_kv):
    rows = mem2d.shape[0]
    tm = 512
    return pl.pallas_call(
        _memkv_kernel,
        grid=(rows // tm,),
        in_specs=[
            pl.BlockSpec((tm, D_MODEL), lambda i: (i, 0)),
            _resident((1, D_MODEL)),
            _resident((D_MODEL, 2 * D_MODEL)),
        ],
        out_specs=pl.BlockSpec((tm, 2 * D_MODEL), lambda i: (i, 0)),
        out_shape=jax.ShapeDtypeStruct((rows, 2 * D_MODEL), BF16),
        compiler_params=pltpu.CompilerParams(
            dimension_semantics=("parallel",), vmem_limit_bytes=VMEM_LIMIT),
        name="memkv",
    )(mem2d, g, w_kv)


def _inproj_kernel(x_ref, pos_ref, invf_ref, phase_ref, g_ref, w_ref, bg_ref, wsp_ref,
                   bsp_ref, lng_ref, lnb_ref, wob_ref,
                   qkv0_ref, qkv1_ref, qkv2_ref, ga_ref, mb_ref, scr_ref, yb_ref):
    tm = TM_PROJ
    xn = _rms(x_ref[...], g_ref[...]).astype(BF16)

    lane = lax.broadcasted_iota(I32, (tm, LANES), 1)
    upper = (lane & 32) != 0
    t1 = jnp.sin(pos_ref[...] * invf_ref[...] + phase_ref[...])
    cosf = jnp.where(upper, t1, pltpu.roll(t1, 96, 1))
    sinf = jnp.where(upper, pltpu.roll(t1, 32, 1), -t1)

    def rope(res):
        outs = []
        for c in range(GROUP_WIDTH // LANES):
            xt = res[:, c * LANES:(c + 1) * LANES]
            rot = jnp.where(upper, pltpu.roll(xt, 32, 1), pltpu.roll(xt, 96, 1))
            outs.append(xt * cosf + rot * sinf)
        return jnp.concatenate(outs, axis=1)

    zu_raw = jnp.dot(xn, w_ref[:, COL_U:COL_V], preferred_element_type=F32)
    zv_raw = jnp.dot(xn, w_ref[:, COL_V:COL_GA], preferred_element_type=F32)

    out_refs = (qkv0_ref, qkv1_ref, qkv2_ref)

    def project_group(gi):
        dil = DILATIONS[gi]
        for which in range(3):
            c0 = which * ATT_WIDTH + gi * GROUP_WIDTH
            res = jnp.dot(xn, w_ref[:, c0:c0 + GROUP_WIDTH], preferred_element_type=F32)
            if which < 2:
                res = rope(res)
            if which == 0:
                res = res * (HEAD_DIM ** -0.5)
            if dil == 1:
                out_refs[gi][0, which, 0] = res.astype(BF16)
            else:
                rows = tm // dil
                for c in range(GROUP_WIDTH // LANES):
                    slot = ((gi - 1) * 3 + which) * (GROUP_WIDTH // LANES) + c
                    scr_ref[slot] = res[:, c * LANES:(c + 1) * LANES]
                    for r in range(dil):
                        out_refs[gi][0, which, r, :, c * LANES:(c + 1) * LANES] = (
                            scr_ref[slot, pl.ds(r, rows, stride=dil), :].astype(BF16))

    for gi in range(len(DILATIONS)):
        project_group(gi)

    zu = jax.nn.gelu(zu_raw)
    zv = jax.nn.gelu(zv_raw)
    mu = jnp.mean(zv, axis=-1, keepdims=True)
    zc = zv - mu
    var = jnp.mean(zc * zc, axis=-1, keepdims=True)
    vn = (zc * lax.rsqrt(var + LN_EPS) * lng_ref[...] + lnb_ref[...]).astype(BF16)
    tri_r = lax.broadcasted_iota(I32, (GMLP_CHUNK, GMLP_CHUNK), 0)
    tri_c = lax.broadcasted_iota(I32, (GMLP_CHUNK, GMLP_CHUNK), 1)
    causal = tri_r >= tri_c
    n_chunks = tm // GMLP_CHUNK
    gw = GMLP_WIDTH // GMLP_GROUPS
    for g in range(GMLP_GROUPS):
        wsg = jnp.where(causal, wsp_ref[g], 0.0).astype(BF16)
        vcat = jnp.concatenate(
            [vn[c * GMLP_CHUNK:(c + 1) * GMLP_CHUNK, g * gw:(g + 1) * gw] for c in range(n_chunks)],
            axis=1)
        mixed = jnp.dot(wsg, vcat, preferred_element_type=F32) + bsp_ref[:, g:g + 1]
        for c in range(n_chunks):
            u_blk = zu[c * GMLP_CHUNK:(c + 1) * GMLP_CHUNK, g * gw:(g + 1) * gw]
            yb_ref[c * GMLP_CHUNK:(c + 1) * GMLP_CHUNK, g * gw:(g + 1) * gw] = (
                u_blk * mixed[:, c * gw:(c + 1) * gw]).astype(BF16)

    gate_a = jax.nn.sigmoid(
        jnp.dot(xn, w_ref[:, COL_GA:COL_GB], preferred_element_type=F32) + bg_ref[:, :D_MODEL])
    ga_ref[...] = gate_a.astype(BF16)
    gate_b = jax.nn.sigmoid(
        jnp.dot(xn, w_ref[:, COL_GB:COL_GB + D_MODEL], preferred_element_type=F32) + bg_ref[:, D_MODEL:])
    mb_ref[...] = (gate_b * jnp.dot(yb_ref[...], wob_ref[...], preferred_element_type=F32)).astype(BF16)


def _inproj(x2d, pos_col, invf, phase, g, w_in, b_gates, w_spatial, b_spatial_t, ln_g, ln_b, w_out_b):
    tm = TM_PROJ
    nt = SEQ // tm
    in_cols = w_in.shape[1]
    qkv_shapes = [jax.ShapeDtypeStruct((BATCH, 3, d, SEQ // d, GROUP_WIDTH), BF16) for d in DILATIONS]
    qkv_specs = [
        pl.BlockSpec((1, 3, d, tm // d, GROUP_WIDTH), lambda i: (i // nt, 0, 0, i % nt, 0))
        for d in DILATIONS
    ]
    tok_spec = pl.BlockSpec((tm, D_MODEL), lambda i: (i, 0))
    return pl.pallas_call(
        _inproj_kernel,
        grid=(TOKENS // tm,),
        in_specs=[
            tok_spec,
            pl.BlockSpec((tm, 1), lambda i: (i, 0)),
            _resident((1, LANES)),
            _resident((1, LANES)),
            _resident((1, D_MODEL)),
            _resident((D_MODEL, in_cols)),
            _resident((1, 2 * D_MODEL)),
            _resident((GMLP_GROUPS, GMLP_CHUNK, GMLP_CHUNK)),
            _resident((GMLP_CHUNK, GMLP_GROUPS)),
            _resident((1, GMLP_WIDTH)),
            _resident((1, GMLP_WIDTH)),
            _resident((GMLP_WIDTH, D_MODEL)),
        ],
        out_specs=qkv_specs + [tok_spec, tok_spec],
        out_shape=qkv_shapes + [jax.ShapeDtypeStruct((TOKENS, D_MODEL), BF16)] * 2,
        scratch_shapes=[
            pltpu.VMEM((6 * (GROUP_WIDTH // LANES), tm, LANES), F32),
            pltpu.VMEM((tm, GMLP_WIDTH), BF16),
        ],
        compiler_params=pltpu.CompilerParams(
            dimension_semantics=("parallel",), vmem_limit_bytes=VMEM_LIMIT),
        name="inproj",
    )(x2d, pos_col, invf, phase, g, w_in, b_gates, w_spatial, b_spatial_t, ln_g, ln_b, w_out_b)


def _attn_kernel(qkv0_ref, qkv1_ref, qkv2_ref, y_ref, acc_ref, m_ref, z_ref, bias_ref):
    blk = BAND_BLOCK
    lane_row = lax.broadcasted_iota(I32, (1, LANES), 1)
    head0_b = jnp.where(lane_row < HEAD_DIM, 1.0, 0.0).astype(BF16)
    head1_b = jnp.where(lane_row < HEAD_DIM, 0.0, 1.0).astype(BF16)
    head0 = lax.broadcasted_iota(I32, (blk, LANES), 1) < HEAD_DIM
    ones_b = jnp.ones((2 * blk, LANES), BF16)

    qi = lax.broadcasted_iota(I32, (2 * blk, 2 * blk), 0) & (blk - 1)
    kc = lax.broadcasted_iota(I32, (2 * blk, 2 * blk), 1)
    for slot, off in enumerate((0, blk)):
        dist = qi + off - kc
        bias_ref[slot] = jnp.where((dist >= 0) & (dist <= blk), 0.0, NEG_INF)

    for gi, (ref, dil) in enumerate(zip((qkv0_ref, qkv1_ref, qkv2_ref), DILATIONS)):
        seq_len = SEQ // dil
        nb = seq_len // blk
        nb_shift = nb.bit_length() - 1

        def body(i, carry, ref=ref, dil=dil, nb=nb, nb_shift=nb_shift, gi=gi):
            r = lax.shift_right_logical(i, nb_shift)
            n = i & (nb - 1)
            q0 = pl.multiple_of(n * blk, blk)
            w0 = pl.multiple_of(jnp.maximum(n - 1, 0) * blk, blk)
            q = ref[0, 0, r, pl.ds(q0, blk), :]
            k = ref[0, 1, r, pl.ds(w0, 2 * blk), :]
            v = ref[0, 2, r, pl.ds(w0, 2 * blk), :]
            q2 = jnp.concatenate([q * head0_b, q * head1_b], axis=0)
            s = lax.dot_general(q2, k, (((1,), (1,)), ((), ())), preferred_element_type=F32)
            s = s + bias_ref[jnp.minimum(n, 1)]
            m2 = jnp.max(s, axis=-1, keepdims=True)
            p = jnp.exp(s - m2)
            v_ext = jnp.concatenate([v, ones_b], axis=1)
            o2 = jnp.dot(p.astype(BF16), v_ext, preferred_element_type=F32)
            o = jnp.where(head0, o2[:blk, :LANES], o2[blk:, :LANES])
            den = jnp.where(head0, o2[:blk, LANES:], o2[blk:, LANES:])
            m = jnp.where(head0, m2[:blk], m2[blk:])
            if gi == 0:
                acc_ref[pl.ds(q0, blk), :] = o
                m_ref[pl.ds(q0, blk), :] = m
                z_ref[pl.ds(q0, blk), :] = den
            else:
                idx = pl.ds(n * (blk * dil) + r, blk, stride=dil)
                m_old = m_ref[idx, :]
                m_new = jnp.maximum(m_old, m)
                e_old = jnp.exp(m_old - m_new)
                e_new = jnp.exp(m - m_new)
                acc_ref[idx, :] = acc_ref[idx, :] * e_old + o * e_new
                z_ref[idx, :] = z_ref[idx, :] * e_old + den * e_new
                m_ref[idx, :] = m_new
            return carry

        lax.fori_loop(0, dil * nb, body, 0, unroll=ATTN_UNROLL)

    y_ref[0] = (acc_ref[...] / z_ref[...]).astype(BF16)


def _attention(qkv):
    in_specs = [
        pl.BlockSpec((1, 3, d, SEQ // d, LANES), lambda b, h: (b, 0, 0, 0, h)) for d in DILATIONS
    ]
    return pl.pallas_call(
        _attn_kernel,
        grid=(BATCH, GROUP_WIDTH // LANES),
        in_specs=in_specs,
        out_specs=pl.BlockSpec((1, SEQ, LANES), lambda b, h: (b, 0, h)),
        out_shape=jax.ShapeDtypeStruct((BATCH, SEQ, GROUP_WIDTH), BF16),
        scratch_shapes=[
            pltpu.VMEM((SEQ, LANES), F32), pltpu.VMEM((SEQ, LANES), F32), pltpu.VMEM((SEQ, LANES), F32),
            pltpu.VMEM((2, 2 * BAND_BLOCK, 2 * BAND_BLOCK), F32),
        ],
        compiler_params=pltpu.CompilerParams(
            dimension_semantics=("parallel", "parallel"), vmem_limit_bytes=VMEM_LIMIT),
        name="attn",
    )(*qkv)


def _post_kernel(ya_ref, ga_ref, mb_ref, x_ref, k_ref, v_ref, woa_ref, wo_ref, xg_ref, wq_ref,
                 wox_ref, mg_ref, wr_ref, br_ref, h_ref, hn_ref, route_ref, o_scr):
    for c in range(TM_POST // POST_SUB):
        rows = slice(c * POST_SUB, (c + 1) * POST_SUB)
        _post_rows(rows, ya_ref, ga_ref, mb_ref, x_ref, k_ref, v_ref, woa_ref, wo_ref, xg_ref, wq_ref,
                   wox_ref, mg_ref, wr_ref, br_ref, h_ref, hn_ref, route_ref, o_scr)


def _post_rows(rows, ya_ref, ga_ref, mb_ref, x_ref, k_ref, v_ref, woa_ref, wo_ref, xg_ref, wq_ref,
               wox_ref, mg_ref, wr_ref, br_ref, h_ref, hn_ref, route_ref, o_scr):
    tm = POST_SUB
    t = jnp.dot(ya_ref[rows, :], woa_ref[...], preferred_element_type=F32)
    merged = (ga_ref[rows, :].astype(F32) * t + mb_ref[rows, :].astype(F32)).astype(BF16)
    h1 = x_ref[rows, :] + jnp.dot(merged, wo_ref[...], preferred_element_type=F32)

    hn = _rms(h1, xg_ref[...]).astype(BF16)
    q = (jnp.dot(hn, wq_ref[...], preferred_element_type=F32) * (XATTN_HEAD_DIM ** -0.5)).astype(BF16)
    hd = XATTN_HEAD_DIM
    for h in range(XATTN_HEADS):
        s = lax.dot_general(q[:, h * hd:(h + 1) * hd], k_ref[:, h * hd:(h + 1) * hd],
                            (((1,), (1,)), ((), ())), preferred_element_type=F32)
        m = jnp.max(s, axis=-1, keepdims=True)
        p = jnp.exp(s - m)
        den = jnp.sum(p, axis=-1, keepdims=True)
        oh = jnp.dot(p.astype(BF16), v_ref[:, h * hd:(h + 1) * hd], preferred_element_type=F32) / den
        o_scr[rows, h * hd:(h + 1) * hd] = oh.astype(BF16)
    h2 = h1 + jnp.dot(o_scr[rows, :], wox_ref[...], preferred_element_type=F32)
    h_ref[rows, :] = h2

    hn2 = _rms(h2, mg_ref[...])
    hn_ref[rows, :] = _pack_row_halves(hn2)

    logits = jnp.dot(hn2.astype(BF16), wr_ref[...], preferred_element_type=F32) + br_ref[...]
    li = lax.broadcasted_iota(I32, (tm, LANES), 1)
    lif = li.astype(F32)
    grp_of_lane = lax.shift_right_logical(li, 3).astype(F32)
    is_grp = (li >= N_EXPERTS) & (li < N_EXPERTS + N_EXPERT_GROUPS)
    gl = jnp.where(is_grp, logits, -jnp.inf)
    gmax = jnp.max(gl, axis=-1, keepdims=True)
    grp = jnp.min(jnp.where(gl == gmax, lif - N_EXPERTS, float(LANES)), axis=-1, keepdims=True)
    gsum = jnp.sum(jnp.where(is_grp, jnp.exp(logits - gmax), 0.0), axis=-1, keepdims=True)
    grp_gate = 1.0 / gsum
    in_grp = grp_of_lane == grp
    el = jnp.where(in_grp, logits, -jnp.inf)
    v1 = jnp.max(el, axis=-1, keepdims=True)
    i1 = jnp.min(jnp.where(el == v1, lif, float(LANES)), axis=-1, keepdims=True)
    el2 = jnp.where(lif == i1, -jnp.inf, el)
    v2 = jnp.max(el2, axis=-1, keepdims=True)
    i2 = jnp.min(jnp.where(el2 == v2, lif, float(LANES)), axis=-1, keepdims=True)
    tt = jnp.exp(v2 - v1)
    w1 = grp_gate / (1.0 + tt)
    w2 = grp_gate * tt / (1.0 + tt)
    l8 = lax.broadcasted_iota(I32, (tm, 8), 1)
    route = jnp.where(l8 == 0, i1,
                      jnp.where(l8 == 1, i2,
                                jnp.where(l8 == 2, w1, jnp.where(l8 == 3, w2, 0.0))))
    route_ref[rows, :] = route


def _post(ya, ga, mb, x2d, kv, w_out_a, w_out, xg, w_q, w_o, mg, w_r, b_r):
    tm = TM_POST
    nt = SEQ // tm
    tok = lambda w: pl.BlockSpec((tm, w), lambda i: (i, 0))
    return pl.pallas_call(
        _post_kernel,
        grid=(TOKENS // tm,),
        in_specs=[
            tok(GROUP_WIDTH), tok(D_MODEL), tok(D_MODEL), tok(D_MODEL),
            pl.BlockSpec((N_MEM, D_MODEL), lambda i: (i // nt, 0)),
            pl.BlockSpec((N_MEM, D_MODEL), lambda i: (i // nt, 1)),
            _resident((GROUP_WIDTH, D_MODEL)),
            _resident((D_MODEL, D_MODEL)),
            _resident((1, D_MODEL)),
            _resident((D_MODEL, D_MODEL)),
            _resident((D_MODEL, D_MODEL)),
            _resident((1, D_MODEL)),
            _resident((D_MODEL, LANES)),
            _resident((1, LANES)),
        ],
        out_specs=[tok(D_MODEL), tok(HALF_MODEL), pl.BlockSpec((tm, 8), lambda i: (i, 0))],
        out_shape=[
            jax.ShapeDtypeStruct((TOKENS, D_MODEL), F32),
            jax.ShapeDtypeStruct((TOKENS, HALF_MODEL), I32),
            jax.ShapeDtypeStruct((TOKENS, 8), F32),
        ],
        scratch_shapes=[pltpu.VMEM((tm, D_MODEL), BF16)],
        compiler_params=pltpu.CompilerParams(
            dimension_semantics=("parallel",), vmem_limit_bytes=VMEM_LIMIT),
        name="post",
    )(ya, ga, mb, x2d, kv, kv, w_out_a, w_out, xg, w_q, w_o, mg, w_r, b_r)


def _route_kernel(rt_ref, dest_ref, meta_ref, rank_scr, carry_ref, pstart_ref):
    pss = pl.program_id(0)
    i = pl.program_id(1)
    tl = TL_ROUTE
    ch = SCAN_CHUNK
    ei = lax.broadcasted_iota(I32, (N_EXPERTS, ch), 0).astype(F32)

    @pl.when((pss == 0) & (i == 0))
    def _():
        carry_ref[...] = jnp.zeros_like(carry_ref)

    @pl.when(pss == 0)
    def _():
        ur = lax.broadcasted_iota(I32, (ch, ch), 0)
        uc = lax.broadcasted_iota(I32, (ch, ch), 1)
        upper = jnp.where(ur < uc, 1.0, 0.0).astype(BF16)
        for c in range(tl // ch):
            e1 = rt_ref[0:1, c * ch:(c + 1) * ch]
            e2 = rt_ref[1:2, c * ch:(c + 1) * ch]
            oh1 = e1 == ei
            oh2 = e2 == ei
            oh = jnp.where(oh1 | oh2, 1.0, 0.0)
            cnt = jnp.dot(oh.astype(BF16), upper, preferred_element_type=F32) + carry_ref[:, 0:1]
            rank1 = jnp.sum(jnp.where(oh1, cnt, 0.0), axis=0, keepdims=True)
            rank2 = jnp.sum(jnp.where(oh2, cnt, 0.0), axis=0, keepdims=True)
            col = pl.multiple_of(i * tl + c * ch, ch)
            rank_scr[0:1, pl.ds(col, ch)] = rank1
            rank_scr[1:2, pl.ds(col, ch)] = rank2
            carry_ref[...] = carry_ref[...] + jnp.sum(oh, axis=1, keepdims=True)

    @pl.when((pss == 1) & (i == 0))
    def _():
        counts = carry_ref[...].astype(I32)
        padded = lax.shift_left(lax.shift_right_logical(counts + (ROW_BLOCK - 1),
                                                        int(math.log2(ROW_BLOCK))),
                                int(math.log2(ROW_BLOCK)))
        row = lax.broadcasted_iota(I32, (N_EXPERTS, LANES), 0)
        lane = lax.broadcasted_iota(I32, (N_EXPERTS, LANES), 1)
        pend = padded
        sh = 1
        while sh < N_EXPERTS:
            pend = pend + jnp.where(row >= sh, pltpu.roll(pend, sh, 0), 0)
            sh *= 2
        pstart = pend - padded
        pstart_ref[...] = pstart
        diag = row == lane

        def as_row(x):
            return jnp.sum(jnp.where(diag, x, 0), axis=0, keepdims=True)

        blk_lane = lax.broadcasted_iota(I32, (N_EXPERTS, META_LANES), 1) * ROW_BLOCK
        blk_exp = jnp.sum(jnp.where(pend[:, 0:1] <= blk_lane, 1, 0), axis=0, keepdims=True)
        blk_exp = jnp.minimum(blk_exp, N_EXPERTS - 1)
        meta_ref[...] = jnp.zeros_like(meta_ref)
        meta_ref[0:1, :] = blk_exp
        meta_ref[1:2, 0:LANES] = as_row(counts)
        meta_ref[2:3, 0:LANES] = as_row(pstart)
        meta_ref[3:4, 0:LANES] = as_row(pend)

    @pl.when(pss == 1)
    def _():
        ps = pstart_ref[:, 0:1].astype(F32)
        for c in range(tl // ch):
            e1 = rt_ref[0:1, c * ch:(c + 1) * ch]
            e2 = rt_ref[1:2, c * ch:(c + 1) * ch]
            col = pl.multiple_of(i * tl + c * ch, ch)
            d1 = rank_scr[0:1, pl.ds(col, ch)] + jnp.sum(jnp.where(e1 == ei, ps, 0.0), axis=0, keepdims=True)
            d2 = rank_scr[1:2, pl.ds(col, ch)] + jnp.sum(jnp.where(e2 == ei, ps, 0.0), axis=0, keepdims=True)
            dest_ref[0:1, c * ch:(c + 1) * ch] = d1.astype(I32)
            dest_ref[1:2, c * ch:(c + 1) * ch] = d2.astype(I32)


def _route(rt):
    tl = TL_ROUTE
    return pl.pallas_call(
        _route_kernel,
        grid=(2, TOKENS // tl),
        in_specs=[pl.BlockSpec((8, tl), lambda p, i: (0, i))],
        out_specs=[
            pl.BlockSpec((2, tl), lambda p, i: (0, i * p)),
            pl.BlockSpec((8, META_LANES), lambda p, i: (0, 0)),
        ],
        out_shape=[
            jax.ShapeDtypeStruct((2, TOKENS), I32),
            jax.ShapeDtypeStruct((8, META_LANES), I32),
        ],
        scratch_shapes=[
            pltpu.VMEM((2, TOKENS), F32),
            pltpu.VMEM((N_EXPERTS, LANES), F32),
            pltpu.VMEM((N_EXPERTS, LANES), I32),
        ],
        compiler_params=pltpu.CompilerParams(
            dimension_semantics=("arbitrary", "arbitrary"), vmem_limit_bytes=VMEM_LIMIT),
        name="route",
    )(rt)


SC_CORES = 2
SC_SUBCORES = 16
SC_WORKERS = SC_CORES * SC_SUBCORES
SC_LANES = 16
SC_WINDOW = 64
SC_SCAN_CHUNK = 4096


def _sc_move_rows(table_hbm, idx_v, out_hbm, out_base, n_windows, rows_v, gsem, wsems):
    assert n_windows % 2 == 0 and n_windows >= 2

    def gather(j, b):
        idx = idx_v.at[pl.ds(j * SC_WINDOW, SC_WINDOW)]
        pltpu.async_copy(table_hbm.at[idx], rows_v.at[b], gsem).wait()

    def write(j, b):
        dst = out_hbm.at[pl.ds(out_base + j * SC_WINDOW, SC_WINDOW)]
        return pltpu.make_async_copy(rows_v.at[b], dst, wsems.at[b])

    for b in range(2):
        gather(b, b)
        write(b, b).start()

    @pl.loop(2, n_windows, step=2)
    def _(j):
        for b in range(2):
            write(j - 2 + b, b).wait()
            gather(j + b, b)
            write(j + b, b).start()

    for b in range(2):
        write(n_windows - 2 + b, b).wait()


def _sc_dispatch(hn, dest0, dest1):
    n_tok, width = hn.shape
    rows_per_w = PADDED_ROWS // SC_WORKERS
    assert rows_per_w % SC_WINDOW == 0 and n_tok % SC_SCAN_CHUNK == 0
    mesh = plsc.VectorSubcoreMesh(core_axis_name="c", subcore_axis_name="s")

    @functools.partial(
        pl.kernel, mesh=mesh,
        out_type=jax.ShapeDtypeStruct((PADDED_ROWS, width), hn.dtype),
        scratch_types=[
            pltpu.VMEM((rows_per_w,), I32),
            pltpu.VMEM((SC_SCAN_CHUNK,), I32),
            pltpu.VMEM((2, SC_WINDOW, width), hn.dtype),
            pltpu.SemaphoreType.DMA,
            pltpu.SemaphoreType.DMA((2,)),
        ],
        compiler_params=pltpu.CompilerParams(needs_layout_passes=False),
        name="sc_dispatch",
    )
    def dispatch(hn_hbm, d0_hbm, d1_hbm, xs_hbm, tok_v, dchunk_v, rows_v, gsem, wsems):
        wid = lax.axis_index("s") * SC_CORES + lax.axis_index("c")
        lo = wid * rows_per_w
        lane = lax.iota(I32, SC_LANES)

        @pl.loop(0, rows_per_w // SC_LANES)
        def _(i):
            tok_v[pl.ds(i * SC_LANES, SC_LANES)] = (lo + i * SC_LANES + lane) & (n_tok - 1)

        for d_hbm in (d0_hbm, d1_hbm):
            @pl.loop(0, n_tok // SC_SCAN_CHUNK)
            def _(c, d_hbm=d_hbm):
                pltpu.sync_copy(d_hbm.at[pl.ds(c * SC_SCAN_CHUNK, SC_SCAN_CHUNK)], dchunk_v)

                @pl.loop(0, SC_SCAN_CHUNK // SC_LANES)
                def _(v):
                    local = dchunk_v[pl.ds(v * SC_LANES, SC_LANES)] - lo
                    mine = (local >= 0) & (local < rows_per_w)
                    tok = c * SC_SCAN_CHUNK + v * SC_LANES + lane
                    plsc.store_scatter(tok_v, [jnp.where(mine, local, 0)], tok, mask=mine)

        _sc_move_rows(hn_hbm, tok_v, xs_hbm, lo, rows_per_w // SC_WINDOW, rows_v, gsem, wsems)

    return dispatch(hn, dest0, dest1)


def _experts_kernel(be_ref, nv_ref, xs_ref, wg_ref, wu_ref, wd_ref, yb_ref):
    j = pl.program_id(0)

    @pl.when(j < nv_ref[0])
    def _():
        lo, hi = _unpack_row_halves(xs_ref[...])
        xb = jnp.concatenate([lo.astype(BF16), hi.astype(BF16)], axis=1)
        a = jnp.dot(xb, wg_ref[0].astype(BF16), preferred_element_type=F32)
        b = jnp.dot(xb, wu_ref[0].astype(BF16), preferred_element_type=F32)
        hb = (jax.nn.silu(a) * b).astype(BF16)
        yb_ref[...] = _pack_row_halves(
            jnp.dot(hb, wd_ref[0].astype(BF16), preferred_element_type=F32))

    @pl.when(j >= nv_ref[0])
    def _():
        yb_ref[...] = jnp.zeros_like(yb_ref)


def _experts(blk_expert, n_valid, xs, w_gate, w_up, w_down):
    def row_map(j, be, nv):
        return (jnp.minimum(j, nv[0] - 1), 0)

    def out_map(j, be, nv):
        return (j, 0)

    def w_map(j, be, nv):
        return (be[jnp.minimum(j, nv[0] - 1)], 0, 0)

    grid_spec = pltpu.PrefetchScalarGridSpec(
        num_scalar_prefetch=2,
        grid=(N_ROW_BLOCKS,),
        in_specs=[
            pl.BlockSpec((ROW_BLOCK, HALF_MODEL), row_map),
            pl.BlockSpec((1, D_MODEL, EXPERT_FF), w_map),
            pl.BlockSpec((1, D_MODEL, EXPERT_FF), w_map),
            pl.BlockSpec((1, EXPERT_FF, D_MODEL), w_map),
        ],
        out_specs=pl.BlockSpec((ROW_BLOCK, HALF_MODEL), out_map),
    )
    return pl.pallas_call(
        _experts_kernel,
        grid_spec=grid_spec,
        out_shape=jax.ShapeDtypeStruct((PADDED_ROWS, HALF_MODEL), I32),
        compiler_params=pltpu.CompilerParams(
            dimension_semantics=("arbitrary",), vmem_limit_bytes=VMEM_LIMIT),
        name="experts",
    )(blk_expert, n_valid, xs, w_gate, w_up, w_down)


def _sc_gather_pair(table, idx0, idx1):
    n = idx0.shape[0]
    width = table.shape[1]
    per_w = n // SC_WORKERS
    mesh = plsc.VectorSubcoreMesh(core_axis_name="c", subcore_axis_name="s")
    out = jax.ShapeDtypeStruct((n, width), table.dtype)

    @functools.partial(
        pl.kernel, mesh=mesh, out_type=(out, out),
        scratch_types=[
            pltpu.VMEM((per_w,), I32),
            pltpu.VMEM((2, SC_WINDOW, width), table.dtype),
            pltpu.SemaphoreType.DMA,
            pltpu.SemaphoreType.DMA((2,)),
        ],
        name="sc_gather",
    )
    def gather(table_hbm, idx0_hbm, idx1_hbm, out0_hbm, out1_hbm, idx_v, rows_v, gsem, wsems):
        wid = lax.axis_index("s") * SC_CORES + lax.axis_index("c")
        base = wid * per_w
        for idx_hbm, out_hbm in ((idx0_hbm, out0_hbm), (idx1_hbm, out1_hbm)):
            pltpu.sync_copy(idx_hbm.at[pl.ds(base, per_w)], idx_v)
            _sc_move_rows(table_hbm, idx_v, out_hbm, base, per_w // SC_WINDOW, rows_v, gsem, wsems)

    return gather(table, idx0, idx1)


def _combine_kernel(route_ref, h_ref, g_ref, y0_ref, y1_ref, out_ref):
    w1 = route_ref[:, 2:3]
    w2 = route_ref[:, 3:4]
    lo0, hi0 = _unpack_row_halves(y0_ref[...])
    lo1, hi1 = _unpack_row_halves(y1_ref[...])
    y = jnp.concatenate([lo0 * w1 + lo1 * w2, hi0 * w1 + hi1 * w2], axis=1)
    out_ref[...] = _rms(h_ref[...] + y, g_ref[...])


def _combine(route, h2, g, y0, y1):
    tm = TM_COMBINE
    tok = pl.BlockSpec((tm, D_MODEL), lambda i: (i, 0))
    packed = pl.BlockSpec((tm, HALF_MODEL), lambda i: (i, 0))
    return pl.pallas_call(
        _combine_kernel,
        grid=(TOKENS // tm,),
        in_specs=[pl.BlockSpec((tm, 8), lambda i: (i, 0)), tok, _resident((1, D_MODEL)), packed, packed],
        out_specs=tok,
        out_shape=jax.ShapeDtypeStruct((TOKENS, D_MODEL), F32),
        compiler_params=pltpu.CompilerParams(
            dimension_semantics=("parallel",), vmem_limit_bytes=VMEM_LIMIT),
        name="combine",
    )(route, h2, g, y0, y1)


def kernel(x, mem, positions, mix_norm_g, w_in, b_gates, w_spatial, b_spatial, v_norm_g, v_norm_b,
           w_out_a, w_out_b, w_out, xattn_norm_g, mem_norm_g, w_q_x, w_kv_x, w_o_x, moe_norm_g,
           w_router_grp, b_router_grp, w_router_exp, b_router_exp, w_gate_e, w_up_e, w_down_e,
           final_norm_g):
    assert x.shape == (BATCH, SEQ, D_MODEL) and mem.shape == (BATCH, N_MEM, D_MODEL)
    assert mix_norm_g.shape[0] == 1, "single layer"
    x2d = x.reshape(TOKENS, D_MODEL)
    pos_col = positions.reshape(TOKENS, 1).astype(F32)
    half = HEAD_DIM // 2
    inv_freq = ROPE_THETA ** (-jnp.arange(half, dtype=F32) / half)
    invf = jnp.tile(inv_freq, LANES // half).reshape(1, LANES)
    phase = jnp.tile(jnp.concatenate([jnp.zeros((half,), F32), jnp.full((half,), math.pi / 2, F32)]),
                     LANES // HEAD_DIM).reshape(1, LANES)

    kv = _memkv(mem.reshape(BATCH * N_MEM, D_MODEL), mem_norm_g[0].reshape(1, D_MODEL),
                w_kv_x[0].astype(BF16))

    qkv0, qkv1, qkv2, ga, mb = _inproj(
        x2d, pos_col, invf, phase, mix_norm_g[0].reshape(1, D_MODEL), w_in[0].astype(BF16),
        b_gates[0].reshape(1, 2 * D_MODEL), w_spatial[0], b_spatial[0].T,
        v_norm_g[0].reshape(1, GMLP_WIDTH), v_norm_b[0].reshape(1, GMLP_WIDTH),
        w_out_b[0].astype(BF16))

    ya = _attention((qkv0, qkv1, qkv2)).reshape(TOKENS, GROUP_WIDTH)

    pad = LANES - N_EXPERTS - N_EXPERT_GROUPS
    w_r = jnp.concatenate([w_router_exp[0], w_router_grp[0], jnp.zeros((D_MODEL, pad), F32)], axis=1)
    b_r = jnp.concatenate([b_router_exp[0], b_router_grp[0], jnp.zeros((pad,), F32)]).reshape(1, LANES)
    h2, hn2, route = _post(
        ya, ga, mb, x2d, kv, w_out_a[0].astype(BF16), w_out[0].astype(BF16),
        xattn_norm_g[0].reshape(1, D_MODEL), w_q_x[0].astype(BF16), w_o_x[0].astype(BF16),
        moe_norm_g[0].reshape(1, D_MODEL), w_r.astype(BF16), b_r)

    dest, meta = _route(route.T)
    d0, d1 = dest[0], dest[1]
    xs = _sc_dispatch(hn2, d0, d1)
    blk_expert = meta[0, :N_ROW_BLOCKS]
    n_valid = (meta[3, N_EXPERTS - 1:N_EXPERTS] // ROW_BLOCK).astype(I32)
    yb = _experts(blk_expert, n_valid, xs, w_gate_e[0], w_up_e[0], w_down_e[0])
    y0, y1 = _sc_gather_pair(yb, d0, d1)
    out = _combine(route, h2, final_norm_g.reshape(1, D_MODEL), y0, y1)
    return out.reshape(BATCH, SEQ, D_MODEL)
```

```python
import functools
import math

import jax
import jax.numpy as jnp
from jax import lax
from jax.experimental import pallas as pl
from jax.experimental.pallas import tpu as pltpu
from jax.experimental.pallas import tpu_sc as plsc

F32 = jnp.float32
BF16 = jnp.bfloat16
I32 = jnp.int32

D_MODEL = 1024
BATCH = 16
SEQ = 4096
TOKENS = BATCH * SEQ

HEAD_DIM = 64
DILATIONS = (1, 4, 16)
HEADS_PER_GROUP = 4
GROUP_WIDTH = HEADS_PER_GROUP * HEAD_DIM
ATT_WIDTH = len(DILATIONS) * GROUP_WIDTH
BAND_BLOCK = 128
ROPE_THETA = 10000.0

GMLP_CHUNK = 128
GMLP_GROUPS = 4
GMLP_WIDTH = 512

N_MEM = 256
XATTN_HEADS = 4
XATTN_HEAD_DIM = D_MODEL // XATTN_HEADS

N_EXPERT_GROUPS = 4
EXPERTS_PER_GROUP = 8
N_EXPERTS = 32
TOP_K = 2
EXPERT_FF = 512

RMS_EPS = 1e-6
LN_EPS = 1e-5
NEG_INF = -1e30

LANES = 128

COL_U = 3 * ATT_WIDTH
COL_V = COL_U + GMLP_WIDTH
COL_GA = COL_V + GMLP_WIDTH
COL_GB = COL_GA + D_MODEL

ROW_BLOCK = 512
ASSIGN = TOKENS * TOP_K
PADDED_ROWS = ASSIGN + N_EXPERTS * ROW_BLOCK
N_ROW_BLOCKS = PADDED_ROWS // ROW_BLOCK
META_LANES = ((N_ROW_BLOCKS + LANES - 1) // LANES) * LANES

TM_PROJ = 1024
PROJ_SUB = 512
TM_POST = 1024
POST_SUB = 512
TL_ROUTE = 2048
SCAN_CHUNK = 256
TM_COMBINE = 512
ATTN_UNROLL = 8

VMEM_LIMIT = 56 * 1024 * 1024


def _rms(x, g):
    return x * lax.rsqrt(jnp.mean(x * x, axis=-1, keepdims=True) + RMS_EPS) * g


HALF_MODEL = D_MODEL // 2


def _pack_row_halves(x):
    return pltpu.pack_elementwise([x[:, :HALF_MODEL], x[:, HALF_MODEL:]], packed_dtype=BF16)


def _unpack_row_halves(p):
    lo = pltpu.unpack_elementwise(p, index=0, packed_dtype=BF16, unpacked_dtype=F32)
    hi = pltpu.unpack_elementwise(p, index=1, packed_dtype=BF16, unpacked_dtype=F32)
    return lo, hi


def _resident(shape):
    nd = len(shape)
    return pl.BlockSpec(shape, lambda *_: (0,) * nd, pipeline_mode=pl.Buffered(1))


def _memkv_kernel(mem_ref, g_ref, w_ref, kv_ref):
    mn = _rms(mem_ref[...], g_ref[...]).astype(BF16)
    kv_ref[...] = jnp.dot(mn, w_ref[...], preferred_element_type=F32).astype(BF16)


def _memkv(mem2d, g, w_kv):
    rows = mem2d.shape[0]
    tm = 512
    return pl.pallas_call(
        _memkv_kernel,
        grid=(rows // tm,),
        in_specs=[
            pl.BlockSpec((tm, D_MODEL), lambda i: (i, 0)),
            _resident((1, D_MODEL)),
            _resident((D_MODEL, 2 * D_MODEL)),
        ],
        out_specs=pl.BlockSpec((tm, 2 * D_MODEL), lambda i: (i, 0)),
        out_shape=jax.ShapeDtypeStruct((rows, 2 * D_MODEL), BF16),
        compiler_params=pltpu.CompilerParams(
            dimension_semantics=("parallel",), vmem_limit_bytes=VMEM_LIMIT),
        name="memkv",
    )(mem2d, g, w_kv)


def _inproj_kernel(x_ref, pos_ref, invf_ref, phase_ref, g_ref, w_ref, bg_ref, wsp_ref,
                   bsp_ref, lng_ref, lnb_ref, wob_ref,
                   qkv0_ref, qkv1_ref, qkv2_ref, ga_ref, mb_ref, scr_ref, yb_ref):
    for sub in range(TM_PROJ // PROJ_SUB):
        _inproj_rows(sub, x_ref, pos_ref, invf_ref, phase_ref, g_ref, w_ref, bg_ref, wsp_ref,
                     bsp_ref, lng_ref, lnb_ref, wob_ref,
                     (qkv0_ref, qkv1_ref, qkv2_ref), ga_ref, mb_ref, scr_ref, yb_ref)


def _inproj_rows(sub, x_ref, pos_ref, invf_ref, phase_ref, g_ref, w_ref, bg_ref, wsp_ref,
                 bsp_ref, lng_ref, lnb_ref, wob_ref, out_refs, ga_ref, mb_ref, scr_ref, yb_ref):
    tm = PROJ_SUB
    rows = slice(sub * tm, (sub + 1) * tm)
    xn = _rms(x_ref[rows, :], g_ref[...]).astype(BF16)

    lane = lax.broadcasted_iota(I32, (tm, LANES), 1)
    upper = (lane & 32) != 0
    t1 = jnp.sin(pos_ref[rows, :] * invf_ref[...] + phase_ref[...])
    cosf = jnp.where(upper, t1, pltpu.roll(t1, 96, 1))
    sinf = jnp.where(upper, pltpu.roll(t1, 32, 1), -t1)

    def rope(res):
        outs = []
        for c in range(GROUP_WIDTH // LANES):
            xt = res[:, c * LANES:(c + 1) * LANES]
            rot = jnp.where(upper, pltpu.roll(xt, 32, 1), pltpu.roll(xt, 96, 1))
            outs.append(xt * cosf + rot * sinf)
        return jnp.concatenate(outs, axis=1)

    zu_raw = jnp.dot(xn, w_ref[:, COL_U:COL_V], preferred_element_type=F32)
    zv_raw = jnp.dot(xn, w_ref[:, COL_V:COL_GA], preferred_element_type=F32)

    slabs = GROUP_WIDTH // LANES

    def project_group(gi):
        dil = DILATIONS[gi]
        for which in range(3):
            c0 = which * ATT_WIDTH + gi * GROUP_WIDTH
            res = jnp.dot(xn, w_ref[:, c0:c0 + GROUP_WIDTH], preferred_element_type=F32)
            if which < 2:
                res = rope(res)
            if which == 0:
                res = res * (HEAD_DIM ** -0.5)
            if dil == 1:
                out_refs[gi][0, which, 0, rows, :] = res.astype(BF16)
            else:
                n = tm // dil
                for c in range(slabs):
                    slot = ((sub * 2 + gi - 1) * 3 + which) * slabs + c
                    scr_ref[slot] = res[:, c * LANES:(c + 1) * LANES]
                    for r in range(dil):
                        out_refs[gi][0, which, r, sub * n:(sub + 1) * n, c * LANES:(c + 1) * LANES] = (
                            scr_ref[slot, pl.ds(r, n, stride=dil), :].astype(BF16))

    for gi in range(len(DILATIONS)):
        project_group(gi)

    zu = jax.nn.gelu(zu_raw)
    zv = jax.nn.gelu(zv_raw)
    mu = jnp.mean(zv, axis=-1, keepdims=True)
    zc = zv - mu
    var = jnp.mean(zc * zc, axis=-1, keepdims=True)
    vn = (zc * lax.rsqrt(var + LN_EPS) * lng_ref[...] + lnb_ref[...]).astype(BF16)
    tri_r = lax.broadcasted_iota(I32, (GMLP_CHUNK, GMLP_CHUNK), 0)
    tri_c = lax.broadcasted_iota(I32, (GMLP_CHUNK, GMLP_CHUNK), 1)
    causal = tri_r >= tri_c
    n_chunks = tm // GMLP_CHUNK
    gw = GMLP_WIDTH // GMLP_GROUPS
    for g in range(GMLP_GROUPS):
        wsg = jnp.where(causal, wsp_ref[g], 0.0).astype(BF16)
        vcat = jnp.concatenate(
            [vn[c * GMLP_CHUNK:(c + 1) * GMLP_CHUNK, g * gw:(g + 1) * gw] for c in range(n_chunks)],
            axis=1)
        mixed = jnp.dot(wsg, vcat, preferred_element_type=F32) + bsp_ref[:, g:g + 1]
        for c in range(n_chunks):
            u_blk = zu[c * GMLP_CHUNK:(c + 1) * GMLP_CHUNK, g * gw:(g + 1) * gw]
            r0 = sub * tm + c * GMLP_CHUNK
            yb_ref[r0:r0 + GMLP_CHUNK, g * gw:(g + 1) * gw] = (
                u_blk * mixed[:, c * gw:(c + 1) * gw]).astype(BF16)

    gate_a = jax.nn.sigmoid(
        jnp.dot(xn, w_ref[:, COL_GA:COL_GB], preferred_element_type=F32) + bg_ref[:, :D_MODEL])
    ga_ref[rows, :] = gate_a.astype(BF16)
    gate_b = jax.nn.sigmoid(
        jnp.dot(xn, w_ref[:, COL_GB:COL_GB + D_MODEL], preferred_element_type=F32) + bg_ref[:, D_MODEL:])
    mb_ref[rows, :] = (gate_b * jnp.dot(yb_ref[rows, :], wob_ref[...], preferred_element_type=F32)).astype(BF16)


def _inproj(x2d, pos_col, invf, phase, g, w_in, b_gates, w_spatial, b_spatial_t, ln_g, ln_b, w_out_b):
    tm = TM_PROJ
    nt = SEQ // tm
    in_cols = w_in.shape[1]
    qkv_shapes = [jax.ShapeDtypeStruct((BATCH, 3, d, SEQ // d, GROUP_WIDTH), BF16) for d in DILATIONS]
    qkv_specs = [
        pl.BlockSpec((1, 3, d, tm // d, GROUP_WIDTH), lambda i: (i // nt, 0, 0, i % nt, 0))
        for d in DILATIONS
    ]
    tok_spec = pl.BlockSpec((tm, D_MODEL), lambda i: (i, 0))
    return pl.pallas_call(
        _inproj_kernel,
        grid=(TOKENS // tm,),
        in_specs=[
            tok_spec,
            pl.BlockSpec((tm, 1), lambda i: (i, 0)),
            _resident((1, LANES)),
            _resident((1, LANES)),
            _resident((1, D_MODEL)),
            _resident((D_MODEL, in_cols)),
            _resident((1, 2 * D_MODEL)),
            _resident((GMLP_GROUPS, GMLP_CHUNK, GMLP_CHUNK)),
            _resident((GMLP_CHUNK, GMLP_GROUPS)),
            _resident((1, GMLP_WIDTH)),
            _resident((1, GMLP_WIDTH)),
            _resident((GMLP_WIDTH, D_MODEL)),
        ],
        out_specs=qkv_specs + [tok_spec, tok_spec],
        out_shape=qkv_shapes + [jax.ShapeDtypeStruct((TOKENS, D_MODEL), BF16)] * 2,
        scratch_shapes=[
            pltpu.VMEM((6 * (GROUP_WIDTH // LANES) * (tm // PROJ_SUB), PROJ_SUB, LANES), F32),
            pltpu.VMEM((tm, GMLP_WIDTH), BF16),
        ],
        compiler_params=pltpu.CompilerParams(
            dimension_semantics=("parallel",), vmem_limit_bytes=VMEM_LIMIT),
        name="inproj",
    )(x2d, pos_col, invf, phase, g, w_in, b_gates, w_spatial, b_spatial_t, ln_g, ln_b, w_out_b)


def _attn_kernel(qkv0_ref, qkv1_ref, qkv2_ref, y_ref, acc_ref, m_ref, z_ref, bias_ref):
    blk = BAND_BLOCK
    lane_row = lax.broadcasted_iota(I32, (1, LANES), 1)
    head0_b = jnp.where(lane_row < HEAD_DIM, 1.0, 0.0).astype(BF16)
    head1_b = jnp.where(lane_row < HEAD_DIM, 0.0, 1.0).astype(BF16)
    head0 = lax.broadcasted_iota(I32, (blk, LANES), 1) < HEAD_DIM
    ones_b = jnp.ones((2 * blk, LANES), BF16)

    qi = lax.broadcasted_iota(I32, (2 * blk, 2 * blk), 0) & (blk - 1)
    kc = lax.broadcasted_iota(I32, (2 * blk, 2 * blk), 1)
    for slot, off in enumerate((0, blk)):
        dist = qi + off - kc
        bias_ref[slot] = jnp.where((dist >= 0) & (dist <= blk), 0.0, NEG_INF)

    for gi, (ref, dil) in enumerate(zip((qkv0_ref, qkv1_ref, qkv2_ref), DILATIONS)):
        seq_len = SEQ // dil
        nb = seq_len // blk
        nb_shift = nb.bit_length() - 1

        def body(i, carry, ref=ref, dil=dil, nb=nb, nb_shift=nb_shift, gi=gi):
            r = lax.shift_right_logical(i, nb_shift)
            n = i & (nb - 1)
            q0 = pl.multiple_of(n * blk, blk)
            w0 = pl.multiple_of(jnp.maximum(n - 1, 0) * blk, blk)
            q = ref[0, 0, r, pl.ds(q0, blk), :]
            k = ref[0, 1, r, pl.ds(w0, 2 * blk), :]
            v = ref[0, 2, r, pl.ds(w0, 2 * blk), :]
            q2 = jnp.concatenate([q * head0_b, q * head1_b], axis=0)
            s = lax.dot_general(q2, k, (((1,), (1,)), ((), ())), preferred_element_type=F32)
            s = s + bias_ref[jnp.minimum(n, 1)]
            m2 = jnp.max(s, axis=-1, keepdims=True)
            p = jnp.exp(s - m2)
            v_ext = jnp.concatenate([v, ones_b], axis=1)
            o2 = jnp.dot(p.astype(BF16), v_ext, preferred_element_type=F32)
            o = jnp.where(head0, o2[:blk, :LANES], o2[blk:, :LANES])
            den = jnp.where(head0, o2[:blk, LANES:], o2[blk:, LANES:])
            m = jnp.where(head0, m2[:blk], m2[blk:])
            if gi == 0:
                acc_ref[pl.ds(q0, blk), :] = o
                m_ref[pl.ds(q0, blk), :] = m
                z_ref[pl.ds(q0, blk), :] = den
            else:
                idx = pl.ds(n * (blk * dil) + r, blk, stride=dil)
                m_old = m_ref[idx, :]
                m_new = jnp.maximum(m_old, m)
                e_old = jnp.exp(m_old - m_new)
                e_new = jnp.exp(m - m_new)
                acc_ref[idx, :] = acc_ref[idx, :] * e_old + o * e_new
                z_ref[idx, :] = z_ref[idx, :] * e_old + den * e_new
                m_ref[idx, :] = m_new
            return carry

        lax.fori_loop(0, dil * nb, body, 0, unroll=ATTN_UNROLL)

    y_ref[0] = (acc_ref[...] / z_ref[...]).astype(BF16)


def _attention(qkv):
    in_specs = [
        pl.BlockSpec((1, 3, d, SEQ // d, LANES), lambda b, h: (b, 0, 0, 0, h)) for d in DILATIONS
    ]
    return pl.pallas_call(
        _attn_kernel,
        grid=(BATCH, GROUP_WIDTH // LANES),
        in_specs=in_specs,
        out_specs=pl.BlockSpec((1, SEQ, LANES), lambda b, h: (b, 0, h)),
        out_shape=jax.ShapeDtypeStruct((BATCH, SEQ, GROUP_WIDTH), BF16),
        scratch_shapes=[
            pltpu.VMEM((SEQ, LANES), F32), pltpu.VMEM((SEQ, LANES), F32), pltpu.VMEM((SEQ, LANES), F32),
            pltpu.VMEM((2, 2 * BAND_BLOCK, 2 * BAND_BLOCK), F32),
        ],
        compiler_params=pltpu.CompilerParams(
            dimension_semantics=("parallel", "parallel"), vmem_limit_bytes=VMEM_LIMIT),
        name="attn",
    )(*qkv)


def _post_kernel(ya_ref, ga_ref, mb_ref, x_ref, k_ref, v_ref, woa_ref, wo_ref, xg_ref, wq_ref,
                 wox_ref, mg_ref, wr_ref, br_ref, h_ref, hn_ref, route_ref, o_scr):
    for c in range(TM_POST // POST_SUB):
        rows = slice(c * POST_SUB, (c + 1) * POST_SUB)
        _post_rows(rows, ya_ref, ga_ref, mb_ref, x_ref, k_ref, v_ref, woa_ref, wo_ref, xg_ref, wq_ref,
                   wox_ref, mg_ref, wr_ref, br_ref, h_ref, hn_ref, route_ref, o_scr)


def _post_rows(rows, ya_ref, ga_ref, mb_ref, x_ref, k_ref, v_ref, woa_ref, wo_ref, xg_ref, wq_ref,
               wox_ref, mg_ref, wr_ref, br_ref, h_ref, hn_ref, route_ref, o_scr):
    tm = POST_SUB
    t = jnp.dot(ya_ref[rows, :], woa_ref[...], preferred_element_type=F32)
    merged = (ga_ref[rows, :].astype(F32) * t + mb_ref[rows, :].astype(F32)).astype(BF16)
    h1 = x_ref[rows, :] + jnp.dot(merged, wo_ref[...], preferred_element_type=F32)

    hn = _rms(h1, xg_ref[...]).astype(BF16)
    q = (jnp.dot(hn, wq_ref[...], preferred_element_type=F32) * (XATTN_HEAD_DIM ** -0.5)).astype(BF16)
    hd = XATTN_HEAD_DIM
    for h in range(XATTN_HEADS):
        s = lax.dot_general(q[:, h * hd:(h + 1) * hd], k_ref[:, h * hd:(h + 1) * hd],
                            (((1,), (1,)), ((), ())), preferred_element_type=F32)
        m = jnp.max(s, axis=-1, keepdims=True)
        p = jnp.exp(s - m)
        den = jnp.sum(p, axis=-1, keepdims=True)
        oh = jnp.dot(p.astype(BF16), v_ref[:, h * hd:(h + 1) * hd], preferred_element_type=F32) / den
        o_scr[rows, h * hd:(h + 1) * hd] = oh.astype(BF16)
    h2 = h1 + jnp.dot(o_scr[rows, :], wox_ref[...], preferred_element_type=F32)
    h_ref[rows, :] = h2

    hn2 = _rms(h2, mg_ref[...])
    hn_ref[rows, :] = _pack_row_halves(hn2)

    logits = jnp.dot(hn2.astype(BF16), wr_ref[...], preferred_element_type=F32) + br_ref[...]
    li = lax.broadcasted_iota(I32, (tm, LANES), 1)
    lif = li.astype(F32)
    grp_of_lane = lax.shift_right_logical(li, 3).astype(F32)
    is_grp = (li >= N_EXPERTS) & (li < N_EXPERTS + N_EXPERT_GROUPS)
    gl = jnp.where(is_grp, logits, -jnp.inf)
    gmax = jnp.max(gl, axis=-1, keepdims=True)
    grp = jnp.min(jnp.where(gl == gmax, lif - N_EXPERTS, float(LANES)), axis=-1, keepdims=True)
    gsum = jnp.sum(jnp.where(is_grp, jnp.exp(logits - gmax), 0.0), axis=-1, keepdims=True)
    grp_gate = 1.0 / gsum
    in_grp = grp_of_lane == grp
    el = jnp.where(in_grp, logits, -jnp.inf)
    v1 = jnp.max(el, axis=-1, keepdims=True)
    i1 = jnp.min(jnp.where(el == v1, lif, float(LANES)), axis=-1, keepdims=True)
    el2 = jnp.where(lif == i1, -jnp.inf, el)
    v2 = jnp.max(el2, axis=-1, keepdims=True)
    i2 = jnp.min(jnp.where(el2 == v2, lif, float(LANES)), axis=-1, keepdims=True)
    tt = jnp.exp(v2 - v1)
    w1 = grp_gate / (1.0 + tt)
    w2 = grp_gate * tt / (1.0 + tt)
    l8 = lax.broadcasted_iota(I32, (tm, 8), 1)
    route = jnp.where(l8 == 0, i1,
                      jnp.where(l8 == 1, i2,
                                jnp.where(l8 == 2, w1, jnp.where(l8 == 3, w2, 0.0))))
    route_ref[rows, :] = route


def _post(ya, ga, mb, x2d, kv, w_out_a, w_out, xg, w_q, w_o, mg, w_r, b_r):
    tm = TM_POST
    nt = SEQ // tm
    tok = lambda w: pl.BlockSpec((tm, w), lambda i: (i, 0))
    return pl.pallas_call(
        _post_kernel,
        grid=(TOKENS // tm,),
        in_specs=[
            tok(GROUP_WIDTH), tok(D_MODEL), tok(D_MODEL), tok(D_MODEL),
            pl.BlockSpec((N_MEM, D_MODEL), lambda i: (i // nt, 0)),
            pl.BlockSpec((N_MEM, D_MODEL), lambda i: (i // nt, 1)),
            _resident((GROUP_WIDTH, D_MODEL)),
            _resident((D_MODEL, D_MODEL)),
            _resident((1, D_MODEL)),
            _resident((D_MODEL, D_MODEL)),
            _resident((D_MODEL, D_MODEL)),
            _resident((1, D_MODEL)),
            _resident((D_MODEL, LANES)),
            _resident((1, LANES)),
        ],
        out_specs=[tok(D_MODEL), tok(HALF_MODEL), pl.BlockSpec((tm, 8), lambda i: (i, 0))],
        out_shape=[
            jax.ShapeDtypeStruct((TOKENS, D_MODEL), F32),
            jax.ShapeDtypeStruct((TOKENS, HALF_MODEL), I32),
            jax.ShapeDtypeStruct((TOKENS, 8), F32),
        ],
        scratch_shapes=[pltpu.VMEM((tm, D_MODEL), BF16)],
        compiler_params=pltpu.CompilerParams(
            dimension_semantics=("parallel",), vmem_limit_bytes=VMEM_LIMIT),
        name="post",
    )(ya, ga, mb, x2d, kv, kv, w_out_a, w_out, xg, w_q, w_o, mg, w_r, b_r)


def _route_kernel(rt_ref, dest_ref, meta_ref, rank_scr, carry_ref, pstart_ref):
    pss = pl.program_id(0)
    i = pl.program_id(1)
    tl = TL_ROUTE
    ch = SCAN_CHUNK
    ei = lax.broadcasted_iota(I32, (N_EXPERTS, ch), 0).astype(F32)

    @pl.when((pss == 0) & (i == 0))
    def _():
        carry_ref[...] = jnp.zeros_like(carry_ref)

    @pl.when(pss == 0)
    def _():
        ur = lax.broadcasted_iota(I32, (ch, ch), 0)
        uc = lax.broadcasted_iota(I32, (ch, ch), 1)
        upper = jnp.where(ur < uc, 1.0, 0.0).astype(BF16)
        for c in range(tl // ch):
            e1 = rt_ref[0:1, c * ch:(c + 1) * ch]
            e2 = rt_ref[1:2, c * ch:(c + 1) * ch]
            oh1 = e1 == ei
            oh2 = e2 == ei
            oh = jnp.where(oh1 | oh2, 1.0, 0.0)
            cnt = jnp.dot(oh.astype(BF16), upper, preferred_element_type=F32) + carry_ref[:, 0:1]
            rank1 = jnp.sum(jnp.where(oh1, cnt, 0.0), axis=0, keepdims=True)
            rank2 = jnp.sum(jnp.where(oh2, cnt, 0.0), axis=0, keepdims=True)
            col = pl.multiple_of(i * tl + c * ch, ch)
            rank_scr[0:1, pl.ds(col, ch)] = rank1
            rank_scr[1:2, pl.ds(col, ch)] = rank2
            carry_ref[...] = carry_ref[...] + jnp.sum(oh, axis=1, keepdims=True)

    @pl.when((pss == 1) & (i == 0))
    def _():
        counts = carry_ref[...].astype(I32)
        padded = lax.shift_left(lax.shift_right_logical(counts + (ROW_BLOCK - 1),
                                                        int(math.log2(ROW_BLOCK))),
                                int(math.log2(ROW_BLOCK)))
        row = lax.broadcasted_iota(I32, (N_EXPERTS, LANES), 0)
        lane = lax.broadcasted_iota(I32, (N_EXPERTS, LANES), 1)
        pend = padded
        sh = 1
        while sh < N_EXPERTS:
            pend = pend + jnp.where(row >= sh, pltpu.roll(pend, sh, 0), 0)
            sh *= 2
        pstart = pend - padded
        pstart_ref[...] = pstart
        diag = row == lane

        def as_row(x):
            return jnp.sum(jnp.where(diag, x, 0), axis=0, keepdims=True)

        blk_lane = lax.broadcasted_iota(I32, (N_EXPERTS, META_LANES), 1) * ROW_BLOCK
        blk_exp = jnp.sum(jnp.where(pend[:, 0:1] <= blk_lane, 1, 0), axis=0, keepdims=True)
        blk_exp = jnp.minimum(blk_exp, N_EXPERTS - 1)
        meta_ref[...] = jnp.zeros_like(meta_ref)
        meta_ref[0:1, :] = blk_exp
        meta_ref[1:2, 0:LANES] = as_row(counts)
        meta_ref[2:3, 0:LANES] = as_row(pstart)
        meta_ref[3:4, 0:LANES] = as_row(pend)

    @pl.when(pss == 1)
    def _():
        ps = pstart_ref[:, 0:1].astype(F32)
        for c in range(tl // ch):
            e1 = rt_ref[0:1, c * ch:(c + 1) * ch]
            e2 = rt_ref[1:2, c * ch:(c + 1) * ch]
            col = pl.multiple_of(i * tl + c * ch, ch)
            d1 = rank_scr[0:1, pl.ds(col, ch)] + jnp.sum(jnp.where(e1 == ei, ps, 0.0), axis=0, keepdims=True)
            d2 = rank_scr[1:2, pl.ds(col, ch)] + jnp.sum(jnp.where(e2 == ei, ps, 0.0), axis=0, keepdims=True)
            dest_ref[0:1, c * ch:(c + 1) * ch] = d1.astype(I32)
            dest_ref[1:2, c * ch:(c + 1) * ch] = d2.astype(I32)


def _route(rt):
    tl = TL_ROUTE
    return pl.pallas_call(
        _route_kernel,
        grid=(2, TOKENS // tl),
        in_specs=[pl.BlockSpec((8, tl), lambda p, i: (0, i))],
        out_specs=[
            pl.BlockSpec((2, tl), lambda p, i: (0, i * p)),
            pl.BlockSpec((8, META_LANES), lambda p, i: (0, 0)),
        ],
        out_shape=[
            jax.ShapeDtypeStruct((2, TOKENS), I32),
            jax.ShapeDtypeStruct((8, META_LANES), I32),
        ],
        scratch_shapes=[
            pltpu.VMEM((2, TOKENS), F32),
            pltpu.VMEM((N_EXPERTS, LANES), F32),
            pltpu.VMEM((N_EXPERTS, LANES), I32),
        ],
        compiler_params=pltpu.CompilerParams(
            dimension_semantics=("arbitrary", "arbitrary"), vmem_limit_bytes=VMEM_LIMIT),
        name="route",
    )(rt)


SC_CORES = 2
SC_SUBCORES = 16
SC_WORKERS = SC_CORES * SC_SUBCORES
SC_LANES = 16
SC_WINDOW = 64
SC_SCAN_CHUNK = 4096


def _sc_move_rows(table_hbm, idx_v, out_hbm, out_base, n_windows, rows_v, gsem, wsems):
    assert n_windows % 2 == 0 and n_windows >= 2

    def gather(j, b):
        idx = idx_v.at[pl.ds(j * SC_WINDOW, SC_WINDOW)]
        pltpu.async_copy(table_hbm.at[idx], rows_v.at[b], gsem).wait()

    def write(j, b):
        dst = out_hbm.at[pl.ds(out_base + j * SC_WINDOW, SC_WINDOW)]
        return pltpu.make_async_copy(rows_v.at[b], dst, wsems.at[b])

    for b in range(2):
        gather(b, b)
        write(b, b).start()

    @pl.loop(2, n_windows, step=2)
    def _(j):
        for b in range(2):
            write(j - 2 + b, b).wait()
            gather(j + b, b)
            write(j + b, b).start()

    for b in range(2):
        write(n_windows - 2 + b, b).wait()


def _sc_dispatch(hn, dest0, dest1, row_start, n_rows):
    n_tok, width = hn.shape
    rows_per_w = n_rows // SC_WORKERS
    assert rows_per_w % SC_WINDOW == 0 and n_tok % SC_SCAN_CHUNK == 0
    mesh = plsc.VectorSubcoreMesh(core_axis_name="c", subcore_axis_name="s")

    @functools.partial(
        pl.kernel, mesh=mesh,
        out_type=jax.ShapeDtypeStruct((n_rows, width), hn.dtype),
        scratch_types=[
            pltpu.VMEM((rows_per_w,), I32),
            pltpu.VMEM((SC_SCAN_CHUNK,), I32),
            pltpu.VMEM((2, SC_WINDOW, width), hn.dtype),
            pltpu.SemaphoreType.DMA,
            pltpu.SemaphoreType.DMA((2,)),
        ],
        compiler_params=pltpu.CompilerParams(needs_layout_passes=False),
        name="sc_dispatch",
    )
    def dispatch(hn_hbm, d0_hbm, d1_hbm, xs_hbm, tok_v, dchunk_v, rows_v, gsem, wsems):
        wid = lax.axis_index("s") * SC_CORES + lax.axis_index("c")
        out_lo = wid * rows_per_w
        lo = row_start + out_lo
        lane = lax.iota(I32, SC_LANES)

        @pl.loop(0, rows_per_w // SC_LANES)
        def _(i):
            tok_v[pl.ds(i * SC_LANES, SC_LANES)] = (lo + i * SC_LANES + lane) & (n_tok - 1)

        for d_hbm in (d0_hbm, d1_hbm):
            @pl.loop(0, n_tok // SC_SCAN_CHUNK)
            def _(c, d_hbm=d_hbm):
                pltpu.sync_copy(d_hbm.at[pl.ds(c * SC_SCAN_CHUNK, SC_SCAN_CHUNK)], dchunk_v)

                @pl.loop(0, SC_SCAN_CHUNK // SC_LANES)
                def _(v):
                    local = dchunk_v[pl.ds(v * SC_LANES, SC_LANES)] - lo
                    mine = (local >= 0) & (local < rows_per_w)
                    tok = c * SC_SCAN_CHUNK + v * SC_LANES + lane
                    plsc.store_scatter(tok_v, [jnp.where(mine, local, 0)], tok, mask=mine)

        _sc_move_rows(hn_hbm, tok_v, xs_hbm, out_lo, rows_per_w // SC_WINDOW, rows_v, gsem, wsems)

    return dispatch(hn, dest0, dest1)


def _experts_kernel(blk_start, be_ref, nv_ref, xs_ref, wg_ref, wu_ref, wd_ref, *rest):
    yb_ref = rest[-1]
    j = pl.program_id(0) + blk_start

    @pl.when(j < nv_ref[0])
    def _():
        lo, hi = _unpack_row_halves(xs_ref[...])
        xb = jnp.concatenate([lo.astype(BF16), hi.astype(BF16)], axis=1)
        a = jnp.dot(xb, wg_ref[0].astype(BF16), preferred_element_type=F32)
        b = jnp.dot(xb, wu_ref[0].astype(BF16), preferred_element_type=F32)
        hb = (jax.nn.silu(a) * b).astype(BF16)
        yb_ref[...] = _pack_row_halves(
            jnp.dot(hb, wd_ref[0].astype(BF16), preferred_element_type=F32))

    @pl.when(j >= nv_ref[0])
    def _():
        yb_ref[...] = jnp.zeros_like(yb_ref)


def _experts(blk_expert, n_valid, xs_part, blk_start, yb_prev, w_gate, w_up, w_down):
    n_blocks = xs_part.shape[0] // ROW_BLOCK

    def row_map(j, be, nv):
        return (jnp.clip(jnp.minimum(j + blk_start, nv[0] - 1) - blk_start, 0, n_blocks - 1), 0)

    def out_map(j, be, nv):
        return (j + blk_start, 0)

    def w_map(j, be, nv):
        return (be[jnp.minimum(j + blk_start, nv[0] - 1)], 0, 0)

    in_specs = [
        pl.BlockSpec((ROW_BLOCK, HALF_MODEL), row_map),
        pl.BlockSpec((1, D_MODEL, EXPERT_FF), w_map),
        pl.BlockSpec((1, D_MODEL, EXPERT_FF), w_map),
        pl.BlockSpec((1, EXPERT_FF, D_MODEL), w_map),
    ]
    operands = [blk_expert, n_valid, xs_part, w_gate, w_up, w_down]
    aliases = {}
    if yb_prev is not None:
        in_specs.append(pl.BlockSpec(memory_space=pl.ANY))
        aliases = {len(operands): 0}
        operands.append(yb_prev)
    grid_spec = pltpu.PrefetchScalarGridSpec(
        num_scalar_prefetch=2,
        grid=(n_blocks,),
        in_specs=in_specs,
        out_specs=pl.BlockSpec((ROW_BLOCK, HALF_MODEL), out_map),
    )
    return pl.pallas_call(
        functools.partial(_experts_kernel, blk_start),
        grid_spec=grid_spec,
        out_shape=jax.ShapeDtypeStruct((PADDED_ROWS, HALF_MODEL), I32),
        input_output_aliases=aliases,
        compiler_params=pltpu.CompilerParams(
            dimension_semantics=("arbitrary",), vmem_limit_bytes=VMEM_LIMIT),
        name="experts",
    )(*operands)


def _sc_gather_pair(table, idx0, idx1):
    n = idx0.shape[0]
    width = table.shape[1]
    per_w = n // SC_WORKERS
    mesh = plsc.VectorSubcoreMesh(core_axis_name="c", subcore_axis_name="s")
    out = jax.ShapeDtypeStruct((n, width), table.dtype)

    @functools.partial(
        pl.kernel, mesh=mesh, out_type=(out, out),
        scratch_types=[
            pltpu.VMEM((per_w,), I32),
            pltpu.VMEM((2, SC_WINDOW, width), table.dtype),
            pltpu.SemaphoreType.DMA,
            pltpu.SemaphoreType.DMA((2,)),
        ],
        name="sc_gather",
    )
    def gather(table_hbm, idx0_hbm, idx1_hbm, out0_hbm, out1_hbm, idx_v, rows_v, gsem, wsems):
        wid = lax.axis_index("s") * SC_CORES + lax.axis_index("c")
        base = wid * per_w
        for idx_hbm, out_hbm in ((idx0_hbm, out0_hbm), (idx1_hbm, out1_hbm)):
            pltpu.sync_copy(idx_hbm.at[pl.ds(base, per_w)], idx_v)
            _sc_move_rows(table_hbm, idx_v, out_hbm, base, per_w // SC_WINDOW, rows_v, gsem, wsems)

    return gather(table, idx0, idx1)


def _combine_kernel(route_ref, h_ref, g_ref, y0_ref, y1_ref, *rest):
    out_ref = rest[-1]
    w1 = route_ref[:, 2:3]
    w2 = route_ref[:, 3:4]
    lo0, hi0 = _unpack_row_halves(y0_ref[...])
    lo1, hi1 = _unpack_row_halves(y1_ref[...])
    y = jnp.concatenate([lo0 * w1 + lo1 * w2, hi0 * w1 + hi1 * w2], axis=1)
    out_ref[...] = _rms(h_ref[...] + y, g_ref[...])


def _combine(route, h2, g, y0, y1, tok_start, out_prev):
    tm = TM_COMBINE
    blk0 = tok_start // tm
    glob = lambda w: pl.BlockSpec((tm, w), lambda i: (i + blk0, 0))
    part = pl.BlockSpec((tm, HALF_MODEL), lambda i: (i, 0))
    in_specs = [glob(8), glob(D_MODEL), _resident((1, D_MODEL)), part, part]
    operands = [route, h2, g, y0, y1]
    aliases = {}
    if out_prev is not None:
        in_specs.append(pl.BlockSpec(memory_space=pl.ANY))
        aliases = {len(operands): 0}
        operands.append(out_prev)
    return pl.pallas_call(
        _combine_kernel,
        grid=(y0.shape[0] // tm,),
        in_specs=in_specs,
        out_specs=glob(D_MODEL),
        out_shape=jax.ShapeDtypeStruct((TOKENS, D_MODEL), F32),
        input_output_aliases=aliases,
        compiler_params=pltpu.CompilerParams(
            dimension_semantics=("parallel",), vmem_limit_bytes=VMEM_LIMIT),
        name="combine",
    )(*operands)


def kernel(x, mem, positions, mix_norm_g, w_in, b_gates, w_spatial, b_spatial, v_norm_g, v_norm_b,
           w_out_a, w_out_b, w_out, xattn_norm_g, mem_norm_g, w_q_x, w_kv_x, w_o_x, moe_norm_g,
           w_router_grp, b_router_grp, w_router_exp, b_router_exp, w_gate_e, w_up_e, w_down_e,
           final_norm_g):
    assert x.shape == (BATCH, SEQ, D_MODEL) and mem.shape == (BATCH, N_MEM, D_MODEL)
    assert mix_norm_g.shape[0] == 1, "single layer"
    x2d = x.reshape(TOKENS, D_MODEL)
    pos_col = positions.reshape(TOKENS, 1).astype(F32)
    half = HEAD_DIM // 2
    inv_freq = ROPE_THETA ** (-jnp.arange(half, dtype=F32) / half)
    invf = jnp.tile(inv_freq, LANES // half).reshape(1, LANES)
    phase = jnp.tile(jnp.concatenate([jnp.zeros((half,), F32), jnp.full((half,), math.pi / 2, F32)]),
                     LANES // HEAD_DIM).reshape(1, LANES)

    kv = _memkv(mem.reshape(BATCH * N_MEM, D_MODEL), mem_norm_g[0].reshape(1, D_MODEL),
                w_kv_x[0].astype(BF16))

    qkv0, qkv1, qkv2, ga, mb = _inproj(
        x2d, pos_col, invf, phase, mix_norm_g[0].reshape(1, D_MODEL), w_in[0].astype(BF16),
        b_gates[0].reshape(1, 2 * D_MODEL), w_spatial[0], b_spatial[0].T,
        v_norm_g[0].reshape(1, GMLP_WIDTH), v_norm_b[0].reshape(1, GMLP_WIDTH),
        w_out_b[0].astype(BF16))

    ya = _attention((qkv0, qkv1, qkv2)).reshape(TOKENS, GROUP_WIDTH)

    pad = LANES - N_EXPERTS - N_EXPERT_GROUPS
    w_r = jnp.concatenate([w_router_exp[0], w_router_grp[0], jnp.zeros((D_MODEL, pad), F32)], axis=1)
    b_r = jnp.concatenate([b_router_exp[0], b_router_grp[0], jnp.zeros((pad,), F32)]).reshape(1, LANES)
    h2, hn2, route = _post(
        ya, ga, mb, x2d, kv, w_out_a[0].astype(BF16), w_out[0].astype(BF16),
        xattn_norm_g[0].reshape(1, D_MODEL), w_q_x[0].astype(BF16), w_o_x[0].astype(BF16),
        moe_norm_g[0].reshape(1, D_MODEL), w_r.astype(BF16), b_r)

    dest, meta = _route(route.T)
    d0, d1 = dest[0], dest[1]
    half_rows = PADDED_ROWS // 2
    half_tok = TOKENS // 2
    xs_a = _sc_dispatch(hn2, d0, d1, 0, half_rows)
    xs_b = _sc_dispatch(hn2, d0, d1, half_rows, half_rows)
    blk_expert = meta[0, :N_ROW_BLOCKS]
    n_valid = (meta[3, N_EXPERTS - 1:N_EXPERTS] // ROW_BLOCK).astype(I32)
    yb = _experts(blk_expert, n_valid, xs_a, 0, None, w_gate_e[0], w_up_e[0], w_down_e[0])
    yb = _experts(blk_expert, n_valid, xs_b, N_ROW_BLOCKS // 2, yb, w_gate_e[0], w_up_e[0], w_down_e[0])
    g_fin = final_norm_g.reshape(1, D_MODEL)
    y0a, y1a = _sc_gather_pair(yb, d0[:half_tok], d1[:half_tok])
    y0b, y1b = _sc_gather_pair(yb, d0[half_tok:], d1[half_tok:])
    out = _combine(route, h2, g_fin, y0a, y1a, 0, None)
    out = _combine(route, h2, g_fin, y0b, y1b, half_tok, out)
    return out.reshape(BATCH, SEQ, D_MODEL)
```

```python
import functools
import math

import jax
import jax.numpy as jnp
from jax import lax
from jax.experimental import pallas as pl
from jax.experimental.pallas import tpu as pltpu
from jax.experimental.pallas import tpu_sc as plsc

F32 = jnp.float32
BF16 = jnp.bfloat16
I32 = jnp.int32

D_MODEL = 1024
BATCH = 16
SEQ = 4096
TOKENS = BATCH * SEQ

HEAD_DIM = 64
DILATIONS = (1, 4, 16)
HEADS_PER_GROUP = 4
GROUP_WIDTH = HEADS_PER_GROUP * HEAD_DIM
ATT_WIDTH = len(DILATIONS) * GROUP_WIDTH
BAND_BLOCK = 128
ROPE_THETA = 10000.0

GMLP_CHUNK = 128
GMLP_GROUPS = 4
GMLP_WIDTH = 512

N_MEM = 256
XATTN_HEADS = 4
XATTN_HEAD_DIM = D_MODEL // XATTN_HEADS

N_EXPERT_GROUPS = 4
EXPERTS_PER_GROUP = 8
N_EXPERTS = 32
TOP_K = 2
EXPERT_FF = 512

RMS_EPS = 1e-6
LN_EPS = 1e-5
NEG_INF = -1e30

LANES = 128

COL_U = 3 * ATT_WIDTH
COL_V = COL_U + GMLP_WIDTH
COL_GA = COL_V + GMLP_WIDTH
COL_GB = COL_GA + D_MODEL

ROW_BLOCK = 512
ASSIGN = TOKENS * TOP_K
PADDED_ROWS = ASSIGN + N_EXPERTS * ROW_BLOCK
N_ROW_BLOCKS = PADDED_ROWS // ROW_BLOCK
META_LANES = ((N_ROW_BLOCKS + LANES - 1) // LANES) * LANES

TM_PROJ = 1024
PROJ_SUB = 512
TM_POST = 1024
POST_SUB = 512
TL_ROUTE = 2048
SCAN_CHUNK = 256
TM_COMBINE = 512
ATTN_UNROLL = 8

VMEM_LIMIT = 56 * 1024 * 1024


def _rms(x, g):
    return x * lax.rsqrt(jnp.mean(x * x, axis=-1, keepdims=True) + RMS_EPS) * g


HALF_MODEL = D_MODEL // 2


def _pack_row_halves(x):
    return pltpu.pack_elementwise([x[:, :HALF_MODEL], x[:, HALF_MODEL:]], packed_dtype=BF16)


def _unpack_row_halves(p):
    lo = pltpu.unpack_elementwise(p, index=0, packed_dtype=BF16, unpacked_dtype=F32)
    hi = pltpu.unpack_elementwise(p, index=1, packed_dtype=BF16, unpacked_dtype=F32)
    return lo, hi


def _resident(shape):
    nd = len(shape)
    return pl.BlockSpec(shape, lambda *_: (0,) * nd, pipeline_mode=pl.Buffered(1))


def _memkv_kernel(mem_ref, g_ref, w_ref, kv_ref):
    mn = _rms(mem_ref[...], g_ref[...]).astype(BF16)
    kv_ref[...] = jnp.dot(mn, w_ref[...], preferred_element_type=F32).astype(BF16)


def _memkv(mem2d, g, w_kv):
    rows = mem2d.shape[0]
    tm = 512
    return pl.pallas_call(
        _memkv_kernel,
        grid=(rows // tm,),
        in_specs=[
            pl.BlockSpec((tm, D_MODEL), lambda i: (i, 0)),
            _resident((1, D_MODEL)),
            _resident((D_MODEL, 2 * D_MODEL)),
        ],
        out_specs=pl.BlockSpec((tm, 2 * D_MODEL), lambda i: (i, 0)),
        out_shape=jax.ShapeDtypeStruct((rows, 2 * D_MODEL), BF16),
        compiler_params=pltpu.CompilerParams(
            dimension_semantics=("parallel",), vmem_limit_bytes=VMEM_LIMIT),
        name="memkv",
    )(mem2d, g, w_kv)


def _inproj_kernel(x_ref, pos_ref, invf_ref, phase_ref, g_ref, w_ref, bg_ref, wsp_ref,
                   bsp_ref, lng_ref, lnb_ref, wob_ref,
                   qkv0_ref, qkv1_ref, qkv2_ref, ga_ref, mb_ref, scr_ref, yb_ref):
    for sub in range(TM_PROJ // PROJ_SUB):
        _inproj_rows(sub, x_ref, pos_ref, invf_ref, phase_ref, g_ref, w_ref, bg_ref, wsp_ref,
                     bsp_ref, lng_ref, lnb_ref, wob_ref,
                     (qkv0_ref, qkv1_ref, qkv2_ref), ga_ref, mb_ref, scr_ref, yb_ref)


def _inproj_rows(sub, x_ref, pos_ref, invf_ref, phase_ref, g_ref, w_ref, bg_ref, wsp_ref,
                 bsp_ref, lng_ref, lnb_ref, wob_ref, out_refs, ga_ref, mb_ref, scr_ref, yb_ref):
    tm = PROJ_SUB
    rows = slice(sub * tm, (sub + 1) * tm)
    xn = _rms(x_ref[rows, :], g_ref[...]).astype(BF16)

    lane = lax.broadcasted_iota(I32, (tm, LANES), 1)
    upper = (lane & 32) != 0
    t1 = jnp.sin(pos_ref[rows, :] * invf_ref[...] + phase_ref[...])
    cosf = jnp.where(upper, t1, pltpu.roll(t1, 96, 1))
    sinf = jnp.where(upper, pltpu.roll(t1, 32, 1), -t1)

    def rope(res):
        outs = []
        for c in range(GROUP_WIDTH // LANES):
            xt = res[:, c * LANES:(c + 1) * LANES]
            rot = jnp.where(upper, pltpu.roll(xt, 32, 1), pltpu.roll(xt, 96, 1))
            outs.append(xt * cosf + rot * sinf)
        return jnp.concatenate(outs, axis=1)

    zu_raw = jnp.dot(xn, w_ref[:, COL_U:COL_V], preferred_element_type=F32)
    zv_raw = jnp.dot(xn, w_ref[:, COL_V:COL_GA], preferred_element_type=F32)

    slabs = GROUP_WIDTH // LANES

    def project_group(gi):
        dil = DILATIONS[gi]
        for which in range(3):
            c0 = which * ATT_WIDTH + gi * GROUP_WIDTH
            res = jnp.dot(xn, w_ref[:, c0:c0 + GROUP_WIDTH], preferred_element_type=F32)
            if which < 2:
                res = rope(res)
            if which == 0:
                res = res * (HEAD_DIM ** -0.5)
            if dil == 1:
                out_refs[gi][0, which, 0, rows, :] = res.astype(BF16)
            else:
                n = tm // dil
                for c in range(slabs):
                    slot = ((sub * 2 + gi - 1) * 3 + which) * slabs + c
                    scr_ref[slot] = res[:, c * LANES:(c + 1) * LANES]
                    for r in range(dil):
                        out_refs[gi][0, which, r, sub * n:(sub + 1) * n, c * LANES:(c + 1) * LANES] = (
                            scr_ref[slot, pl.ds(r, n, stride=dil), :].astype(BF16))

    for gi in range(len(DILATIONS)):
        project_group(gi)

    zu = jax.nn.gelu(zu_raw)
    zv = jax.nn.gelu(zv_raw)
    mu = jnp.mean(zv, axis=-1, keepdims=True)
    zc = zv - mu
    var = jnp.mean(zc * zc, axis=-1, keepdims=True)
    vn = (zc * lax.rsqrt(var + LN_EPS) * lng_ref[...] + lnb_ref[...]).astype(BF16)
    tri_r = lax.broadcasted_iota(I32, (GMLP_CHUNK, GMLP_CHUNK), 0)
    tri_c = lax.broadcasted_iota(I32, (GMLP_CHUNK, GMLP_CHUNK), 1)
    causal = tri_r >= tri_c
    n_chunks = tm // GMLP_CHUNK
    gw = GMLP_WIDTH // GMLP_GROUPS
    for g in range(GMLP_GROUPS):
        wsg = jnp.where(causal, wsp_ref[g], 0.0).astype(BF16)
        vcat = jnp.concatenate(
            [vn[c * GMLP_CHUNK:(c + 1) * GMLP_CHUNK, g * gw:(g + 1) * gw] for c in range(n_chunks)],
            axis=1)
        mixed = jnp.dot(wsg, vcat, preferred_element_type=F32) + bsp_ref[:, g:g + 1]
        for c in range(n_chunks):
            u_blk = zu[c * GMLP_CHUNK:(c + 1) * GMLP_CHUNK, g * gw:(g + 1) * gw]
            r0 = sub * tm + c * GMLP_CHUNK
            yb_ref[r0:r0 + GMLP_CHUNK, g * gw:(g + 1) * gw] = (
                u_blk * mixed[:, c * gw:(c + 1) * gw]).astype(BF16)

    gate_a = jax.nn.sigmoid(
        jnp.dot(xn, w_ref[:, COL_GA:COL_GB], preferred_element_type=F32) + bg_ref[:, :D_MODEL])
    ga_ref[rows, :] = gate_a.astype(BF16)
    gate_b = jax.nn.sigmoid(
        jnp.dot(xn, w_ref[:, COL_GB:COL_GB + D_MODEL], preferred_element_type=F32) + bg_ref[:, D_MODEL:])
    mb_ref[rows, :] = (gate_b * jnp.dot(yb_ref[rows, :], wob_ref[...], preferred_element_type=F32)).astype(BF16)


def _inproj(x2d, pos_col, invf, phase, g, w_in, b_gates, w_spatial, b_spatial_t, ln_g, ln_b, w_out_b):
    tm = TM_PROJ
    nt = SEQ // tm
    in_cols = w_in.shape[1]
    qkv_shapes = [jax.ShapeDtypeStruct((BATCH, 3, d, SEQ // d, GROUP_WIDTH), BF16) for d in DILATIONS]
    qkv_specs = [
        pl.BlockSpec((1, 3, d, tm // d, GROUP_WIDTH), lambda i: (i // nt, 0, 0, i % nt, 0))
        for d in DILATIONS
    ]
    tok_spec = pl.BlockSpec((tm, D_MODEL), lambda i: (i, 0))
    return pl.pallas_call(
        _inproj_kernel,
        grid=(TOKENS // tm,),
        in_specs=[
            tok_spec,
            pl.BlockSpec((tm, 1), lambda i: (i, 0)),
            _resident((1, LANES)),
            _resident((1, LANES)),
            _resident((1, D_MODEL)),
            _resident((D_MODEL, in_cols)),
            _resident((1, 2 * D_MODEL)),
            _resident((GMLP_GROUPS, GMLP_CHUNK, GMLP_CHUNK)),
            _resident((GMLP_CHUNK, GMLP_GROUPS)),
            _resident((1, GMLP_WIDTH)),
            _resident((1, GMLP_WIDTH)),
            _resident((GMLP_WIDTH, D_MODEL)),
        ],
        out_specs=qkv_specs + [tok_spec, tok_spec],
        out_shape=qkv_shapes + [jax.ShapeDtypeStruct((TOKENS, D_MODEL), BF16)] * 2,
        scratch_shapes=[
            pltpu.VMEM((6 * (GROUP_WIDTH // LANES) * (tm // PROJ_SUB), PROJ_SUB, LANES), F32),
            pltpu.VMEM((tm, GMLP_WIDTH), BF16),
        ],
        compiler_params=pltpu.CompilerParams(
            dimension_semantics=("parallel",), vmem_limit_bytes=VMEM_LIMIT),
        name="inproj",
    )(x2d, pos_col, invf, phase, g, w_in, b_gates, w_spatial, b_spatial_t, ln_g, ln_b, w_out_b)


def _attn_kernel(qkv0_ref, qkv1_ref, qkv2_ref, y_ref, acc_ref, m_ref, z_ref, bias_ref):
    blk = BAND_BLOCK
    lane_row = lax.broadcasted_iota(I32, (1, LANES), 1)
    head0_b = jnp.where(lane_row < HEAD_DIM, 1.0, 0.0).astype(BF16)
    head1_b = jnp.where(lane_row < HEAD_DIM, 0.0, 1.0).astype(BF16)
    head0 = lax.broadcasted_iota(I32, (blk, LANES), 1) < HEAD_DIM
    ones_b = jnp.ones((2 * blk, LANES), BF16)

    qi = lax.broadcasted_iota(I32, (2 * blk, 2 * blk), 0) & (blk - 1)
    kc = lax.broadcasted_iota(I32, (2 * blk, 2 * blk), 1)
    for slot, off in enumerate((0, blk)):
        dist = qi + off - kc
        bias_ref[slot] = jnp.where((dist >= 0) & (dist <= blk), 0.0, NEG_INF)

    for gi, (ref, dil) in enumerate(zip((qkv0_ref, qkv1_ref, qkv2_ref), DILATIONS)):
        seq_len = SEQ // dil
        nb = seq_len // blk
        nb_shift = nb.bit_length() - 1

        def body(i, carry, ref=ref, dil=dil, nb=nb, nb_shift=nb_shift, gi=gi):
            r = lax.shift_right_logical(i, nb_shift)
            n = i & (nb - 1)
            q0 = pl.multiple_of(n * blk, blk)
            w0 = pl.multiple_of(jnp.maximum(n - 1, 0) * blk, blk)
            q = ref[0, 0, r, pl.ds(q0, blk), :]
            k = ref[0, 1, r, pl.ds(w0, 2 * blk), :]
            v = ref[0, 2, r, pl.ds(w0, 2 * blk), :]
            q2 = jnp.concatenate([q * head0_b, q * head1_b], axis=0)
            s = lax.dot_general(q2, k, (((1,), (1,)), ((), ())), preferred_element_type=F32)
            s = s + bias_ref[jnp.minimum(n, 1)]
            m2 = jnp.max(s, axis=-1, keepdims=True)
            p = jnp.exp(s - m2)
            v_ext = jnp.concatenate([v, ones_b], axis=1)
            o2 = jnp.dot(p.astype(BF16), v_ext, preferred_element_type=F32)
            o = jnp.where(head0, o2[:blk, :LANES], o2[blk:, :LANES])
            den = jnp.where(head0, o2[:blk, LANES:], o2[blk:, LANES:])
            m = jnp.where(head0, m2[:blk], m2[blk:])
            if gi == 0:
                acc_ref[pl.ds(q0, blk), :] = o
                m_ref[pl.ds(q0, blk), :] = m
                z_ref[pl.ds(q0, blk), :] = den
            else:
                idx = pl.ds(n * (blk * dil) + r, blk, stride=dil)
                m_old = m_ref[idx, :]
                m_new = jnp.maximum(m_old, m)
                e_old = jnp.exp(m_old - m_new)
                e_new = jnp.exp(m - m_new)
                acc_ref[idx, :] = acc_ref[idx, :] * e_old + o * e_new
                z_ref[idx, :] = z_ref[idx, :] * e_old + den * e_new
                m_ref[idx, :] = m_new
            return carry

        lax.fori_loop(0, dil * nb, body, 0, unroll=ATTN_UNROLL)

    y_ref[0] = (acc_ref[...] / z_ref[...]).astype(BF16)


def _attention(qkv):
    in_specs = [
        pl.BlockSpec((1, 3, d, SEQ // d, LANES), lambda b, h: (b, 0, 0, 0, h)) for d in DILATIONS
    ]
    return pl.pallas_call(
        _attn_kernel,
        grid=(BATCH, GROUP_WIDTH // LANES),
        in_specs=in_specs,
        out_specs=pl.BlockSpec((1, SEQ, LANES), lambda b, h: (b, 0, h)),
        out_shape=jax.ShapeDtypeStruct((BATCH, SEQ, GROUP_WIDTH), BF16),
        scratch_shapes=[
            pltpu.VMEM((SEQ, LANES), F32), pltpu.VMEM((SEQ, LANES), F32), pltpu.VMEM((SEQ, LANES), F32),
            pltpu.VMEM((2, 2 * BAND_BLOCK, 2 * BAND_BLOCK), F32),
        ],
        compiler_params=pltpu.CompilerParams(
            dimension_semantics=("parallel", "parallel"), vmem_limit_bytes=VMEM_LIMIT),
        name="attn",
    )(*qkv)


def _post_kernel(ya_ref, ga_ref, mb_ref, x_ref, k_ref, v_ref, woa_ref, wo_ref, xg_ref, wq_ref,
                 wox_ref, mg_ref, wr_ref, br_ref, h_ref, hn_ref, route_ref, o_scr):
    for c in range(TM_POST // POST_SUB):
        rows = slice(c * POST_SUB, (c + 1) * POST_SUB)
        _post_rows(rows, ya_ref, ga_ref, mb_ref, x_ref, k_ref, v_ref, woa_ref, wo_ref, xg_ref, wq_ref,
                   wox_ref, mg_ref, wr_ref, br_ref, h_ref, hn_ref, route_ref, o_scr)


def _post_rows(rows, ya_ref, ga_ref, mb_ref, x_ref, k_ref, v_ref, woa_ref, wo_ref, xg_ref, wq_ref,
               wox_ref, mg_ref, wr_ref, br_ref, h_ref, hn_ref, route_ref, o_scr):
    tm = POST_SUB
    t = jnp.dot(ya_ref[rows, :], woa_ref[...], preferred_element_type=F32)
    merged = (ga_ref[rows, :].astype(F32) * t + mb_ref[rows, :].astype(F32)).astype(BF16)
    h1 = x_ref[rows, :] + jnp.dot(merged, wo_ref[...], preferred_element_type=F32)

    hn = _rms(h1, xg_ref[...]).astype(BF16)
    q = (jnp.dot(hn, wq_ref[...], preferred_element_type=F32) * (XATTN_HEAD_DIM ** -0.5)).astype(BF16)
    hd = XATTN_HEAD_DIM
    for h in range(XATTN_HEADS):
        s = lax.dot_general(q[:, h * hd:(h + 1) * hd], k_ref[:, h * hd:(h + 1) * hd],
                            (((1,), (1,)), ((), ())), preferred_element_type=F32)
        m = jnp.max(s, axis=-1, keepdims=True)
        p = jnp.exp(s - m)
        den = jnp.sum(p, axis=-1, keepdims=True)
        oh = jnp.dot(p.astype(BF16), v_ref[:, h * hd:(h + 1) * hd], preferred_element_type=F32) / den
        o_scr[rows, h * hd:(h + 1) * hd] = oh.astype(BF16)
    h2 = h1 + jnp.dot(o_scr[rows, :], wox_ref[...], preferred_element_type=F32)
    h_ref[rows, :] = h2

    hn2 = _rms(h2, mg_ref[...])
    hn_ref[rows, :] = _pack_row_halves(hn2)

    logits = jnp.dot(hn2.astype(BF16), wr_ref[...], preferred_element_type=F32) + br_ref[...]
    li = lax.broadcasted_iota(I32, (tm, LANES), 1)
    lif = li.astype(F32)
    grp_of_lane = lax.shift_right_logical(li, 3).astype(F32)
    is_grp = (li >= N_EXPERTS) & (li < N_EXPERTS + N_EXPERT_GROUPS)
    gl = jnp.where(is_grp, logits, -jnp.inf)
    gmax = jnp.max(gl, axis=-1, keepdims=True)
    grp = jnp.min(jnp.where(gl == gmax, lif - N_EXPERTS, float(LANES)), axis=-1, keepdims=True)
    gsum = jnp.sum(jnp.where(is_grp, jnp.exp(logits - gmax), 0.0), axis=-1, keepdims=True)
    grp_gate = 1.0 / gsum
    in_grp = grp_of_lane == grp
    el = jnp.where(in_grp, logits, -jnp.inf)
    v1 = jnp.max(el, axis=-1, keepdims=True)
    i1 = jnp.min(jnp.where(el == v1, lif, float(LANES)), axis=-1, keepdims=True)
    el2 = jnp.where(lif == i1, -jnp.inf, el)
    v2 = jnp.max(el2, axis=-1, keepdims=True)
    i2 = jnp.min(jnp.where(el2 == v2, lif, float(LANES)), axis=-1, keepdims=True)
    tt = jnp.exp(v2 - v1)
    w1 = grp_gate / (1.0 + tt)
    w2 = grp_gate * tt / (1.0 + tt)
    l8 = lax.broadcasted_iota(I32, (tm, 8), 1)
    route = jnp.where(l8 == 0, i1,
                      jnp.where(l8 == 1, i2,
                                jnp.where(l8 == 2, w1, jnp.where(l8 == 3, w2, 0.0))))
    route_ref[rows, :] = route


def _post(ya, ga, mb, x2d, kv, w_out_a, w_out, xg, w_q, w_o, mg, w_r, b_r):
    tm = TM_POST
    nt = SEQ // tm
    tok = lambda w: pl.BlockSpec((tm, w), lambda i: (i, 0))
    return pl.pallas_call(
        _post_kernel,
        grid=(TOKENS // tm,),
        in_specs=[
            tok(GROUP_WIDTH), tok(D_MODEL), tok(D_MODEL), tok(D_MODEL),
            pl.BlockSpec((N_MEM, D_MODEL), lambda i: (i // nt, 0)),
            pl.BlockSpec((N_MEM, D_MODEL), lambda i: (i // nt, 1)),
            _resident((GROUP_WIDTH, D_MODEL)),
            _resident((D_MODEL, D_MODEL)),
            _resident((1, D_MODEL)),
            _resident((D_MODEL, D_MODEL)),
            _resident((D_MODEL, D_MODEL)),
            _resident((1, D_MODEL)),
            _resident((D_MODEL, LANES)),
            _resident((1, LANES)),
        ],
        out_specs=[tok(D_MODEL), tok(HALF_MODEL), pl.BlockSpec((tm, 8), lambda i: (i, 0))],
        out_shape=[
            jax.ShapeDtypeStruct((TOKENS, D_MODEL), F32),
            jax.ShapeDtypeStruct((TOKENS, HALF_MODEL), I32),
            jax.ShapeDtypeStruct((TOKENS, 8), F32),
        ],
        scratch_shapes=[pltpu.VMEM((tm, D_MODEL), BF16)],
        compiler_params=pltpu.CompilerParams(
            dimension_semantics=("parallel",), vmem_limit_bytes=VMEM_LIMIT),
        name="post",
    )(ya, ga, mb, x2d, kv, kv, w_out_a, w_out, xg, w_q, w_o, mg, w_r, b_r)


def _route_kernel(rt_ref, dest_ref, meta_ref, rank_scr, carry_ref, pstart_ref):
    pss = pl.program_id(0)
    i = pl.program_id(1)
    tl = TL_ROUTE
    ch = SCAN_CHUNK
    ei = lax.broadcasted_iota(I32, (N_EXPERTS, ch), 0).astype(F32)

    @pl.when((pss == 0) & (i == 0))
    def _():
        carry_ref[...] = jnp.zeros_like(carry_ref)

    @pl.when(pss == 0)
    def _():
        ur = lax.broadcasted_iota(I32, (ch, ch), 0)
        uc = lax.broadcasted_iota(I32, (ch, ch), 1)
        upper = jnp.where(ur < uc, 1.0, 0.0).astype(BF16)
        for c in range(tl // ch):
            e1 = rt_ref[0:1, c * ch:(c + 1) * ch]
            e2 = rt_ref[1:2, c * ch:(c + 1) * ch]
            oh1 = e1 == ei
            oh2 = e2 == ei
            oh = jnp.where(oh1 | oh2, 1.0, 0.0)
            cnt = jnp.dot(oh.astype(BF16), upper, preferred_element_type=F32) + carry_ref[:, 0:1]
            rank1 = jnp.sum(jnp.where(oh1, cnt, 0.0), axis=0, keepdims=True)
            rank2 = jnp.sum(jnp.where(oh2, cnt, 0.0), axis=0, keepdims=True)
            col = pl.multiple_of(i * tl + c * ch, ch)
            rank_scr[0:1, pl.ds(col, ch)] = rank1
            rank_scr[1:2, pl.ds(col, ch)] = rank2
            carry_ref[...] = carry_ref[...] + jnp.sum(oh, axis=1, keepdims=True)

    @pl.when((pss == 1) & (i == 0))
    def _():
        counts = carry_ref[...].astype(I32)
        padded = lax.shift_left(lax.shift_right_logical(counts + (ROW_BLOCK - 1),
                                                        int(math.log2(ROW_BLOCK))),
                                int(math.log2(ROW_BLOCK)))
        row = lax.broadcasted_iota(I32, (N_EXPERTS, LANES), 0)
        lane = lax.broadcasted_iota(I32, (N_EXPERTS, LANES), 1)
        pend = padded
        sh = 1
        while sh < N_EXPERTS:
            pend = pend + jnp.where(row >= sh, pltpu.roll(pend, sh, 0), 0)
            sh *= 2
        pstart = pend - padded
        pstart_ref[...] = pstart
        diag = row == lane

        def as_row(x):
            return jnp.sum(jnp.where(diag, x, 0), axis=0, keepdims=True)

        blk_lane = lax.broadcasted_iota(I32, (N_EXPERTS, META_LANES), 1) * ROW_BLOCK
        blk_exp = jnp.sum(jnp.where(pend[:, 0:1] <= blk_lane, 1, 0), axis=0, keepdims=True)
        blk_exp = jnp.minimum(blk_exp, N_EXPERTS - 1)
        meta_ref[...] = jnp.zeros_like(meta_ref)
        meta_ref[0:1, :] = blk_exp
        meta_ref[1:2, 0:LANES] = as_row(counts)
        meta_ref[2:3, 0:LANES] = as_row(pstart)
        meta_ref[3:4, 0:LANES] = as_row(pend)

    @pl.when(pss == 1)
    def _():
        ps = pstart_ref[:, 0:1].astype(F32)
        for c in range(tl // ch):
            e1 = rt_ref[0:1, c * ch:(c + 1) * ch]
            e2 = rt_ref[1:2, c * ch:(c + 1) * ch]
            col = pl.multiple_of(i * tl + c * ch, ch)
            d1 = rank_scr[0:1, pl.ds(col, ch)] + jnp.sum(jnp.where(e1 == ei, ps, 0.0), axis=0, keepdims=True)
            d2 = rank_scr[1:2, pl.ds(col, ch)] + jnp.sum(jnp.where(e2 == ei, ps, 0.0), axis=0, keepdims=True)
            dest_ref[0:1, c * ch:(c + 1) * ch] = d1.astype(I32)
            dest_ref[1:2, c * ch:(c + 1) * ch] = d2.astype(I32)


def _route(rt):
    tl = TL_ROUTE
    return pl.pallas_call(
        _route_kernel,
        grid=(2, TOKENS // tl),
        in_specs=[pl.BlockSpec((8, tl), lambda p, i: (0, i))],
        out_specs=[
            pl.BlockSpec((2, tl), lambda p, i: (0, i * p)),
            pl.BlockSpec((8, META_LANES), lambda p, i: (0, 0)),
        ],
        out_shape=[
            jax.ShapeDtypeStruct((2, TOKENS), I32),
            jax.ShapeDtypeStruct((8, META_LANES), I32),
        ],
        scratch_shapes=[
            pltpu.VMEM((2, TOKENS), F32),
            pltpu.VMEM((N_EXPERTS, LANES), F32),
            pltpu.VMEM((N_EXPERTS, LANES), I32),
        ],
        compiler_params=pltpu.CompilerParams(
            dimension_semantics=("arbitrary", "arbitrary"), vmem_limit_bytes=VMEM_LIMIT),
        name="route",
    )(rt)


SC_CORES = 2
SC_SUBCORES = 16
SC_WORKERS = SC_CORES * SC_SUBCORES
SC_LANES = 16
SC_WINDOW = 64
SC_SCAN_CHUNK = 32768
DISPATCH_PARTS = 4


def _sc_move_rows(table_hbm, idx_v, out_hbm, out_base, n_windows, rows_v, gsem, wsems):
    assert n_windows % 2 == 0 and n_windows >= 2

    def gather(j, b):
        idx = idx_v.at[pl.ds(j * SC_WINDOW, SC_WINDOW)]
        pltpu.async_copy(table_hbm.at[idx], rows_v.at[b], gsem).wait()

    def write(j, b):
        dst = out_hbm.at[pl.ds(out_base + j * SC_WINDOW, SC_WINDOW)]
        return pltpu.make_async_copy(rows_v.at[b], dst, wsems.at[b])

    for b in range(2):
        gather(b, b)
        write(b, b).start()

    @pl.loop(2, n_windows, step=2)
    def _(j):
        for b in range(2):
            write(j - 2 + b, b).wait()
            gather(j + b, b)
            write(j + b, b).start()

    for b in range(2):
        write(n_windows - 2 + b, b).wait()


def _sc_inverse_map(dest0, dest1):
    n_tok = dest0.shape[0]
    rows_per_w = PADDED_ROWS // SC_WORKERS
    assert rows_per_w % SC_LANES == 0 and n_tok % SC_SCAN_CHUNK == 0
    mesh = plsc.VectorSubcoreMesh(core_axis_name="c", subcore_axis_name="s")

    @functools.partial(
        pl.kernel, mesh=mesh,
        out_type=jax.ShapeDtypeStruct((PADDED_ROWS,), I32),
        scratch_types=[pltpu.VMEM((rows_per_w,), I32), pltpu.VMEM((SC_SCAN_CHUNK,), I32)],
        compiler_params=pltpu.CompilerParams(needs_layout_passes=False),
        name="sc_inverse_map",
    )
    def inverse_map(d0_hbm, d1_hbm, tok_hbm, tok_v, dchunk_v):
        wid = lax.axis_index("s") * SC_CORES + lax.axis_index("c")
        lo = wid * rows_per_w
        lane = lax.iota(I32, SC_LANES)

        @pl.loop(0, rows_per_w // SC_LANES)
        def _(i):
            tok_v[pl.ds(i * SC_LANES, SC_LANES)] = (lo + i * SC_LANES + lane) & (n_tok - 1)

        for d_hbm in (d0_hbm, d1_hbm):
            @pl.loop(0, n_tok // SC_SCAN_CHUNK)
            def _(c, d_hbm=d_hbm):
                pltpu.sync_copy(d_hbm.at[pl.ds(c * SC_SCAN_CHUNK, SC_SCAN_CHUNK)], dchunk_v)

                @plsc.parallel_loop(0, SC_SCAN_CHUNK // SC_LANES, unroll=4)
                def _(v):
                    local = dchunk_v[pl.ds(v * SC_LANES, SC_LANES)] - lo
                    mine = (local >= 0) & (local < rows_per_w)
                    tok = c * SC_SCAN_CHUNK + v * SC_LANES + lane
                    plsc.store_scatter(tok_v, [jnp.where(mine, local, 0)], tok, mask=mine)

        pltpu.sync_copy(tok_v, tok_hbm.at[pl.ds(lo, rows_per_w)])

    return inverse_map(dest0, dest1)


def _sc_gather(table, idxs):
    n = idxs[0].shape[0]
    width = table.shape[1]
    per_w = n // SC_WORKERS
    assert per_w % (2 * SC_WINDOW) == 0
    mesh = plsc.VectorSubcoreMesh(core_axis_name="c", subcore_axis_name="s")
    out = jax.ShapeDtypeStruct((n, width), table.dtype)
    k = len(idxs)

    @functools.partial(
        pl.kernel, mesh=mesh, out_type=(out,) * k,
        scratch_types=[
            pltpu.VMEM((per_w,), I32),
            pltpu.VMEM((2, SC_WINDOW, width), table.dtype),
            pltpu.SemaphoreType.DMA,
            pltpu.SemaphoreType.DMA((2,)),
        ],
        name="sc_gather",
    )
    def gather(table_hbm, *refs):
        idx_hbms, out_hbms = refs[:k], refs[k:2 * k]
        idx_v, rows_v, gsem, wsems = refs[2 * k:]
        wid = lax.axis_index("s") * SC_CORES + lax.axis_index("c")
        base = wid * per_w
        for idx_hbm, out_hbm in zip(idx_hbms, out_hbms):
            pltpu.sync_copy(idx_hbm.at[pl.ds(base, per_w)], idx_v)
            _sc_move_rows(table_hbm, idx_v, out_hbm, base, per_w // SC_WINDOW, rows_v, gsem, wsems)

    return gather(table, *idxs)


def _experts_kernel(blk_start, be_ref, nv_ref, xs_ref, wg_ref, wu_ref, wd_ref, *rest):
    yb_ref = rest[-1]
    j = pl.program_id(0) + blk_start

    @pl.when(j < nv_ref[0])
    def _():
        lo, hi = _unpack_row_halves(xs_ref[...])
        xb = jnp.concatenate([lo.astype(BF16), hi.astype(BF16)], axis=1)
        a = jnp.dot(xb, wg_ref[0].astype(BF16), preferred_element_type=F32)
        b = jnp.dot(xb, wu_ref[0].astype(BF16), preferred_element_type=F32)
        hb = (jax.nn.silu(a) * b).astype(BF16)
        yb_ref[...] = _pack_row_halves(
            jnp.dot(hb, wd_ref[0].astype(BF16), preferred_element_type=F32))

    @pl.when(j >= nv_ref[0])
    def _():
        yb_ref[...] = jnp.zeros_like(yb_ref)


def _experts(blk_expert, n_valid, xs_part, blk_start, yb_prev, w_gate, w_up, w_down):
    n_blocks = xs_part.shape[0] // ROW_BLOCK

    def row_map(j, be, nv):
        return (jnp.clip(jnp.minimum(j + blk_start, nv[0] - 1) - blk_start, 0, n_blocks - 1), 0)

    def out_map(j, be, nv):
        return (j + blk_start, 0)

    def w_map(j, be, nv):
        return (be[jnp.minimum(j + blk_start, nv[0] - 1)], 0, 0)

    in_specs = [
        pl.BlockSpec((ROW_BLOCK, HALF_MODEL), row_map),
        pl.BlockSpec((1, D_MODEL, EXPERT_FF), w_map),
        pl.BlockSpec((1, D_MODEL, EXPERT_FF), w_map),
        pl.BlockSpec((1, EXPERT_FF, D_MODEL), w_map),
    ]
    operands = [blk_expert, n_valid, xs_part, w_gate, w_up, w_down]
    aliases = {}
    if yb_prev is not None:
        in_specs.append(pl.BlockSpec(memory_space=pl.ANY))
        aliases = {len(operands): 0}
        operands.append(yb_prev)
    grid_spec = pltpu.PrefetchScalarGridSpec(
        num_scalar_prefetch=2,
        grid=(n_blocks,),
        in_specs=in_specs,
        out_specs=pl.BlockSpec((ROW_BLOCK, HALF_MODEL), out_map),
    )
    return pl.pallas_call(
        functools.partial(_experts_kernel, blk_start),
        grid_spec=grid_spec,
        out_shape=jax.ShapeDtypeStruct((PADDED_ROWS, HALF_MODEL), I32),
        input_output_aliases=aliases,
        compiler_params=pltpu.CompilerParams(
            dimension_semantics=("arbitrary",), vmem_limit_bytes=VMEM_LIMIT),
        name="experts",
    )(*operands)


def _combine_kernel(route_ref, h_ref, g_ref, y0_ref, y1_ref, *rest):
    out_ref = rest[-1]
    w1 = route_ref[:, 2:3]
    w2 = route_ref[:, 3:4]
    lo0, hi0 = _unpack_row_halves(y0_ref[...])
    lo1, hi1 = _unpack_row_halves(y1_ref[...])
    y = jnp.concatenate([lo0 * w1 + lo1 * w2, hi0 * w1 + hi1 * w2], axis=1)
    out_ref[...] = _rms(h_ref[...] + y, g_ref[...])


def _combine(route, h2, g, y0, y1, tok_start, out_prev):
    tm = TM_COMBINE
    blk0 = tok_start // tm
    glob = lambda w: pl.BlockSpec((tm, w), lambda i: (i + blk0, 0))
    part = pl.BlockSpec((tm, HALF_MODEL), lambda i: (i, 0))
    in_specs = [glob(8), glob(D_MODEL), _resident((1, D_MODEL)), part, part]
    operands = [route, h2, g, y0, y1]
    aliases = {}
    if out_prev is not None:
        in_specs.append(pl.BlockSpec(memory_space=pl.ANY))
        aliases = {len(operands): 0}
        operands.append(out_prev)
    return pl.pallas_call(
        _combine_kernel,
        grid=(y0.shape[0] // tm,),
        in_specs=in_specs,
        out_specs=glob(D_MODEL),
        out_shape=jax.ShapeDtypeStruct((TOKENS, D_MODEL), F32),
        input_output_aliases=aliases,
        compiler_params=pltpu.CompilerParams(
            dimension_semantics=("parallel",), vmem_limit_bytes=VMEM_LIMIT),
        name="combine",
    )(*operands)


def kernel(x, mem, positions, mix_norm_g, w_in, b_gates, w_spatial, b_spatial, v_norm_g, v_norm_b,
           w_out_a, w_out_b, w_out, xattn_norm_g, mem_norm_g, w_q_x, w_kv_x, w_o_x, moe_norm_g,
           w_router_grp, b_router_grp, w_router_exp, b_router_exp, w_gate_e, w_up_e, w_down_e,
           final_norm_g):
    assert x.shape == (BATCH, SEQ, D_MODEL) and mem.shape == (BATCH, N_MEM, D_MODEL)
    assert mix_norm_g.shape[0] == 1, "single layer"
    x2d = x.reshape(TOKENS, D_MODEL)
    pos_col = positions.reshape(TOKENS, 1).astype(F32)
    half = HEAD_DIM // 2
    inv_freq = ROPE_THETA ** (-jnp.arange(half, dtype=F32) / half)
    invf = jnp.tile(inv_freq, LANES // half).reshape(1, LANES)
    phase = jnp.tile(jnp.concatenate([jnp.zeros((half,), F32), jnp.full((half,), math.pi / 2, F32)]),
                     LANES // HEAD_DIM).reshape(1, LANES)

    kv = _memkv(mem.reshape(BATCH * N_MEM, D_MODEL), mem_norm_g[0].reshape(1, D_MODEL),
                w_kv_x[0].astype(BF16))

    qkv0, qkv1, qkv2, ga, mb = _inproj(
        x2d, pos_col, invf, phase, mix_norm_g[0].reshape(1, D_MODEL), w_in[0].astype(BF16),
        b_gates[0].reshape(1, 2 * D_MODEL), w_spatial[0], b_spatial[0].T,
        v_norm_g[0].reshape(1, GMLP_WIDTH), v_norm_b[0].reshape(1, GMLP_WIDTH),
        w_out_b[0].astype(BF16))

    ya = _attention((qkv0, qkv1, qkv2)).reshape(TOKENS, GROUP_WIDTH)

    pad = LANES - N_EXPERTS - N_EXPERT_GROUPS
    w_r = jnp.concatenate([w_router_exp[0], w_router_grp[0], jnp.zeros((D_MODEL, pad), F32)], axis=1)
    b_r = jnp.concatenate([b_router_exp[0], b_router_grp[0], jnp.zeros((pad,), F32)]).reshape(1, LANES)
    h2, hn2, route = _post(
        ya, ga, mb, x2d, kv, w_out_a[0].astype(BF16), w_out[0].astype(BF16),
        xattn_norm_g[0].reshape(1, D_MODEL), w_q_x[0].astype(BF16), w_o_x[0].astype(BF16),
        moe_norm_g[0].reshape(1, D_MODEL), w_r.astype(BF16), b_r)

    dest, meta = _route(route.T)
    d0, d1 = dest[0], dest[1]
    tok_of_row = _sc_inverse_map(d0, d1)
    blk_expert = meta[0, :N_ROW_BLOCKS]
    n_valid = (meta[3, N_EXPERTS - 1:N_EXPERTS] // ROW_BLOCK).astype(I32)
    part_rows = PADDED_ROWS // DISPATCH_PARTS
    xs_parts = [
        _sc_gather(hn2, [tok_of_row[p * part_rows:(p + 1) * part_rows]])[0] for p in range(DISPATCH_PARTS)
    ]
    yb = None
    for p, xs_part in enumerate(xs_parts):
        yb = _experts(blk_expert, n_valid, xs_part, p * (part_rows // ROW_BLOCK), yb,
                      w_gate_e[0], w_up_e[0], w_down_e[0])
    g_fin = final_norm_g.reshape(1, D_MODEL)
    half_tok = TOKENS // 2
    y0a, y1a = _sc_gather(yb, [d0[:half_tok], d1[:half_tok]])
    y0b, y1b = _sc_gather(yb, [d0[half_tok:], d1[half_tok:]])
    out = _combine(route, h2, g_fin, y0a, y1a, 0, None)
    out = _combine(route, h2, g_fin, y0b, y1b, half_tok, out)
    return out.reshape(BATCH, SEQ, D_MODEL)
```

```python
import functools
import math

import jax
import jax.numpy as jnp
from jax import lax
from jax.experimental import pallas as pl
from jax.experimental.pallas import tpu as pltpu
from jax.experimental.pallas import tpu_sc as plsc

F32 = jnp.float32
BF16 = jnp.bfloat16
I32 = jnp.int32

D_MODEL = 1024
BATCH = 16
SEQ = 4096
TOKENS = BATCH * SEQ

HEAD_DIM = 64
DILATIONS = (1, 4, 16)
HEADS_PER_GROUP = 4
GROUP_WIDTH = HEADS_PER_GROUP * HEAD_DIM
ATT_WIDTH = len(DILATIONS) * GROUP_WIDTH
BAND_BLOCK = 128
ROPE_THETA = 10000.0

GMLP_CHUNK = 128
GMLP_GROUPS = 4
GMLP_WIDTH = 512

N_MEM = 256
XATTN_HEADS = 4
XATTN_HEAD_DIM = D_MODEL // XATTN_HEADS

N_EXPERT_GROUPS = 4
EXPERTS_PER_GROUP = 8
N_EXPERTS = 32
TOP_K = 2
EXPERT_FF = 512

RMS_EPS = 1e-6
LN_EPS = 1e-5
NEG_INF = -1e30

LANES = 128

COL_U = 3 * ATT_WIDTH
COL_V = COL_U + GMLP_WIDTH
COL_GA = COL_V + GMLP_WIDTH
COL_GB = COL_GA + D_MODEL

ROW_BLOCK = 512
ASSIGN = TOKENS * TOP_K
PADDED_ROWS = ASSIGN + N_EXPERTS * ROW_BLOCK
N_ROW_BLOCKS = PADDED_ROWS // ROW_BLOCK
META_LANES = ((N_ROW_BLOCKS + LANES - 1) // LANES) * LANES

TM_PROJ = 1024
PROJ_SUB = 512
TM_POST = 1024
POST_SUB = 512
TL_ROUTE = 8192
SCAN_CHUNK = 256
TM_COMBINE = 1024
ATTN_UNROLL = 8

VMEM_LIMIT = 56 * 1024 * 1024


def _rms(x, g):
    return x * lax.rsqrt(jnp.mean(x * x, axis=-1, keepdims=True) + RMS_EPS) * g


HALF_MODEL = D_MODEL // 2


def _pack_row_halves(x):
    return pltpu.pack_elementwise([x[:, :HALF_MODEL], x[:, HALF_MODEL:]], packed_dtype=BF16)


def _unpack_row_halves(p):
    lo = pltpu.unpack_elementwise(p, index=0, packed_dtype=BF16, unpacked_dtype=F32)
    hi = pltpu.unpack_elementwise(p, index=1, packed_dtype=BF16, unpacked_dtype=F32)
    return lo, hi


def _resident(shape):
    nd = len(shape)
    return pl.BlockSpec(shape, lambda *_: (0,) * nd, pipeline_mode=pl.Buffered(1))


def _memkv_kernel(mem_ref, g_ref, w_ref, kv_ref):
    mn = _rms(mem_ref[...], g_ref[...]).astype(BF16)
    kv_ref[...] = jnp.dot(mn, w_ref[...], preferred_element_type=F32).astype(BF16)


def _memkv(mem2d, g, w_kv):
    rows = mem2d.shape[0]
    tm = 512
    return pl.pallas_call(
        _memkv_kernel,
        grid=(rows // tm,),
        in_specs=[
            pl.BlockSpec((tm, D_MODEL), lambda i: (i, 0)),
            _resident((1, D_MODEL)),
            _resident((D_MODEL, 2 * D_MODEL)),
        ],
        out_specs=pl.BlockSpec((tm, 2 * D_MODEL), lambda i: (i, 0)),
        out_shape=jax.ShapeDtypeStruct((rows, 2 * D_MODEL), BF16),
        compiler_params=pltpu.CompilerParams(
            dimension_semantics=("parallel",), vmem_limit_bytes=VMEM_LIMIT),
        name="memkv",
    )(mem2d, g, w_kv)


def _inproj_kernel(x_ref, pos_ref, invf_ref, phase_ref, g_ref, w_ref, bg_ref, wsp_ref,
                   bsp_ref, lng_ref, lnb_ref, wob_ref,
                   qkv0_ref, qkv1_ref, qkv2_ref, ga_ref, mb_ref, scr_ref, yb_ref):
    for sub in range(TM_PROJ // PROJ_SUB):
        _inproj_rows(sub, x_ref, pos_ref, invf_ref, phase_ref, g_ref, w_ref, bg_ref, wsp_ref,
                     bsp_ref, lng_ref, lnb_ref, wob_ref,
                     (qkv0_ref, qkv1_ref, qkv2_ref), ga_ref, mb_ref, scr_ref, yb_ref)


def _inproj_rows(sub, x_ref, pos_ref, invf_ref, phase_ref, g_ref, w_ref, bg_ref, wsp_ref,
                 bsp_ref, lng_ref, lnb_ref, wob_ref, out_refs, ga_ref, mb_ref, scr_ref, yb_ref):
    tm = PROJ_SUB
    rows = slice(sub * tm, (sub + 1) * tm)
    xn = _rms(x_ref[rows, :], g_ref[...]).astype(BF16)

    lane = lax.broadcasted_iota(I32, (tm, LANES), 1)
    upper = (lane & 32) != 0
    t1 = jnp.sin(pos_ref[rows, :] * invf_ref[...] + phase_ref[...])
    cosf = jnp.where(upper, t1, pltpu.roll(t1, 96, 1))
    sinf = jnp.where(upper, pltpu.roll(t1, 32, 1), -t1)

    def rope(res):
        outs = []
        for c in range(GROUP_WIDTH // LANES):
            xt = res[:, c * LANES:(c + 1) * LANES]
            rot = jnp.where(upper, pltpu.roll(xt, 32, 1), pltpu.roll(xt, 96, 1))
            outs.append(xt * cosf + rot * sinf)
        return jnp.concatenate(outs, axis=1)

    zu_raw = jnp.dot(xn, w_ref[:, COL_U:COL_V], preferred_element_type=F32)
    zv_raw = jnp.dot(xn, w_ref[:, COL_V:COL_GA], preferred_element_type=F32)

    slabs = GROUP_WIDTH // LANES

    def project_group(gi):
        dil = DILATIONS[gi]
        for which in range(3):
            c0 = which * ATT_WIDTH + gi * GROUP_WIDTH
            res = jnp.dot(xn, w_ref[:, c0:c0 + GROUP_WIDTH], preferred_element_type=F32)
            if which < 2:
                res = rope(res)
            if which == 0:
                res = res * (HEAD_DIM ** -0.5)
            if dil == 1:
                out_refs[gi][0, which, 0, rows, :] = res.astype(BF16)
            else:
                n = tm // dil
                for c in range(slabs):
                    slot = ((sub * 2 + gi - 1) * 3 + which) * slabs + c
                    scr_ref[slot] = res[:, c * LANES:(c + 1) * LANES]
                    for r in range(dil):
                        out_refs[gi][0, which, r, sub * n:(sub + 1) * n, c * LANES:(c + 1) * LANES] = (
                            scr_ref[slot, pl.ds(r, n, stride=dil), :].astype(BF16))

    for gi in range(len(DILATIONS)):
        project_group(gi)

    zu = jax.nn.gelu(zu_raw)
    zv = jax.nn.gelu(zv_raw)
    mu = jnp.mean(zv, axis=-1, keepdims=True)
    zc = zv - mu
    var = jnp.mean(zc * zc, axis=-1, keepdims=True)
    vn = (zc * lax.rsqrt(var + LN_EPS) * lng_ref[...] + lnb_ref[...]).astype(BF16)
    tri_r = lax.broadcasted_iota(I32, (GMLP_CHUNK, GMLP_CHUNK), 0)
    tri_c = lax.broadcasted_iota(I32, (GMLP_CHUNK, GMLP_CHUNK), 1)
    causal = tri_r >= tri_c
    n_chunks = tm // GMLP_CHUNK
    gw = GMLP_WIDTH // GMLP_GROUPS
    for g in range(GMLP_GROUPS):
        wsg = jnp.where(causal, wsp_ref[g], 0.0).astype(BF16)
        vcat = jnp.concatenate(
            [vn[c * GMLP_CHUNK:(c + 1) * GMLP_CHUNK, g * gw:(g + 1) * gw] for c in range(n_chunks)],
            axis=1)
        mixed = jnp.dot(wsg, vcat, preferred_element_type=F32) + bsp_ref[:, g:g + 1]
        for c in range(n_chunks):
            u_blk = zu[c * GMLP_CHUNK:(c + 1) * GMLP_CHUNK, g * gw:(g + 1) * gw]
            r0 = sub * tm + c * GMLP_CHUNK
            yb_ref[r0:r0 + GMLP_CHUNK, g * gw:(g + 1) * gw] = (
                u_blk * mixed[:, c * gw:(c + 1) * gw]).astype(BF16)

    gate_a = jax.nn.sigmoid(
        jnp.dot(xn, w_ref[:, COL_GA:COL_GB], preferred_element_type=F32) + bg_ref[:, :D_MODEL])
    ga_ref[rows, :] = gate_a.astype(BF16)
    gate_b = jax.nn.sigmoid(
        jnp.dot(xn, w_ref[:, COL_GB:COL_GB + D_MODEL], preferred_element_type=F32) + bg_ref[:, D_MODEL:])
    mb_ref[rows, :] = (gate_b * jnp.dot(yb_ref[rows, :], wob_ref[...], preferred_element_type=F32)).astype(BF16)


def _inproj(x2d, pos_col, invf, phase, g, w_in, b_gates, w_spatial, b_spatial_t, ln_g, ln_b, w_out_b):
    tm = TM_PROJ
    nt = SEQ // tm
    in_cols = w_in.shape[1]
    qkv_shapes = [jax.ShapeDtypeStruct((BATCH, 3, d, SEQ // d, GROUP_WIDTH), BF16) for d in DILATIONS]
    qkv_specs = [
        pl.BlockSpec((1, 3, d, tm // d, GROUP_WIDTH), lambda i: (i // nt, 0, 0, i % nt, 0))
        for d in DILATIONS
    ]
    tok_spec = pl.BlockSpec((tm, D_MODEL), lambda i: (i, 0))
    return pl.pallas_call(
        _inproj_kernel,
        grid=(TOKENS // tm,),
        in_specs=[
            tok_spec,
            pl.BlockSpec((tm, 1), lambda i: (i, 0)),
            _resident((1, LANES)),
            _resident((1, LANES)),
            _resident((1, D_MODEL)),
            _resident((D_MODEL, in_cols)),
            _resident((1, 2 * D_MODEL)),
            _resident((GMLP_GROUPS, GMLP_CHUNK, GMLP_CHUNK)),
            _resident((GMLP_CHUNK, GMLP_GROUPS)),
            _resident((1, GMLP_WIDTH)),
            _resident((1, GMLP_WIDTH)),
            _resident((GMLP_WIDTH, D_MODEL)),
        ],
        out_specs=qkv_specs + [tok_spec, tok_spec],
        out_shape=qkv_shapes + [jax.ShapeDtypeStruct((TOKENS, D_MODEL), BF16)] * 2,
        scratch_shapes=[
            pltpu.VMEM((6 * (GROUP_WIDTH // LANES) * (tm // PROJ_SUB), PROJ_SUB, LANES), F32),
            pltpu.VMEM((tm, GMLP_WIDTH), BF16),
        ],
        compiler_params=pltpu.CompilerParams(
            dimension_semantics=("parallel",), vmem_limit_bytes=VMEM_LIMIT),
        name="inproj",
    )(x2d, pos_col, invf, phase, g, w_in, b_gates, w_spatial, b_spatial_t, ln_g, ln_b, w_out_b)


def _attn_kernel(qkv0_ref, qkv1_ref, qkv2_ref, y_ref, acc_ref, m_ref, z_ref, bias_ref):
    blk = BAND_BLOCK
    lane_row = lax.broadcasted_iota(I32, (1, LANES), 1)
    head0_b = jnp.where(lane_row < HEAD_DIM, 1.0, 0.0).astype(BF16)
    head1_b = jnp.where(lane_row < HEAD_DIM, 0.0, 1.0).astype(BF16)
    head0 = lax.broadcasted_iota(I32, (blk, LANES), 1) < HEAD_DIM
    ones_b = jnp.ones((2 * blk, LANES), BF16)

    qi = lax.broadcasted_iota(I32, (2 * blk, 2 * blk), 0) & (blk - 1)
    kc = lax.broadcasted_iota(I32, (2 * blk, 2 * blk), 1)
    for slot, off in enumerate((0, blk)):
        dist = qi + off - kc
        bias_ref[slot] = jnp.where((dist >= 0) & (dist <= blk), 0.0, NEG_INF)

    for gi, (ref, dil) in enumerate(zip((qkv0_ref, qkv1_ref, qkv2_ref), DILATIONS)):
        seq_len = SEQ // dil
        nb = seq_len // blk
        nb_shift = nb.bit_length() - 1

        def body(i, carry, ref=ref, dil=dil, nb=nb, nb_shift=nb_shift, gi=gi):
            r = lax.shift_right_logical(i, nb_shift)
            n = i & (nb - 1)
            q0 = pl.multiple_of(n * blk, blk)
            w0 = pl.multiple_of(jnp.maximum(n - 1, 0) * blk, blk)
            q = ref[0, 0, r, pl.ds(q0, blk), :]
            k = ref[0, 1, r, pl.ds(w0, 2 * blk), :]
            v = ref[0, 2, r, pl.ds(w0, 2 * blk), :]
            q2 = jnp.concatenate([q * head0_b, q * head1_b], axis=0)
            s = lax.dot_general(q2, k, (((1,), (1,)), ((), ())), preferred_element_type=F32)
            s = s + bias_ref[jnp.minimum(n, 1)]
            m2 = jnp.max(s, axis=-1, keepdims=True)
            p = jnp.exp(s - m2)
            v_ext = jnp.concatenate([v, ones_b], axis=1)
            o2 = jnp.dot(p.astype(BF16), v_ext, preferred_element_type=F32)
            o = jnp.where(head0, o2[:blk, :LANES], o2[blk:, :LANES])
            den = jnp.where(head0, o2[:blk, LANES:], o2[blk:, LANES:])
            m = jnp.where(head0, m2[:blk], m2[blk:])
            if gi == 0:
                acc_ref[pl.ds(q0, blk), :] = o
                m_ref[pl.ds(q0, blk), :] = m
                z_ref[pl.ds(q0, blk), :] = den
            else:
                idx = pl.ds(n * (blk * dil) + r, blk, stride=dil)
                m_old = m_ref[idx, :]
                m_new = jnp.maximum(m_old, m)
                e_old = jnp.exp(m_old - m_new)
                e_new = jnp.exp(m - m_new)
                acc_ref[idx, :] = acc_ref[idx, :] * e_old + o * e_new
                z_ref[idx, :] = z_ref[idx, :] * e_old + den * e_new
                m_ref[idx, :] = m_new
            return carry

        lax.fori_loop(0, dil * nb, body, 0, unroll=ATTN_UNROLL)

    y_ref[0] = (acc_ref[...] / z_ref[...]).astype(BF16)


def _attention(qkv):
    in_specs = [
        pl.BlockSpec((1, 3, d, SEQ // d, LANES), lambda b, h: (b, 0, 0, 0, h)) for d in DILATIONS
    ]
    return pl.pallas_call(
        _attn_kernel,
        grid=(BATCH, GROUP_WIDTH // LANES),
        in_specs=in_specs,
        out_specs=pl.BlockSpec((1, SEQ, LANES), lambda b, h: (b, 0, h)),
        out_shape=jax.ShapeDtypeStruct((BATCH, SEQ, GROUP_WIDTH), BF16),
        scratch_shapes=[
            pltpu.VMEM((SEQ, LANES), F32), pltpu.VMEM((SEQ, LANES), F32), pltpu.VMEM((SEQ, LANES), F32),
            pltpu.VMEM((2, 2 * BAND_BLOCK, 2 * BAND_BLOCK), F32),
        ],
        compiler_params=pltpu.CompilerParams(
            dimension_semantics=("parallel", "parallel"), vmem_limit_bytes=VMEM_LIMIT),
        name="attn",
    )(*qkv)


def _post_kernel(ya_ref, ga_ref, mb_ref, x_ref, k_ref, v_ref, woa_ref, wo_ref, xg_ref, wq_ref,
                 wox_ref, mg_ref, wr_ref, br_ref, h_ref, hn_ref, route_ref, route_t_ref, o_scr):
    for c in range(TM_POST // POST_SUB):
        rows = slice(c * POST_SUB, (c + 1) * POST_SUB)
        _post_rows(rows, ya_ref, ga_ref, mb_ref, x_ref, k_ref, v_ref, woa_ref, wo_ref, xg_ref, wq_ref,
                   wox_ref, mg_ref, wr_ref, br_ref, h_ref, hn_ref, route_ref, route_t_ref, o_scr)


def _post_rows(rows, ya_ref, ga_ref, mb_ref, x_ref, k_ref, v_ref, woa_ref, wo_ref, xg_ref, wq_ref,
               wox_ref, mg_ref, wr_ref, br_ref, h_ref, hn_ref, route_ref, route_t_ref, o_scr):
    tm = POST_SUB
    t = jnp.dot(ya_ref[rows, :], woa_ref[...], preferred_element_type=F32)
    merged = (ga_ref[rows, :].astype(F32) * t + mb_ref[rows, :].astype(F32)).astype(BF16)
    h1 = x_ref[rows, :] + jnp.dot(merged, wo_ref[...], preferred_element_type=F32)

    hn = _rms(h1, xg_ref[...]).astype(BF16)
    q = (jnp.dot(hn, wq_ref[...], preferred_element_type=F32) * (XATTN_HEAD_DIM ** -0.5)).astype(BF16)
    hd = XATTN_HEAD_DIM
    for h in range(XATTN_HEADS):
        s = lax.dot_general(q[:, h * hd:(h + 1) * hd], k_ref[:, h * hd:(h + 1) * hd],
                            (((1,), (1,)), ((), ())), preferred_element_type=F32)
        m = jnp.max(s, axis=-1, keepdims=True)
        p = jnp.exp(s - m)
        den = jnp.sum(p, axis=-1, keepdims=True)
        oh = jnp.dot(p.astype(BF16), v_ref[:, h * hd:(h + 1) * hd], preferred_element_type=F32) / den
        o_scr[rows, h * hd:(h + 1) * hd] = oh.astype(BF16)
    h2 = h1 + jnp.dot(o_scr[rows, :], wox_ref[...], preferred_element_type=F32)
    h_ref[rows, :] = h2

    hn2 = _rms(h2, mg_ref[...])
    hn_ref[rows, :] = _pack_row_halves(hn2)

    logits = jnp.dot(hn2.astype(BF16), wr_ref[...], preferred_element_type=F32) + br_ref[...]
    li = lax.broadcasted_iota(I32, (tm, LANES), 1)
    lif = li.astype(F32)
    grp_of_lane = lax.shift_right_logical(li, 3).astype(F32)
    is_grp = (li >= N_EXPERTS) & (li < N_EXPERTS + N_EXPERT_GROUPS)
    gl = jnp.where(is_grp, logits, -jnp.inf)
    gmax = jnp.max(gl, axis=-1, keepdims=True)
    grp = jnp.min(jnp.where(gl == gmax, lif - N_EXPERTS, float(LANES)), axis=-1, keepdims=True)
    gsum = jnp.sum(jnp.where(is_grp, jnp.exp(logits - gmax), 0.0), axis=-1, keepdims=True)
    grp_gate = 1.0 / gsum
    in_grp = grp_of_lane == grp
    el = jnp.where(in_grp, logits, -jnp.inf)
    v1 = jnp.max(el, axis=-1, keepdims=True)
    i1 = jnp.min(jnp.where(el == v1, lif, float(LANES)), axis=-1, keepdims=True)
    el2 = jnp.where(lif == i1, -jnp.inf, el)
    v2 = jnp.max(el2, axis=-1, keepdims=True)
    i2 = jnp.min(jnp.where(el2 == v2, lif, float(LANES)), axis=-1, keepdims=True)
    tt = jnp.exp(v2 - v1)
    w1 = grp_gate / (1.0 + tt)
    w2 = grp_gate * tt / (1.0 + tt)
    route = jnp.where(li == 0, i1,
                      jnp.where(li == 1, i2,
                                jnp.where(li == 2, w1, jnp.where(li == 3, w2, 0.0))))
    route_ref[rows, :] = route[:, :8]
    route_t_ref[:, rows] = route.T[:8, :]


def _post(ya, ga, mb, x2d, kv, w_out_a, w_out, xg, w_q, w_o, mg, w_r, b_r):
    tm = TM_POST
    nt = SEQ // tm
    tok = lambda w: pl.BlockSpec((tm, w), lambda i: (i, 0))
    return pl.pallas_call(
        _post_kernel,
        grid=(TOKENS // tm,),
        in_specs=[
            tok(GROUP_WIDTH), tok(D_MODEL), tok(D_MODEL), tok(D_MODEL),
            pl.BlockSpec((N_MEM, D_MODEL), lambda i: (i // nt, 0)),
            pl.BlockSpec((N_MEM, D_MODEL), lambda i: (i // nt, 1)),
            _resident((GROUP_WIDTH, D_MODEL)),
            _resident((D_MODEL, D_MODEL)),
            _resident((1, D_MODEL)),
            _resident((D_MODEL, D_MODEL)),
            _resident((D_MODEL, D_MODEL)),
            _resident((1, D_MODEL)),
            _resident((D_MODEL, LANES)),
            _resident((1, LANES)),
        ],
        out_specs=[tok(D_MODEL), tok(HALF_MODEL), pl.BlockSpec((tm, 8), lambda i: (i, 0)),
                   pl.BlockSpec((8, tm), lambda i: (0, i))],
        out_shape=[
            jax.ShapeDtypeStruct((TOKENS, D_MODEL), F32),
            jax.ShapeDtypeStruct((TOKENS, HALF_MODEL), I32),
            jax.ShapeDtypeStruct((TOKENS, 8), F32),
            jax.ShapeDtypeStruct((8, TOKENS), F32),
        ],
        scratch_shapes=[pltpu.VMEM((tm, D_MODEL), BF16)],
        compiler_params=pltpu.CompilerParams(
            dimension_semantics=("parallel",), vmem_limit_bytes=VMEM_LIMIT),
        name="post",
    )(ya, ga, mb, x2d, kv, kv, w_out_a, w_out, xg, w_q, w_o, mg, w_r, b_r)


def _route_kernel(rt_ref, dest_ref, meta_ref, rank_scr, carry_ref, pstart_ref):
    pss = pl.program_id(0)
    i = pl.program_id(1)
    tl = TL_ROUTE
    ch = SCAN_CHUNK
    ei = lax.broadcasted_iota(I32, (N_EXPERTS, ch), 0).astype(F32)

    @pl.when((pss == 0) & (i == 0))
    def _():
        carry_ref[...] = jnp.zeros_like(carry_ref)

    @pl.when(pss == 0)
    def _():
        ur = lax.broadcasted_iota(I32, (ch, ch), 0)
        uc = lax.broadcasted_iota(I32, (ch, ch), 1)
        upper = jnp.where(ur < uc, 1.0, 0.0).astype(BF16)
        for c in range(tl // ch):
            e1 = rt_ref[0:1, c * ch:(c + 1) * ch]
            e2 = rt_ref[1:2, c * ch:(c + 1) * ch]
            oh1 = e1 == ei
            oh2 = e2 == ei
            oh = jnp.where(oh1 | oh2, 1.0, 0.0)
            cnt = jnp.dot(oh.astype(BF16), upper, preferred_element_type=F32) + carry_ref[:, 0:1]
            rank1 = jnp.sum(jnp.where(oh1, cnt, 0.0), axis=0, keepdims=True)
            rank2 = jnp.sum(jnp.where(oh2, cnt, 0.0), axis=0, keepdims=True)
            col = pl.multiple_of(i * tl + c * ch, ch)
            rank_scr[0:1, pl.ds(col, ch)] = rank1
            rank_scr[1:2, pl.ds(col, ch)] = rank2
            carry_ref[...] = carry_ref[...] + jnp.sum(oh, axis=1, keepdims=True)

    @pl.when((pss == 1) & (i == 0))
    def _():
        counts = carry_ref[...].astype(I32)
        padded = lax.shift_left(lax.shift_right_logical(counts + (ROW_BLOCK - 1),
                                                        int(math.log2(ROW_BLOCK))),
                                int(math.log2(ROW_BLOCK)))
        row = lax.broadcasted_iota(I32, (N_EXPERTS, LANES), 0)
        lane = lax.broadcasted_iota(I32, (N_EXPERTS, LANES), 1)
        pend = padded
        sh = 1
        while sh < N_EXPERTS:
            pend = pend + jnp.where(row >= sh, pltpu.roll(pend, sh, 0), 0)
            sh *= 2
        pstart = pend - padded
        pstart_ref[...] = pstart
        diag = row == lane

        def as_row(x):
            return jnp.sum(jnp.where(diag, x, 0), axis=0, keepdims=True)

        blk_lane = lax.broadcasted_iota(I32, (N_EXPERTS, META_LANES), 1) * ROW_BLOCK
        blk_exp = jnp.sum(jnp.where(pend[:, 0:1] <= blk_lane, 1, 0), axis=0, keepdims=True)
        blk_exp = jnp.minimum(blk_exp, N_EXPERTS - 1)
        meta_ref[...] = jnp.zeros_like(meta_ref)
        meta_ref[0:1, :] = blk_exp
        meta_ref[1:2, 0:LANES] = as_row(counts)
        meta_ref[2:3, 0:LANES] = as_row(pstart)
        meta_ref[3:4, 0:LANES] = as_row(pend)

    @pl.when(pss == 1)
    def _():
        ps = pstart_ref[:, 0:1].astype(F32)
        for c in range(tl // ch):
            e1 = rt_ref[0:1, c * ch:(c + 1) * ch]
            e2 = rt_ref[1:2, c * ch:(c + 1) * ch]
            col = pl.multiple_of(i * tl + c * ch, ch)
            d1 = rank_scr[0:1, pl.ds(col, ch)] + jnp.sum(jnp.where(e1 == ei, ps, 0.0), axis=0, keepdims=True)
            d2 = rank_scr[1:2, pl.ds(col, ch)] + jnp.sum(jnp.where(e2 == ei, ps, 0.0), axis=0, keepdims=True)
            dest_ref[0:1, c * ch:(c + 1) * ch] = d1.astype(I32)
            dest_ref[1:2, c * ch:(c + 1) * ch] = d2.astype(I32)


def _route(rt):
    tl = TL_ROUTE
    return pl.pallas_call(
        _route_kernel,
        grid=(2, TOKENS // tl),
        in_specs=[pl.BlockSpec((8, tl), lambda p, i: (0, i))],
        out_specs=[
            pl.BlockSpec((2, tl), lambda p, i: (0, i * p)),
            pl.BlockSpec((8, META_LANES), lambda p, i: (0, 0)),
        ],
        out_shape=[
            jax.ShapeDtypeStruct((2, TOKENS), I32),
            jax.ShapeDtypeStruct((8, META_LANES), I32),
        ],
        scratch_shapes=[
            pltpu.VMEM((2, TOKENS), F32),
            pltpu.VMEM((N_EXPERTS, LANES), F32),
            pltpu.VMEM((N_EXPERTS, LANES), I32),
        ],
        compiler_params=pltpu.CompilerParams(
            dimension_semantics=("arbitrary", "arbitrary"), vmem_limit_bytes=VMEM_LIMIT),
        name="route",
    )(rt)


SC_CORES = 2
SC_SUBCORES = 16
SC_WORKERS = SC_CORES * SC_SUBCORES
SC_LANES = 16
SC_WINDOW = 64
SC_SCAN_CHUNK = 32768
DISPATCH_PARTS = 4


def _sc_move_rows(table_hbm, idx_v, out_hbm, out_base, n_windows, rows_v, gsem, wsems):
    assert n_windows % 2 == 0 and n_windows >= 2

    def gather(j, b):
        idx = idx_v.at[pl.ds(j * SC_WINDOW, SC_WINDOW)]
        pltpu.async_copy(table_hbm.at[idx], rows_v.at[b], gsem).wait()

    def write(j, b):
        dst = out_hbm.at[pl.ds(out_base + j * SC_WINDOW, SC_WINDOW)]
        return pltpu.make_async_copy(rows_v.at[b], dst, wsems.at[b])

    for b in range(2):
        gather(b, b)
        write(b, b).start()

    @pl.loop(2, n_windows, step=2)
    def _(j):
        for b in range(2):
            write(j - 2 + b, b).wait()
            gather(j + b, b)
            write(j + b, b).start()

    for b in range(2):
        write(n_windows - 2 + b, b).wait()


def _sc_inverse_map(dest0, dest1):
    n_tok = dest0.shape[0]
    rows_per_w = PADDED_ROWS // SC_WORKERS
    assert rows_per_w % SC_LANES == 0 and n_tok % SC_SCAN_CHUNK == 0
    mesh = plsc.VectorSubcoreMesh(core_axis_name="c", subcore_axis_name="s")

    @functools.partial(
        pl.kernel, mesh=mesh,
        out_type=jax.ShapeDtypeStruct((PADDED_ROWS,), I32),
        scratch_types=[pltpu.VMEM((rows_per_w,), I32), pltpu.VMEM((SC_SCAN_CHUNK,), I32)],
        compiler_params=pltpu.CompilerParams(needs_layout_passes=False),
        name="sc_inverse_map",
    )
    def inverse_map(d0_hbm, d1_hbm, tok_hbm, tok_v, dchunk_v):
        wid = lax.axis_index("s") * SC_CORES + lax.axis_index("c")
        lo = wid * rows_per_w
        lane = lax.iota(I32, SC_LANES)

        @pl.loop(0, rows_per_w // SC_LANES)
        def _(i):
            tok_v[pl.ds(i * SC_LANES, SC_LANES)] = (lo + i * SC_LANES + lane) & (n_tok - 1)

        for d_hbm in (d0_hbm, d1_hbm):
            @pl.loop(0, n_tok // SC_SCAN_CHUNK)
            def _(c, d_hbm=d_hbm):
                pltpu.sync_copy(d_hbm.at[pl.ds(c * SC_SCAN_CHUNK, SC_SCAN_CHUNK)], dchunk_v)

                @plsc.parallel_loop(0, SC_SCAN_CHUNK // SC_LANES, unroll=4)
                def _(v):
                    local = dchunk_v[pl.ds(v * SC_LANES, SC_LANES)] - lo
                    mine = (local >= 0) & (local < rows_per_w)
                    tok = c * SC_SCAN_CHUNK + v * SC_LANES + lane
                    plsc.store_scatter(tok_v, [jnp.where(mine, local, 0)], tok, mask=mine)

        pltpu.sync_copy(tok_v, tok_hbm.at[pl.ds(lo, rows_per_w)])

    return inverse_map(dest0, dest1)


def _sc_gather(table, idxs):
    n = idxs[0].shape[0]
    width = table.shape[1]
    per_w = n // SC_WORKERS
    assert per_w % (2 * SC_WINDOW) == 0
    mesh = plsc.VectorSubcoreMesh(core_axis_name="c", subcore_axis_name="s")
    out = jax.ShapeDtypeStruct((n, width), table.dtype)
    k = len(idxs)

    @functools.partial(
        pl.kernel, mesh=mesh, out_type=(out,) * k,
        scratch_types=[
            pltpu.VMEM((per_w,), I32),
            pltpu.VMEM((2, SC_WINDOW, width), table.dtype),
            pltpu.SemaphoreType.DMA,
            pltpu.SemaphoreType.DMA((2,)),
        ],
        name="sc_gather",
    )
    def gather(table_hbm, *refs):
        idx_hbms, out_hbms = refs[:k], refs[k:2 * k]
        idx_v, rows_v, gsem, wsems = refs[2 * k:]
        wid = lax.axis_index("s") * SC_CORES + lax.axis_index("c")
        base = wid * per_w
        for idx_hbm, out_hbm in zip(idx_hbms, out_hbms):
            pltpu.sync_copy(idx_hbm.at[pl.ds(base, per_w)], idx_v)
            _sc_move_rows(table_hbm, idx_v, out_hbm, base, per_w // SC_WINDOW, rows_v, gsem, wsems)

    return gather(table, *idxs)


def _experts_kernel(blk_start, be_ref, nv_ref, xs_ref, wg_ref, wu_ref, wd_ref, *rest):
    yb_ref = rest[-1]
    j = pl.program_id(0) + blk_start

    @pl.when(j < nv_ref[0])
    def _():
        lo, hi = _unpack_row_halves(xs_ref[...])
        xb = jnp.concatenate([lo.astype(BF16), hi.astype(BF16)], axis=1)
        a = jnp.dot(xb, wg_ref[0].astype(BF16), preferred_element_type=F32)
        b = jnp.dot(xb, wu_ref[0].astype(BF16), preferred_element_type=F32)
        hb = (jax.nn.silu(a) * b).astype(BF16)
        yb_ref[...] = _pack_row_halves(
            jnp.dot(hb, wd_ref[0].astype(BF16), preferred_element_type=F32))

    @pl.when(j >= nv_ref[0])
    def _():
        yb_ref[...] = jnp.zeros_like(yb_ref)


def _experts(blk_expert, n_valid, xs_part, blk_start, yb_prev, w_gate, w_up, w_down):
    n_blocks = xs_part.shape[0] // ROW_BLOCK

    def row_map(j, be, nv):
        return (jnp.clip(jnp.minimum(j + blk_start, nv[0] - 1) - blk_start, 0, n_blocks - 1), 0)

    def out_map(j, be, nv):
        return (j + blk_start, 0)

    def w_map(j, be, nv):
        return (be[jnp.minimum(j + blk_start, nv[0] - 1)], 0, 0)

    in_specs = [
        pl.BlockSpec((ROW_BLOCK, HALF_MODEL), row_map),
        pl.BlockSpec((1, D_MODEL, EXPERT_FF), w_map),
        pl.BlockSpec((1, D_MODEL, EXPERT_FF), w_map),
        pl.BlockSpec((1, EXPERT_FF, D_MODEL), w_map),
    ]
    operands = [blk_expert, n_valid, xs_part, w_gate, w_up, w_down]
    aliases = {}
    if yb_prev is not None:
        in_specs.append(pl.BlockSpec(memory_space=pl.ANY))
        aliases = {len(operands): 0}
        operands.append(yb_prev)
    grid_spec = pltpu.PrefetchScalarGridSpec(
        num_scalar_prefetch=2,
        grid=(n_blocks,),
        in_specs=in_specs,
        out_specs=pl.BlockSpec((ROW_BLOCK, HALF_MODEL), out_map),
    )
    return pl.pallas_call(
        functools.partial(_experts_kernel, blk_start),
        grid_spec=grid_spec,
        out_shape=jax.ShapeDtypeStruct((PADDED_ROWS, HALF_MODEL), I32),
        input_output_aliases=aliases,
        compiler_params=pltpu.CompilerParams(
            dimension_semantics=("arbitrary",), vmem_limit_bytes=VMEM_LIMIT),
        name="experts",
    )(*operands)


def _combine_kernel(route_ref, h_ref, g_ref, y0_ref, y1_ref, *rest):
    out_ref = rest[-1]
    w1 = route_ref[:, 2:3]
    w2 = route_ref[:, 3:4]
    lo0, hi0 = _unpack_row_halves(y0_ref[...])
    lo1, hi1 = _unpack_row_halves(y1_ref[...])
    y = jnp.concatenate([lo0 * w1 + lo1 * w2, hi0 * w1 + hi1 * w2], axis=1)
    out_ref[...] = _rms(h_ref[...] + y, g_ref[...])


def _combine(route, h2, g, y0, y1, tok_start, out_prev):
    tm = TM_COMBINE
    blk0 = tok_start // tm
    glob = lambda w: pl.BlockSpec((tm, w), lambda i: (i + blk0, 0))
    part = pl.BlockSpec((tm, HALF_MODEL), lambda i: (i, 0))
    in_specs = [glob(8), glob(D_MODEL), _resident((1, D_MODEL)), part, part]
    operands = [route, h2, g, y0, y1]
    aliases = {}
    if out_prev is not None:
        in_specs.append(pl.BlockSpec(memory_space=pl.ANY))
        aliases = {len(operands): 0}
        operands.append(out_prev)
    return pl.pallas_call(
        _combine_kernel,
        grid=(y0.shape[0] // tm,),
        in_specs=in_specs,
        out_specs=glob(D_MODEL),
        out_shape=jax.ShapeDtypeStruct((TOKENS, D_MODEL), F32),
        input_output_aliases=aliases,
        compiler_params=pltpu.CompilerParams(
            dimension_semantics=("parallel",), vmem_limit_bytes=VMEM_LIMIT),
        name="combine",
    )(*operands)


def kernel(x, mem, positions, mix_norm_g, w_in, b_gates, w_spatial, b_spatial, v_norm_g, v_norm_b,
           w_out_a, w_out_b, w_out, xattn_norm_g, mem_norm_g, w_q_x, w_kv_x, w_o_x, moe_norm_g,
           w_router_grp, b_router_grp, w_router_exp, b_router_exp, w_gate_e, w_up_e, w_down_e,
           final_norm_g):
    assert x.shape == (BATCH, SEQ, D_MODEL) and mem.shape == (BATCH, N_MEM, D_MODEL)
    assert mix_norm_g.shape[0] == 1, "single layer"
    x2d = x.reshape(TOKENS, D_MODEL)
    pos_col = positions.reshape(TOKENS, 1).astype(F32)
    half = HEAD_DIM // 2
    inv_freq = ROPE_THETA ** (-jnp.arange(half, dtype=F32) / half)
    invf = jnp.tile(inv_freq, LANES // half).reshape(1, LANES)
    phase = jnp.tile(jnp.concatenate([jnp.zeros((half,), F32), jnp.full((half,), math.pi / 2, F32)]),
                     LANES // HEAD_DIM).reshape(1, LANES)

    kv = _memkv(mem.reshape(BATCH * N_MEM, D_MODEL), mem_norm_g[0].reshape(1, D_MODEL),
                w_kv_x[0].astype(BF16))

    qkv0, qkv1, qkv2, ga, mb = _inproj(
        x2d, pos_col, invf, phase, mix_norm_g[0].reshape(1, D_MODEL), w_in[0].astype(BF16),
        b_gates[0].reshape(1, 2 * D_MODEL), w_spatial[0], b_spatial[0].T,
        v_norm_g[0].reshape(1, GMLP_WIDTH), v_norm_b[0].reshape(1, GMLP_WIDTH),
        w_out_b[0].astype(BF16))

    ya = _attention((qkv0, qkv1, qkv2)).reshape(TOKENS, GROUP_WIDTH)

    pad = LANES - N_EXPERTS - N_EXPERT_GROUPS
    w_r = jnp.concatenate([w_router_exp[0], w_router_grp[0], jnp.zeros((D_MODEL, pad), F32)], axis=1)
    b_r = jnp.concatenate([b_router_exp[0], b_router_grp[0], jnp.zeros((pad,), F32)]).reshape(1, LANES)
    h2, hn2, route, route_t = _post(
        ya, ga, mb, x2d, kv, w_out_a[0].astype(BF16), w_out[0].astype(BF16),
        xattn_norm_g[0].reshape(1, D_MODEL), w_q_x[0].astype(BF16), w_o_x[0].astype(BF16),
        moe_norm_g[0].reshape(1, D_MODEL), w_r.astype(BF16), b_r)

    dest, meta = _route(route_t)
    d0, d1 = dest[0], dest[1]
    tok_of_row = _sc_inverse_map(d0, d1)
    blk_expert = meta[0, :N_ROW_BLOCKS]
    n_valid = (meta[3, N_EXPERTS - 1:N_EXPERTS] // ROW_BLOCK).astype(I32)
    part_rows = PADDED_ROWS // DISPATCH_PARTS
    xs_parts = [
        _sc_gather(hn2, [tok_of_row[p * part_rows:(p + 1) * part_rows]])[0] for p in range(DISPATCH_PARTS)
    ]
    yb = None
    for p, xs_part in enumerate(xs_parts):
        yb = _experts(blk_expert, n_valid, xs_part, p * (part_rows // ROW_BLOCK), yb,
                      w_gate_e[0], w_up_e[0], w_down_e[0])
    g_fin = final_norm_g.reshape(1, D_MODEL)
    half_tok = TOKENS // 2
    y0a, y1a = _sc_gather(yb, [d0[:half_tok], d1[:half_tok]])
    y0b, y1b = _sc_gather(yb, [d0[half_tok:], d1[half_tok:]])
    out = _combine(route, h2, g_fin, y0a, y1a, 0, None)
    out = _combine(route, h2, g_fin, y0b, y1b, half_tok, out)
    return out.reshape(BATCH, SEQ, D_MODEL)
```

```python
import functools
import math

import jax
import jax.numpy as jnp
from jax import lax
from jax.experimental import pallas as pl
from jax.experimental.pallas import tpu as pltpu
from jax.experimental.pallas import tpu_sc as plsc

F32 = jnp.float32
BF16 = jnp.bfloat16
I32 = jnp.int32

D_MODEL = 1024
BATCH = 16
SEQ = 4096
TOKENS = BATCH * SEQ

HEAD_DIM = 64
DILATIONS = (1, 4, 16)
HEADS_PER_GROUP = 4
GROUP_WIDTH = HEADS_PER_GROUP * HEAD_DIM
ATT_WIDTH = len(DILATIONS) * GROUP_WIDTH
BAND_BLOCK = 128
ROPE_THETA = 10000.0

GMLP_CHUNK = 128
GMLP_GROUPS = 4
GMLP_WIDTH = 512

N_MEM = 256
XATTN_HEADS = 4
XATTN_HEAD_DIM = D_MODEL // XATTN_HEADS

N_EXPERT_GROUPS = 4
EXPERTS_PER_GROUP = 8
N_EXPERTS = 32
TOP_K = 2
EXPERT_FF = 512

RMS_EPS = 1e-6
LN_EPS = 1e-5
NEG_INF = -1e30

LANES = 128

COL_U = 3 * ATT_WIDTH
COL_V = COL_U + GMLP_WIDTH
COL_GA = COL_V + GMLP_WIDTH
COL_GB = COL_GA + D_MODEL

ROW_BLOCK = 512
ASSIGN = TOKENS * TOP_K
PADDED_ROWS = ASSIGN + N_EXPERTS * ROW_BLOCK
N_ROW_BLOCKS = PADDED_ROWS // ROW_BLOCK
META_LANES = ((N_ROW_BLOCKS + LANES - 1) // LANES) * LANES

TM_PROJ = 1024
PROJ_SUB = 512
TM_POST = 1024
POST_SUB = 512
TL_ROUTE = 8192
SCAN_CHUNK = 256
TM_COMBINE = 1024
ATTN_UNROLL = 8

VMEM_LIMIT = 56 * 1024 * 1024


def _rms(x, g):
    return x * lax.rsqrt(jnp.mean(x * x, axis=-1, keepdims=True) + RMS_EPS) * g


HALF_MODEL = D_MODEL // 2


def _pack_row_halves(x):
    return pltpu.pack_elementwise([x[:, :HALF_MODEL], x[:, HALF_MODEL:]], packed_dtype=BF16)


def _unpack_row_halves(p):
    lo = pltpu.unpack_elementwise(p, index=0, packed_dtype=BF16, unpacked_dtype=F32)
    hi = pltpu.unpack_elementwise(p, index=1, packed_dtype=BF16, unpacked_dtype=F32)
    return lo, hi


def _resident(shape):
    nd = len(shape)
    return pl.BlockSpec(shape, lambda *_: (0,) * nd, pipeline_mode=pl.Buffered(1))


def _memkv_kernel(mem_ref, g_ref, w_ref, kv_ref):
    mn = _rms(mem_ref[...], g_ref[...]).astype(BF16)
    kv_ref[...] = jnp.dot(mn, w_ref[...], preferred_element_type=F32).astype(BF16)


def _memkv(mem2d, g, w_kv):
    rows = mem2d.shape[0]
    tm = 512
    return pl.pallas_call(
        _memkv_kernel,
        grid=(rows // tm,),
        in_specs=[
            pl.BlockSpec((tm, D_MODEL), lambda i: (i, 0)),
            _resident((1, D_MODEL)),
            _resident((D_MODEL, 2 * D_MODEL)),
        ],
        out_specs=pl.BlockSpec((tm, 2 * D_MODEL), lambda i: (i, 0)),
        out_shape=jax.ShapeDtypeStruct((rows, 2 * D_MODEL), BF16),
        compiler_params=pltpu.CompilerParams(
            dimension_semantics=("parallel",), vmem_limit_bytes=VMEM_LIMIT),
        name="memkv",
    )(mem2d, g, w_kv)


def _inproj_kernel(x_ref, pos_ref, invf_ref, phase_ref, g_ref, w_ref, bg_ref, wsp_ref,
                   bsp_ref, lng_ref, lnb_ref, wob_ref,
                   qkv0_ref, qkv1_ref, qkv2_ref, ga_ref, mb_ref, scr_ref, yb_ref):
    for sub in range(TM_PROJ // PROJ_SUB):
        _inproj_rows(sub, x_ref, pos_ref, invf_ref, phase_ref, g_ref, w_ref, bg_ref, wsp_ref,
                     bsp_ref, lng_ref, lnb_ref, wob_ref,
                     (qkv0_ref, qkv1_ref, qkv2_ref), ga_ref, mb_ref, scr_ref, yb_ref)


def _inproj_rows(sub, x_ref, pos_ref, invf_ref, phase_ref, g_ref, w_ref, bg_ref, wsp_ref,
                 bsp_ref, lng_ref, lnb_ref, wob_ref, out_refs, ga_ref, mb_ref, scr_ref, yb_ref):
    tm = PROJ_SUB
    rows = slice(sub * tm, (sub + 1) * tm)
    xn = _rms(x_ref[rows, :], g_ref[...]).astype(BF16)

    lane = lax.broadcasted_iota(I32, (tm, LANES), 1)
    upper = (lane & 32) != 0
    t1 = jnp.sin(pos_ref[rows, :] * invf_ref[...] + phase_ref[...])
    cosf = jnp.where(upper, t1, pltpu.roll(t1, 96, 1))
    sinf = jnp.where(upper, pltpu.roll(t1, 32, 1), -t1)

    def rope(res):
        outs = []
        for c in range(GROUP_WIDTH // LANES):
            xt = res[:, c * LANES:(c + 1) * LANES]
            rot = jnp.where(upper, pltpu.roll(xt, 32, 1), pltpu.roll(xt, 96, 1))
            outs.append(xt * cosf + rot * sinf)
        return jnp.concatenate(outs, axis=1)

    zu_raw = jnp.dot(xn, w_ref[:, COL_U:COL_V], preferred_element_type=F32)
    zv_raw = jnp.dot(xn, w_ref[:, COL_V:COL_GA], preferred_element_type=F32)

    slabs = GROUP_WIDTH // LANES

    def project_group(gi):
        dil = DILATIONS[gi]
        for which in range(3):
            c0 = which * ATT_WIDTH + gi * GROUP_WIDTH
            res = jnp.dot(xn, w_ref[:, c0:c0 + GROUP_WIDTH], preferred_element_type=F32)
            if which < 2:
                res = rope(res)
            if which == 0:
                res = res * (HEAD_DIM ** -0.5)
            if dil == 1:
                out_refs[gi][0, which, 0, rows, :] = res.astype(BF16)
            else:
                n = tm // dil
                for c in range(slabs):
                    slot = ((sub * 2 + gi - 1) * 3 + which) * slabs + c
                    scr_ref[slot] = res[:, c * LANES:(c + 1) * LANES]
                    for r in range(dil):
                        out_refs[gi][0, which, r, sub * n:(sub + 1) * n, c * LANES:(c + 1) * LANES] = (
                            scr_ref[slot, pl.ds(r, n, stride=dil), :].astype(BF16))

    for gi in range(len(DILATIONS)):
        project_group(gi)

    zu = jax.nn.gelu(zu_raw)
    zv = jax.nn.gelu(zv_raw)
    mu = jnp.mean(zv, axis=-1, keepdims=True)
    zc = zv - mu
    var = jnp.mean(zc * zc, axis=-1, keepdims=True)
    vn = (zc * lax.rsqrt(var + LN_EPS) * lng_ref[...] + lnb_ref[...]).astype(BF16)
    tri_r = lax.broadcasted_iota(I32, (GMLP_CHUNK, GMLP_CHUNK), 0)
    tri_c = lax.broadcasted_iota(I32, (GMLP_CHUNK, GMLP_CHUNK), 1)
    causal = tri_r >= tri_c
    n_chunks = tm // GMLP_CHUNK
    gw = GMLP_WIDTH // GMLP_GROUPS
    for g in range(GMLP_GROUPS):
        wsg = jnp.where(causal, wsp_ref[g], 0.0).astype(BF16)
        vcat = jnp.concatenate(
            [vn[c * GMLP_CHUNK:(c + 1) * GMLP_CHUNK, g * gw:(g + 1) * gw] for c in range(n_chunks)],
            axis=1)
        mixed = jnp.dot(wsg, vcat, preferred_element_type=F32) + bsp_ref[:, g:g + 1]
        for c in range(n_chunks):
            u_blk = zu[c * GMLP_CHUNK:(c + 1) * GMLP_CHUNK, g * gw:(g + 1) * gw]
            r0 = sub * tm + c * GMLP_CHUNK
            yb_ref[r0:r0 + GMLP_CHUNK, g * gw:(g + 1) * gw] = (
                u_blk * mixed[:, c * gw:(c + 1) * gw]).astype(BF16)

    gate_a = jax.nn.sigmoid(
        jnp.dot(xn, w_ref[:, COL_GA:COL_GB], preferred_element_type=F32) + bg_ref[:, :D_MODEL])
    ga_ref[rows, :] = gate_a.astype(BF16)
    gate_b = jax.nn.sigmoid(
        jnp.dot(xn, w_ref[:, COL_GB:COL_GB + D_MODEL], preferred_element_type=F32) + bg_ref[:, D_MODEL:])
    mb_ref[rows, :] = (gate_b * jnp.dot(yb_ref[rows, :], wob_ref[...], preferred_element_type=F32)).astype(BF16)


def _inproj(x2d, pos_col, invf, phase, g, w_in, b_gates, w_spatial, b_spatial_t, ln_g, ln_b, w_out_b):
    tm = TM_PROJ
    nt = SEQ // tm
    in_cols = w_in.shape[1]
    qkv_shapes = [jax.ShapeDtypeStruct((BATCH, 3, d, SEQ // d, GROUP_WIDTH), BF16) for d in DILATIONS]
    qkv_specs = [
        pl.BlockSpec((1, 3, d, tm // d, GROUP_WIDTH), lambda i: (i // nt, 0, 0, i % nt, 0))
        for d in DILATIONS
    ]
    tok_spec = pl.BlockSpec((tm, D_MODEL), lambda i: (i, 0))
    return pl.pallas_call(
        _inproj_kernel,
        grid=(TOKENS // tm,),
        in_specs=[
            tok_spec,
            pl.BlockSpec((tm, 1), lambda i: (i, 0)),
            _resident((1, LANES)),
            _resident((1, LANES)),
            _resident((1, D_MODEL)),
            _resident((D_MODEL, in_cols)),
            _resident((1, 2 * D_MODEL)),
            _resident((GMLP_GROUPS, GMLP_CHUNK, GMLP_CHUNK)),
            _resident((GMLP_CHUNK, GMLP_GROUPS)),
            _resident((1, GMLP_WIDTH)),
            _resident((1, GMLP_WIDTH)),
            _resident((GMLP_WIDTH, D_MODEL)),
        ],
        out_specs=qkv_specs + [tok_spec, tok_spec],
        out_shape=qkv_shapes + [jax.ShapeDtypeStruct((TOKENS, D_MODEL), BF16)] * 2,
        scratch_shapes=[
            pltpu.VMEM((6 * (GROUP_WIDTH // LANES) * (tm // PROJ_SUB), PROJ_SUB, LANES), F32),
            pltpu.VMEM((tm, GMLP_WIDTH), BF16),
        ],
        compiler_params=pltpu.CompilerParams(
            dimension_semantics=("parallel",), vmem_limit_bytes=VMEM_LIMIT),
        name="inproj",
    )(x2d, pos_col, invf, phase, g, w_in, b_gates, w_spatial, b_spatial_t, ln_g, ln_b, w_out_b)


def _attn_kernel(qkv0_ref, qkv1_ref, qkv2_ref, y_ref, acc_ref, m_ref, z_ref, bias_ref):
    blk = BAND_BLOCK
    lane_row = lax.broadcasted_iota(I32, (1, LANES), 1)
    head0_b = jnp.where(lane_row < HEAD_DIM, 1.0, 0.0).astype(BF16)
    head1_b = jnp.where(lane_row < HEAD_DIM, 0.0, 1.0).astype(BF16)
    head0 = lax.broadcasted_iota(I32, (blk, LANES), 1) < HEAD_DIM
    ones_b = jnp.ones((2 * blk, LANES), BF16)

    qi = lax.broadcasted_iota(I32, (2 * blk, 2 * blk), 0) & (blk - 1)
    kc = lax.broadcasted_iota(I32, (2 * blk, 2 * blk), 1)
    for slot, off in enumerate((0, blk)):
        dist = qi + off - kc
        bias_ref[slot] = jnp.where((dist >= 0) & (dist <= blk), 0.0, NEG_INF)

    for gi, (ref, dil) in enumerate(zip((qkv0_ref, qkv1_ref, qkv2_ref), DILATIONS)):
        seq_len = SEQ // dil
        nb = seq_len // blk
        nb_shift = nb.bit_length() - 1

        def body(i, carry, ref=ref, dil=dil, nb=nb, nb_shift=nb_shift, gi=gi):
            r = lax.shift_right_logical(i, nb_shift)
            n = i & (nb - 1)
            q0 = pl.multiple_of(n * blk, blk)
            w0 = pl.multiple_of(jnp.maximum(n - 1, 0) * blk, blk)
            q = ref[0, 0, r, pl.ds(q0, blk), :]
            k = ref[0, 1, r, pl.ds(w0, 2 * blk), :]
            v = ref[0, 2, r, pl.ds(w0, 2 * blk), :]
            q2 = jnp.concatenate([q * head0_b, q * head1_b], axis=0)
            s = lax.dot_general(q2, k, (((1,), (1,)), ((), ())), preferred_element_type=F32)
            s = s + bias_ref[jnp.minimum(n, 1)]
            m2 = jnp.max(s, axis=-1, keepdims=True)
            p = jnp.exp(s - m2)
            v_ext = jnp.concatenate([v, ones_b], axis=1)
            o2 = jnp.dot(p.astype(BF16), v_ext, preferred_element_type=F32)
            o = jnp.where(head0, o2[:blk, :LANES], o2[blk:, :LANES])
            den = jnp.where(head0, o2[:blk, LANES:], o2[blk:, LANES:])
            m = jnp.where(head0, m2[:blk], m2[blk:])
            if gi == 0:
                acc_ref[pl.ds(q0, blk), :] = o
                m_ref[pl.ds(q0, blk), :] = m
                z_ref[pl.ds(q0, blk), :] = den
            else:
                idx = pl.ds(n * (blk * dil) + r, blk, stride=dil)
                m_old = m_ref[idx, :]
                m_new = jnp.maximum(m_old, m)
                e_old = jnp.exp(m_old - m_new)
                e_new = jnp.exp(m - m_new)
                acc_ref[idx, :] = acc_ref[idx, :] * e_old + o * e_new
                z_ref[idx, :] = z_ref[idx, :] * e_old + den * e_new
                m_ref[idx, :] = m_new
            return carry

        lax.fori_loop(0, dil * nb, body, 0, unroll=ATTN_UNROLL)

    y_ref[0] = (acc_ref[...] / z_ref[...]).astype(BF16)


def _attention(qkv):
    in_specs = [
        pl.BlockSpec((1, 3, d, SEQ // d, LANES), lambda b, h: (b, 0, 0, 0, h)) for d in DILATIONS
    ]
    return pl.pallas_call(
        _attn_kernel,
        grid=(BATCH, GROUP_WIDTH // LANES),
        in_specs=in_specs,
        out_specs=pl.BlockSpec((1, SEQ, LANES), lambda b, h: (b, 0, h)),
        out_shape=jax.ShapeDtypeStruct((BATCH, SEQ, GROUP_WIDTH), BF16),
        scratch_shapes=[
            pltpu.VMEM((SEQ, LANES), F32), pltpu.VMEM((SEQ, LANES), F32), pltpu.VMEM((SEQ, LANES), F32),
            pltpu.VMEM((2, 2 * BAND_BLOCK, 2 * BAND_BLOCK), F32),
        ],
        compiler_params=pltpu.CompilerParams(
            dimension_semantics=("parallel", "parallel"), vmem_limit_bytes=VMEM_LIMIT),
        name="attn",
    )(*qkv)


def _post_kernel(ya_ref, ga_ref, mb_ref, x_ref, k_ref, v_ref, woa_ref, wo_ref, xg_ref, wq_ref,
                 wox_ref, mg_ref, wr_ref, br_ref, h_ref, hn_ref, route_ref, route_t_ref, o_scr):
    for c in range(TM_POST // POST_SUB):
        rows = slice(c * POST_SUB, (c + 1) * POST_SUB)
        _post_rows(rows, ya_ref, ga_ref, mb_ref, x_ref, k_ref, v_ref, woa_ref, wo_ref, xg_ref, wq_ref,
                   wox_ref, mg_ref, wr_ref, br_ref, h_ref, hn_ref, route_ref, route_t_ref, o_scr)


def _post_rows(rows, ya_ref, ga_ref, mb_ref, x_ref, k_ref, v_ref, woa_ref, wo_ref, xg_ref, wq_ref,
               wox_ref, mg_ref, wr_ref, br_ref, h_ref, hn_ref, route_ref, route_t_ref, o_scr):
    tm = POST_SUB
    t = jnp.dot(ya_ref[rows, :], woa_ref[...], preferred_element_type=F32)
    merged = (ga_ref[rows, :].astype(F32) * t + mb_ref[rows, :].astype(F32)).astype(BF16)
    h1 = x_ref[rows, :] + jnp.dot(merged, wo_ref[...], preferred_element_type=F32)

    hn = _rms(h1, xg_ref[...]).astype(BF16)
    q = (jnp.dot(hn, wq_ref[...], preferred_element_type=F32) * (XATTN_HEAD_DIM ** -0.5)).astype(BF16)
    hd = XATTN_HEAD_DIM
    for h in range(XATTN_HEADS):
        s = lax.dot_general(q[:, h * hd:(h + 1) * hd], k_ref[:, h * hd:(h + 1) * hd],
                            (((1,), (1,)), ((), ())), preferred_element_type=F32)
        m = jnp.max(s, axis=-1, keepdims=True)
        p = jnp.exp(s - m)
        den = jnp.sum(p, axis=-1, keepdims=True)
        oh = jnp.dot(p.astype(BF16), v_ref[:, h * hd:(h + 1) * hd], preferred_element_type=F32) / den
        o_scr[rows, h * hd:(h + 1) * hd] = oh.astype(BF16)
    h2 = h1 + jnp.dot(o_scr[rows, :], wox_ref[...], preferred_element_type=F32)
    h_ref[rows, :] = h2

    hn2 = _rms(h2, mg_ref[...])
    hn_ref[rows, :] = _pack_row_halves(hn2)

    logits = jnp.dot(hn2.astype(BF16), wr_ref[...], preferred_element_type=F32) + br_ref[...]
    li = lax.broadcasted_iota(I32, (tm, LANES), 1)
    lif = li.astype(F32)
    grp_of_lane = lax.shift_right_logical(li, 3).astype(F32)
    is_grp = (li >= N_EXPERTS) & (li < N_EXPERTS + N_EXPERT_GROUPS)
    gl = jnp.where(is_grp, logits, -jnp.inf)
    gmax = jnp.max(gl, axis=-1, keepdims=True)
    grp = jnp.min(jnp.where(gl == gmax, lif - N_EXPERTS, float(LANES)), axis=-1, keepdims=True)
    gsum = jnp.sum(jnp.where(is_grp, jnp.exp(logits - gmax), 0.0), axis=-1, keepdims=True)
    grp_gate = 1.0 / gsum
    in_grp = grp_of_lane == grp
    el = jnp.where(in_grp, logits, -jnp.inf)
    v1 = jnp.max(el, axis=-1, keepdims=True)
    i1 = jnp.min(jnp.where(el == v1, lif, float(LANES)), axis=-1, keepdims=True)
    el2 = jnp.where(lif == i1, -jnp.inf, el)
    v2 = jnp.max(el2, axis=-1, keepdims=True)
    i2 = jnp.min(jnp.where(el2 == v2, lif, float(LANES)), axis=-1, keepdims=True)
    tt = jnp.exp(v2 - v1)
    w1 = grp_gate / (1.0 + tt)
    w2 = grp_gate * tt / (1.0 + tt)
    route = jnp.where(li == 0, i1,
                      jnp.where(li == 1, i2,
                                jnp.where(li == 2, w1, jnp.where(li == 3, w2, 0.0))))
    route_ref[rows, :] = route[:, :8]
    route_t_ref[:, rows] = route.T[:8, :]


def _post(ya, ga, mb, x2d, kv, w_out_a, w_out, xg, w_q, w_o, mg, w_r, b_r):
    tm = TM_POST
    nt = SEQ // tm
    tok = lambda w: pl.BlockSpec((tm, w), lambda i: (i, 0))
    return pl.pallas_call(
        _post_kernel,
        grid=(TOKENS // tm,),
        in_specs=[
            tok(GROUP_WIDTH), tok(D_MODEL), tok(D_MODEL), tok(D_MODEL),
            pl.BlockSpec((N_MEM, D_MODEL), lambda i: (i // nt, 0)),
            pl.BlockSpec((N_MEM, D_MODEL), lambda i: (i // nt, 1)),
            _resident((GROUP_WIDTH, D_MODEL)),
            _resident((D_MODEL, D_MODEL)),
            _resident((1, D_MODEL)),
            _resident((D_MODEL, D_MODEL)),
            _resident((D_MODEL, D_MODEL)),
            _resident((1, D_MODEL)),
            _resident((D_MODEL, LANES)),
            _resident((1, LANES)),
        ],
        out_specs=[tok(D_MODEL), tok(HALF_MODEL), pl.BlockSpec((tm, 8), lambda i: (i, 0)),
                   pl.BlockSpec((8, tm), lambda i: (0, i))],
        out_shape=[
            jax.ShapeDtypeStruct((TOKENS, D_MODEL), F32),
            jax.ShapeDtypeStruct((TOKENS, HALF_MODEL), I32),
            jax.ShapeDtypeStruct((TOKENS, 8), F32),
            jax.ShapeDtypeStruct((8, TOKENS), F32),
        ],
        scratch_shapes=[pltpu.VMEM((tm, D_MODEL), BF16)],
        compiler_params=pltpu.CompilerParams(
            dimension_semantics=("parallel",), vmem_limit_bytes=VMEM_LIMIT),
        name="post",
    )(ya, ga, mb, x2d, kv, kv, w_out_a, w_out, xg, w_q, w_o, mg, w_r, b_r)


def _route_kernel(rt_ref, dest_ref, meta_ref, rank_scr, carry_ref, pstart_ref):
    pss = pl.program_id(0)
    i = pl.program_id(1)
    tl = TL_ROUTE
    ch = SCAN_CHUNK
    ei = lax.broadcasted_iota(I32, (N_EXPERTS, ch), 0).astype(F32)

    @pl.when((pss == 0) & (i == 0))
    def _():
        carry_ref[...] = jnp.zeros_like(carry_ref)

    @pl.when(pss == 0)
    def _():
        ur = lax.broadcasted_iota(I32, (ch, ch), 0)
        uc = lax.broadcasted_iota(I32, (ch, ch), 1)
        upper = jnp.where(ur < uc, 1.0, 0.0).astype(BF16)
        for c in range(tl // ch):
            e1 = rt_ref[0:1, c * ch:(c + 1) * ch]
            e2 = rt_ref[1:2, c * ch:(c + 1) * ch]
            oh1 = e1 == ei
            oh2 = e2 == ei
            oh = jnp.where(oh1 | oh2, 1.0, 0.0)
            cnt = jnp.dot(oh.astype(BF16), upper, preferred_element_type=F32) + carry_ref[:, 0:1]
            rank1 = jnp.sum(jnp.where(oh1, cnt, 0.0), axis=0, keepdims=True)
            rank2 = jnp.sum(jnp.where(oh2, cnt, 0.0), axis=0, keepdims=True)
            col = pl.multiple_of(i * tl + c * ch, ch)
            rank_scr[0:1, pl.ds(col, ch)] = rank1
            rank_scr[1:2, pl.ds(col, ch)] = rank2
            carry_ref[...] = carry_ref[...] + jnp.sum(oh, axis=1, keepdims=True)

    @pl.when((pss == 1) & (i == 0))
    def _():
        counts = carry_ref[...].astype(I32)
        padded = lax.shift_left(lax.shift_right_logical(counts + (ROW_BLOCK - 1),
                                                        int(math.log2(ROW_BLOCK))),
                                int(math.log2(ROW_BLOCK)))
        row = lax.broadcasted_iota(I32, (N_EXPERTS, LANES), 0)
        lane = lax.broadcasted_iota(I32, (N_EXPERTS, LANES), 1)
        pend = padded
        sh = 1
        while sh < N_EXPERTS:
            pend = pend + jnp.where(row >= sh, pltpu.roll(pend, sh, 0), 0)
            sh *= 2
        pstart = pend - padded
        pstart_ref[...] = pstart
        diag = row == lane

        def as_row(x):
            return jnp.sum(jnp.where(diag, x, 0), axis=0, keepdims=True)

        blk_lane = lax.broadcasted_iota(I32, (N_EXPERTS, META_LANES), 1) * ROW_BLOCK
        blk_exp = jnp.sum(jnp.where(pend[:, 0:1] <= blk_lane, 1, 0), axis=0, keepdims=True)
        blk_exp = jnp.minimum(blk_exp, N_EXPERTS - 1)
        meta_ref[...] = jnp.zeros_like(meta_ref)
        meta_ref[0:1, :] = blk_exp
        meta_ref[1:2, 0:LANES] = as_row(counts)
        meta_ref[2:3, 0:LANES] = as_row(pstart)
        meta_ref[3:4, 0:LANES] = as_row(pend)

    @pl.when(pss == 1)
    def _():
        ps = pstart_ref[:, 0:1].astype(F32)
        for c in range(tl // ch):
            e1 = rt_ref[0:1, c * ch:(c + 1) * ch]
            e2 = rt_ref[1:2, c * ch:(c + 1) * ch]
            col = pl.multiple_of(i * tl + c * ch, ch)
            d1 = rank_scr[0:1, pl.ds(col, ch)] + jnp.sum(jnp.where(e1 == ei, ps, 0.0), axis=0, keepdims=True)
            d2 = rank_scr[1:2, pl.ds(col, ch)] + jnp.sum(jnp.where(e2 == ei, ps, 0.0), axis=0, keepdims=True)
            dest_ref[0:1, c * ch:(c + 1) * ch] = d1.astype(I32)
            dest_ref[1:2, c * ch:(c + 1) * ch] = d2.astype(I32)


def _route(rt):
    tl = TL_ROUTE
    return pl.pallas_call(
        _route_kernel,
        grid=(2, TOKENS // tl),
        in_specs=[pl.BlockSpec((8, tl), lambda p, i: (0, i))],
        out_specs=[
            pl.BlockSpec((2, tl), lambda p, i: (0, i * p)),
            pl.BlockSpec((8, META_LANES), lambda p, i: (0, 0)),
        ],
        out_shape=[
            jax.ShapeDtypeStruct((2, TOKENS), I32),
            jax.ShapeDtypeStruct((8, META_LANES), I32),
        ],
        scratch_shapes=[
            pltpu.VMEM((2, TOKENS), F32),
            pltpu.VMEM((N_EXPERTS, LANES), F32),
            pltpu.VMEM((N_EXPERTS, LANES), I32),
        ],
        compiler_params=pltpu.CompilerParams(
            dimension_semantics=("arbitrary", "arbitrary"), vmem_limit_bytes=VMEM_LIMIT),
        name="route",
    )(rt)


SC_CORES = 2
SC_SUBCORES = 16
SC_WORKERS = SC_CORES * SC_SUBCORES
SC_LANES = 16
SC_WINDOW = 64
SC_SCAN_CHUNK = 32768
DISPATCH_PARTS = 4
COMBINE_PARTS = 4


def _sc_move_rows(table_hbm, idx_v, out_hbm, out_base, n_windows, rows_v, gsem, wsems):
    assert n_windows % 2 == 0 and n_windows >= 2

    def gather(j, b):
        idx = idx_v.at[pl.ds(j * SC_WINDOW, SC_WINDOW)]
        pltpu.async_copy(table_hbm.at[idx], rows_v.at[b], gsem).wait()

    def write(j, b):
        dst = out_hbm.at[pl.ds(out_base + j * SC_WINDOW, SC_WINDOW)]
        return pltpu.make_async_copy(rows_v.at[b], dst, wsems.at[b])

    for b in range(2):
        gather(b, b)
        write(b, b).start()

    @pl.loop(2, n_windows, step=2)
    def _(j):
        for b in range(2):
            write(j - 2 + b, b).wait()
            gather(j + b, b)
            write(j + b, b).start()

    for b in range(2):
        write(n_windows - 2 + b, b).wait()


def _sc_inverse_map(dest0, dest1):
    n_tok = dest0.shape[0]
    rows_per_w = PADDED_ROWS // SC_WORKERS
    assert rows_per_w % SC_LANES == 0 and n_tok % SC_SCAN_CHUNK == 0
    mesh = plsc.VectorSubcoreMesh(core_axis_name="c", subcore_axis_name="s")

    @functools.partial(
        pl.kernel, mesh=mesh,
        out_type=jax.ShapeDtypeStruct((PADDED_ROWS,), I32),
        scratch_types=[pltpu.VMEM((rows_per_w,), I32), pltpu.VMEM((SC_SCAN_CHUNK,), I32)],
        compiler_params=pltpu.CompilerParams(needs_layout_passes=False),
        name="sc_inverse_map",
    )
    def inverse_map(d0_hbm, d1_hbm, tok_hbm, tok_v, dchunk_v):
        wid = lax.axis_index("s") * SC_CORES + lax.axis_index("c")
        lo = wid * rows_per_w
        lane = lax.iota(I32, SC_LANES)

        @pl.loop(0, rows_per_w // SC_LANES)
        def _(i):
            tok_v[pl.ds(i * SC_LANES, SC_LANES)] = (lo + i * SC_LANES + lane) & (n_tok - 1)

        for d_hbm in (d0_hbm, d1_hbm):
            @pl.loop(0, n_tok // SC_SCAN_CHUNK)
            def _(c, d_hbm=d_hbm):
                pltpu.sync_copy(d_hbm.at[pl.ds(c * SC_SCAN_CHUNK, SC_SCAN_CHUNK)], dchunk_v)

                @plsc.parallel_loop(0, SC_SCAN_CHUNK // SC_LANES, unroll=4)
                def _(v):
                    local = dchunk_v[pl.ds(v * SC_LANES, SC_LANES)] - lo
                    mine = (local >= 0) & (local < rows_per_w)
                    tok = c * SC_SCAN_CHUNK + v * SC_LANES + lane
                    plsc.store_scatter(tok_v, [jnp.where(mine, local, 0)], tok, mask=mine)

        pltpu.sync_copy(tok_v, tok_hbm.at[pl.ds(lo, rows_per_w)])

    return inverse_map(dest0, dest1)


def _sc_gather(table, idxs):
    n = idxs[0].shape[0]
    width = table.shape[1]
    per_w = n // SC_WORKERS
    assert per_w % (2 * SC_WINDOW) == 0
    mesh = plsc.VectorSubcoreMesh(core_axis_name="c", subcore_axis_name="s")
    out = jax.ShapeDtypeStruct((n, width), table.dtype)
    k = len(idxs)

    @functools.partial(
        pl.kernel, mesh=mesh, out_type=(out,) * k,
        scratch_types=[
            pltpu.VMEM((per_w,), I32),
            pltpu.VMEM((2, SC_WINDOW, width), table.dtype),
            pltpu.SemaphoreType.DMA,
            pltpu.SemaphoreType.DMA((2,)),
        ],
        name="sc_gather",
    )
    def gather(table_hbm, *refs):
        idx_hbms, out_hbms = refs[:k], refs[k:2 * k]
        idx_v, rows_v, gsem, wsems = refs[2 * k:]
        wid = lax.axis_index("s") * SC_CORES + lax.axis_index("c")
        base = wid * per_w
        for idx_hbm, out_hbm in zip(idx_hbms, out_hbms):
            pltpu.sync_copy(idx_hbm.at[pl.ds(base, per_w)], idx_v)
            _sc_move_rows(table_hbm, idx_v, out_hbm, base, per_w // SC_WINDOW, rows_v, gsem, wsems)

    return gather(table, *idxs)


def _experts_kernel(blk_start, be_ref, nv_ref, xs_ref, wg_ref, wu_ref, wd_ref, *rest):
    yb_ref = rest[-1]
    j = pl.program_id(0) + blk_start

    @pl.when(j < nv_ref[0])
    def _():
        lo, hi = _unpack_row_halves(xs_ref[...])
        xb = jnp.concatenate([lo.astype(BF16), hi.astype(BF16)], axis=1)
        a = jnp.dot(xb, wg_ref[0].astype(BF16), preferred_element_type=F32)
        b = jnp.dot(xb, wu_ref[0].astype(BF16), preferred_element_type=F32)
        hb = (jax.nn.silu(a) * b).astype(BF16)
        yb_ref[...] = _pack_row_halves(
            jnp.dot(hb, wd_ref[0].astype(BF16), preferred_element_type=F32))

    @pl.when(j >= nv_ref[0])
    def _():
        yb_ref[...] = jnp.zeros_like(yb_ref)


def _experts(blk_expert, n_valid, xs_part, blk_start, yb_prev, w_gate, w_up, w_down):
    n_blocks = xs_part.shape[0] // ROW_BLOCK

    def row_map(j, be, nv):
        return (jnp.clip(jnp.minimum(j + blk_start, nv[0] - 1) - blk_start, 0, n_blocks - 1), 0)

    def out_map(j, be, nv):
        return (j + blk_start, 0)

    def w_map(j, be, nv):
        return (be[jnp.minimum(j + blk_start, nv[0] - 1)], 0, 0)

    in_specs = [
        pl.BlockSpec((ROW_BLOCK, HALF_MODEL), row_map),
        pl.BlockSpec((1, D_MODEL, EXPERT_FF), w_map),
        pl.BlockSpec((1, D_MODEL, EXPERT_FF), w_map),
        pl.BlockSpec((1, EXPERT_FF, D_MODEL), w_map),
    ]
    operands = [blk_expert, n_valid, xs_part, w_gate, w_up, w_down]
    aliases = {}
    if yb_prev is not None:
        in_specs.append(pl.BlockSpec(memory_space=pl.ANY))
        aliases = {len(operands): 0}
        operands.append(yb_prev)
    grid_spec = pltpu.PrefetchScalarGridSpec(
        num_scalar_prefetch=2,
        grid=(n_blocks,),
        in_specs=in_specs,
        out_specs=pl.BlockSpec((ROW_BLOCK, HALF_MODEL), out_map),
    )
    return pl.pallas_call(
        functools.partial(_experts_kernel, blk_start),
        grid_spec=grid_spec,
        out_shape=jax.ShapeDtypeStruct((PADDED_ROWS, HALF_MODEL), I32),
        input_output_aliases=aliases,
        compiler_params=pltpu.CompilerParams(
            dimension_semantics=("arbitrary",), vmem_limit_bytes=VMEM_LIMIT),
        name="experts",
    )(*operands)


def _combine_kernel(route_ref, h_ref, g_ref, y0_ref, y1_ref, *rest):
    out_ref = rest[-1]
    w1 = route_ref[:, 2:3]
    w2 = route_ref[:, 3:4]
    lo0, hi0 = _unpack_row_halves(y0_ref[...])
    lo1, hi1 = _unpack_row_halves(y1_ref[...])
    y = jnp.concatenate([lo0 * w1 + lo1 * w2, hi0 * w1 + hi1 * w2], axis=1)
    out_ref[...] = _rms(h_ref[...] + y, g_ref[...])


def _combine(route, h2, g, y0, y1, tok_start, out_prev):
    tm = TM_COMBINE
    blk0 = tok_start // tm
    glob = lambda w: pl.BlockSpec((tm, w), lambda i: (i + blk0, 0))
    part = pl.BlockSpec((tm, HALF_MODEL), lambda i: (i, 0))
    in_specs = [glob(8), glob(D_MODEL), _resident((1, D_MODEL)), part, part]
    operands = [route, h2, g, y0, y1]
    aliases = {}
    if out_prev is not None:
        in_specs.append(pl.BlockSpec(memory_space=pl.ANY))
        aliases = {len(operands): 0}
        operands.append(out_prev)
    return pl.pallas_call(
        _combine_kernel,
        grid=(y0.shape[0] // tm,),
        in_specs=in_specs,
        out_specs=glob(D_MODEL),
        out_shape=jax.ShapeDtypeStruct((TOKENS, D_MODEL), F32),
        input_output_aliases=aliases,
        compiler_params=pltpu.CompilerParams(
            dimension_semantics=("parallel",), vmem_limit_bytes=VMEM_LIMIT),
        name="combine",
    )(*operands)


def kernel(x, mem, positions, mix_norm_g, w_in, b_gates, w_spatial, b_spatial, v_norm_g, v_norm_b,
           w_out_a, w_out_b, w_out, xattn_norm_g, mem_norm_g, w_q_x, w_kv_x, w_o_x, moe_norm_g,
           w_router_grp, b_router_grp, w_router_exp, b_router_exp, w_gate_e, w_up_e, w_down_e,
           final_norm_g):
    assert x.shape == (BATCH, SEQ, D_MODEL) and mem.shape == (BATCH, N_MEM, D_MODEL)
    assert mix_norm_g.shape[0] == 1, "single layer"
    x2d = x.reshape(TOKENS, D_MODEL)
    pos_col = positions.reshape(TOKENS, 1).astype(F32)
    half = HEAD_DIM // 2
    inv_freq = ROPE_THETA ** (-jnp.arange(half, dtype=F32) / half)
    invf = jnp.tile(inv_freq, LANES // half).reshape(1, LANES)
    phase = jnp.tile(jnp.concatenate([jnp.zeros((half,), F32), jnp.full((half,), math.pi / 2, F32)]),
                     LANES // HEAD_DIM).reshape(1, LANES)

    kv = _memkv(mem.reshape(BATCH * N_MEM, D_MODEL), mem_norm_g[0].reshape(1, D_MODEL),
                w_kv_x[0].astype(BF16))

    qkv0, qkv1, qkv2, ga, mb = _inproj(
        x2d, pos_col, invf, phase, mix_norm_g[0].reshape(1, D_MODEL), w_in[0].astype(BF16),
        b_gates[0].reshape(1, 2 * D_MODEL), w_spatial[0], b_spatial[0].T,
        v_norm_g[0].reshape(1, GMLP_WIDTH), v_norm_b[0].reshape(1, GMLP_WIDTH),
        w_out_b[0].astype(BF16))

    ya = _attention((qkv0, qkv1, qkv2)).reshape(TOKENS, GROUP_WIDTH)

    pad = LANES - N_EXPERTS - N_EXPERT_GROUPS
    w_r = jnp.concatenate([w_router_exp[0], w_router_grp[0], jnp.zeros((D_MODEL, pad), F32)], axis=1)
    b_r = jnp.concatenate([b_router_exp[0], b_router_grp[0], jnp.zeros((pad,), F32)]).reshape(1, LANES)
    h2, hn2, route, route_t = _post(
        ya, ga, mb, x2d, kv, w_out_a[0].astype(BF16), w_out[0].astype(BF16),
        xattn_norm_g[0].reshape(1, D_MODEL), w_q_x[0].astype(BF16), w_o_x[0].astype(BF16),
        moe_norm_g[0].reshape(1, D_MODEL), w_r.astype(BF16), b_r)

    dest, meta = _route(route_t)
    d0, d1 = dest[0], dest[1]
    tok_of_row = _sc_inverse_map(d0, d1)
    blk_expert = meta[0, :N_ROW_BLOCKS]
    n_valid = (meta[3, N_EXPERTS - 1:N_EXPERTS] // ROW_BLOCK).astype(I32)
    part_rows = PADDED_ROWS // DISPATCH_PARTS
    xs_parts = [
        _sc_gather(hn2, [tok_of_row[p * part_rows:(p + 1) * part_rows]])[0] for p in range(DISPATCH_PARTS)
    ]
    yb = None
    for p, xs_part in enumerate(xs_parts):
        yb = _experts(blk_expert, n_valid, xs_part, p * (part_rows // ROW_BLOCK), yb,
                      w_gate_e[0], w_up_e[0], w_down_e[0])
    g_fin = final_norm_g.reshape(1, D_MODEL)
    part_tok = TOKENS // COMBINE_PARTS
    gathered = [
        _sc_gather(yb, [d0[p * part_tok:(p + 1) * part_tok], d1[p * part_tok:(p + 1) * part_tok]])
        for p in range(COMBINE_PARTS)
    ]
    out = None
    for p, (y0, y1) in enumerate(gathered):
        out = _combine(route, h2, g_fin, y0, y1, p * part_tok, out)
    return out.reshape(BATCH, SEQ, D_MODEL)
```

```python
import functools
import itertools
import math

import jax
import jax.numpy as jnp
from jax import lax
from jax.experimental import pallas as pl
from jax.experimental.pallas import tpu as pltpu
from jax.experimental.pallas import tpu_sc as plsc

F32 = jnp.float32
BF16 = jnp.bfloat16
I32 = jnp.int32

D_MODEL = 1024
BATCH = 16
SEQ = 4096
TOKENS = BATCH * SEQ

HEAD_DIM = 64
DILATIONS = (1, 4, 16)
HEADS_PER_GROUP = 4
GROUP_WIDTH = HEADS_PER_GROUP * HEAD_DIM
ATT_WIDTH = len(DILATIONS) * GROUP_WIDTH
BAND_BLOCK = 128
ROPE_THETA = 10000.0

GMLP_CHUNK = 128
GMLP_GROUPS = 4
GMLP_WIDTH = 512

N_MEM = 256
XATTN_HEADS = 4
XATTN_HEAD_DIM = D_MODEL // XATTN_HEADS

N_EXPERT_GROUPS = 4
EXPERTS_PER_GROUP = 8
N_EXPERTS = 32
TOP_K = 2
EXPERT_FF = 512

RMS_EPS = 1e-6
LN_EPS = 1e-5
NEG_INF = -1e30

LANES = 128

COL_U = 3 * ATT_WIDTH
COL_V = COL_U + GMLP_WIDTH
COL_GA = COL_V + GMLP_WIDTH
COL_GB = COL_GA + D_MODEL

ROW_BLOCK = 512
ASSIGN = TOKENS * TOP_K
PADDED_ROWS = ASSIGN + N_EXPERTS * ROW_BLOCK
N_ROW_BLOCKS = PADDED_ROWS // ROW_BLOCK
META_LANES = ((N_ROW_BLOCKS + LANES - 1) // LANES) * LANES

TM_PROJ = 1024
PROJ_SUB = 512
TM_POST = 1024
POST_SUB = 512
TL_ROUTE = 8192
SCAN_CHUNK = 256
TM_COMBINE = 1024
ATTN_UNROLL = 8

VMEM_LIMIT = 56 * 1024 * 1024


def _rms(x, g):
    return x * lax.rsqrt(jnp.mean(x * x, axis=-1, keepdims=True) + RMS_EPS) * g


HALF_MODEL = D_MODEL // 2


def _pack_row_halves(x):
    return pltpu.pack_elementwise([x[:, :HALF_MODEL], x[:, HALF_MODEL:]], packed_dtype=BF16)


def _unpack_row_halves(p):
    lo = pltpu.unpack_elementwise(p, index=0, packed_dtype=BF16, unpacked_dtype=F32)
    hi = pltpu.unpack_elementwise(p, index=1, packed_dtype=BF16, unpacked_dtype=F32)
    return lo, hi


def _resident(shape):
    nd = len(shape)
    return pl.BlockSpec(shape, lambda *_: (0,) * nd, pipeline_mode=pl.Buffered(1))


def _memkv_kernel(mem_ref, g_ref, w_ref, kv_ref):
    mn = _rms(mem_ref[...], g_ref[...]).astype(BF16)
    kv_ref[...] = jnp.dot(mn, w_ref[...], preferred_element_type=F32).astype(BF16)


def _memkv(mem2d, g, w_kv):
    rows = mem2d.shape[0]
    tm = 512
    return pl.pallas_call(
        _memkv_kernel,
        grid=(rows // tm,),
        in_specs=[
            pl.BlockSpec((tm, D_MODEL), lambda i: (i, 0)),
            _resident((1, D_MODEL)),
            _resident((D_MODEL, 2 * D_MODEL)),
        ],
        out_specs=pl.BlockSpec((tm, 2 * D_MODEL), lambda i: (i, 0)),
        out_shape=jax.ShapeDtypeStruct((rows, 2 * D_MODEL), BF16),
        compiler_params=pltpu.CompilerParams(
            dimension_semantics=("parallel",), vmem_limit_bytes=VMEM_LIMIT),
        name="memkv",
    )(mem2d, g, w_kv)


def _inproj_kernel(x_ref, pos_ref, invf_ref, phase_ref, g_ref, w_ref, bg_ref, wsp_ref,
                   bsp_ref, lng_ref, lnb_ref, wob_ref,
                   qkv0_ref, qkv1_ref, qkv2_ref, ga_ref, mb_ref, scr_ref, yb_ref):
    for sub in range(TM_PROJ // PROJ_SUB):
        _inproj_rows(sub, x_ref, pos_ref, invf_ref, phase_ref, g_ref, w_ref, bg_ref, wsp_ref,
                     bsp_ref, lng_ref, lnb_ref, wob_ref,
                     (qkv0_ref, qkv1_ref, qkv2_ref), ga_ref, mb_ref, scr_ref, yb_ref)


def _inproj_rows(sub, x_ref, pos_ref, invf_ref, phase_ref, g_ref, w_ref, bg_ref, wsp_ref,
                 bsp_ref, lng_ref, lnb_ref, wob_ref, out_refs, ga_ref, mb_ref, scr_ref, yb_ref):
    tm = PROJ_SUB
    rows = slice(sub * tm, (sub + 1) * tm)
    xn = _rms(x_ref[rows, :], g_ref[...]).astype(BF16)

    lane = lax.broadcasted_iota(I32, (tm, LANES), 1)
    upper = (lane & 32) != 0
    t1 = jnp.sin(pos_ref[rows, :] * invf_ref[...] + phase_ref[...])
    cosf = jnp.where(upper, t1, pltpu.roll(t1, 96, 1))
    sinf = jnp.where(upper, pltpu.roll(t1, 32, 1), -t1)

    def rope(res):
        outs = []
        for c in range(GROUP_WIDTH // LANES):
            xt = res[:, c * LANES:(c + 1) * LANES]
            rot = jnp.where(upper, pltpu.roll(xt, 32, 1), pltpu.roll(xt, 96, 1))
            outs.append(xt * cosf + rot * sinf)
        return jnp.concatenate(outs, axis=1)

    zu_raw = jnp.dot(xn, w_ref[:, COL_U:COL_V], preferred_element_type=F32)
    zv_raw = jnp.dot(xn, w_ref[:, COL_V:COL_GA], preferred_element_type=F32)

    slabs = GROUP_WIDTH // LANES

    def project_group(gi):
        dil = DILATIONS[gi]
        for which in range(3):
            c0 = which * ATT_WIDTH + gi * GROUP_WIDTH
            res = jnp.dot(xn, w_ref[:, c0:c0 + GROUP_WIDTH], preferred_element_type=F32)
            if which < 2:
                res = rope(res)
            if which == 0:
                res = res * (HEAD_DIM ** -0.5)
            if dil == 1:
                out_refs[gi][0, which, 0, rows, :] = res.astype(BF16)
            else:
                n = tm // dil
                for c in range(slabs):
                    slot = ((sub * 2 + gi - 1) * 3 + which) * slabs + c
                    scr_ref[slot] = res[:, c * LANES:(c + 1) * LANES]
                    for r in range(dil):
                        out_refs[gi][0, which, r, sub * n:(sub + 1) * n, c * LANES:(c + 1) * LANES] = (
                            scr_ref[slot, pl.ds(r, n, stride=dil), :].astype(BF16))

    for gi in range(len(DILATIONS)):
        project_group(gi)

    zu = jax.nn.gelu(zu_raw)
    zv = jax.nn.gelu(zv_raw)
    mu = jnp.mean(zv, axis=-1, keepdims=True)
    zc = zv - mu
    var = jnp.mean(zc * zc, axis=-1, keepdims=True)
    vn = (zc * lax.rsqrt(var + LN_EPS) * lng_ref[...] + lnb_ref[...]).astype(BF16)
    tri_r = lax.broadcasted_iota(I32, (GMLP_CHUNK, GMLP_CHUNK), 0)
    tri_c = lax.broadcasted_iota(I32, (GMLP_CHUNK, GMLP_CHUNK), 1)
    causal = tri_r >= tri_c
    n_chunks = tm // GMLP_CHUNK
    gw = GMLP_WIDTH // GMLP_GROUPS
    for g in range(GMLP_GROUPS):
        wsg = jnp.where(causal, wsp_ref[g], 0.0).astype(BF16)
        vcat = jnp.concatenate(
            [vn[c * GMLP_CHUNK:(c + 1) * GMLP_CHUNK, g * gw:(g + 1) * gw] for c in range(n_chunks)],
            axis=1)
        mixed = jnp.dot(wsg, vcat, preferred_element_type=F32) + bsp_ref[:, g:g + 1]
        for c in range(n_chunks):
            u_blk = zu[c * GMLP_CHUNK:(c + 1) * GMLP_CHUNK, g * gw:(g + 1) * gw]
            r0 = sub * tm + c * GMLP_CHUNK
            yb_ref[r0:r0 + GMLP_CHUNK, g * gw:(g + 1) * gw] = (
                u_blk * mixed[:, c * gw:(c + 1) * gw]).astype(BF16)

    gate_a = jax.nn.sigmoid(
        jnp.dot(xn, w_ref[:, COL_GA:COL_GB], preferred_element_type=F32) + bg_ref[:, :D_MODEL])
    ga_ref[rows, :] = gate_a.astype(BF16)
    gate_b = jax.nn.sigmoid(
        jnp.dot(xn, w_ref[:, COL_GB:COL_GB + D_MODEL], preferred_element_type=F32) + bg_ref[:, D_MODEL:])
    mb_ref[rows, :] = (gate_b * jnp.dot(yb_ref[rows, :], wob_ref[...], preferred_element_type=F32)).astype(BF16)


def _inproj(x2d, pos_col, invf, phase, g, w_in, b_gates, w_spatial, b_spatial_t, ln_g, ln_b, w_out_b):
    tm = TM_PROJ
    nt = SEQ // tm
    in_cols = w_in.shape[1]
    qkv_shapes = [jax.ShapeDtypeStruct((BATCH, 3, d, SEQ // d, GROUP_WIDTH), BF16) for d in DILATIONS]
    qkv_specs = [
        pl.BlockSpec((1, 3, d, tm // d, GROUP_WIDTH), lambda i: (i // nt, 0, 0, i % nt, 0))
        for d in DILATIONS
    ]
    tok_spec = pl.BlockSpec((tm, D_MODEL), lambda i: (i, 0))
    return pl.pallas_call(
        _inproj_kernel,
        grid=(TOKENS // tm,),
        in_specs=[
            tok_spec,
            pl.BlockSpec((tm, 1), lambda i: (i, 0)),
            _resident((1, LANES)),
            _resident((1, LANES)),
            _resident((1, D_MODEL)),
            _resident((D_MODEL, in_cols)),
            _resident((1, 2 * D_MODEL)),
            _resident((GMLP_GROUPS, GMLP_CHUNK, GMLP_CHUNK)),
            _resident((GMLP_CHUNK, GMLP_GROUPS)),
            _resident((1, GMLP_WIDTH)),
            _resident((1, GMLP_WIDTH)),
            _resident((GMLP_WIDTH, D_MODEL)),
        ],
        out_specs=qkv_specs + [tok_spec, tok_spec],
        out_shape=qkv_shapes + [jax.ShapeDtypeStruct((TOKENS, D_MODEL), BF16)] * 2,
        scratch_shapes=[
            pltpu.VMEM((6 * (GROUP_WIDTH // LANES) * (tm // PROJ_SUB), PROJ_SUB, LANES), F32),
            pltpu.VMEM((tm, GMLP_WIDTH), BF16),
        ],
        compiler_params=pltpu.CompilerParams(
            dimension_semantics=("parallel",), vmem_limit_bytes=VMEM_LIMIT),
        name="inproj",
    )(x2d, pos_col, invf, phase, g, w_in, b_gates, w_spatial, b_spatial_t, ln_g, ln_b, w_out_b)


def _attn_kernel(qkv0_ref, qkv1_ref, qkv2_ref, y_ref, acc_ref, m_ref, z_ref, bias_ref):
    blk = BAND_BLOCK
    lane_row = lax.broadcasted_iota(I32, (1, LANES), 1)
    head0_b = jnp.where(lane_row < HEAD_DIM, 1.0, 0.0).astype(BF16)
    head1_b = jnp.where(lane_row < HEAD_DIM, 0.0, 1.0).astype(BF16)
    head0 = lax.broadcasted_iota(I32, (blk, LANES), 1) < HEAD_DIM
    ones_b = jnp.ones((2 * blk, LANES), BF16)

    qi = lax.broadcasted_iota(I32, (2 * blk, 2 * blk), 0) & (blk - 1)
    kc = lax.broadcasted_iota(I32, (2 * blk, 2 * blk), 1)
    for slot, off in enumerate((0, blk)):
        dist = qi + off - kc
        bias_ref[slot] = jnp.where((dist >= 0) & (dist <= blk), 0.0, NEG_INF)

    for gi, (ref, dil) in enumerate(zip((qkv0_ref, qkv1_ref, qkv2_ref), DILATIONS)):
        seq_len = SEQ // dil
        nb = seq_len // blk
        nb_shift = nb.bit_length() - 1

        def body(i, carry, ref=ref, dil=dil, nb=nb, nb_shift=nb_shift, gi=gi):
            r = lax.shift_right_logical(i, nb_shift)
            n = i & (nb - 1)
            q0 = pl.multiple_of(n * blk, blk)
            w0 = pl.multiple_of(jnp.maximum(n - 1, 0) * blk, blk)
            q = ref[0, 0, r, pl.ds(q0, blk), :]
            k = ref[0, 1, r, pl.ds(w0, 2 * blk), :]
            v = ref[0, 2, r, pl.ds(w0, 2 * blk), :]
            q2 = jnp.concatenate([q * head0_b, q * head1_b], axis=0)
            s = lax.dot_general(q2, k, (((1,), (1,)), ((), ())), preferred_element_type=F32)
            s = s + bias_ref[jnp.minimum(n, 1)]
            m2 = jnp.max(s, axis=-1, keepdims=True)
            p = jnp.exp(s - m2)
            v_ext = jnp.concatenate([v, ones_b], axis=1)
            o2 = jnp.dot(p.astype(BF16), v_ext, preferred_element_type=F32)
            o = jnp.where(head0, o2[:blk, :LANES], o2[blk:, :LANES])
            den = jnp.where(head0, o2[:blk, LANES:], o2[blk:, LANES:])
            m = jnp.where(head0, m2[:blk], m2[blk:])
            if gi == 0:
                acc_ref[pl.ds(q0, blk), :] = o
                m_ref[pl.ds(q0, blk), :] = m
                z_ref[pl.ds(q0, blk), :] = den
            else:
                idx = pl.ds(n * (blk * dil) + r, blk, stride=dil)
                m_old = m_ref[idx, :]
                m_new = jnp.maximum(m_old, m)
                e_old = jnp.exp(m_old - m_new)
                e_new = jnp.exp(m - m_new)
                acc_ref[idx, :] = acc_ref[idx, :] * e_old + o * e_new
                z_ref[idx, :] = z_ref[idx, :] * e_old + den * e_new
                m_ref[idx, :] = m_new
            return carry

        lax.fori_loop(0, dil * nb, body, 0, unroll=ATTN_UNROLL)

    y_ref[0] = (acc_ref[...] / z_ref[...]).astype(BF16)


def _attention(qkv):
    in_specs = [
        pl.BlockSpec((1, 3, d, SEQ // d, LANES), lambda b, h: (b, 0, 0, 0, h)) for d in DILATIONS
    ]
    return pl.pallas_call(
        _attn_kernel,
        grid=(BATCH, GROUP_WIDTH // LANES),
        in_specs=in_specs,
        out_specs=pl.BlockSpec((1, SEQ, LANES), lambda b, h: (b, 0, h)),
        out_shape=jax.ShapeDtypeStruct((BATCH, SEQ, GROUP_WIDTH), BF16),
        scratch_shapes=[
            pltpu.VMEM((SEQ, LANES), F32), pltpu.VMEM((SEQ, LANES), F32), pltpu.VMEM((SEQ, LANES), F32),
            pltpu.VMEM((2, 2 * BAND_BLOCK, 2 * BAND_BLOCK), F32),
        ],
        compiler_params=pltpu.CompilerParams(
            dimension_semantics=("parallel", "parallel"), vmem_limit_bytes=VMEM_LIMIT),
        name="attn",
    )(*qkv)


def _post_kernel(ya_ref, ga_ref, mb_ref, x_ref, k_ref, v_ref, woa_ref, wo_ref, xg_ref, wq_ref,
                 wox_ref, mg_ref, wr_ref, br_ref, h_ref, hn_ref, route_ref, route_t_ref, o_scr):
    for c in range(TM_POST // POST_SUB):
        rows = slice(c * POST_SUB, (c + 1) * POST_SUB)
        _post_rows(rows, ya_ref, ga_ref, mb_ref, x_ref, k_ref, v_ref, woa_ref, wo_ref, xg_ref, wq_ref,
                   wox_ref, mg_ref, wr_ref, br_ref, h_ref, hn_ref, route_ref, route_t_ref, o_scr)


def _post_rows(rows, ya_ref, ga_ref, mb_ref, x_ref, k_ref, v_ref, woa_ref, wo_ref, xg_ref, wq_ref,
               wox_ref, mg_ref, wr_ref, br_ref, h_ref, hn_ref, route_ref, route_t_ref, o_scr):
    tm = POST_SUB
    t = jnp.dot(ya_ref[rows, :], woa_ref[...], preferred_element_type=F32)
    merged = (ga_ref[rows, :].astype(F32) * t + mb_ref[rows, :].astype(F32)).astype(BF16)
    h1 = x_ref[rows, :] + jnp.dot(merged, wo_ref[...], preferred_element_type=F32)

    hn = _rms(h1, xg_ref[...]).astype(BF16)
    q = (jnp.dot(hn, wq_ref[...], preferred_element_type=F32) * (XATTN_HEAD_DIM ** -0.5)).astype(BF16)
    hd = XATTN_HEAD_DIM
    for h in range(XATTN_HEADS):
        s = lax.dot_general(q[:, h * hd:(h + 1) * hd], k_ref[:, h * hd:(h + 1) * hd],
                            (((1,), (1,)), ((), ())), preferred_element_type=F32)
        m = jnp.max(s, axis=-1, keepdims=True)
        p = jnp.exp(s - m)
        den = jnp.sum(p, axis=-1, keepdims=True)
        oh = jnp.dot(p.astype(BF16), v_ref[:, h * hd:(h + 1) * hd], preferred_element_type=F32) / den
        o_scr[rows, h * hd:(h + 1) * hd] = oh.astype(BF16)
    h2 = h1 + jnp.dot(o_scr[rows, :], wox_ref[...], preferred_element_type=F32)
    h_ref[rows, :] = h2

    hn2 = _rms(h2, mg_ref[...])
    hn_ref[rows, :] = _pack_row_halves(hn2)

    logits = jnp.dot(hn2.astype(BF16), wr_ref[...], preferred_element_type=F32) + br_ref[...]
    li = lax.broadcasted_iota(I32, (tm, LANES), 1)
    lif = li.astype(F32)
    grp_of_lane = lax.shift_right_logical(li, 3).astype(F32)
    is_grp = (li >= N_EXPERTS) & (li < N_EXPERTS + N_EXPERT_GROUPS)
    gl = jnp.where(is_grp, logits, -jnp.inf)
    gmax = jnp.max(gl, axis=-1, keepdims=True)
    grp = jnp.min(jnp.where(gl == gmax, lif - N_EXPERTS, float(LANES)), axis=-1, keepdims=True)
    gsum = jnp.sum(jnp.where(is_grp, jnp.exp(logits - gmax), 0.0), axis=-1, keepdims=True)
    grp_gate = 1.0 / gsum
    in_grp = grp_of_lane == grp
    el = jnp.where(in_grp, logits, -jnp.inf)
    v1 = jnp.max(el, axis=-1, keepdims=True)
    i1 = jnp.min(jnp.where(el == v1, lif, float(LANES)), axis=-1, keepdims=True)
    el2 = jnp.where(lif == i1, -jnp.inf, el)
    v2 = jnp.max(el2, axis=-1, keepdims=True)
    i2 = jnp.min(jnp.where(el2 == v2, lif, float(LANES)), axis=-1, keepdims=True)
    tt = jnp.exp(v2 - v1)
    w1 = grp_gate / (1.0 + tt)
    w2 = grp_gate * tt / (1.0 + tt)
    route = jnp.where(li == 0, i1,
                      jnp.where(li == 1, i2,
                                jnp.where(li == 2, w1, jnp.where(li == 3, w2, 0.0))))
    route_ref[rows, :] = route[:, :8]
    route_t_ref[:, rows] = route.T[:8, :]


def _post(ya, ga, mb, x2d, kv, w_out_a, w_out, xg, w_q, w_o, mg, w_r, b_r):
    tm = TM_POST
    nt = SEQ // tm
    tok = lambda w: pl.BlockSpec((tm, w), lambda i: (i, 0))
    return pl.pallas_call(
        _post_kernel,
        grid=(TOKENS // tm,),
        in_specs=[
            tok(GROUP_WIDTH), tok(D_MODEL), tok(D_MODEL), tok(D_MODEL),
            pl.BlockSpec((N_MEM, D_MODEL), lambda i: (i // nt, 0)),
            pl.BlockSpec((N_MEM, D_MODEL), lambda i: (i // nt, 1)),
            _resident((GROUP_WIDTH, D_MODEL)),
            _resident((D_MODEL, D_MODEL)),
            _resident((1, D_MODEL)),
            _resident((D_MODEL, D_MODEL)),
            _resident((D_MODEL, D_MODEL)),
            _resident((1, D_MODEL)),
            _resident((D_MODEL, LANES)),
            _resident((1, LANES)),
        ],
        out_specs=[tok(D_MODEL), tok(HALF_MODEL), pl.BlockSpec((tm, 8), lambda i: (i, 0)),
                   pl.BlockSpec((8, tm), lambda i: (0, i))],
        out_shape=[
            jax.ShapeDtypeStruct((TOKENS, D_MODEL), F32),
            jax.ShapeDtypeStruct((TOKENS, HALF_MODEL), I32),
            jax.ShapeDtypeStruct((TOKENS, 8), F32),
            jax.ShapeDtypeStruct((8, TOKENS), F32),
        ],
        scratch_shapes=[pltpu.VMEM((tm, D_MODEL), BF16)],
        compiler_params=pltpu.CompilerParams(
            dimension_semantics=("parallel",), vmem_limit_bytes=VMEM_LIMIT),
        name="post",
    )(ya, ga, mb, x2d, kv, kv, w_out_a, w_out, xg, w_q, w_o, mg, w_r, b_r)


def _route_kernel(rt_ref, dest_ref, meta_ref, rank_scr, carry_ref, pstart_ref):
    pss = pl.program_id(0)
    i = pl.program_id(1)
    tl = TL_ROUTE
    ch = SCAN_CHUNK
    ei = lax.broadcasted_iota(I32, (N_EXPERTS, ch), 0).astype(F32)

    @pl.when((pss == 0) & (i == 0))
    def _():
        carry_ref[...] = jnp.zeros_like(carry_ref)

    @pl.when(pss == 0)
    def _():
        ur = lax.broadcasted_iota(I32, (ch, ch), 0)
        uc = lax.broadcasted_iota(I32, (ch, ch), 1)
        upper = jnp.where(ur < uc, 1.0, 0.0).astype(BF16)
        for c in range(tl // ch):
            e1 = rt_ref[0:1, c * ch:(c + 1) * ch]
            e2 = rt_ref[1:2, c * ch:(c + 1) * ch]
            oh1 = e1 == ei
            oh2 = e2 == ei
            oh = jnp.where(oh1 | oh2, 1.0, 0.0)
            cnt = jnp.dot(oh.astype(BF16), upper, preferred_element_type=F32) + carry_ref[:, 0:1]
            rank1 = jnp.sum(jnp.where(oh1, cnt, 0.0), axis=0, keepdims=True)
            rank2 = jnp.sum(jnp.where(oh2, cnt, 0.0), axis=0, keepdims=True)
            col = pl.multiple_of(i * tl + c * ch, ch)
            rank_scr[0:1, pl.ds(col, ch)] = rank1
            rank_scr[1:2, pl.ds(col, ch)] = rank2
            carry_ref[...] = carry_ref[...] + jnp.sum(oh, axis=1, keepdims=True)

    @pl.when((pss == 1) & (i == 0))
    def _():
        counts = carry_ref[...].astype(I32)
        padded = lax.shift_left(lax.shift_right_logical(counts + (ROW_BLOCK - 1),
                                                        int(math.log2(ROW_BLOCK))),
                                int(math.log2(ROW_BLOCK)))
        row = lax.broadcasted_iota(I32, (N_EXPERTS, LANES), 0)
        lane = lax.broadcasted_iota(I32, (N_EXPERTS, LANES), 1)
        pend = padded
        sh = 1
        while sh < N_EXPERTS:
            pend = pend + jnp.where(row >= sh, pltpu.roll(pend, sh, 0), 0)
            sh *= 2
        pstart = pend - padded
        pstart_ref[...] = pstart
        diag = row == lane

        def as_row(x):
            return jnp.sum(jnp.where(diag, x, 0), axis=0, keepdims=True)

        blk_lane = lax.broadcasted_iota(I32, (N_EXPERTS, META_LANES), 1) * ROW_BLOCK
        blk_exp = jnp.sum(jnp.where(pend[:, 0:1] <= blk_lane, 1, 0), axis=0, keepdims=True)
        blk_exp = jnp.minimum(blk_exp, N_EXPERTS - 1)
        meta_ref[...] = jnp.zeros_like(meta_ref)
        meta_ref[0:1, :] = blk_exp
        meta_ref[1:2, 0:LANES] = as_row(counts)
        meta_ref[2:3, 0:LANES] = as_row(pstart)
        meta_ref[3:4, 0:LANES] = as_row(pend)

    @pl.when(pss == 1)
    def _():
        ps = pstart_ref[:, 0:1].astype(F32)
        for c in range(tl // ch):
            e1 = rt_ref[0:1, c * ch:(c + 1) * ch]
            e2 = rt_ref[1:2, c * ch:(c + 1) * ch]
            col = pl.multiple_of(i * tl + c * ch, ch)
            d1 = rank_scr[0:1, pl.ds(col, ch)] + jnp.sum(jnp.where(e1 == ei, ps, 0.0), axis=0, keepdims=True)
            d2 = rank_scr[1:2, pl.ds(col, ch)] + jnp.sum(jnp.where(e2 == ei, ps, 0.0), axis=0, keepdims=True)
            dest_ref[0:1, c * ch:(c + 1) * ch] = d1.astype(I32)
            dest_ref[1:2, c * ch:(c + 1) * ch] = d2.astype(I32)


def _route(rt):
    tl = TL_ROUTE
    return pl.pallas_call(
        _route_kernel,
        grid=(2, TOKENS // tl),
        in_specs=[pl.BlockSpec((8, tl), lambda p, i: (0, i))],
        out_specs=[
            pl.BlockSpec((2, tl), lambda p, i: (0, i * p)),
            pl.BlockSpec((8, META_LANES), lambda p, i: (0, 0)),
        ],
        out_shape=[
            jax.ShapeDtypeStruct((2, TOKENS), I32),
            jax.ShapeDtypeStruct((8, META_LANES), I32),
        ],
        scratch_shapes=[
            pltpu.VMEM((2, TOKENS), F32),
            pltpu.VMEM((N_EXPERTS, LANES), F32),
            pltpu.VMEM((N_EXPERTS, LANES), I32),
        ],
        compiler_params=pltpu.CompilerParams(
            dimension_semantics=("arbitrary", "arbitrary"), vmem_limit_bytes=VMEM_LIMIT),
        name="route",
    )(rt)


SC_CORES = 2
SC_SUBCORES = 16
SC_WORKERS = SC_CORES * SC_SUBCORES
SC_LANES = 16
SC_WINDOW = 64
SC_SCAN_CHUNK = 32768
PART_UNIT = SC_WORKERS * 2 * SC_WINDOW
DISPATCH_SPLIT = (4, 10, 10, 12)
COMBINE_SPLIT = (2, 6, 8)


def _sc_move_rows(table_hbm, idx_v, out_hbm, out_base, n_windows, rows_v, gsem, wsems):
    assert n_windows % 2 == 0 and n_windows >= 2

    def gather(j, b):
        idx = idx_v.at[pl.ds(j * SC_WINDOW, SC_WINDOW)]
        pltpu.async_copy(table_hbm.at[idx], rows_v.at[b], gsem).wait()

    def write(j, b):
        dst = out_hbm.at[pl.ds(out_base + j * SC_WINDOW, SC_WINDOW)]
        return pltpu.make_async_copy(rows_v.at[b], dst, wsems.at[b])

    for b in range(2):
        gather(b, b)
        write(b, b).start()

    @pl.loop(2, n_windows, step=2)
    def _(j):
        for b in range(2):
            write(j - 2 + b, b).wait()
            gather(j + b, b)
            write(j + b, b).start()

    for b in range(2):
        write(n_windows - 2 + b, b).wait()


def _sc_inverse_map(dest0, dest1):
    n_tok = dest0.shape[0]
    rows_per_w = PADDED_ROWS // SC_WORKERS
    assert rows_per_w % SC_LANES == 0 and n_tok % SC_SCAN_CHUNK == 0
    mesh = plsc.VectorSubcoreMesh(core_axis_name="c", subcore_axis_name="s")

    @functools.partial(
        pl.kernel, mesh=mesh,
        out_type=jax.ShapeDtypeStruct((PADDED_ROWS,), I32),
        scratch_types=[pltpu.VMEM((rows_per_w,), I32), pltpu.VMEM((SC_SCAN_CHUNK,), I32)],
        compiler_params=pltpu.CompilerParams(needs_layout_passes=False),
        name="sc_inverse_map",
    )
    def inverse_map(d0_hbm, d1_hbm, tok_hbm, tok_v, dchunk_v):
        wid = lax.axis_index("s") * SC_CORES + lax.axis_index("c")
        lo = wid * rows_per_w
        lane = lax.iota(I32, SC_LANES)

        @pl.loop(0, rows_per_w // SC_LANES)
        def _(i):
            tok_v[pl.ds(i * SC_LANES, SC_LANES)] = (lo + i * SC_LANES + lane) & (n_tok - 1)

        for d_hbm in (d0_hbm, d1_hbm):
            @pl.loop(0, n_tok // SC_SCAN_CHUNK)
            def _(c, d_hbm=d_hbm):
                pltpu.sync_copy(d_hbm.at[pl.ds(c * SC_SCAN_CHUNK, SC_SCAN_CHUNK)], dchunk_v)

                @plsc.parallel_loop(0, SC_SCAN_CHUNK // SC_LANES, unroll=4)
                def _(v):
                    local = dchunk_v[pl.ds(v * SC_LANES, SC_LANES)] - lo
                    mine = (local >= 0) & (local < rows_per_w)
                    tok = c * SC_SCAN_CHUNK + v * SC_LANES + lane
                    plsc.store_scatter(tok_v, [jnp.where(mine, local, 0)], tok, mask=mine)

        pltpu.sync_copy(tok_v, tok_hbm.at[pl.ds(lo, rows_per_w)])

    return inverse_map(dest0, dest1)


def _sc_gather(table, idxs):
    n = idxs[0].shape[0]
    width = table.shape[1]
    per_w = n // SC_WORKERS
    assert per_w % (2 * SC_WINDOW) == 0
    mesh = plsc.VectorSubcoreMesh(core_axis_name="c", subcore_axis_name="s")
    out = jax.ShapeDtypeStruct((n, width), table.dtype)
    k = len(idxs)

    @functools.partial(
        pl.kernel, mesh=mesh, out_type=(out,) * k,
        scratch_types=[
            pltpu.VMEM((per_w,), I32),
            pltpu.VMEM((2, SC_WINDOW, width), table.dtype),
            pltpu.SemaphoreType.DMA,
            pltpu.SemaphoreType.DMA((2,)),
        ],
        name="sc_gather",
    )
    def gather(table_hbm, *refs):
        idx_hbms, out_hbms = refs[:k], refs[k:2 * k]
        idx_v, rows_v, gsem, wsems = refs[2 * k:]
        wid = lax.axis_index("s") * SC_CORES + lax.axis_index("c")
        base = wid * per_w
        for idx_hbm, out_hbm in zip(idx_hbms, out_hbms):
            pltpu.sync_copy(idx_hbm.at[pl.ds(base, per_w)], idx_v)
            _sc_move_rows(table_hbm, idx_v, out_hbm, base, per_w // SC_WINDOW, rows_v, gsem, wsems)

    return gather(table, *idxs)


def _experts_kernel(blk_start, be_ref, nv_ref, xs_ref, wg_ref, wu_ref, wd_ref, *rest):
    yb_ref = rest[-1]
    j = pl.program_id(0) + blk_start

    @pl.when(j < nv_ref[0])
    def _():
        lo, hi = _unpack_row_halves(xs_ref[...])
        xb = jnp.concatenate([lo.astype(BF16), hi.astype(BF16)], axis=1)
        a = jnp.dot(xb, wg_ref[0].astype(BF16), preferred_element_type=F32)
        b = jnp.dot(xb, wu_ref[0].astype(BF16), preferred_element_type=F32)
        hb = (jax.nn.silu(a) * b).astype(BF16)
        yb_ref[...] = _pack_row_halves(
            jnp.dot(hb, wd_ref[0].astype(BF16), preferred_element_type=F32))

    @pl.when(j >= nv_ref[0])
    def _():
        yb_ref[...] = jnp.zeros_like(yb_ref)


def _experts(blk_expert, n_valid, xs_part, blk_start, yb_prev, w_gate, w_up, w_down):
    n_blocks = xs_part.shape[0] // ROW_BLOCK

    def row_map(j, be, nv):
        return (jnp.clip(jnp.minimum(j + blk_start, nv[0] - 1) - blk_start, 0, n_blocks - 1), 0)

    def out_map(j, be, nv):
        return (j + blk_start, 0)

    def w_map(j, be, nv):
        return (be[jnp.minimum(j + blk_start, nv[0] - 1)], 0, 0)

    in_specs = [
        pl.BlockSpec((ROW_BLOCK, HALF_MODEL), row_map),
        pl.BlockSpec((1, D_MODEL, EXPERT_FF), w_map),
        pl.BlockSpec((1, D_MODEL, EXPERT_FF), w_map),
        pl.BlockSpec((1, EXPERT_FF, D_MODEL), w_map),
    ]
    operands = [blk_expert, n_valid, xs_part, w_gate, w_up, w_down]
    aliases = {}
    if yb_prev is not None:
        in_specs.append(pl.BlockSpec(memory_space=pl.ANY))
        aliases = {len(operands): 0}
        operands.append(yb_prev)
    grid_spec = pltpu.PrefetchScalarGridSpec(
        num_scalar_prefetch=2,
        grid=(n_blocks,),
        in_specs=in_specs,
        out_specs=pl.BlockSpec((ROW_BLOCK, HALF_MODEL), out_map),
    )
    return pl.pallas_call(
        functools.partial(_experts_kernel, blk_start),
        grid_spec=grid_spec,
        out_shape=jax.ShapeDtypeStruct((PADDED_ROWS, HALF_MODEL), I32),
        input_output_aliases=aliases,
        compiler_params=pltpu.CompilerParams(
            dimension_semantics=("arbitrary",), vmem_limit_bytes=VMEM_LIMIT),
        name="experts",
    )(*operands)


def _combine_kernel(route_ref, h_ref, g_ref, y0_ref, y1_ref, *rest):
    out_ref = rest[-1]
    w1 = route_ref[:, 2:3]
    w2 = route_ref[:, 3:4]
    lo0, hi0 = _unpack_row_halves(y0_ref[...])
    lo1, hi1 = _unpack_row_halves(y1_ref[...])
    y = jnp.concatenate([lo0 * w1 + lo1 * w2, hi0 * w1 + hi1 * w2], axis=1)
    out_ref[...] = _rms(h_ref[...] + y, g_ref[...])


def _combine(route, h2, g, y0, y1, tok_start, out_prev):
    tm = TM_COMBINE
    blk0 = tok_start // tm
    glob = lambda w: pl.BlockSpec((tm, w), lambda i: (i + blk0, 0))
    part = pl.BlockSpec((tm, HALF_MODEL), lambda i: (i, 0))
    in_specs = [glob(8), glob(D_MODEL), _resident((1, D_MODEL)), part, part]
    operands = [route, h2, g, y0, y1]
    aliases = {}
    if out_prev is not None:
        in_specs.append(pl.BlockSpec(memory_space=pl.ANY))
        aliases = {len(operands): 0}
        operands.append(out_prev)
    return pl.pallas_call(
        _combine_kernel,
        grid=(y0.shape[0] // tm,),
        in_specs=in_specs,
        out_specs=glob(D_MODEL),
        out_shape=jax.ShapeDtypeStruct((TOKENS, D_MODEL), F32),
        input_output_aliases=aliases,
        compiler_params=pltpu.CompilerParams(
            dimension_semantics=("parallel",), vmem_limit_bytes=VMEM_LIMIT),
        name="combine",
    )(*operands)


def kernel(x, mem, positions, mix_norm_g, w_in, b_gates, w_spatial, b_spatial, v_norm_g, v_norm_b,
           w_out_a, w_out_b, w_out, xattn_norm_g, mem_norm_g, w_q_x, w_kv_x, w_o_x, moe_norm_g,
           w_router_grp, b_router_grp, w_router_exp, b_router_exp, w_gate_e, w_up_e, w_down_e,
           final_norm_g):
    assert x.shape == (BATCH, SEQ, D_MODEL) and mem.shape == (BATCH, N_MEM, D_MODEL)
    assert mix_norm_g.shape[0] == 1, "single layer"
    x2d = x.reshape(TOKENS, D_MODEL)
    pos_col = positions.reshape(TOKENS, 1).astype(F32)
    half = HEAD_DIM // 2
    inv_freq = ROPE_THETA ** (-jnp.arange(half, dtype=F32) / half)
    invf = jnp.tile(inv_freq, LANES // half).reshape(1, LANES)
    phase = jnp.tile(jnp.concatenate([jnp.zeros((half,), F32), jnp.full((half,), math.pi / 2, F32)]),
                     LANES // HEAD_DIM).reshape(1, LANES)

    kv = _memkv(mem.reshape(BATCH * N_MEM, D_MODEL), mem_norm_g[0].reshape(1, D_MODEL),
                w_kv_x[0].astype(BF16))

    qkv0, qkv1, qkv2, ga, mb = _inproj(
        x2d, pos_col, invf, phase, mix_norm_g[0].reshape(1, D_MODEL), w_in[0].astype(BF16),
        b_gates[0].reshape(1, 2 * D_MODEL), w_spatial[0], b_spatial[0].T,
        v_norm_g[0].reshape(1, GMLP_WIDTH), v_norm_b[0].reshape(1, GMLP_WIDTH),
        w_out_b[0].astype(BF16))

    ya = _attention((qkv0, qkv1, qkv2)).reshape(TOKENS, GROUP_WIDTH)

    pad = LANES - N_EXPERTS - N_EXPERT_GROUPS
    w_r = jnp.concatenate([w_router_exp[0], w_router_grp[0], jnp.zeros((D_MODEL, pad), F32)], axis=1)
    b_r = jnp.concatenate([b_router_exp[0], b_router_grp[0], jnp.zeros((pad,), F32)]).reshape(1, LANES)
    h2, hn2, route, route_t = _post(
        ya, ga, mb, x2d, kv, w_out_a[0].astype(BF16), w_out[0].astype(BF16),
        xattn_norm_g[0].reshape(1, D_MODEL), w_q_x[0].astype(BF16), w_o_x[0].astype(BF16),
        moe_norm_g[0].reshape(1, D_MODEL), w_r.astype(BF16), b_r)

    dest, meta = _route(route_t)
    d0, d1 = dest[0], dest[1]
    tok_of_row = _sc_inverse_map(d0, d1)
    blk_expert = meta[0, :N_ROW_BLOCKS]
    n_valid = (meta[3, N_EXPERTS - 1:N_EXPERTS] // ROW_BLOCK).astype(I32)
    row_cuts = [PART_UNIT * c for c in itertools.accumulate((0,) + DISPATCH_SPLIT)]
    assert row_cuts[-1] == PADDED_ROWS
    xs_parts = [_sc_gather(hn2, [tok_of_row[a:b]])[0] for a, b in zip(row_cuts, row_cuts[1:])]
    yb = None
    for a, xs_part in zip(row_cuts, xs_parts):
        yb = _experts(blk_expert, n_valid, xs_part, a // ROW_BLOCK, yb, w_gate_e[0], w_up_e[0], w_down_e[0])
    g_fin = final_norm_g.reshape(1, D_MODEL)
    tok_cuts = [PART_UNIT * c for c in itertools.accumulate((0,) + COMBINE_SPLIT)]
    assert tok_cuts[-1] == TOKENS
    gathered = [_sc_gather(yb, [d0[a:b], d1[a:b]]) for a, b in zip(tok_cuts, tok_cuts[1:])]
    out = None
    for a, (y0, y1) in zip(tok_cuts, gathered):
        out = _combine(route, h2, g_fin, y0, y1, a, out)
    return out.reshape(BATCH, SEQ, D_MODEL)
```

```python
import functools
import itertools
import math

import jax
import jax.numpy as jnp
from jax import lax
from jax.experimental import pallas as pl
from jax.experimental.pallas import tpu as pltpu
from jax.experimental.pallas import tpu_sc as plsc

F32 = jnp.float32
BF16 = jnp.bfloat16
I32 = jnp.int32

D_MODEL = 1024
BATCH = 16
SEQ = 4096
TOKENS = BATCH * SEQ

HEAD_DIM = 64
DILATIONS = (1, 4, 16)
HEADS_PER_GROUP = 4
GROUP_WIDTH = HEADS_PER_GROUP * HEAD_DIM
ATT_WIDTH = len(DILATIONS) * GROUP_WIDTH
BAND_BLOCK = 128
ROPE_THETA = 10000.0

GMLP_CHUNK = 128
GMLP_GROUPS = 4
GMLP_WIDTH = 512

N_MEM = 256
XATTN_HEADS = 4
XATTN_HEAD_DIM = D_MODEL // XATTN_HEADS

N_EXPERT_GROUPS = 4
EXPERTS_PER_GROUP = 8
N_EXPERTS = 32
TOP_K = 2
EXPERT_FF = 512

RMS_EPS = 1e-6
LN_EPS = 1e-5
NEG_INF = -1e30

LANES = 128

COL_U = 3 * ATT_WIDTH
COL_V = COL_U + GMLP_WIDTH
COL_GA = COL_V + GMLP_WIDTH
COL_GB = COL_GA + D_MODEL

ROW_BLOCK = 512
ASSIGN = TOKENS * TOP_K
PADDED_ROWS = ASSIGN + N_EXPERTS * ROW_BLOCK
N_ROW_BLOCKS = PADDED_ROWS // ROW_BLOCK
META_LANES = ((N_ROW_BLOCKS + LANES - 1) // LANES) * LANES

TM_PROJ = 1024
PROJ_SUB = 512
TM_POST = 1024
POST_SUB = 512
TL_ROUTE = 8192
SCAN_CHUNK = 256
TM_COMBINE = 1024
ATTN_UNROLL = 8

VMEM_LIMIT = 56 * 1024 * 1024


def _rms(x, g):
    return x * lax.rsqrt(jnp.mean(x * x, axis=-1, keepdims=True) + RMS_EPS) * g


HALF_MODEL = D_MODEL // 2


def _pack_row_halves(x):
    return pltpu.pack_elementwise([x[:, :HALF_MODEL], x[:, HALF_MODEL:]], packed_dtype=BF16)


def _unpack_row_halves(p):
    lo = pltpu.unpack_elementwise(p, index=0, packed_dtype=BF16, unpacked_dtype=F32)
    hi = pltpu.unpack_elementwise(p, index=1, packed_dtype=BF16, unpacked_dtype=F32)
    return lo, hi


def _resident(shape):
    nd = len(shape)
    return pl.BlockSpec(shape, lambda *_: (0,) * nd, pipeline_mode=pl.Buffered(1))


def _memkv_kernel(mem_ref, g_ref, w_ref, kv_ref):
    mn = _rms(mem_ref[...], g_ref[...]).astype(BF16)
    kv_ref[...] = jnp.dot(mn, w_ref[...], preferred_element_type=F32).astype(BF16)


def _memkv(mem2d, g, w_kv):
    rows = mem2d.shape[0]
    tm = 512
    return pl.pallas_call(
        _memkv_kernel,
        grid=(rows // tm,),
        in_specs=[
            pl.BlockSpec((tm, D_MODEL), lambda i: (i, 0)),
            _resident((1, D_MODEL)),
            _resident((D_MODEL, 2 * D_MODEL)),
        ],
        out_specs=pl.BlockSpec((tm, 2 * D_MODEL), lambda i: (i, 0)),
        out_shape=jax.ShapeDtypeStruct((rows, 2 * D_MODEL), BF16),
        compiler_params=pltpu.CompilerParams(
            dimension_semantics=("parallel",), vmem_limit_bytes=VMEM_LIMIT),
        name="memkv",
    )(mem2d, g, w_kv)


def _inproj_kernel(x_ref, pos_ref, invf_ref, phase_ref, g_ref, w_ref, bg_ref, wsp_ref,
                   bsp_ref, lng_ref, lnb_ref, wob_ref,
                   qkv0_ref, qkv1_ref, qkv2_ref, ga_ref, mb_ref, scr_ref, yb_ref):
    for sub in range(TM_PROJ // PROJ_SUB):
        _inproj_rows(sub, x_ref, pos_ref, invf_ref, phase_ref, g_ref, w_ref, bg_ref, wsp_ref,
                     bsp_ref, lng_ref, lnb_ref, wob_ref,
                     (qkv0_ref, qkv1_ref, qkv2_ref), ga_ref, mb_ref, scr_ref, yb_ref)


def _inproj_rows(sub, x_ref, pos_ref, invf_ref, phase_ref, g_ref, w_ref, bg_ref, wsp_ref,
                 bsp_ref, lng_ref, lnb_ref, wob_ref, out_refs, ga_ref, mb_ref, scr_ref, yb_ref):
    tm = PROJ_SUB
    rows = slice(sub * tm, (sub + 1) * tm)
    xn = _rms(x_ref[rows, :], g_ref[...]).astype(BF16)

    lane = lax.broadcasted_iota(I32, (tm, LANES), 1)
    upper = (lane & 32) != 0
    t1 = jnp.sin(pos_ref[rows, :] * invf_ref[...] + phase_ref[...])
    cosf = jnp.where(upper, t1, pltpu.roll(t1, 96, 1))
    sinf = jnp.where(upper, pltpu.roll(t1, 32, 1), -t1)

    def rope(res):
        outs = []
        for c in range(GROUP_WIDTH // LANES):
            xt = res[:, c * LANES:(c + 1) * LANES]
            rot = jnp.where(upper, pltpu.roll(xt, 32, 1), pltpu.roll(xt, 96, 1))
            outs.append(xt * cosf + rot * sinf)
        return jnp.concatenate(outs, axis=1)

    zu_raw = jnp.dot(xn, w_ref[:, COL_U:COL_V], preferred_element_type=F32)
    zv_raw = jnp.dot(xn, w_ref[:, COL_V:COL_GA], preferred_element_type=F32)

    slabs = GROUP_WIDTH // LANES

    def project_group(gi):
        dil = DILATIONS[gi]
        for which in range(3):
            c0 = which * ATT_WIDTH + gi * GROUP_WIDTH
            res = jnp.dot(xn, w_ref[:, c0:c0 + GROUP_WIDTH], preferred_element_type=F32)
            if which < 2:
                res = rope(res)
            if which == 0:
                res = res * (HEAD_DIM ** -0.5)
            if dil == 1:
                out_refs[gi][0, which, 0, rows, :] = res.astype(BF16)
            else:
                n = tm // dil
                for c in range(slabs):
                    slot = ((sub * 2 + gi - 1) * 3 + which) * slabs + c
                    scr_ref[slot] = res[:, c * LANES:(c + 1) * LANES]
                    for r in range(dil):
                        out_refs[gi][0, which, r, sub * n:(sub + 1) * n, c * LANES:(c + 1) * LANES] = (
                            scr_ref[slot, pl.ds(r, n, stride=dil), :].astype(BF16))

    for gi in range(len(DILATIONS)):
        project_group(gi)

    zu = jax.nn.gelu(zu_raw)
    zv = jax.nn.gelu(zv_raw)
    mu = jnp.mean(zv, axis=-1, keepdims=True)
    zc = zv - mu
    var = jnp.mean(zc * zc, axis=-1, keepdims=True)
    vn = (zc * lax.rsqrt(var + LN_EPS) * lng_ref[...] + lnb_ref[...]).astype(BF16)
    tri_r = lax.broadcasted_iota(I32, (GMLP_CHUNK, GMLP_CHUNK), 0)
    tri_c = lax.broadcasted_iota(I32, (GMLP_CHUNK, GMLP_CHUNK), 1)
    causal = tri_r >= tri_c
    n_chunks = tm // GMLP_CHUNK
    gw = GMLP_WIDTH // GMLP_GROUPS
    for g in range(GMLP_GROUPS):
        wsg = jnp.where(causal, wsp_ref[g], 0.0).astype(BF16)
        vcat = jnp.concatenate(
            [vn[c * GMLP_CHUNK:(c + 1) * GMLP_CHUNK, g * gw:(g + 1) * gw] for c in range(n_chunks)],
            axis=1)
        mixed = jnp.dot(wsg, vcat, preferred_element_type=F32) + bsp_ref[:, g:g + 1]
        for c in range(n_chunks):
            u_blk = zu[c * GMLP_CHUNK:(c + 1) * GMLP_CHUNK, g * gw:(g + 1) * gw]
            r0 = sub * tm + c * GMLP_CHUNK
            yb_ref[r0:r0 + GMLP_CHUNK, g * gw:(g + 1) * gw] = (
                u_blk * mixed[:, c * gw:(c + 1) * gw]).astype(BF16)

    gate_a = jax.nn.sigmoid(
        jnp.dot(xn, w_ref[:, COL_GA:COL_GB], preferred_element_type=F32) + bg_ref[:, :D_MODEL])
    ga_ref[rows, :] = gate_a.astype(BF16)
    gate_b = jax.nn.sigmoid(
        jnp.dot(xn, w_ref[:, COL_GB:COL_GB + D_MODEL], preferred_element_type=F32) + bg_ref[:, D_MODEL:])
    mb_ref[rows, :] = (gate_b * jnp.dot(yb_ref[rows, :], wob_ref[...], preferred_element_type=F32)).astype(BF16)


def _inproj(x2d, pos_col, invf, phase, g, w_in, b_gates, w_spatial, b_spatial_t, ln_g, ln_b, w_out_b):
    tm = TM_PROJ
    nt = SEQ // tm
    in_cols = w_in.shape[1]
    qkv_shapes = [jax.ShapeDtypeStruct((BATCH, 3, d, SEQ // d, GROUP_WIDTH), BF16) for d in DILATIONS]
    qkv_specs = [
        pl.BlockSpec((1, 3, d, tm // d, GROUP_WIDTH), lambda i: (i // nt, 0, 0, i % nt, 0))
        for d in DILATIONS
    ]
    tok_spec = pl.BlockSpec((tm, D_MODEL), lambda i: (i, 0))
    return pl.pallas_call(
        _inproj_kernel,
        grid=(TOKENS // tm,),
        in_specs=[
            tok_spec,
            pl.BlockSpec((tm, 1), lambda i: (i, 0)),
            _resident((1, LANES)),
            _resident((1, LANES)),
            _resident((1, D_MODEL)),
            _resident((D_MODEL, in_cols)),
            _resident((1, 2 * D_MODEL)),
            _resident((GMLP_GROUPS, GMLP_CHUNK, GMLP_CHUNK)),
            _resident((GMLP_CHUNK, GMLP_GROUPS)),
            _resident((1, GMLP_WIDTH)),
            _resident((1, GMLP_WIDTH)),
            _resident((GMLP_WIDTH, D_MODEL)),
        ],
        out_specs=qkv_specs + [tok_spec, tok_spec],
        out_shape=qkv_shapes + [jax.ShapeDtypeStruct((TOKENS, D_MODEL), BF16)] * 2,
        scratch_shapes=[
            pltpu.VMEM((6 * (GROUP_WIDTH // LANES) * (tm // PROJ_SUB), PROJ_SUB, LANES), F32),
            pltpu.VMEM((tm, GMLP_WIDTH), BF16),
        ],
        compiler_params=pltpu.CompilerParams(
            dimension_semantics=("parallel",), vmem_limit_bytes=VMEM_LIMIT),
        name="inproj",
    )(x2d, pos_col, invf, phase, g, w_in, b_gates, w_spatial, b_spatial_t, ln_g, ln_b, w_out_b)


def _attn_kernel(qkv0_ref, qkv1_ref, qkv2_ref, y_ref, acc_ref, m_ref, z_ref, bias_ref):
    blk = BAND_BLOCK
    lane_row = lax.broadcasted_iota(I32, (1, LANES), 1)
    head0_b = jnp.where(lane_row < HEAD_DIM, 1.0, 0.0).astype(BF16)
    head1_b = jnp.where(lane_row < HEAD_DIM, 0.0, 1.0).astype(BF16)
    head0 = lax.broadcasted_iota(I32, (blk, LANES), 1) < HEAD_DIM
    ones_b = jnp.ones((2 * blk, LANES), BF16)

    qi = lax.broadcasted_iota(I32, (2 * blk, 2 * blk), 0) & (blk - 1)
    kc = lax.broadcasted_iota(I32, (2 * blk, 2 * blk), 1)
    for slot, off in enumerate((0, blk)):
        dist = qi + off - kc
        bias_ref[slot] = jnp.where((dist >= 0) & (dist <= blk), 0.0, NEG_INF)

    for gi, (ref, dil) in enumerate(zip((qkv0_ref, qkv1_ref, qkv2_ref), DILATIONS)):
        seq_len = SEQ // dil
        nb = seq_len // blk
        nb_shift = nb.bit_length() - 1

        def body(i, carry, ref=ref, dil=dil, nb=nb, nb_shift=nb_shift, gi=gi):
            r = lax.shift_right_logical(i, nb_shift)
            n = i & (nb - 1)
            q0 = pl.multiple_of(n * blk, blk)
            w0 = pl.multiple_of(jnp.maximum(n - 1, 0) * blk, blk)
            q = ref[0, 0, r, pl.ds(q0, blk), :]
            k = ref[0, 1, r, pl.ds(w0, 2 * blk), :]
            v = ref[0, 2, r, pl.ds(w0, 2 * blk), :]
            q2 = jnp.concatenate([q * head0_b, q * head1_b], axis=0)
            s = lax.dot_general(q2, k, (((1,), (1,)), ((), ())), preferred_element_type=F32)
            s = s + bias_ref[jnp.minimum(n, 1)]
            m2 = jnp.max(s, axis=-1, keepdims=True)
            p = jnp.exp(s - m2)
            v_ext = jnp.concatenate([v, ones_b], axis=1)
            o2 = jnp.dot(p.astype(BF16), v_ext, preferred_element_type=F32)
            o = jnp.where(head0, o2[:blk, :LANES], o2[blk:, :LANES])
            den = jnp.where(head0, o2[:blk, LANES:], o2[blk:, LANES:])
            m = jnp.where(head0, m2[:blk], m2[blk:])
            if gi == 0:
                acc_ref[pl.ds(q0, blk), :] = o
                m_ref[pl.ds(q0, blk), :] = m
                z_ref[pl.ds(q0, blk), :] = den
            else:
                idx = pl.ds(n * (blk * dil) + r, blk, stride=dil)
                m_old = m_ref[idx, :]
                m_new = jnp.maximum(m_old, m)
                e_old = jnp.exp(m_old - m_new)
                e_new = jnp.exp(m - m_new)
                acc_ref[idx, :] = acc_ref[idx, :] * e_old + o * e_new
                z_ref[idx, :] = z_ref[idx, :] * e_old + den * e_new
                m_ref[idx, :] = m_new
            return carry

        lax.fori_loop(0, dil * nb, body, 0, unroll=ATTN_UNROLL)

    y_ref[0] = (acc_ref[...] / z_ref[...]).astype(BF16)


def _attention(qkv):
    in_specs = [
        pl.BlockSpec((1, 3, d, SEQ // d, LANES), lambda b, h: (b, 0, 0, 0, h)) for d in DILATIONS
    ]
    return pl.pallas_call(
        _attn_kernel,
        grid=(BATCH, GROUP_WIDTH // LANES),
        in_specs=in_specs,
        out_specs=pl.BlockSpec((1, SEQ, LANES), lambda b, h: (b, 0, h)),
        out_shape=jax.ShapeDtypeStruct((BATCH, SEQ, GROUP_WIDTH), BF16),
        scratch_shapes=[
            pltpu.VMEM((SEQ, LANES), F32), pltpu.VMEM((SEQ, LANES), F32), pltpu.VMEM((SEQ, LANES), F32),
            pltpu.VMEM((2, 2 * BAND_BLOCK, 2 * BAND_BLOCK), F32),
        ],
        compiler_params=pltpu.CompilerParams(
            dimension_semantics=("parallel", "parallel"), vmem_limit_bytes=VMEM_LIMIT),
        name="attn",
    )(*qkv)


def _post_kernel(ya_ref, ga_ref, mb_ref, x_ref, k_ref, v_ref, woa_ref, wo_ref, xg_ref, wq_ref,
                 wox_ref, mg_ref, wr_ref, br_ref, h_ref, hn_ref, route_ref, route_t_ref, o_scr):
    for c in range(TM_POST // POST_SUB):
        rows = slice(c * POST_SUB, (c + 1) * POST_SUB)
        _post_rows(rows, ya_ref, ga_ref, mb_ref, x_ref, k_ref, v_ref, woa_ref, wo_ref, xg_ref, wq_ref,
                   wox_ref, mg_ref, wr_ref, br_ref, h_ref, hn_ref, route_ref, route_t_ref, o_scr)


def _post_rows(rows, ya_ref, ga_ref, mb_ref, x_ref, k_ref, v_ref, woa_ref, wo_ref, xg_ref, wq_ref,
               wox_ref, mg_ref, wr_ref, br_ref, h_ref, hn_ref, route_ref, route_t_ref, o_scr):
    tm = POST_SUB
    t = jnp.dot(ya_ref[rows, :], woa_ref[...], preferred_element_type=F32)
    merged = (ga_ref[rows, :].astype(F32) * t + mb_ref[rows, :].astype(F32)).astype(BF16)
    h1 = x_ref[rows, :] + jnp.dot(merged, wo_ref[...], preferred_element_type=F32)

    hn = _rms(h1, xg_ref[...]).astype(BF16)
    q = (jnp.dot(hn, wq_ref[...], preferred_element_type=F32) * (XATTN_HEAD_DIM ** -0.5)).astype(BF16)
    hd = XATTN_HEAD_DIM
    for h in range(XATTN_HEADS):
        s = lax.dot_general(q[:, h * hd:(h + 1) * hd], k_ref[:, h * hd:(h + 1) * hd],
                            (((1,), (1,)), ((), ())), preferred_element_type=F32)
        m = jnp.max(s, axis=-1, keepdims=True)
        p = jnp.exp(s - m)
        den = jnp.sum(p, axis=-1, keepdims=True)
        oh = jnp.dot(p.astype(BF16), v_ref[:, h * hd:(h + 1) * hd], preferred_element_type=F32) / den
        o_scr[rows, h * hd:(h + 1) * hd] = oh.astype(BF16)
    h2 = h1 + jnp.dot(o_scr[rows, :], wox_ref[...], preferred_element_type=F32)
    h_ref[rows, :] = h2

    hn2 = _rms(h2, mg_ref[...])
    hn_ref[rows, :] = _pack_row_halves(hn2)

    logits = jnp.dot(hn2.astype(BF16), wr_ref[...], preferred_element_type=F32) + br_ref[...]
    li = lax.broadcasted_iota(I32, (tm, LANES), 1)
    lif = li.astype(F32)
    grp_of_lane = lax.shift_right_logical(li, 3).astype(F32)
    is_grp = (li >= N_EXPERTS) & (li < N_EXPERTS + N_EXPERT_GROUPS)
    gl = jnp.where(is_grp, logits, -jnp.inf)
    gmax = jnp.max(gl, axis=-1, keepdims=True)
    grp = jnp.min(jnp.where(gl == gmax, lif - N_EXPERTS, float(LANES)), axis=-1, keepdims=True)
    gsum = jnp.sum(jnp.where(is_grp, jnp.exp(logits - gmax), 0.0), axis=-1, keepdims=True)
    grp_gate = 1.0 / gsum
    in_grp = grp_of_lane == grp
    el = jnp.where(in_grp, logits, -jnp.inf)
    v1 = jnp.max(el, axis=-1, keepdims=True)
    i1 = jnp.min(jnp.where(el == v1, lif, float(LANES)), axis=-1, keepdims=True)
    el2 = jnp.where(lif == i1, -jnp.inf, el)
    v2 = jnp.max(el2, axis=-1, keepdims=True)
    i2 = jnp.min(jnp.where(el2 == v2, lif, float(LANES)), axis=-1, keepdims=True)
    tt = jnp.exp(v2 - v1)
    w1 = grp_gate / (1.0 + tt)
    w2 = grp_gate * tt / (1.0 + tt)
    route = jnp.where(li == 0, i1,
                      jnp.where(li == 1, i2,
                                jnp.where(li == 2, w1, jnp.where(li == 3, w2, 0.0))))
    route_ref[rows, :] = route[:, :8]
    route_t_ref[:, rows] = route.T[:8, :]


def _post(ya, ga, mb, x2d, kv, w_out_a, w_out, xg, w_q, w_o, mg, w_r, b_r):
    tm = TM_POST
    nt = SEQ // tm
    tok = lambda w: pl.BlockSpec((tm, w), lambda i: (i, 0))
    return pl.pallas_call(
        _post_kernel,
        grid=(TOKENS // tm,),
        in_specs=[
            tok(GROUP_WIDTH), tok(D_MODEL), tok(D_MODEL), tok(D_MODEL),
            pl.BlockSpec((N_MEM, D_MODEL), lambda i: (i // nt, 0)),
            pl.BlockSpec((N_MEM, D_MODEL), lambda i: (i // nt, 1)),
            _resident((GROUP_WIDTH, D_MODEL)),
            _resident((D_MODEL, D_MODEL)),
            _resident((1, D_MODEL)),
            _resident((D_MODEL, D_MODEL)),
            _resident((D_MODEL, D_MODEL)),
            _resident((1, D_MODEL)),
            _resident((D_MODEL, LANES)),
            _resident((1, LANES)),
        ],
        out_specs=[tok(D_MODEL), tok(HALF_MODEL), pl.BlockSpec((tm, 8), lambda i: (i, 0)),
                   pl.BlockSpec((8, tm), lambda i: (0, i))],
        out_shape=[
            jax.ShapeDtypeStruct((TOKENS, D_MODEL), F32),
            jax.ShapeDtypeStruct((TOKENS, HALF_MODEL), I32),
            jax.ShapeDtypeStruct((TOKENS, 8), F32),
            jax.ShapeDtypeStruct((8, TOKENS), F32),
        ],
        scratch_shapes=[pltpu.VMEM((tm, D_MODEL), BF16)],
        compiler_params=pltpu.CompilerParams(
            dimension_semantics=("parallel",), vmem_limit_bytes=VMEM_LIMIT),
        name="post",
    )(ya, ga, mb, x2d, kv, kv, w_out_a, w_out, xg, w_q, w_o, mg, w_r, b_r)


def _route_kernel(rt_ref, dest_ref, meta_ref, rank_scr, carry_ref, pstart_ref):
    pss = pl.program_id(0)
    i = pl.program_id(1)
    tl = TL_ROUTE
    ch = SCAN_CHUNK
    ei = lax.broadcasted_iota(I32, (N_EXPERTS, ch), 0).astype(F32)

    @pl.when((pss == 0) & (i == 0))
    def _():
        carry_ref[...] = jnp.zeros_like(carry_ref)

    @pl.when(pss == 0)
    def _():
        ur = lax.broadcasted_iota(I32, (ch, ch), 0)
        uc = lax.broadcasted_iota(I32, (ch, ch), 1)
        upper = jnp.where(ur < uc, 1.0, 0.0).astype(BF16)
        for c in range(tl // ch):
            e1 = rt_ref[0:1, c * ch:(c + 1) * ch]
            e2 = rt_ref[1:2, c * ch:(c + 1) * ch]
            oh1 = e1 == ei
            oh2 = e2 == ei
            oh = jnp.where(oh1 | oh2, 1.0, 0.0)
            cnt = jnp.dot(oh.astype(BF16), upper, preferred_element_type=F32) + carry_ref[:, 0:1]
            rank1 = jnp.sum(jnp.where(oh1, cnt, 0.0), axis=0, keepdims=True)
            rank2 = jnp.sum(jnp.where(oh2, cnt, 0.0), axis=0, keepdims=True)
            col = pl.multiple_of(i * tl + c * ch, ch)
            rank_scr[0:1, pl.ds(col, ch)] = rank1
            rank_scr[1:2, pl.ds(col, ch)] = rank2
            carry_ref[...] = carry_ref[...] + jnp.sum(oh, axis=1, keepdims=True)

    @pl.when((pss == 1) & (i == 0))
    def _():
        counts = carry_ref[...].astype(I32)
        padded = lax.shift_left(lax.shift_right_logical(counts + (ROW_BLOCK - 1),
                                                        int(math.log2(ROW_BLOCK))),
                                int(math.log2(ROW_BLOCK)))
        row = lax.broadcasted_iota(I32, (N_EXPERTS, LANES), 0)
        lane = lax.broadcasted_iota(I32, (N_EXPERTS, LANES), 1)
        pend = padded
        sh = 1
        while sh < N_EXPERTS:
            pend = pend + jnp.where(row >= sh, pltpu.roll(pend, sh, 0), 0)
            sh *= 2
        pstart = pend - padded
        pstart_ref[...] = pstart
        diag = row == lane

        def as_row(x):
            return jnp.sum(jnp.where(diag, x, 0), axis=0, keepdims=True)

        blk_lane = lax.broadcasted_iota(I32, (N_EXPERTS, META_LANES), 1) * ROW_BLOCK
        blk_exp = jnp.sum(jnp.where(pend[:, 0:1] <= blk_lane, 1, 0), axis=0, keepdims=True)
        blk_exp = jnp.minimum(blk_exp, N_EXPERTS - 1)
        meta_ref[...] = jnp.zeros_like(meta_ref)
        meta_ref[0:1, :] = blk_exp
        meta_ref[3:4, 0:LANES] = as_row(pend)

    @pl.when(pss == 1)
    def _():
        ps = pstart_ref[:, 0:1].astype(F32)
        for c in range(tl // ch):
            e1 = rt_ref[0:1, c * ch:(c + 1) * ch]
            e2 = rt_ref[1:2, c * ch:(c + 1) * ch]
            col = pl.multiple_of(i * tl + c * ch, ch)
            d1 = rank_scr[0:1, pl.ds(col, ch)] + jnp.sum(jnp.where(e1 == ei, ps, 0.0), axis=0, keepdims=True)
            d2 = rank_scr[1:2, pl.ds(col, ch)] + jnp.sum(jnp.where(e2 == ei, ps, 0.0), axis=0, keepdims=True)
            dest_ref[0:1, c * ch:(c + 1) * ch] = d1.astype(I32)
            dest_ref[1:2, c * ch:(c + 1) * ch] = d2.astype(I32)


def _route(rt):
    tl = TL_ROUTE
    return pl.pallas_call(
        _route_kernel,
        grid=(2, TOKENS // tl),
        in_specs=[pl.BlockSpec((8, tl), lambda p, i: (0, i))],
        out_specs=[
            pl.BlockSpec((2, tl), lambda p, i: (0, i * p)),
            pl.BlockSpec((8, META_LANES), lambda p, i: (0, 0)),
        ],
        out_shape=[
            jax.ShapeDtypeStruct((2, TOKENS), I32),
            jax.ShapeDtypeStruct((8, META_LANES), I32),
        ],
        scratch_shapes=[
            pltpu.VMEM((2, TOKENS), F32),
            pltpu.VMEM((N_EXPERTS, LANES), F32),
            pltpu.VMEM((N_EXPERTS, LANES), I32),
        ],
        compiler_params=pltpu.CompilerParams(
            dimension_semantics=("arbitrary", "arbitrary"), vmem_limit_bytes=VMEM_LIMIT),
        name="route",
    )(rt)


SC_CORES = 2
SC_SUBCORES = 16
SC_WORKERS = SC_CORES * SC_SUBCORES
SC_LANES = 16
SC_WINDOW = 64
SC_SCAN_CHUNK = 32768
PART_UNIT = SC_WORKERS * 2 * SC_WINDOW
DISPATCH_SPLIT = (4, 10, 10, 12)
COMBINE_SPLIT = (2, 6, 8)


def _sc_move_rows(table_hbm, idx_v, out_hbm, out_base, n_windows, rows_v, gsem, wsems):
    assert n_windows % 2 == 0 and n_windows >= 2

    def gather(j, b):
        idx = idx_v.at[pl.ds(j * SC_WINDOW, SC_WINDOW)]
        pltpu.async_copy(table_hbm.at[idx], rows_v.at[b], gsem).wait()

    def write(j, b):
        dst = out_hbm.at[pl.ds(out_base + j * SC_WINDOW, SC_WINDOW)]
        return pltpu.make_async_copy(rows_v.at[b], dst, wsems.at[b])

    for b in range(2):
        gather(b, b)
        write(b, b).start()

    @pl.loop(2, n_windows, step=2)
    def _(j):
        for b in range(2):
            write(j - 2 + b, b).wait()
            gather(j + b, b)
            write(j + b, b).start()

    for b in range(2):
        write(n_windows - 2 + b, b).wait()


def _sc_inverse_map(dest0, dest1):
    n_tok = dest0.shape[0]
    rows_per_w = PADDED_ROWS // SC_WORKERS
    assert rows_per_w % SC_LANES == 0 and n_tok % SC_SCAN_CHUNK == 0
    mesh = plsc.VectorSubcoreMesh(core_axis_name="c", subcore_axis_name="s")

    @functools.partial(
        pl.kernel, mesh=mesh,
        out_type=jax.ShapeDtypeStruct((PADDED_ROWS,), I32),
        scratch_types=[pltpu.VMEM((rows_per_w,), I32), pltpu.VMEM((SC_SCAN_CHUNK,), I32)],
        compiler_params=pltpu.CompilerParams(needs_layout_passes=False),
        name="sc_inverse_map",
    )
    def inverse_map(d0_hbm, d1_hbm, tok_hbm, tok_v, dchunk_v):
        wid = lax.axis_index("s") * SC_CORES + lax.axis_index("c")
        lo = wid * rows_per_w
        lane = lax.iota(I32, SC_LANES)

        @pl.loop(0, rows_per_w // SC_LANES)
        def _(i):
            tok_v[pl.ds(i * SC_LANES, SC_LANES)] = (lo + i * SC_LANES + lane) & (n_tok - 1)

        for d_hbm in (d0_hbm, d1_hbm):
            @pl.loop(0, n_tok // SC_SCAN_CHUNK)
            def _(c, d_hbm=d_hbm):
                pltpu.sync_copy(d_hbm.at[pl.ds(c * SC_SCAN_CHUNK, SC_SCAN_CHUNK)], dchunk_v)

                @plsc.parallel_loop(0, SC_SCAN_CHUNK // SC_LANES, unroll=4)
                def _(v):
                    local = dchunk_v[pl.ds(v * SC_LANES, SC_LANES)] - lo
                    mine = (local >= 0) & (local < rows_per_w)
                    tok = c * SC_SCAN_CHUNK + v * SC_LANES + lane
                    plsc.store_scatter(tok_v, [jnp.where(mine, local, 0)], tok, mask=mine)

        pltpu.sync_copy(tok_v, tok_hbm.at[pl.ds(lo, rows_per_w)])

    return inverse_map(dest0, dest1)


def _sc_gather(table, idxs):
    n = idxs[0].shape[0]
    width = table.shape[1]
    per_w = n // SC_WORKERS
    assert per_w % (2 * SC_WINDOW) == 0
    mesh = plsc.VectorSubcoreMesh(core_axis_name="c", subcore_axis_name="s")
    out = jax.ShapeDtypeStruct((n, width), table.dtype)
    k = len(idxs)

    @functools.partial(
        pl.kernel, mesh=mesh, out_type=(out,) * k,
        scratch_types=[
            pltpu.VMEM((per_w,), I32),
            pltpu.VMEM((2, SC_WINDOW, width), table.dtype),
            pltpu.SemaphoreType.DMA,
            pltpu.SemaphoreType.DMA((2,)),
        ],
        name="sc_gather",
    )
    def gather(table_hbm, *refs):
        idx_hbms, out_hbms = refs[:k], refs[k:2 * k]
        idx_v, rows_v, gsem, wsems = refs[2 * k:]
        wid = lax.axis_index("s") * SC_CORES + lax.axis_index("c")
        base = wid * per_w
        for idx_hbm, out_hbm in zip(idx_hbms, out_hbms):
            pltpu.sync_copy(idx_hbm.at[pl.ds(base, per_w)], idx_v)
            _sc_move_rows(table_hbm, idx_v, out_hbm, base, per_w // SC_WINDOW, rows_v, gsem, wsems)

    return gather(table, *idxs)


def _experts_kernel(blk_start, be_ref, nv_ref, xs_ref, wg_ref, wu_ref, wd_ref, *rest):
    yb_ref = rest[-1]
    j = pl.program_id(0) + blk_start

    @pl.when(j < nv_ref[0])
    def _():
        lo, hi = _unpack_row_halves(xs_ref[...])
        xb = jnp.concatenate([lo.astype(BF16), hi.astype(BF16)], axis=1)
        a = jnp.dot(xb, wg_ref[0].astype(BF16), preferred_element_type=F32)
        b = jnp.dot(xb, wu_ref[0].astype(BF16), preferred_element_type=F32)
        hb = (jax.nn.silu(a) * b).astype(BF16)
        yb_ref[...] = _pack_row_halves(
            jnp.dot(hb, wd_ref[0].astype(BF16), preferred_element_type=F32))

    @pl.when(j >= nv_ref[0])
    def _():
        yb_ref[...] = jnp.zeros_like(yb_ref)


def _experts(blk_expert, n_valid, xs_part, blk_start, yb_prev, w_gate, w_up, w_down):
    n_blocks = xs_part.shape[0] // ROW_BLOCK

    def row_map(j, be, nv):
        return (jnp.clip(jnp.minimum(j + blk_start, nv[0] - 1) - blk_start, 0, n_blocks - 1), 0)

    def out_map(j, be, nv):
        return (j + blk_start, 0)

    def w_map(j, be, nv):
        return (be[jnp.minimum(j + blk_start, nv[0] - 1)], 0, 0)

    in_specs = [
        pl.BlockSpec((ROW_BLOCK, HALF_MODEL), row_map),
        pl.BlockSpec((1, D_MODEL, EXPERT_FF), w_map),
        pl.BlockSpec((1, D_MODEL, EXPERT_FF), w_map),
        pl.BlockSpec((1, EXPERT_FF, D_MODEL), w_map),
    ]
    operands = [blk_expert, n_valid, xs_part, w_gate, w_up, w_down]
    aliases = {}
    if yb_prev is not None:
        in_specs.append(pl.BlockSpec(memory_space=pl.ANY))
        aliases = {len(operands): 0}
        operands.append(yb_prev)
    grid_spec = pltpu.PrefetchScalarGridSpec(
        num_scalar_prefetch=2,
        grid=(n_blocks,),
        in_specs=in_specs,
        out_specs=pl.BlockSpec((ROW_BLOCK, HALF_MODEL), out_map),
    )
    return pl.pallas_call(
        functools.partial(_experts_kernel, blk_start),
        grid_spec=grid_spec,
        out_shape=jax.ShapeDtypeStruct((PADDED_ROWS, HALF_MODEL), I32),
        input_output_aliases=aliases,
        compiler_params=pltpu.CompilerParams(
            dimension_semantics=("arbitrary",), vmem_limit_bytes=VMEM_LIMIT),
        name="experts",
    )(*operands)


def _combine_kernel(route_ref, h_ref, g_ref, y0_ref, y1_ref, *rest):
    out_ref = rest[-1]
    w1 = route_ref[:, 2:3]
    w2 = route_ref[:, 3:4]
    lo0, hi0 = _unpack_row_halves(y0_ref[...])
    lo1, hi1 = _unpack_row_halves(y1_ref[...])
    y = jnp.concatenate([lo0 * w1 + lo1 * w2, hi0 * w1 + hi1 * w2], axis=1)
    out_ref[...] = _rms(h_ref[...] + y, g_ref[...])


def _combine(route, h2, g, y0, y1, tok_start, out_prev):
    tm = TM_COMBINE
    blk0 = tok_start // tm
    glob = lambda w: pl.BlockSpec((tm, w), lambda i: (i + blk0, 0))
    part = pl.BlockSpec((tm, HALF_MODEL), lambda i: (i, 0))
    in_specs = [glob(8), glob(D_MODEL), _resident((1, D_MODEL)), part, part]
    operands = [route, h2, g, y0, y1]
    aliases = {}
    if out_prev is not None:
        in_specs.append(pl.BlockSpec(memory_space=pl.ANY))
        aliases = {len(operands): 0}
        operands.append(out_prev)
    return pl.pallas_call(
        _combine_kernel,
        grid=(y0.shape[0] // tm,),
        in_specs=in_specs,
        out_specs=glob(D_MODEL),
        out_shape=jax.ShapeDtypeStruct((TOKENS, D_MODEL), F32),
        input_output_aliases=aliases,
        compiler_params=pltpu.CompilerParams(
            dimension_semantics=("parallel",), vmem_limit_bytes=VMEM_LIMIT),
        name="combine",
    )(*operands)


def kernel(x, mem, positions, mix_norm_g, w_in, b_gates, w_spatial, b_spatial, v_norm_g, v_norm_b,
           w_out_a, w_out_b, w_out, xattn_norm_g, mem_norm_g, w_q_x, w_kv_x, w_o_x, moe_norm_g,
           w_router_grp, b_router_grp, w_router_exp, b_router_exp, w_gate_e, w_up_e, w_down_e,
           final_norm_g):
    assert x.shape == (BATCH, SEQ, D_MODEL) and mem.shape == (BATCH, N_MEM, D_MODEL)
    assert mix_norm_g.shape[0] == 1, "single layer"
    x2d = x.reshape(TOKENS, D_MODEL)
    pos_col = positions.reshape(TOKENS, 1).astype(F32)
    half = HEAD_DIM // 2
    inv_freq = ROPE_THETA ** (-jnp.arange(half, dtype=F32) / half)
    invf = jnp.tile(inv_freq, LANES // half).reshape(1, LANES)
    phase = jnp.tile(jnp.concatenate([jnp.zeros((half,), F32), jnp.full((half,), math.pi / 2, F32)]),
                     LANES // HEAD_DIM).reshape(1, LANES)

    kv = _memkv(mem.reshape(BATCH * N_MEM, D_MODEL), mem_norm_g[0].reshape(1, D_MODEL),
                w_kv_x[0].astype(BF16))

    qkv0, qkv1, qkv2, ga, mb = _inproj(
        x2d, pos_col, invf, phase, mix_norm_g[0].reshape(1, D_MODEL), w_in[0].astype(BF16),
        b_gates[0].reshape(1, 2 * D_MODEL), w_spatial[0], b_spatial[0].T,
        v_norm_g[0].reshape(1, GMLP_WIDTH), v_norm_b[0].reshape(1, GMLP_WIDTH),
        w_out_b[0].astype(BF16))

    ya = _attention((qkv0, qkv1, qkv2)).reshape(TOKENS, GROUP_WIDTH)

    pad = LANES - N_EXPERTS - N_EXPERT_GROUPS
    w_r = jnp.concatenate([w_router_exp[0], w_router_grp[0], jnp.zeros((D_MODEL, pad), F32)], axis=1)
    b_r = jnp.concatenate([b_router_exp[0], b_router_grp[0], jnp.zeros((pad,), F32)]).reshape(1, LANES)
    h2, hn2, route, route_t = _post(
        ya, ga, mb, x2d, kv, w_out_a[0].astype(BF16), w_out[0].astype(BF16),
        xattn_norm_g[0].reshape(1, D_MODEL), w_q_x[0].astype(BF16), w_o_x[0].astype(BF16),
        moe_norm_g[0].reshape(1, D_MODEL), w_r.astype(BF16), b_r)

    dest, meta = _route(route_t)
    d0, d1 = dest[0], dest[1]
    tok_of_row = _sc_inverse_map(d0, d1)
    blk_expert = meta[0, :N_ROW_BLOCKS]
    n_valid = (meta[3, N_EXPERTS - 1:N_EXPERTS] // ROW_BLOCK).astype(I32)
    row_cuts = [PART_UNIT * c for c in itertools.accumulate((0,) + DISPATCH_SPLIT)]
    assert row_cuts[-1] == PADDED_ROWS
    xs_parts = [_sc_gather(hn2, [tok_of_row[a:b]])[0] for a, b in zip(row_cuts, row_cuts[1:])]
    yb = None
    for a, xs_part in zip(row_cuts, xs_parts):
        yb = _experts(blk_expert, n_valid, xs_part, a // ROW_BLOCK, yb, w_gate_e[0], w_up_e[0], w_down_e[0])
    g_fin = final_norm_g.reshape(1, D_MODEL)
    tok_cuts = [PART_UNIT * c for c in itertools.accumulate((0,) + COMBINE_SPLIT)]
    assert tok_cuts[-1] == TOKENS
    gathered = [_sc_gather(yb, [d0[a:b], d1[a:b]]) for a, b in zip(tok_cuts, tok_cuts[1:])]
    out = None
    for a, (y0, y1) in zip(tok_cuts, gathered):
        out = _combine(route, h2, g_fin, y0, y1, a, out)
    return out.reshape(BATCH, SEQ, D_MODEL)
```

```python
import functools
import itertools
import math

import jax
import jax.numpy as jnp
from jax import lax
from jax.experimental import pallas as pl
from jax.experimental.pallas import tpu as pltpu
from jax.experimental.pallas import tpu_sc as plsc

F32 = jnp.float32
BF16 = jnp.bfloat16
I32 = jnp.int32

D_MODEL = 1024
BATCH = 16
SEQ = 4096
TOKENS = BATCH * SEQ

HEAD_DIM = 64
DILATIONS = (1, 4, 16)
HEADS_PER_GROUP = 4
GROUP_WIDTH = HEADS_PER_GROUP * HEAD_DIM
ATT_WIDTH = len(DILATIONS) * GROUP_WIDTH
BAND_BLOCK = 128
ROPE_THETA = 10000.0

GMLP_CHUNK = 128
GMLP_GROUPS = 4
GMLP_WIDTH = 512

N_MEM = 256
XATTN_HEADS = 4
XATTN_HEAD_DIM = D_MODEL // XATTN_HEADS

N_EXPERT_GROUPS = 4
EXPERTS_PER_GROUP = 8
N_EXPERTS = 32
TOP_K = 2
EXPERT_FF = 512

RMS_EPS = 1e-6
LN_EPS = 1e-5
NEG_INF = -1e30

LANES = 128

COL_U = 3 * ATT_WIDTH
COL_V = COL_U + GMLP_WIDTH
COL_GA = COL_V + GMLP_WIDTH
COL_GB = COL_GA + D_MODEL

ROW_BLOCK = 512
ASSIGN = TOKENS * TOP_K
PADDED_ROWS = ASSIGN + N_EXPERTS * ROW_BLOCK
N_ROW_BLOCKS = PADDED_ROWS // ROW_BLOCK
META_LANES = ((N_ROW_BLOCKS + LANES - 1) // LANES) * LANES

TM_PROJ = 1024
PROJ_SUB = 512
TM_POST = 1024
POST_SUB = 512
TL_ROUTE = 8192
SCAN_CHUNK = 256
TM_COMBINE = 1024
ATTN_UNROLL = (16, 16, 8)

VMEM_LIMIT = 56 * 1024 * 1024


def _rms(x, g):
    return x * lax.rsqrt(jnp.mean(x * x, axis=-1, keepdims=True) + RMS_EPS) * g


HALF_MODEL = D_MODEL // 2


def _pack_row_halves(x):
    return pltpu.pack_elementwise([x[:, :HALF_MODEL], x[:, HALF_MODEL:]], packed_dtype=BF16)


def _unpack_row_halves(p):
    lo = pltpu.unpack_elementwise(p, index=0, packed_dtype=BF16, unpacked_dtype=F32)
    hi = pltpu.unpack_elementwise(p, index=1, packed_dtype=BF16, unpacked_dtype=F32)
    return lo, hi


def _resident(shape):
    nd = len(shape)
    return pl.BlockSpec(shape, lambda *_: (0,) * nd, pipeline_mode=pl.Buffered(1))


def _memkv_kernel(mem_ref, g_ref, w_ref, kv_ref):
    mn = _rms(mem_ref[...], g_ref[...]).astype(BF16)
    kv_ref[...] = jnp.dot(mn, w_ref[...], preferred_element_type=F32).astype(BF16)


def _memkv(mem2d, g, w_kv):
    rows = mem2d.shape[0]
    tm = 512
    return pl.pallas_call(
        _memkv_kernel,
        grid=(rows // tm,),
        in_specs=[
            pl.BlockSpec((tm, D_MODEL), lambda i: (i, 0)),
            _resident((1, D_MODEL)),
            _resident((D_MODEL, 2 * D_MODEL)),
        ],
        out_specs=pl.BlockSpec((tm, 2 * D_MODEL), lambda i: (i, 0)),
        out_shape=jax.ShapeDtypeStruct((rows, 2 * D_MODEL), BF16),
        compiler_params=pltpu.CompilerParams(
            dimension_semantics=("parallel",), vmem_limit_bytes=VMEM_LIMIT),
        name="memkv",
    )(mem2d, g, w_kv)


def _inproj_kernel(x_ref, pos_ref, invf_ref, phase_ref, g_ref, w_ref, bg_ref, wsp_ref,
                   bsp_ref, lng_ref, lnb_ref, wob_ref,
                   qkv0_ref, qkv1_ref, qkv2_ref, ga_ref, mb_ref, scr_ref, yb_ref):
    for sub in range(TM_PROJ // PROJ_SUB):
        _inproj_rows(sub, x_ref, pos_ref, invf_ref, phase_ref, g_ref, w_ref, bg_ref, wsp_ref,
                     bsp_ref, lng_ref, lnb_ref, wob_ref,
                     (qkv0_ref, qkv1_ref, qkv2_ref), ga_ref, mb_ref, scr_ref, yb_ref)


def _inproj_rows(sub, x_ref, pos_ref, invf_ref, phase_ref, g_ref, w_ref, bg_ref, wsp_ref,
                 bsp_ref, lng_ref, lnb_ref, wob_ref, out_refs, ga_ref, mb_ref, scr_ref, yb_ref):
    tm = PROJ_SUB
    rows = slice(sub * tm, (sub + 1) * tm)
    xn = _rms(x_ref[rows, :], g_ref[...]).astype(BF16)

    lane = lax.broadcasted_iota(I32, (tm, LANES), 1)
    upper = (lane & 32) != 0
    t1 = jnp.sin(pos_ref[rows, :] * invf_ref[...] + phase_ref[...])
    cosf = jnp.where(upper, t1, pltpu.roll(t1, 96, 1))
    sinf = jnp.where(upper, pltpu.roll(t1, 32, 1), -t1)

    def rope(res):
        outs = []
        for c in range(GROUP_WIDTH // LANES):
            xt = res[:, c * LANES:(c + 1) * LANES]
            rot = jnp.where(upper, pltpu.roll(xt, 32, 1), pltpu.roll(xt, 96, 1))
            outs.append(xt * cosf + rot * sinf)
        return jnp.concatenate(outs, axis=1)

    zu_raw = jnp.dot(xn, w_ref[:, COL_U:COL_V], preferred_element_type=F32)
    zv_raw = jnp.dot(xn, w_ref[:, COL_V:COL_GA], preferred_element_type=F32)

    slabs = GROUP_WIDTH // LANES

    def project_group(gi):
        dil = DILATIONS[gi]
        for which in range(3):
            c0 = which * ATT_WIDTH + gi * GROUP_WIDTH
            res = jnp.dot(xn, w_ref[:, c0:c0 + GROUP_WIDTH], preferred_element_type=F32)
            if which < 2:
                res = rope(res)
            if which == 0:
                res = res * (HEAD_DIM ** -0.5)
            if dil == 1:
                out_refs[gi][0, which, 0, rows, :] = res.astype(BF16)
            else:
                n = tm // dil
                for c in range(slabs):
                    slot = ((sub * 2 + gi - 1) * 3 + which) * slabs + c
                    scr_ref[slot] = res[:, c * LANES:(c + 1) * LANES]
                    for r in range(dil):
                        out_refs[gi][0, which, r, sub * n:(sub + 1) * n, c * LANES:(c + 1) * LANES] = (
                            scr_ref[slot, pl.ds(r, n, stride=dil), :].astype(BF16))

    for gi in range(len(DILATIONS)):
        project_group(gi)

    zu = jax.nn.gelu(zu_raw)
    zv = jax.nn.gelu(zv_raw)
    mu = jnp.mean(zv, axis=-1, keepdims=True)
    zc = zv - mu
    var = jnp.mean(zc * zc, axis=-1, keepdims=True)
    vn = (zc * lax.rsqrt(var + LN_EPS) * lng_ref[...] + lnb_ref[...]).astype(BF16)
    tri_r = lax.broadcasted_iota(I32, (GMLP_CHUNK, GMLP_CHUNK), 0)
    tri_c = lax.broadcasted_iota(I32, (GMLP_CHUNK, GMLP_CHUNK), 1)
    causal = tri_r >= tri_c
    n_chunks = tm // GMLP_CHUNK
    gw = GMLP_WIDTH // GMLP_GROUPS
    for g in range(GMLP_GROUPS):
        wsg = jnp.where(causal, wsp_ref[g], 0.0).astype(BF16)
        vcat = jnp.concatenate(
            [vn[c * GMLP_CHUNK:(c + 1) * GMLP_CHUNK, g * gw:(g + 1) * gw] for c in range(n_chunks)],
            axis=1)
        mixed = jnp.dot(wsg, vcat, preferred_element_type=F32) + bsp_ref[:, g:g + 1]
        for c in range(n_chunks):
            u_blk = zu[c * GMLP_CHUNK:(c + 1) * GMLP_CHUNK, g * gw:(g + 1) * gw]
            r0 = sub * tm + c * GMLP_CHUNK
            yb_ref[r0:r0 + GMLP_CHUNK, g * gw:(g + 1) * gw] = (
                u_blk * mixed[:, c * gw:(c + 1) * gw]).astype(BF16)

    gate_a = jax.nn.sigmoid(
        jnp.dot(xn, w_ref[:, COL_GA:COL_GB], preferred_element_type=F32) + bg_ref[:, :D_MODEL])
    ga_ref[rows, :] = gate_a.astype(BF16)
    gate_b = jax.nn.sigmoid(
        jnp.dot(xn, w_ref[:, COL_GB:COL_GB + D_MODEL], preferred_element_type=F32) + bg_ref[:, D_MODEL:])
    mb_ref[rows, :] = (gate_b * jnp.dot(yb_ref[rows, :], wob_ref[...], preferred_element_type=F32)).astype(BF16)


def _inproj(x2d, pos_col, invf, phase, g, w_in, b_gates, w_spatial, b_spatial_t, ln_g, ln_b, w_out_b):
    tm = TM_PROJ
    nt = SEQ // tm
    in_cols = w_in.shape[1]
    qkv_shapes = [jax.ShapeDtypeStruct((BATCH, 3, d, SEQ // d, GROUP_WIDTH), BF16) for d in DILATIONS]
    qkv_specs = [
        pl.BlockSpec((1, 3, d, tm // d, GROUP_WIDTH), lambda i: (i // nt, 0, 0, i % nt, 0))
        for d in DILATIONS
    ]
    tok_spec = pl.BlockSpec((tm, D_MODEL), lambda i: (i, 0))
    return pl.pallas_call(
        _inproj_kernel,
        grid=(TOKENS // tm,),
        in_specs=[
            tok_spec,
            pl.BlockSpec((tm, 1), lambda i: (i, 0)),
            _resident((1, LANES)),
            _resident((1, LANES)),
            _resident((1, D_MODEL)),
            _resident((D_MODEL, in_cols)),
            _resident((1, 2 * D_MODEL)),
            _resident((GMLP_GROUPS, GMLP_CHUNK, GMLP_CHUNK)),
            _resident((GMLP_CHUNK, GMLP_GROUPS)),
            _resident((1, GMLP_WIDTH)),
            _resident((1, GMLP_WIDTH)),
            _resident((GMLP_WIDTH, D_MODEL)),
        ],
        out_specs=qkv_specs + [tok_spec, tok_spec],
        out_shape=qkv_shapes + [jax.ShapeDtypeStruct((TOKENS, D_MODEL), BF16)] * 2,
        scratch_shapes=[
            pltpu.VMEM((6 * (GROUP_WIDTH // LANES) * (tm // PROJ_SUB), PROJ_SUB, LANES), F32),
            pltpu.VMEM((tm, GMLP_WIDTH), BF16),
        ],
        compiler_params=pltpu.CompilerParams(
            dimension_semantics=("parallel",), vmem_limit_bytes=VMEM_LIMIT),
        name="inproj",
    )(x2d, pos_col, invf, phase, g, w_in, b_gates, w_spatial, b_spatial_t, ln_g, ln_b, w_out_b)


def _attn_kernel(qkv0_ref, qkv1_ref, qkv2_ref, y_ref, acc_ref, m_ref, z_ref, bias_ref):
    blk = BAND_BLOCK
    lane_row = lax.broadcasted_iota(I32, (1, LANES), 1)
    head0_b = jnp.where(lane_row < HEAD_DIM, 1.0, 0.0).astype(BF16)
    head1_b = jnp.where(lane_row < HEAD_DIM, 0.0, 1.0).astype(BF16)
    head0 = lax.broadcasted_iota(I32, (blk, LANES), 1) < HEAD_DIM
    ones_b = jnp.ones((2 * blk, LANES), BF16)

    qi = lax.broadcasted_iota(I32, (2 * blk, 2 * blk), 0) & (blk - 1)
    kc = lax.broadcasted_iota(I32, (2 * blk, 2 * blk), 1)
    for slot, off in enumerate((0, blk)):
        dist = qi + off - kc
        bias_ref[slot] = jnp.where((dist >= 0) & (dist <= blk), 0.0, NEG_INF)

    for gi, (ref, dil) in enumerate(zip((qkv0_ref, qkv1_ref, qkv2_ref), DILATIONS)):
        seq_len = SEQ // dil
        nb = seq_len // blk
        nb_shift = nb.bit_length() - 1

        def body(i, carry, ref=ref, dil=dil, nb=nb, nb_shift=nb_shift, gi=gi):
            r = lax.shift_right_logical(i, nb_shift)
            n = i & (nb - 1)
            q0 = pl.multiple_of(n * blk, blk)
            w0 = pl.multiple_of(jnp.maximum(n - 1, 0) * blk, blk)
            q = ref[0, 0, r, pl.ds(q0, blk), :]
            k = ref[0, 1, r, pl.ds(w0, 2 * blk), :]
            v = ref[0, 2, r, pl.ds(w0, 2 * blk), :]
            q2 = jnp.concatenate([q * head0_b, q * head1_b], axis=0)
            s = lax.dot_general(q2, k, (((1,), (1,)), ((), ())), preferred_element_type=F32)
            s = s + bias_ref[jnp.minimum(n, 1)]
            m2 = jnp.max(s, axis=-1, keepdims=True)
            p = jnp.exp(s - m2)
            v_ext = jnp.concatenate([v, ones_b], axis=1)
            o2 = jnp.dot(p.astype(BF16), v_ext, preferred_element_type=F32)
            o = jnp.where(head0, o2[:blk, :LANES], o2[blk:, :LANES])
            den = jnp.where(head0, o2[:blk, LANES:], o2[blk:, LANES:])
            m = jnp.where(head0, m2[:blk], m2[blk:])
            if gi == 0:
                acc_ref[pl.ds(q0, blk), :] = o
                m_ref[pl.ds(q0, blk), :] = m
                z_ref[pl.ds(q0, blk), :] = den
            else:
                idx = pl.ds(n * (blk * dil) + r, blk, stride=dil)
                m_old = m_ref[idx, :]
                m_new = jnp.maximum(m_old, m)
                e_old = jnp.exp(m_old - m_new)
                e_new = jnp.exp(m - m_new)
                acc_ref[idx, :] = acc_ref[idx, :] * e_old + o * e_new
                z_ref[idx, :] = z_ref[idx, :] * e_old + den * e_new
                m_ref[idx, :] = m_new
            return carry

        lax.fori_loop(0, dil * nb, body, 0, unroll=ATTN_UNROLL[gi])

    y_ref[0] = (acc_ref[...] / z_ref[...]).astype(BF16)


def _attention(qkv):
    in_specs = [
        pl.BlockSpec((1, 3, d, SEQ // d, LANES), lambda b, h: (b, 0, 0, 0, h)) for d in DILATIONS
    ]
    return pl.pallas_call(
        _attn_kernel,
        grid=(BATCH, GROUP_WIDTH // LANES),
        in_specs=in_specs,
        out_specs=pl.BlockSpec((1, SEQ, LANES), lambda b, h: (b, 0, h)),
        out_shape=jax.ShapeDtypeStruct((BATCH, SEQ, GROUP_WIDTH), BF16),
        scratch_shapes=[
            pltpu.VMEM((SEQ, LANES), F32), pltpu.VMEM((SEQ, LANES), F32), pltpu.VMEM((SEQ, LANES), F32),
            pltpu.VMEM((2, 2 * BAND_BLOCK, 2 * BAND_BLOCK), F32),
        ],
        compiler_params=pltpu.CompilerParams(
            dimension_semantics=("parallel", "parallel"), vmem_limit_bytes=VMEM_LIMIT),
        name="attn",
    )(*qkv)


def _post_kernel(ya_ref, ga_ref, mb_ref, x_ref, k_ref, v_ref, woa_ref, wo_ref, xg_ref, wq_ref,
                 wox_ref, mg_ref, wr_ref, br_ref, h_ref, hn_ref, route_ref, route_t_ref, o_scr):
    for c in range(TM_POST // POST_SUB):
        rows = slice(c * POST_SUB, (c + 1) * POST_SUB)
        _post_rows(rows, ya_ref, ga_ref, mb_ref, x_ref, k_ref, v_ref, woa_ref, wo_ref, xg_ref, wq_ref,
                   wox_ref, mg_ref, wr_ref, br_ref, h_ref, hn_ref, route_ref, route_t_ref, o_scr)


def _post_rows(rows, ya_ref, ga_ref, mb_ref, x_ref, k_ref, v_ref, woa_ref, wo_ref, xg_ref, wq_ref,
               wox_ref, mg_ref, wr_ref, br_ref, h_ref, hn_ref, route_ref, route_t_ref, o_scr):
    tm = POST_SUB
    t = jnp.dot(ya_ref[rows, :], woa_ref[...], preferred_element_type=F32)
    merged = (ga_ref[rows, :].astype(F32) * t + mb_ref[rows, :].astype(F32)).astype(BF16)
    h1 = x_ref[rows, :] + jnp.dot(merged, wo_ref[...], preferred_element_type=F32)

    hn = _rms(h1, xg_ref[...]).astype(BF16)
    q = (jnp.dot(hn, wq_ref[...], preferred_element_type=F32) * (XATTN_HEAD_DIM ** -0.5)).astype(BF16)
    hd = XATTN_HEAD_DIM
    for h in range(XATTN_HEADS):
        s = lax.dot_general(q[:, h * hd:(h + 1) * hd], k_ref[:, h * hd:(h + 1) * hd],
                            (((1,), (1,)), ((), ())), preferred_element_type=F32)
        m = jnp.max(s, axis=-1, keepdims=True)
        p = jnp.exp(s - m)
        den = jnp.sum(p, axis=-1, keepdims=True)
        oh = jnp.dot(p.astype(BF16), v_ref[:, h * hd:(h + 1) * hd], preferred_element_type=F32) / den
        o_scr[rows, h * hd:(h + 1) * hd] = oh.astype(BF16)
    h2 = h1 + jnp.dot(o_scr[rows, :], wox_ref[...], preferred_element_type=F32)
    h_ref[rows, :] = h2

    hn2 = _rms(h2, mg_ref[...])
    hn_ref[rows, :] = _pack_row_halves(hn2)

    logits = jnp.dot(hn2.astype(BF16), wr_ref[...], preferred_element_type=F32) + br_ref[...]
    li = lax.broadcasted_iota(I32, (tm, LANES), 1)
    lif = li.astype(F32)
    grp_of_lane = lax.shift_right_logical(li, 3).astype(F32)
    is_grp = (li >= N_EXPERTS) & (li < N_EXPERTS + N_EXPERT_GROUPS)
    gl = jnp.where(is_grp, logits, -jnp.inf)
    gmax = jnp.max(gl, axis=-1, keepdims=True)
    grp = jnp.min(jnp.where(gl == gmax, lif - N_EXPERTS, float(LANES)), axis=-1, keepdims=True)
    gsum = jnp.sum(jnp.where(is_grp, jnp.exp(logits - gmax), 0.0), axis=-1, keepdims=True)
    grp_gate = 1.0 / gsum
    in_grp = grp_of_lane == grp
    el = jnp.where(in_grp, logits, -jnp.inf)
    v1 = jnp.max(el, axis=-1, keepdims=True)
    i1 = jnp.min(jnp.where(el == v1, lif, float(LANES)), axis=-1, keepdims=True)
    el2 = jnp.where(lif == i1, -jnp.inf, el)
    v2 = jnp.max(el2, axis=-1, keepdims=True)
    i2 = jnp.min(jnp.where(el2 == v2, lif, float(LANES)), axis=-1, keepdims=True)
    tt = jnp.exp(v2 - v1)
    w1 = grp_gate / (1.0 + tt)
    w2 = grp_gate * tt / (1.0 + tt)
    route = jnp.where(li == 0, i1,
                      jnp.where(li == 1, i2,
                                jnp.where(li == 2, w1, jnp.where(li == 3, w2, 0.0))))
    route_ref[rows, :] = route[:, :8]
    route_t_ref[:, rows] = route.T[:8, :]


def _post(ya, ga, mb, x2d, kv, w_out_a, w_out, xg, w_q, w_o, mg, w_r, b_r):
    tm = TM_POST
    nt = SEQ // tm
    tok = lambda w: pl.BlockSpec((tm, w), lambda i: (i, 0))
    return pl.pallas_call(
        _post_kernel,
        grid=(TOKENS // tm,),
        in_specs=[
            tok(GROUP_WIDTH), tok(D_MODEL), tok(D_MODEL), tok(D_MODEL),
            pl.BlockSpec((N_MEM, D_MODEL), lambda i: (i // nt, 0)),
            pl.BlockSpec((N_MEM, D_MODEL), lambda i: (i // nt, 1)),
            _resident((GROUP_WIDTH, D_MODEL)),
            _resident((D_MODEL, D_MODEL)),
            _resident((1, D_MODEL)),
            _resident((D_MODEL, D_MODEL)),
            _resident((D_MODEL, D_MODEL)),
            _resident((1, D_MODEL)),
            _resident((D_MODEL, LANES)),
            _resident((1, LANES)),
        ],
        out_specs=[tok(D_MODEL), tok(HALF_MODEL), pl.BlockSpec((tm, 8), lambda i: (i, 0)),
                   pl.BlockSpec((8, tm), lambda i: (0, i))],
        out_shape=[
            jax.ShapeDtypeStruct((TOKENS, D_MODEL), F32),
            jax.ShapeDtypeStruct((TOKENS, HALF_MODEL), I32),
            jax.ShapeDtypeStruct((TOKENS, 8), F32),
            jax.ShapeDtypeStruct((8, TOKENS), F32),
        ],
        scratch_shapes=[pltpu.VMEM((tm, D_MODEL), BF16)],
        compiler_params=pltpu.CompilerParams(
            dimension_semantics=("parallel",), vmem_limit_bytes=VMEM_LIMIT),
        name="post",
    )(ya, ga, mb, x2d, kv, kv, w_out_a, w_out, xg, w_q, w_o, mg, w_r, b_r)


def _route_kernel(rt_ref, dest_ref, meta_ref, rank_scr, carry_ref, pstart_ref):
    pss = pl.program_id(0)
    i = pl.program_id(1)
    tl = TL_ROUTE
    ch = SCAN_CHUNK
    ei = lax.broadcasted_iota(I32, (N_EXPERTS, ch), 0).astype(F32)

    @pl.when((pss == 0) & (i == 0))
    def _():
        carry_ref[...] = jnp.zeros_like(carry_ref)

    @pl.when(pss == 0)
    def _():
        ur = lax.broadcasted_iota(I32, (ch, ch), 0)
        uc = lax.broadcasted_iota(I32, (ch, ch), 1)
        upper = jnp.where(ur < uc, 1.0, 0.0).astype(BF16)
        for c in range(tl // ch):
            e1 = rt_ref[0:1, c * ch:(c + 1) * ch]
            e2 = rt_ref[1:2, c * ch:(c + 1) * ch]
            oh1 = e1 == ei
            oh2 = e2 == ei
            oh = jnp.where(oh1 | oh2, 1.0, 0.0)
            cnt = jnp.dot(oh.astype(BF16), upper, preferred_element_type=F32) + carry_ref[:, 0:1]
            rank1 = jnp.sum(jnp.where(oh1, cnt, 0.0), axis=0, keepdims=True)
            rank2 = jnp.sum(jnp.where(oh2, cnt, 0.0), axis=0, keepdims=True)
            col = pl.multiple_of(i * tl + c * ch, ch)
            rank_scr[0:1, pl.ds(col, ch)] = rank1
            rank_scr[1:2, pl.ds(col, ch)] = rank2
            carry_ref[...] = carry_ref[...] + jnp.sum(oh, axis=1, keepdims=True)

    @pl.when((pss == 1) & (i == 0))
    def _():
        counts = carry_ref[...].astype(I32)
        padded = lax.shift_left(lax.shift_right_logical(counts + (ROW_BLOCK - 1),
                                                        int(math.log2(ROW_BLOCK))),
                                int(math.log2(ROW_BLOCK)))
        row = lax.broadcasted_iota(I32, (N_EXPERTS, LANES), 0)
        lane = lax.broadcasted_iota(I32, (N_EXPERTS, LANES), 1)
        pend = padded
        sh = 1
        while sh < N_EXPERTS:
            pend = pend + jnp.where(row >= sh, pltpu.roll(pend, sh, 0), 0)
            sh *= 2
        pstart = pend - padded
        pstart_ref[...] = pstart
        diag = row == lane

        def as_row(x):
            return jnp.sum(jnp.where(diag, x, 0), axis=0, keepdims=True)

        blk_lane = lax.broadcasted_iota(I32, (N_EXPERTS, META_LANES), 1) * ROW_BLOCK
        blk_exp = jnp.sum(jnp.where(pend[:, 0:1] <= blk_lane, 1, 0), axis=0, keepdims=True)
        blk_exp = jnp.minimum(blk_exp, N_EXPERTS - 1)
        meta_ref[...] = jnp.zeros_like(meta_ref)
        meta_ref[0:1, :] = blk_exp
        meta_ref[3:4, 0:LANES] = as_row(pend)

    @pl.when(pss == 1)
    def _():
        ps = pstart_ref[:, 0:1].astype(F32)
        for c in range(tl // ch):
            e1 = rt_ref[0:1, c * ch:(c + 1) * ch]
            e2 = rt_ref[1:2, c * ch:(c + 1) * ch]
            col = pl.multiple_of(i * tl + c * ch, ch)
            d1 = rank_scr[0:1, pl.ds(col, ch)] + jnp.sum(jnp.where(e1 == ei, ps, 0.0), axis=0, keepdims=True)
            d2 = rank_scr[1:2, pl.ds(col, ch)] + jnp.sum(jnp.where(e2 == ei, ps, 0.0), axis=0, keepdims=True)
            dest_ref[0:1, c * ch:(c + 1) * ch] = d1.astype(I32)
            dest_ref[1:2, c * ch:(c + 1) * ch] = d2.astype(I32)


def _route(rt):
    tl = TL_ROUTE
    return pl.pallas_call(
        _route_kernel,
        grid=(2, TOKENS // tl),
        in_specs=[pl.BlockSpec((8, tl), lambda p, i: (0, i))],
        out_specs=[
            pl.BlockSpec((2, tl), lambda p, i: (0, i * p)),
            pl.BlockSpec((8, META_LANES), lambda p, i: (0, 0)),
        ],
        out_shape=[
            jax.ShapeDtypeStruct((2, TOKENS), I32),
            jax.ShapeDtypeStruct((8, META_LANES), I32),
        ],
        scratch_shapes=[
            pltpu.VMEM((2, TOKENS), F32),
            pltpu.VMEM((N_EXPERTS, LANES), F32),
            pltpu.VMEM((N_EXPERTS, LANES), I32),
        ],
        compiler_params=pltpu.CompilerParams(
            dimension_semantics=("arbitrary", "arbitrary"), vmem_limit_bytes=VMEM_LIMIT),
        name="route",
    )(rt)


SC_CORES = 2
SC_SUBCORES = 16
SC_WORKERS = SC_CORES * SC_SUBCORES
SC_LANES = 16
SC_WINDOW = 64
SC_SCAN_CHUNK = 32768
PART_UNIT = SC_WORKERS * 2 * SC_WINDOW
DISPATCH_SPLIT = (4, 10, 10, 12)
COMBINE_SPLIT = (2, 6, 8)


def _sc_move_rows(table_hbm, idx_v, out_hbm, out_base, n_windows, rows_v, gsem, wsems):
    assert n_windows % 2 == 0 and n_windows >= 2

    def gather(j, b):
        idx = idx_v.at[pl.ds(j * SC_WINDOW, SC_WINDOW)]
        pltpu.async_copy(table_hbm.at[idx], rows_v.at[b], gsem).wait()

    def write(j, b):
        dst = out_hbm.at[pl.ds(out_base + j * SC_WINDOW, SC_WINDOW)]
        return pltpu.make_async_copy(rows_v.at[b], dst, wsems.at[b])

    for b in range(2):
        gather(b, b)
        write(b, b).start()

    @pl.loop(2, n_windows, step=2)
    def _(j):
        for b in range(2):
            write(j - 2 + b, b).wait()
            gather(j + b, b)
            write(j + b, b).start()

    for b in range(2):
        write(n_windows - 2 + b, b).wait()


def _sc_inverse_map(dest0, dest1):
    n_tok = dest0.shape[0]
    rows_per_w = PADDED_ROWS // SC_WORKERS
    assert rows_per_w % SC_LANES == 0 and n_tok % SC_SCAN_CHUNK == 0
    mesh = plsc.VectorSubcoreMesh(core_axis_name="c", subcore_axis_name="s")

    @functools.partial(
        pl.kernel, mesh=mesh,
        out_type=jax.ShapeDtypeStruct((PADDED_ROWS,), I32),
        scratch_types=[pltpu.VMEM((rows_per_w,), I32), pltpu.VMEM((SC_SCAN_CHUNK,), I32)],
        compiler_params=pltpu.CompilerParams(needs_layout_passes=False),
        name="sc_inverse_map",
    )
    def inverse_map(d0_hbm, d1_hbm, tok_hbm, tok_v, dchunk_v):
        wid = lax.axis_index("s") * SC_CORES + lax.axis_index("c")
        lo = wid * rows_per_w
        lane = lax.iota(I32, SC_LANES)

        @pl.loop(0, rows_per_w // SC_LANES)
        def _(i):
            tok_v[pl.ds(i * SC_LANES, SC_LANES)] = (lo + i * SC_LANES + lane) & (n_tok - 1)

        for d_hbm in (d0_hbm, d1_hbm):
            @pl.loop(0, n_tok // SC_SCAN_CHUNK)
            def _(c, d_hbm=d_hbm):
                pltpu.sync_copy(d_hbm.at[pl.ds(c * SC_SCAN_CHUNK, SC_SCAN_CHUNK)], dchunk_v)

                @plsc.parallel_loop(0, SC_SCAN_CHUNK // SC_LANES, unroll=4)
                def _(v):
                    local = dchunk_v[pl.ds(v * SC_LANES, SC_LANES)] - lo
                    mine = (local >= 0) & (local < rows_per_w)
                    tok = c * SC_SCAN_CHUNK + v * SC_LANES + lane
                    plsc.store_scatter(tok_v, [jnp.where(mine, local, 0)], tok, mask=mine)

        pltpu.sync_copy(tok_v, tok_hbm.at[pl.ds(lo, rows_per_w)])

    return inverse_map(dest0, dest1)


def _sc_gather(table, idxs):
    n = idxs[0].shape[0]
    width = table.shape[1]
    per_w = n // SC_WORKERS
    assert per_w % (2 * SC_WINDOW) == 0
    mesh = plsc.VectorSubcoreMesh(core_axis_name="c", subcore_axis_name="s")
    out = jax.ShapeDtypeStruct((n, width), table.dtype)
    k = len(idxs)

    @functools.partial(
        pl.kernel, mesh=mesh, out_type=(out,) * k,
        scratch_types=[
            pltpu.VMEM((per_w,), I32),
            pltpu.VMEM((2, SC_WINDOW, width), table.dtype),
            pltpu.SemaphoreType.DMA,
            pltpu.SemaphoreType.DMA((2,)),
        ],
        name="sc_gather",
    )
    def gather(table_hbm, *refs):
        idx_hbms, out_hbms = refs[:k], refs[k:2 * k]
        idx_v, rows_v, gsem, wsems = refs[2 * k:]
        wid = lax.axis_index("s") * SC_CORES + lax.axis_index("c")
        base = wid * per_w
        for idx_hbm, out_hbm in zip(idx_hbms, out_hbms):
            pltpu.sync_copy(idx_hbm.at[pl.ds(base, per_w)], idx_v)
            _sc_move_rows(table_hbm, idx_v, out_hbm, base, per_w // SC_WINDOW, rows_v, gsem, wsems)

    return gather(table, *idxs)


def _experts_kernel(blk_start, be_ref, nv_ref, xs_ref, wg_ref, wu_ref, wd_ref, *rest):
    yb_ref = rest[-1]
    j = pl.program_id(0) + blk_start

    @pl.when(j < nv_ref[0])
    def _():
        lo, hi = _unpack_row_halves(xs_ref[...])
        xb = jnp.concatenate([lo.astype(BF16), hi.astype(BF16)], axis=1)
        a = jnp.dot(xb, wg_ref[0].astype(BF16), preferred_element_type=F32)
        b = jnp.dot(xb, wu_ref[0].astype(BF16), preferred_element_type=F32)
        hb = (jax.nn.silu(a) * b).astype(BF16)
        yb_ref[...] = _pack_row_halves(
            jnp.dot(hb, wd_ref[0].astype(BF16), preferred_element_type=F32))

    @pl.when(j >= nv_ref[0])
    def _():
        yb_ref[...] = jnp.zeros_like(yb_ref)


def _experts(blk_expert, n_valid, xs_part, blk_start, yb_prev, w_gate, w_up, w_down):
    n_blocks = xs_part.shape[0] // ROW_BLOCK

    def row_map(j, be, nv):
        return (jnp.clip(jnp.minimum(j + blk_start, nv[0] - 1) - blk_start, 0, n_blocks - 1), 0)

    def out_map(j, be, nv):
        return (j + blk_start, 0)

    def w_map(j, be, nv):
        return (be[jnp.minimum(j + blk_start, nv[0] - 1)], 0, 0)

    in_specs = [
        pl.BlockSpec((ROW_BLOCK, HALF_MODEL), row_map),
        pl.BlockSpec((1, D_MODEL, EXPERT_FF), w_map),
        pl.BlockSpec((1, D_MODEL, EXPERT_FF), w_map),
        pl.BlockSpec((1, EXPERT_FF, D_MODEL), w_map),
    ]
    operands = [blk_expert, n_valid, xs_part, w_gate, w_up, w_down]
    aliases = {}
    if yb_prev is not None:
        in_specs.append(pl.BlockSpec(memory_space=pl.ANY))
        aliases = {len(operands): 0}
        operands.append(yb_prev)
    grid_spec = pltpu.PrefetchScalarGridSpec(
        num_scalar_prefetch=2,
        grid=(n_blocks,),
        in_specs=in_specs,
        out_specs=pl.BlockSpec((ROW_BLOCK, HALF_MODEL), out_map),
    )
    return pl.pallas_call(
        functools.partial(_experts_kernel, blk_start),
        grid_spec=grid_spec,
        out_shape=jax.ShapeDtypeStruct((PADDED_ROWS, HALF_MODEL), I32),
        input_output_aliases=aliases,
        compiler_params=pltpu.CompilerParams(
            dimension_semantics=("arbitrary",), vmem_limit_bytes=VMEM_LIMIT),
        name="experts",
    )(*operands)


def _combine_kernel(route_ref, h_ref, g_ref, y0_ref, y1_ref, *rest):
    out_ref = rest[-1]
    w1 = route_ref[:, 2:3]
    w2 = route_ref[:, 3:4]
    lo0, hi0 = _unpack_row_halves(y0_ref[...])
    lo1, hi1 = _unpack_row_halves(y1_ref[...])
    y = jnp.concatenate([lo0 * w1 + lo1 * w2, hi0 * w1 + hi1 * w2], axis=1)
    out_ref[...] = _rms(h_ref[...] + y, g_ref[...])


def _combine(route, h2, g, y0, y1, tok_start, out_prev):
    tm = TM_COMBINE
    blk0 = tok_start // tm
    glob = lambda w: pl.BlockSpec((tm, w), lambda i: (i + blk0, 0))
    part = pl.BlockSpec((tm, HALF_MODEL), lambda i: (i, 0))
    in_specs = [glob(8), glob(D_MODEL), _resident((1, D_MODEL)), part, part]
    operands = [route, h2, g, y0, y1]
    aliases = {}
    if out_prev is not None:
        in_specs.append(pl.BlockSpec(memory_space=pl.ANY))
        aliases = {len(operands): 0}
        operands.append(out_prev)
    return pl.pallas_call(
        _combine_kernel,
        grid=(y0.shape[0] // tm,),
        in_specs=in_specs,
        out_specs=glob(D_MODEL),
        out_shape=jax.ShapeDtypeStruct((TOKENS, D_MODEL), F32),
        input_output_aliases=aliases,
        compiler_params=pltpu.CompilerParams(
            dimension_semantics=("parallel",), vmem_limit_bytes=VMEM_LIMIT),
        name="combine",
    )(*operands)


def kernel(x, mem, positions, mix_norm_g, w_in, b_gates, w_spatial, b_spatial, v_norm_g, v_norm_b,
           w_out_a, w_out_b, w_out, xattn_norm_g, mem_norm_g, w_q_x, w_kv_x, w_o_x, moe_norm_g,
           w_router_grp, b_router_grp, w_router_exp, b_router_exp, w_gate_e, w_up_e, w_down_e,
           final_norm_g):
    assert x.shape == (BATCH, SEQ, D_MODEL) and mem.shape == (BATCH, N_MEM, D_MODEL)
    assert mix_norm_g.shape[0] == 1, "single layer"
    x2d = x.reshape(TOKENS, D_MODEL)
    pos_col = positions.reshape(TOKENS, 1).astype(F32)
    half = HEAD_DIM // 2
    inv_freq = ROPE_THETA ** (-jnp.arange(half, dtype=F32) / half)
    invf = jnp.tile(inv_freq, LANES // half).reshape(1, LANES)
    phase = jnp.tile(jnp.concatenate([jnp.zeros((half,), F32), jnp.full((half,), math.pi / 2, F32)]),
                     LANES // HEAD_DIM).reshape(1, LANES)

    kv = _memkv(mem.reshape(BATCH * N_MEM, D_MODEL), mem_norm_g[0].reshape(1, D_MODEL),
                w_kv_x[0].astype(BF16))

    qkv0, qkv1, qkv2, ga, mb = _inproj(
        x2d, pos_col, invf, phase, mix_norm_g[0].reshape(1, D_MODEL), w_in[0].astype(BF16),
        b_gates[0].reshape(1, 2 * D_MODEL), w_spatial[0], b_spatial[0].T,
        v_norm_g[0].reshape(1, GMLP_WIDTH), v_norm_b[0].reshape(1, GMLP_WIDTH),
        w_out_b[0].astype(BF16))

    ya = _attention((qkv0, qkv1, qkv2)).reshape(TOKENS, GROUP_WIDTH)

    pad = LANES - N_EXPERTS - N_EXPERT_GROUPS
    w_r = jnp.concatenate([w_router_exp[0], w_router_grp[0], jnp.zeros((D_MODEL, pad), F32)], axis=1)
    b_r = jnp.concatenate([b_router_exp[0], b_router_grp[0], jnp.zeros((pad,), F32)]).reshape(1, LANES)
    h2, hn2, route, route_t = _post(
        ya, ga, mb, x2d, kv, w_out_a[0].astype(BF16), w_out[0].astype(BF16),
        xattn_norm_g[0].reshape(1, D_MODEL), w_q_x[0].astype(BF16), w_o_x[0].astype(BF16),
        moe_norm_g[0].reshape(1, D_MODEL), w_r.astype(BF16), b_r)

    dest, meta = _route(route_t)
    d0, d1 = dest[0], dest[1]
    tok_of_row = _sc_inverse_map(d0, d1)
    blk_expert = meta[0, :N_ROW_BLOCKS]
    n_valid = (meta[3, N_EXPERTS - 1:N_EXPERTS] // ROW_BLOCK).astype(I32)
    row_cuts = [PART_UNIT * c for c in itertools.accumulate((0,) + DISPATCH_SPLIT)]
    assert row_cuts[-1] == PADDED_ROWS
    xs_parts = [_sc_gather(hn2, [tok_of_row[a:b]])[0] for a, b in zip(row_cuts, row_cuts[1:])]
    yb = None
    for a, xs_part in zip(row_cuts, xs_parts):
        yb = _experts(blk_expert, n_valid, xs_part, a // ROW_BLOCK, yb, w_gate_e[0], w_up_e[0], w_down_e[0])
    g_fin = final_norm_g.reshape(1, D_MODEL)
    tok_cuts = [PART_UNIT * c for c in itertools.accumulate((0,) + COMBINE_SPLIT)]
    assert tok_cuts[-1] == TOKENS
    gathered = [_sc_gather(yb, [d0[a:b], d1[a:b]]) for a, b in zip(tok_cuts, tok_cuts[1:])]
    out = None
    for a, (y0, y1) in zip(tok_cuts, gathered):
        out = _combine(route, h2, g_fin, y0, y1, a, out)
    return out.reshape(BATCH, SEQ, D_MODEL)
```

```python
import functools
import itertools
import math

import jax
import jax.numpy as jnp
from jax import lax
from jax.experimental import pallas as pl
from jax.experimental.pallas import tpu as pltpu
from jax.experimental.pallas import tpu_sc as plsc

F32 = jnp.float32
BF16 = jnp.bfloat16
I32 = jnp.int32

D_MODEL = 1024
BATCH = 16
SEQ = 4096
TOKENS = BATCH * SEQ

HEAD_DIM = 64
DILATIONS = (1, 4, 16)
HEADS_PER_GROUP = 4
GROUP_WIDTH = HEADS_PER_GROUP * HEAD_DIM
ATT_WIDTH = len(DILATIONS) * GROUP_WIDTH
BAND_BLOCK = 128
ROPE_THETA = 10000.0

GMLP_CHUNK = 128
GMLP_GROUPS = 4
GMLP_WIDTH = 512

N_MEM = 256
XATTN_HEADS = 4
XATTN_HEAD_DIM = D_MODEL // XATTN_HEADS

N_EXPERT_GROUPS = 4
EXPERTS_PER_GROUP = 8
N_EXPERTS = 32
TOP_K = 2
EXPERT_FF = 512

RMS_EPS = 1e-6
LN_EPS = 1e-5
NEG_INF = -1e30

LANES = 128

COL_U = 3 * ATT_WIDTH
COL_V = COL_U + GMLP_WIDTH
COL_GA = COL_V + GMLP_WIDTH
COL_GB = COL_GA + D_MODEL

ROW_BLOCK = 512
ASSIGN = TOKENS * TOP_K
PADDED_ROWS = ASSIGN + N_EXPERTS * ROW_BLOCK
N_ROW_BLOCKS = PADDED_ROWS // ROW_BLOCK
META_LANES = ((N_ROW_BLOCKS + LANES - 1) // LANES) * LANES

TM_PROJ = 1024
PROJ_SUB = 512
TM_POST = 1024
POST_SUB = 512
TL_ROUTE = 8192
SCAN_CHUNK = 256
TM_COMBINE = 1024
ATTN_UNROLL = (32, 16, 16)

VMEM_LIMIT = 56 * 1024 * 1024


def _rms(x, g):
    return x * lax.rsqrt(jnp.mean(x * x, axis=-1, keepdims=True) + RMS_EPS) * g


HALF_MODEL = D_MODEL // 2


def _pack_row_halves(x):
    return pltpu.pack_elementwise([x[:, :HALF_MODEL], x[:, HALF_MODEL:]], packed_dtype=BF16)


def _unpack_row_halves(p):
    lo = pltpu.unpack_elementwise(p, index=0, packed_dtype=BF16, unpacked_dtype=F32)
    hi = pltpu.unpack_elementwise(p, index=1, packed_dtype=BF16, unpacked_dtype=F32)
    return lo, hi


def _resident(shape):
    nd = len(shape)
    return pl.BlockSpec(shape, lambda *_: (0,) * nd, pipeline_mode=pl.Buffered(1))


def _memkv_kernel(mem_ref, g_ref, w_ref, kv_ref):
    mn = _rms(mem_ref[...], g_ref[...]).astype(BF16)
    kv_ref[...] = jnp.dot(mn, w_ref[...], preferred_element_type=F32).astype(BF16)


def _memkv(mem2d, g, w_kv):
    rows = mem2d.shape[0]
    tm = 512
    return pl.pallas_call(
        _memkv_kernel,
        grid=(rows // tm,),
        in_specs=[
            pl.BlockSpec((tm, D_MODEL), lambda i: (i, 0)),
            _resident((1, D_MODEL)),
            _resident((D_MODEL, 2 * D_MODEL)),
        ],
        out_specs=pl.BlockSpec((tm, 2 * D_MODEL), lambda i: (i, 0)),
        out_shape=jax.ShapeDtypeStruct((rows, 2 * D_MODEL), BF16),
        compiler_params=pltpu.CompilerParams(
            dimension_semantics=("parallel",), vmem_limit_bytes=VMEM_LIMIT),
        name="memkv",
    )(mem2d, g, w_kv)


def _inproj_kernel(x_ref, pos_ref, invf_ref, phase_ref, g_ref, w_ref, bg_ref, wsp_ref,
                   bsp_ref, lng_ref, lnb_ref, wob_ref,
                   qkv0_ref, qkv1_ref, qkv2_ref, ga_ref, mb_ref, scr_ref, yb_ref):
    for sub in range(TM_PROJ // PROJ_SUB):
        _inproj_rows(sub, x_ref, pos_ref, invf_ref, phase_ref, g_ref, w_ref, bg_ref, wsp_ref,
                     bsp_ref, lng_ref, lnb_ref, wob_ref,
                     (qkv0_ref, qkv1_ref, qkv2_ref), ga_ref, mb_ref, scr_ref, yb_ref)


def _inproj_rows(sub, x_ref, pos_ref, invf_ref, phase_ref, g_ref, w_ref, bg_ref, wsp_ref,
                 bsp_ref, lng_ref, lnb_ref, wob_ref, out_refs, ga_ref, mb_ref, scr_ref, yb_ref):
    tm = PROJ_SUB
    rows = slice(sub * tm, (sub + 1) * tm)
    xn = _rms(x_ref[rows, :], g_ref[...]).astype(BF16)

    lane = lax.broadcasted_iota(I32, (tm, LANES), 1)
    upper = (lane & 32) != 0
    t1 = jnp.sin(pos_ref[rows, :] * invf_ref[...] + phase_ref[...])
    cosf = jnp.where(upper, t1, pltpu.roll(t1, 96, 1))
    sinf = jnp.where(upper, pltpu.roll(t1, 32, 1), -t1)

    def rope(res):
        outs = []
        for c in range(GROUP_WIDTH // LANES):
            xt = res[:, c * LANES:(c + 1) * LANES]
            rot = jnp.where(upper, pltpu.roll(xt, 32, 1), pltpu.roll(xt, 96, 1))
            outs.append(xt * cosf + rot * sinf)
        return jnp.concatenate(outs, axis=1)

    zu_raw = jnp.dot(xn, w_ref[:, COL_U:COL_V], preferred_element_type=F32)
    zv_raw = jnp.dot(xn, w_ref[:, COL_V:COL_GA], preferred_element_type=F32)

    slabs = GROUP_WIDTH // LANES

    def project_group(gi):
        dil = DILATIONS[gi]
        for which in range(3):
            c0 = which * ATT_WIDTH + gi * GROUP_WIDTH
            res = jnp.dot(xn, w_ref[:, c0:c0 + GROUP_WIDTH], preferred_element_type=F32)
            if which < 2:
                res = rope(res)
            if which == 0:
                res = res * (HEAD_DIM ** -0.5)
            if dil == 1:
                out_refs[gi][0, which, 0, rows, :] = res.astype(BF16)
            else:
                n = tm // dil
                for c in range(slabs):
                    slot = ((sub * 2 + gi - 1) * 3 + which) * slabs + c
                    scr_ref[slot] = res[:, c * LANES:(c + 1) * LANES]
                    for r in range(dil):
                        out_refs[gi][0, which, r, sub * n:(sub + 1) * n, c * LANES:(c + 1) * LANES] = (
                            scr_ref[slot, pl.ds(r, n, stride=dil), :].astype(BF16))

    for gi in range(len(DILATIONS)):
        project_group(gi)

    zu = jax.nn.gelu(zu_raw)
    zv = jax.nn.gelu(zv_raw)
    mu = jnp.mean(zv, axis=-1, keepdims=True)
    zc = zv - mu
    var = jnp.mean(zc * zc, axis=-1, keepdims=True)
    vn = (zc * lax.rsqrt(var + LN_EPS) * lng_ref[...] + lnb_ref[...]).astype(BF16)
    tri_r = lax.broadcasted_iota(I32, (GMLP_CHUNK, GMLP_CHUNK), 0)
    tri_c = lax.broadcasted_iota(I32, (GMLP_CHUNK, GMLP_CHUNK), 1)
    causal = tri_r >= tri_c
    n_chunks = tm // GMLP_CHUNK
    gw = GMLP_WIDTH // GMLP_GROUPS
    for g in range(GMLP_GROUPS):
        wsg = jnp.where(causal, wsp_ref[g], 0.0).astype(BF16)
        vcat = jnp.concatenate(
            [vn[c * GMLP_CHUNK:(c + 1) * GMLP_CHUNK, g * gw:(g + 1) * gw] for c in range(n_chunks)],
            axis=1)
        mixed = jnp.dot(wsg, vcat, preferred_element_type=F32) + bsp_ref[:, g:g + 1]
        for c in range(n_chunks):
            u_blk = zu[c * GMLP_CHUNK:(c + 1) * GMLP_CHUNK, g * gw:(g + 1) * gw]
            r0 = sub * tm + c * GMLP_CHUNK
            yb_ref[r0:r0 + GMLP_CHUNK, g * gw:(g + 1) * gw] = (
                u_blk * mixed[:, c * gw:(c + 1) * gw]).astype(BF16)

    gate_a = jax.nn.sigmoid(
        jnp.dot(xn, w_ref[:, COL_GA:COL_GB], preferred_element_type=F32) + bg_ref[:, :D_MODEL])
    ga_ref[rows, :] = gate_a.astype(BF16)
    gate_b = jax.nn.sigmoid(
        jnp.dot(xn, w_ref[:, COL_GB:COL_GB + D_MODEL], preferred_element_type=F32) + bg_ref[:, D_MODEL:])
    mb_ref[rows, :] = (gate_b * jnp.dot(yb_ref[rows, :], wob_ref[...], preferred_element_type=F32)).astype(BF16)


def _inproj(x2d, pos_col, invf, phase, g, w_in, b_gates, w_spatial, b_spatial_t, ln_g, ln_b, w_out_b):
    tm = TM_PROJ
    nt = SEQ // tm
    in_cols = w_in.shape[1]
    qkv_shapes = [jax.ShapeDtypeStruct((BATCH, 3, d, SEQ // d, GROUP_WIDTH), BF16) for d in DILATIONS]
    qkv_specs = [
        pl.BlockSpec((1, 3, d, tm // d, GROUP_WIDTH), lambda i: (i // nt, 0, 0, i % nt, 0))
        for d in DILATIONS
    ]
    tok_spec = pl.BlockSpec((tm, D_MODEL), lambda i: (i, 0))
    return pl.pallas_call(
        _inproj_kernel,
        grid=(TOKENS // tm,),
        in_specs=[
            tok_spec,
            pl.BlockSpec((tm, 1), lambda i: (i, 0)),
            _resident((1, LANES)),
            _resident((1, LANES)),
            _resident((1, D_MODEL)),
            _resident((D_MODEL, in_cols)),
            _resident((1, 2 * D_MODEL)),
            _resident((GMLP_GROUPS, GMLP_CHUNK, GMLP_CHUNK)),
            _resident((GMLP_CHUNK, GMLP_GROUPS)),
            _resident((1, GMLP_WIDTH)),
            _resident((1, GMLP_WIDTH)),
            _resident((GMLP_WIDTH, D_MODEL)),
        ],
        out_specs=qkv_specs + [tok_spec, tok_spec],
        out_shape=qkv_shapes + [jax.ShapeDtypeStruct((TOKENS, D_MODEL), BF16)] * 2,
        scratch_shapes=[
            pltpu.VMEM((6 * (GROUP_WIDTH // LANES) * (tm // PROJ_SUB), PROJ_SUB, LANES), F32),
            pltpu.VMEM((tm, GMLP_WIDTH), BF16),
        ],
        compiler_params=pltpu.CompilerParams(
            dimension_semantics=("parallel",), vmem_limit_bytes=VMEM_LIMIT),
        name="inproj",
    )(x2d, pos_col, invf, phase, g, w_in, b_gates, w_spatial, b_spatial_t, ln_g, ln_b, w_out_b)


def _attn_kernel(qkv0_ref, qkv1_ref, qkv2_ref, y_ref, acc_ref, m_ref, z_ref, bias_ref):
    blk = BAND_BLOCK
    lane_row = lax.broadcasted_iota(I32, (1, LANES), 1)
    head0_b = jnp.where(lane_row < HEAD_DIM, 1.0, 0.0).astype(BF16)
    head1_b = jnp.where(lane_row < HEAD_DIM, 0.0, 1.0).astype(BF16)
    head0 = lax.broadcasted_iota(I32, (blk, LANES), 1) < HEAD_DIM
    ones_b = jnp.ones((2 * blk, LANES), BF16)

    qi = lax.broadcasted_iota(I32, (2 * blk, 2 * blk), 0) & (blk - 1)
    kc = lax.broadcasted_iota(I32, (2 * blk, 2 * blk), 1)
    for slot, off in enumerate((0, blk)):
        dist = qi + off - kc
        bias_ref[slot] = jnp.where((dist >= 0) & (dist <= blk), 0.0, NEG_INF)

    for gi, (ref, dil) in enumerate(zip((qkv0_ref, qkv1_ref, qkv2_ref), DILATIONS)):
        seq_len = SEQ // dil
        nb = seq_len // blk
        nb_shift = nb.bit_length() - 1

        def body(i, carry, ref=ref, dil=dil, nb=nb, nb_shift=nb_shift, gi=gi):
            r = lax.shift_right_logical(i, nb_shift)
            n = i & (nb - 1)
            q0 = pl.multiple_of(n * blk, blk)
            w0 = pl.multiple_of(jnp.maximum(n - 1, 0) * blk, blk)
            q = ref[0, 0, r, pl.ds(q0, blk), :]
            k = ref[0, 1, r, pl.ds(w0, 2 * blk), :]
            v = ref[0, 2, r, pl.ds(w0, 2 * blk), :]
            q2 = jnp.concatenate([q * head0_b, q * head1_b], axis=0)
            s = lax.dot_general(q2, k, (((1,), (1,)), ((), ())), preferred_element_type=F32)
            s = s + bias_ref[jnp.minimum(n, 1)]
            m2 = jnp.max(s, axis=-1, keepdims=True)
            p = jnp.exp(s - m2)
            v_ext = jnp.concatenate([v, ones_b], axis=1)
            o2 = jnp.dot(p.astype(BF16), v_ext, preferred_element_type=F32)
            o = jnp.where(head0, o2[:blk, :LANES], o2[blk:, :LANES])
            den = jnp.where(head0, o2[:blk, LANES:], o2[blk:, LANES:])
            m = jnp.where(head0, m2[:blk], m2[blk:])
            if gi == 0:
                acc_ref[pl.ds(q0, blk), :] = o
                m_ref[pl.ds(q0, blk), :] = m
                z_ref[pl.ds(q0, blk), :] = den
            else:
                idx = pl.ds(n * (blk * dil) + r, blk, stride=dil)
                m_old = m_ref[idx, :]
                m_new = jnp.maximum(m_old, m)
                e_old = jnp.exp(m_old - m_new)
                e_new = jnp.exp(m - m_new)
                acc_ref[idx, :] = acc_ref[idx, :] * e_old + o * e_new
                z_ref[idx, :] = z_ref[idx, :] * e_old + den * e_new
                m_ref[idx, :] = m_new
            return carry

        lax.fori_loop(0, dil * nb, body, 0, unroll=ATTN_UNROLL[gi])

    y_ref[0] = (acc_ref[...] / z_ref[...]).astype(BF16)


def _attention(qkv):
    in_specs = [
        pl.BlockSpec((1, 3, d, SEQ // d, LANES), lambda b, h: (b, 0, 0, 0, h)) for d in DILATIONS
    ]
    return pl.pallas_call(
        _attn_kernel,
        grid=(BATCH, GROUP_WIDTH // LANES),
        in_specs=in_specs,
        out_specs=pl.BlockSpec((1, SEQ, LANES), lambda b, h: (b, 0, h)),
        out_shape=jax.ShapeDtypeStruct((BATCH, SEQ, GROUP_WIDTH), BF16),
        scratch_shapes=[
            pltpu.VMEM((SEQ, LANES), F32), pltpu.VMEM((SEQ, LANES), F32), pltpu.VMEM((SEQ, LANES), F32),
            pltpu.VMEM((2, 2 * BAND_BLOCK, 2 * BAND_BLOCK), F32),
        ],
        compiler_params=pltpu.CompilerParams(
            dimension_semantics=("parallel", "parallel"), vmem_limit_bytes=VMEM_LIMIT),
        name="attn",
    )(*qkv)


def _post_kernel(ya_ref, ga_ref, mb_ref, x_ref, k_ref, v_ref, woa_ref, wo_ref, xg_ref, wq_ref,
                 wox_ref, mg_ref, wr_ref, br_ref, h_ref, hn_ref, route_ref, route_t_ref, o_scr):
    for c in range(TM_POST // POST_SUB):
        rows = slice(c * POST_SUB, (c + 1) * POST_SUB)
        _post_rows(rows, ya_ref, ga_ref, mb_ref, x_ref, k_ref, v_ref, woa_ref, wo_ref, xg_ref, wq_ref,
                   wox_ref, mg_ref, wr_ref, br_ref, h_ref, hn_ref, route_ref, route_t_ref, o_scr)


def _post_rows(rows, ya_ref, ga_ref, mb_ref, x_ref, k_ref, v_ref, woa_ref, wo_ref, xg_ref, wq_ref,
               wox_ref, mg_ref, wr_ref, br_ref, h_ref, hn_ref, route_ref, route_t_ref, o_scr):
    tm = POST_SUB
    t = jnp.dot(ya_ref[rows, :], woa_ref[...], preferred_element_type=F32)
    merged = (ga_ref[rows, :].astype(F32) * t + mb_ref[rows, :].astype(F32)).astype(BF16)
    h1 = x_ref[rows, :] + jnp.dot(merged, wo_ref[...], preferred_element_type=F32)

    hn = _rms(h1, xg_ref[...]).astype(BF16)
    q = (jnp.dot(hn, wq_ref[...], preferred_element_type=F32) * (XATTN_HEAD_DIM ** -0.5)).astype(BF16)
    hd = XATTN_HEAD_DIM
    for h in range(XATTN_HEADS):
        s = lax.dot_general(q[:, h * hd:(h + 1) * hd], k_ref[:, h * hd:(h + 1) * hd],
                            (((1,), (1,)), ((), ())), preferred_element_type=F32)
        m = jnp.max(s, axis=-1, keepdims=True)
        p = jnp.exp(s - m)
        den = jnp.sum(p, axis=-1, keepdims=True)
        oh = jnp.dot(p.astype(BF16), v_ref[:, h * hd:(h + 1) * hd], preferred_element_type=F32) / den
        o_scr[rows, h * hd:(h + 1) * hd] = oh.astype(BF16)
    h2 = h1 + jnp.dot(o_scr[rows, :], wox_ref[...], preferred_element_type=F32)
    h_ref[rows, :] = h2

    hn2 = _rms(h2, mg_ref[...])
    hn_ref[rows, :] = _pack_row_halves(hn2)

    logits = jnp.dot(hn2.astype(BF16), wr_ref[...], preferred_element_type=F32) + br_ref[...]
    li = lax.broadcasted_iota(I32, (tm, LANES), 1)
    lif = li.astype(F32)
    grp_of_lane = lax.shift_right_logical(li, 3).astype(F32)
    is_grp = (li >= N_EXPERTS) & (li < N_EXPERTS + N_EXPERT_GROUPS)
    gl = jnp.where(is_grp, logits, -jnp.inf)
    gmax = jnp.max(gl, axis=-1, keepdims=True)
    grp = jnp.min(jnp.where(gl == gmax, lif - N_EXPERTS, float(LANES)), axis=-1, keepdims=True)
    gsum = jnp.sum(jnp.where(is_grp, jnp.exp(logits - gmax), 0.0), axis=-1, keepdims=True)
    grp_gate = 1.0 / gsum
    in_grp = grp_of_lane == grp
    el = jnp.where(in_grp, logits, -jnp.inf)
    v1 = jnp.max(el, axis=-1, keepdims=True)
    i1 = jnp.min(jnp.where(el == v1, lif, float(LANES)), axis=-1, keepdims=True)
    el2 = jnp.where(lif == i1, -jnp.inf, el)
    v2 = jnp.max(el2, axis=-1, keepdims=True)
    i2 = jnp.min(jnp.where(el2 == v2, lif, float(LANES)), axis=-1, keepdims=True)
    tt = jnp.exp(v2 - v1)
    w1 = grp_gate / (1.0 + tt)
    w2 = grp_gate * tt / (1.0 + tt)
    route = jnp.where(li == 0, i1,
                      jnp.where(li == 1, i2,
                                jnp.where(li == 2, w1, jnp.where(li == 3, w2, 0.0))))
    route_ref[rows, :] = route[:, :8]
    route_t_ref[:, rows] = route.T[:8, :]


def _post(ya, ga, mb, x2d, kv, w_out_a, w_out, xg, w_q, w_o, mg, w_r, b_r):
    tm = TM_POST
    nt = SEQ // tm
    tok = lambda w: pl.BlockSpec((tm, w), lambda i: (i, 0))
    return pl.pallas_call(
        _post_kernel,
        grid=(TOKENS // tm,),
        in_specs=[
            tok(GROUP_WIDTH), tok(D_MODEL), tok(D_MODEL), tok(D_MODEL),
            pl.BlockSpec((N_MEM, D_MODEL), lambda i: (i // nt, 0)),
            pl.BlockSpec((N_MEM, D_MODEL), lambda i: (i // nt, 1)),
            _resident((GROUP_WIDTH, D_MODEL)),
            _resident((D_MODEL, D_MODEL)),
            _resident((1, D_MODEL)),
            _resident((D_MODEL, D_MODEL)),
            _resident((D_MODEL, D_MODEL)),
            _resident((1, D_MODEL)),
            _resident((D_MODEL, LANES)),
            _resident((1, LANES)),
        ],
        out_specs=[tok(D_MODEL), tok(HALF_MODEL), pl.BlockSpec((tm, 8), lambda i: (i, 0)),
                   pl.BlockSpec((8, tm), lambda i: (0, i))],
        out_shape=[
            jax.ShapeDtypeStruct((TOKENS, D_MODEL), F32),
            jax.ShapeDtypeStruct((TOKENS, HALF_MODEL), I32),
            jax.ShapeDtypeStruct((TOKENS, 8), F32),
            jax.ShapeDtypeStruct((8, TOKENS), F32),
        ],
        scratch_shapes=[pltpu.VMEM((tm, D_MODEL), BF16)],
        compiler_params=pltpu.CompilerParams(
            dimension_semantics=("parallel",), vmem_limit_bytes=VMEM_LIMIT),
        name="post",
    )(ya, ga, mb, x2d, kv, kv, w_out_a, w_out, xg, w_q, w_o, mg, w_r, b_r)


def _route_kernel(rt_ref, dest_ref, meta_ref, rank_scr, carry_ref, pstart_ref):
    pss = pl.program_id(0)
    i = pl.program_id(1)
    tl = TL_ROUTE
    ch = SCAN_CHUNK
    ei = lax.broadcasted_iota(I32, (N_EXPERTS, ch), 0).astype(F32)

    @pl.when((pss == 0) & (i == 0))
    def _():
        carry_ref[...] = jnp.zeros_like(carry_ref)

    @pl.when(pss == 0)
    def _():
        ur = lax.broadcasted_iota(I32, (ch, ch), 0)
        uc = lax.broadcasted_iota(I32, (ch, ch), 1)
        upper = jnp.where(ur < uc, 1.0, 0.0).astype(BF16)
        for c in range(tl // ch):
            e1 = rt_ref[0:1, c * ch:(c + 1) * ch]
            e2 = rt_ref[1:2, c * ch:(c + 1) * ch]
            oh1 = e1 == ei
            oh2 = e2 == ei
            oh = jnp.where(oh1 | oh2, 1.0, 0.0)
            cnt = jnp.dot(oh.astype(BF16), upper, preferred_element_type=F32) + carry_ref[:, 0:1]
            rank1 = jnp.sum(jnp.where(oh1, cnt, 0.0), axis=0, keepdims=True)
            rank2 = jnp.sum(jnp.where(oh2, cnt, 0.0), axis=0, keepdims=True)
            col = pl.multiple_of(i * tl + c * ch, ch)
            rank_scr[0:1, pl.ds(col, ch)] = rank1
            rank_scr[1:2, pl.ds(col, ch)] = rank2
            carry_ref[...] = carry_ref[...] + jnp.sum(oh, axis=1, keepdims=True)

    @pl.when((pss == 1) & (i == 0))
    def _():
        counts = carry_ref[...].astype(I32)
        padded = lax.shift_left(lax.shift_right_logical(counts + (ROW_BLOCK - 1),
                                                        int(math.log2(ROW_BLOCK))),
                                int(math.log2(ROW_BLOCK)))
        row = lax.broadcasted_iota(I32, (N_EXPERTS, LANES), 0)
        lane = lax.broadcasted_iota(I32, (N_EXPERTS, LANES), 1)
        pend = padded
        sh = 1
        while sh < N_EXPERTS:
            pend = pend + jnp.where(row >= sh, pltpu.roll(pend, sh, 0), 0)
            sh *= 2
        pstart = pend - padded
        pstart_ref[...] = pstart
        diag = row == lane

        def as_row(x):
            return jnp.sum(jnp.where(diag, x, 0), axis=0, keepdims=True)

        blk_lane = lax.broadcasted_iota(I32, (N_EXPERTS, META_LANES), 1) * ROW_BLOCK
        blk_exp = jnp.sum(jnp.where(pend[:, 0:1] <= blk_lane, 1, 0), axis=0, keepdims=True)
        blk_exp = jnp.minimum(blk_exp, N_EXPERTS - 1)
        meta_ref[...] = jnp.zeros_like(meta_ref)
        meta_ref[0:1, :] = blk_exp
        meta_ref[3:4, 0:LANES] = as_row(pend)

    @pl.when(pss == 1)
    def _():
        ps = pstart_ref[:, 0:1].astype(F32)
        for c in range(tl // ch):
            e1 = rt_ref[0:1, c * ch:(c + 1) * ch]
            e2 = rt_ref[1:2, c * ch:(c + 1) * ch]
            col = pl.multiple_of(i * tl + c * ch, ch)
            d1 = rank_scr[0:1, pl.ds(col, ch)] + jnp.sum(jnp.where(e1 == ei, ps, 0.0), axis=0, keepdims=True)
            d2 = rank_scr[1:2, pl.ds(col, ch)] + jnp.sum(jnp.where(e2 == ei, ps, 0.0), axis=0, keepdims=True)
            dest_ref[0:1, c * ch:(c + 1) * ch] = d1.astype(I32)
            dest_ref[1:2, c * ch:(c + 1) * ch] = d2.astype(I32)


def _route(rt):
    tl = TL_ROUTE
    return pl.pallas_call(
        _route_kernel,
        grid=(2, TOKENS // tl),
        in_specs=[pl.BlockSpec((8, tl), lambda p, i: (0, i))],
        out_specs=[
            pl.BlockSpec((2, tl), lambda p, i: (0, i * p)),
            pl.BlockSpec((8, META_LANES), lambda p, i: (0, 0)),
        ],
        out_shape=[
            jax.ShapeDtypeStruct((2, TOKENS), I32),
            jax.ShapeDtypeStruct((8, META_LANES), I32),
        ],
        scratch_shapes=[
            pltpu.VMEM((2, TOKENS), F32),
            pltpu.VMEM((N_EXPERTS, LANES), F32),
            pltpu.VMEM((N_EXPERTS, LANES), I32),
        ],
        compiler_params=pltpu.CompilerParams(
            dimension_semantics=("arbitrary", "arbitrary"), vmem_limit_bytes=VMEM_LIMIT),
        name="route",
    )(rt)


SC_CORES = 2
SC_SUBCORES = 16
SC_WORKERS = SC_CORES * SC_SUBCORES
SC_LANES = 16
SC_WINDOW = 64
SC_SCAN_CHUNK = 32768
PART_UNIT = SC_WORKERS * 2 * SC_WINDOW
DISPATCH_SPLIT = (4, 10, 10, 12)
COMBINE_SPLIT = (2, 6, 8)


def _sc_move_rows(table_hbm, idx_v, out_hbm, out_base, n_windows, rows_v, gsem, wsems):
    assert n_windows % 2 == 0 and n_windows >= 2

    def gather(j, b):
        idx = idx_v.at[pl.ds(j * SC_WINDOW, SC_WINDOW)]
        pltpu.async_copy(table_hbm.at[idx], rows_v.at[b], gsem).wait()

    def write(j, b):
        dst = out_hbm.at[pl.ds(out_base + j * SC_WINDOW, SC_WINDOW)]
        return pltpu.make_async_copy(rows_v.at[b], dst, wsems.at[b])

    for b in range(2):
        gather(b, b)
        write(b, b).start()

    @pl.loop(2, n_windows, step=2)
    def _(j):
        for b in range(2):
            write(j - 2 + b, b).wait()
            gather(j + b, b)
            write(j + b, b).start()

    for b in range(2):
        write(n_windows - 2 + b, b).wait()


def _sc_inverse_map(dest0, dest1):
    n_tok = dest0.shape[0]
    rows_per_w = PADDED_ROWS // SC_WORKERS
    assert rows_per_w % SC_LANES == 0 and n_tok % SC_SCAN_CHUNK == 0
    mesh = plsc.VectorSubcoreMesh(core_axis_name="c", subcore_axis_name="s")

    @functools.partial(
        pl.kernel, mesh=mesh,
        out_type=jax.ShapeDtypeStruct((PADDED_ROWS,), I32),
        scratch_types=[pltpu.VMEM((rows_per_w,), I32), pltpu.VMEM((SC_SCAN_CHUNK,), I32)],
        compiler_params=pltpu.CompilerParams(needs_layout_passes=False),
        name="sc_inverse_map",
    )
    def inverse_map(d0_hbm, d1_hbm, tok_hbm, tok_v, dchunk_v):
        wid = lax.axis_index("s") * SC_CORES + lax.axis_index("c")
        lo = wid * rows_per_w
        lane = lax.iota(I32, SC_LANES)

        @pl.loop(0, rows_per_w // SC_LANES)
        def _(i):
            tok_v[pl.ds(i * SC_LANES, SC_LANES)] = (lo + i * SC_LANES + lane) & (n_tok - 1)

        for d_hbm in (d0_hbm, d1_hbm):
            @pl.loop(0, n_tok // SC_SCAN_CHUNK)
            def _(c, d_hbm=d_hbm):
                pltpu.sync_copy(d_hbm.at[pl.ds(c * SC_SCAN_CHUNK, SC_SCAN_CHUNK)], dchunk_v)

                @plsc.parallel_loop(0, SC_SCAN_CHUNK // SC_LANES, unroll=4)
                def _(v):
                    local = dchunk_v[pl.ds(v * SC_LANES, SC_LANES)] - lo
                    mine = (local >= 0) & (local < rows_per_w)
                    tok = c * SC_SCAN_CHUNK + v * SC_LANES + lane
                    plsc.store_scatter(tok_v, [jnp.where(mine, local, 0)], tok, mask=mine)

        pltpu.sync_copy(tok_v, tok_hbm.at[pl.ds(lo, rows_per_w)])

    return inverse_map(dest0, dest1)


def _sc_gather(table, idxs):
    n = idxs[0].shape[0]
    width = table.shape[1]
    per_w = n // SC_WORKERS
    assert per_w % (2 * SC_WINDOW) == 0
    mesh = plsc.VectorSubcoreMesh(core_axis_name="c", subcore_axis_name="s")
    out = jax.ShapeDtypeStruct((n, width), table.dtype)
    k = len(idxs)

    @functools.partial(
        pl.kernel, mesh=mesh, out_type=(out,) * k,
        scratch_types=[
            pltpu.VMEM((per_w,), I32),
            pltpu.VMEM((2, SC_WINDOW, width), table.dtype),
            pltpu.SemaphoreType.DMA,
            pltpu.SemaphoreType.DMA((2,)),
        ],
        name="sc_gather",
    )
    def gather(table_hbm, *refs):
        idx_hbms, out_hbms = refs[:k], refs[k:2 * k]
        idx_v, rows_v, gsem, wsems = refs[2 * k:]
        wid = lax.axis_index("s") * SC_CORES + lax.axis_index("c")
        base = wid * per_w
        for idx_hbm, out_hbm in zip(idx_hbms, out_hbms):
            pltpu.sync_copy(idx_hbm.at[pl.ds(base, per_w)], idx_v)
            _sc_move_rows(table_hbm, idx_v, out_hbm, base, per_w // SC_WINDOW, rows_v, gsem, wsems)

    return gather(table, *idxs)


def _experts_kernel(blk_start, be_ref, nv_ref, xs_ref, wg_ref, wu_ref, wd_ref, *rest):
    yb_ref = rest[-1]
    j = pl.program_id(0) + blk_start

    @pl.when(j < nv_ref[0])
    def _():
        lo, hi = _unpack_row_halves(xs_ref[...])
        xb = jnp.concatenate([lo.astype(BF16), hi.astype(BF16)], axis=1)
        a = jnp.dot(xb, wg_ref[0].astype(BF16), preferred_element_type=F32)
        b = jnp.dot(xb, wu_ref[0].astype(BF16), preferred_element_type=F32)
        hb = (jax.nn.silu(a) * b).astype(BF16)
        yb_ref[...] = _pack_row_halves(
            jnp.dot(hb, wd_ref[0].astype(BF16), preferred_element_type=F32))

    @pl.when(j >= nv_ref[0])
    def _():
        yb_ref[...] = jnp.zeros_like(yb_ref)


def _experts(blk_expert, n_valid, xs_part, blk_start, yb_prev, w_gate, w_up, w_down):
    n_blocks = xs_part.shape[0] // ROW_BLOCK

    def row_map(j, be, nv):
        return (jnp.clip(jnp.minimum(j + blk_start, nv[0] - 1) - blk_start, 0, n_blocks - 1), 0)

    def out_map(j, be, nv):
        return (j + blk_start, 0)

    def w_map(j, be, nv):
        return (be[jnp.minimum(j + blk_start, nv[0] - 1)], 0, 0)

    in_specs = [
        pl.BlockSpec((ROW_BLOCK, HALF_MODEL), row_map),
        pl.BlockSpec((1, D_MODEL, EXPERT_FF), w_map),
        pl.BlockSpec((1, D_MODEL, EXPERT_FF), w_map),
        pl.BlockSpec((1, EXPERT_FF, D_MODEL), w_map),
    ]
    operands = [blk_expert, n_valid, xs_part, w_gate, w_up, w_down]
    aliases = {}
    if yb_prev is not None:
        in_specs.append(pl.BlockSpec(memory_space=pl.ANY))
        aliases = {len(operands): 0}
        operands.append(yb_prev)
    grid_spec = pltpu.PrefetchScalarGridSpec(
        num_scalar_prefetch=2,
        grid=(n_blocks,),
        in_specs=in_specs,
        out_specs=pl.BlockSpec((ROW_BLOCK, HALF_MODEL), out_map),
    )
    return pl.pallas_call(
        functools.partial(_experts_kernel, blk_start),
        grid_spec=grid_spec,
        out_shape=jax.ShapeDtypeStruct((PADDED_ROWS, HALF_MODEL), I32),
        input_output_aliases=aliases,
        compiler_params=pltpu.CompilerParams(
            dimension_semantics=("arbitrary",), vmem_limit_bytes=VMEM_LIMIT),
        name="experts",
    )(*operands)


def _combine_kernel(route_ref, h_ref, g_ref, y0_ref, y1_ref, *rest):
    out_ref = rest[-1]
    w1 = route_ref[:, 2:3]
    w2 = route_ref[:, 3:4]
    lo0, hi0 = _unpack_row_halves(y0_ref[...])
    lo1, hi1 = _unpack_row_halves(y1_ref[...])
    y = jnp.concatenate([lo0 * w1 + lo1 * w2, hi0 * w1 + hi1 * w2], axis=1)
    out_ref[...] = _rms(h_ref[...] + y, g_ref[...])


def _combine(route, h2, g, y0, y1, tok_start, out_prev):
    tm = TM_COMBINE
    blk0 = tok_start // tm
    glob = lambda w: pl.BlockSpec((tm, w), lambda i: (i + blk0, 0))
    part = pl.BlockSpec((tm, HALF_MODEL), lambda i: (i, 0))
    in_specs = [glob(8), glob(D_MODEL), _resident((1, D_MODEL)), part, part]
    operands = [route, h2, g, y0, y1]
    aliases = {}
    if out_prev is not None:
        in_specs.append(pl.BlockSpec(memory_space=pl.ANY))
        aliases = {len(operands): 0}
        operands.append(out_prev)
    return pl.pallas_call(
        _combine_kernel,
        grid=(y0.shape[0] // tm,),
        in_specs=in_specs,
        out_specs=glob(D_MODEL),
        out_shape=jax.ShapeDtypeStruct((TOKENS, D_MODEL), F32),
        input_output_aliases=aliases,
        compiler_params=pltpu.CompilerParams(
            dimension_semantics=("parallel",), vmem_limit_bytes=VMEM_LIMIT),
        name="combine",
    )(*operands)


def kernel(x, mem, positions, mix_norm_g, w_in, b_gates, w_spatial, b_spatial, v_norm_g, v_norm_b,
           w_out_a, w_out_b, w_out, xattn_norm_g, mem_norm_g, w_q_x, w_kv_x, w_o_x, moe_norm_g,
           w_router_grp, b_router_grp, w_router_exp, b_router_exp, w_gate_e, w_up_e, w_down_e,
           final_norm_g):
    assert x.shape == (BATCH, SEQ, D_MODEL) and mem.shape == (BATCH, N_MEM, D_MODEL)
    assert mix_norm_g.shape[0] == 1, "single layer"
    x2d = x.reshape(TOKENS, D_MODEL)
    pos_col = positions.reshape(TOKENS, 1).astype(F32)
    half = HEAD_DIM // 2
    inv_freq = ROPE_THETA ** (-jnp.arange(half, dtype=F32) / half)
    invf = jnp.tile(inv_freq, LANES // half).reshape(1, LANES)
    phase = jnp.tile(jnp.concatenate([jnp.zeros((half,), F32), jnp.full((half,), math.pi / 2, F32)]),
                     LANES // HEAD_DIM).reshape(1, LANES)

    kv = _memkv(mem.reshape(BATCH * N_MEM, D_MODEL), mem_norm_g[0].reshape(1, D_MODEL),
                w_kv_x[0].astype(BF16))

    qkv0, qkv1, qkv2, ga, mb = _inproj(
        x2d, pos_col, invf, phase, mix_norm_g[0].reshape(1, D_MODEL), w_in[0].astype(BF16),
        b_gates[0].reshape(1, 2 * D_MODEL), w_spatial[0], b_spatial[0].T,
        v_norm_g[0].reshape(1, GMLP_WIDTH), v_norm_b[0].reshape(1, GMLP_WIDTH),
        w_out_b[0].astype(BF16))

    ya = _attention((qkv0, qkv1, qkv2)).reshape(TOKENS, GROUP_WIDTH)

    pad = LANES - N_EXPERTS - N_EXPERT_GROUPS
    w_r = jnp.concatenate([w_router_exp[0], w_router_grp[0], jnp.zeros((D_MODEL, pad), F32)], axis=1)
    b_r = jnp.concatenate([b_router_exp[0], b_router_grp[0], jnp.zeros((pad,), F32)]).reshape(1, LANES)
    h2, hn2, route, route_t = _post(
        ya, ga, mb, x2d, kv, w_out_a[0].astype(BF16), w_out[0].astype(BF16),
        xattn_norm_g[0].reshape(1, D_MODEL), w_q_x[0].astype(BF16), w_o_x[0].astype(BF16),
        moe_norm_g[0].reshape(1, D_MODEL), w_r.astype(BF16), b_r)

    dest, meta = _route(route_t)
    d0, d1 = dest[0], dest[1]
    tok_of_row = _sc_inverse_map(d0, d1)
    blk_expert = meta[0, :N_ROW_BLOCKS]
    n_valid = (meta[3, N_EXPERTS - 1:N_EXPERTS] // ROW_BLOCK).astype(I32)
    row_cuts = [PART_UNIT * c for c in itertools.accumulate((0,) + DISPATCH_SPLIT)]
    assert row_cuts[-1] == PADDED_ROWS
    xs_parts = [_sc_gather(hn2, [tok_of_row[a:b]])[0] for a, b in zip(row_cuts, row_cuts[1:])]
    yb = None
    for a, xs_part in zip(row_cuts, xs_parts):
        yb = _experts(blk_expert, n_valid, xs_part, a // ROW_BLOCK, yb, w_gate_e[0], w_up_e[0], w_down_e[0])
    g_fin = final_norm_g.reshape(1, D_MODEL)
    tok_cuts = [PART_UNIT * c for c in itertools.accumulate((0,) + COMBINE_SPLIT)]
    assert tok_cuts[-1] == TOKENS
    gathered = [_sc_gather(yb, [d0[a:b], d1[a:b]]) for a, b in zip(tok_cuts, tok_cuts[1:])]
    out = None
    for a, (y0, y1) in zip(tok_cuts, gathered):
        out = _combine(route, h2, g_fin, y0, y1, a, out)
    return out.reshape(BATCH, SEQ, D_MODEL)
```

```python
import functools
import itertools
import math

import jax
import jax.numpy as jnp
from jax import lax
from jax.experimental import pallas as pl
from jax.experimental.pallas import tpu as pltpu
from jax.experimental.pallas import tpu_sc as plsc

F32 = jnp.float32
BF16 = jnp.bfloat16
I32 = jnp.int32

D_MODEL = 1024
BATCH = 16
SEQ = 4096
TOKENS = BATCH * SEQ

HEAD_DIM = 64
DILATIONS = (1, 4, 16)
HEADS_PER_GROUP = 4
GROUP_WIDTH = HEADS_PER_GROUP * HEAD_DIM
ATT_WIDTH = len(DILATIONS) * GROUP_WIDTH
BAND_BLOCK = 128
ROPE_THETA = 10000.0

GMLP_CHUNK = 128
GMLP_GROUPS = 4
GMLP_WIDTH = 512

N_MEM = 256
XATTN_HEADS = 4
XATTN_HEAD_DIM = D_MODEL // XATTN_HEADS

N_EXPERT_GROUPS = 4
EXPERTS_PER_GROUP = 8
N_EXPERTS = 32
TOP_K = 2
EXPERT_FF = 512

RMS_EPS = 1e-6
LN_EPS = 1e-5
NEG_INF = -1e30

LANES = 128

COL_U = 3 * ATT_WIDTH
COL_V = COL_U + GMLP_WIDTH
COL_GA = COL_V + GMLP_WIDTH
COL_GB = COL_GA + D_MODEL

ROW_BLOCK = 512
ASSIGN = TOKENS * TOP_K
PADDED_ROWS = ASSIGN + N_EXPERTS * ROW_BLOCK
N_ROW_BLOCKS = PADDED_ROWS // ROW_BLOCK
META_LANES = ((N_ROW_BLOCKS + LANES - 1) // LANES) * LANES

TM_PROJ = 1024
PROJ_SUB = 512
TM_POST = 1024
POST_SUB = 512
TL_ROUTE = 8192
SCAN_CHUNK = 256
TM_COMBINE = 1024
ATTN_UNROLL = (32, 32, 16)

VMEM_LIMIT = 56 * 1024 * 1024


def _rms(x, g):
    return x * lax.rsqrt(jnp.mean(x * x, axis=-1, keepdims=True) + RMS_EPS) * g


HALF_MODEL = D_MODEL // 2


def _pack_row_halves(x):
    return pltpu.pack_elementwise([x[:, :HALF_MODEL], x[:, HALF_MODEL:]], packed_dtype=BF16)


def _unpack_row_halves(p):
    lo = pltpu.unpack_elementwise(p, index=0, packed_dtype=BF16, unpacked_dtype=F32)
    hi = pltpu.unpack_elementwise(p, index=1, packed_dtype=BF16, unpacked_dtype=F32)
    return lo, hi


def _resident(shape):
    nd = len(shape)
    return pl.BlockSpec(shape, lambda *_: (0,) * nd, pipeline_mode=pl.Buffered(1))


def _memkv_kernel(mem_ref, g_ref, w_ref, kv_ref):
    mn = _rms(mem_ref[...], g_ref[...]).astype(BF16)
    kv_ref[...] = jnp.dot(mn, w_ref[...], preferred_element_type=F32).astype(BF16)


def _memkv(mem2d, g, w_kv):
    rows = mem2d.shape[0]
    tm = 512
    return pl.pallas_call(
        _memkv_kernel,
        grid=(rows // tm,),
        in_specs=[
            pl.BlockSpec((tm, D_MODEL), lambda i: (i, 0)),
            _resident((1, D_MODEL)),
            _resident((D_MODEL, 2 * D_MODEL)),
        ],
        out_specs=pl.BlockSpec((tm, 2 * D_MODEL), lambda i: (i, 0)),
        out_shape=jax.ShapeDtypeStruct((rows, 2 * D_MODEL), BF16),
        compiler_params=pltpu.CompilerParams(
            dimension_semantics=("parallel",), vmem_limit_bytes=VMEM_LIMIT),
        name="memkv",
    )(mem2d, g, w_kv)


def _inproj_kernel(x_ref, pos_ref, invf_ref, phase_ref, g_ref, w_ref, bg_ref, wsp_ref,
                   bsp_ref, lng_ref, lnb_ref, wob_ref,
                   qkv0_ref, qkv1_ref, qkv2_ref, ga_ref, mb_ref, scr_ref, yb_ref):
    for sub in range(TM_PROJ // PROJ_SUB):
        _inproj_rows(sub, x_ref, pos_ref, invf_ref, phase_ref, g_ref, w_ref, bg_ref, wsp_ref,
                     bsp_ref, lng_ref, lnb_ref, wob_ref,
                     (qkv0_ref, qkv1_ref, qkv2_ref), ga_ref, mb_ref, scr_ref, yb_ref)


def _inproj_rows(sub, x_ref, pos_ref, invf_ref, phase_ref, g_ref, w_ref, bg_ref, wsp_ref,
                 bsp_ref, lng_ref, lnb_ref, wob_ref, out_refs, ga_ref, mb_ref, scr_ref, yb_ref):
    tm = PROJ_SUB
    rows = slice(sub * tm, (sub + 1) * tm)
    xn = _rms(x_ref[rows, :], g_ref[...]).astype(BF16)

    lane = lax.broadcasted_iota(I32, (tm, LANES), 1)
    upper = (lane & 32) != 0
    t1 = jnp.sin(pos_ref[rows, :] * invf_ref[...] + phase_ref[...])
    cosf = jnp.where(upper, t1, pltpu.roll(t1, 96, 1))
    sinf = jnp.where(upper, pltpu.roll(t1, 32, 1), -t1)

    def rope(res):
        outs = []
        for c in range(GROUP_WIDTH // LANES):
            xt = res[:, c * LANES:(c + 1) * LANES]
            rot = jnp.where(upper, pltpu.roll(xt, 32, 1), pltpu.roll(xt, 96, 1))
            outs.append(xt * cosf + rot * sinf)
        return jnp.concatenate(outs, axis=1)

    zu_raw = jnp.dot(xn, w_ref[:, COL_U:COL_V], preferred_element_type=F32)
    zv_raw = jnp.dot(xn, w_ref[:, COL_V:COL_GA], preferred_element_type=F32)

    slabs = GROUP_WIDTH // LANES

    def project_group(gi):
        dil = DILATIONS[gi]
        for which in range(3):
            c0 = which * ATT_WIDTH + gi * GROUP_WIDTH
            res = jnp.dot(xn, w_ref[:, c0:c0 + GROUP_WIDTH], preferred_element_type=F32)
            if which < 2:
                res = rope(res)
            if which == 0:
                res = res * (HEAD_DIM ** -0.5)
            if dil == 1:
                out_refs[gi][0, which, 0, rows, :] = res.astype(BF16)
            else:
                n = tm // dil
                for c in range(slabs):
                    slot = ((sub * 2 + gi - 1) * 3 + which) * slabs + c
                    scr_ref[slot] = res[:, c * LANES:(c + 1) * LANES]
                    for r in range(dil):
                        out_refs[gi][0, which, r, sub * n:(sub + 1) * n, c * LANES:(c + 1) * LANES] = (
                            scr_ref[slot, pl.ds(r, n, stride=dil), :].astype(BF16))

    for gi in range(len(DILATIONS)):
        project_group(gi)

    zu = jax.nn.gelu(zu_raw)
    zv = jax.nn.gelu(zv_raw)
    mu = jnp.mean(zv, axis=-1, keepdims=True)
    zc = zv - mu
    var = jnp.mean(zc * zc, axis=-1, keepdims=True)
    vn = (zc * lax.rsqrt(var + LN_EPS) * lng_ref[...] + lnb_ref[...]).astype(BF16)
    tri_r = lax.broadcasted_iota(I32, (GMLP_CHUNK, GMLP_CHUNK), 0)
    tri_c = lax.broadcasted_iota(I32, (GMLP_CHUNK, GMLP_CHUNK), 1)
    causal = tri_r >= tri_c
    n_chunks = tm // GMLP_CHUNK
    gw = GMLP_WIDTH // GMLP_GROUPS
    for g in range(GMLP_GROUPS):
        wsg = jnp.where(causal, wsp_ref[g], 0.0).astype(BF16)
        vcat = jnp.concatenate(
            [vn[c * GMLP_CHUNK:(c + 1) * GMLP_CHUNK, g * gw:(g + 1) * gw] for c in range(n_chunks)],
            axis=1)
        mixed = jnp.dot(wsg, vcat, preferred_element_type=F32) + bsp_ref[:, g:g + 1]
        for c in range(n_chunks):
            u_blk = zu[c * GMLP_CHUNK:(c + 1) * GMLP_CHUNK, g * gw:(g + 1) * gw]
            r0 = sub * tm + c * GMLP_CHUNK
            yb_ref[r0:r0 + GMLP_CHUNK, g * gw:(g + 1) * gw] = (
                u_blk * mixed[:, c * gw:(c + 1) * gw]).astype(BF16)

    gate_a = jax.nn.sigmoid(
        jnp.dot(xn, w_ref[:, COL_GA:COL_GB], preferred_element_type=F32) + bg_ref[:, :D_MODEL])
    ga_ref[rows, :] = gate_a.astype(BF16)
    gate_b = jax.nn.sigmoid(
        jnp.dot(xn, w_ref[:, COL_GB:COL_GB + D_MODEL], preferred_element_type=F32) + bg_ref[:, D_MODEL:])
    mb_ref[rows, :] = (gate_b * jnp.dot(yb_ref[rows, :], wob_ref[...], preferred_element_type=F32)).astype(BF16)


def _inproj(x2d, pos_col, invf, phase, g, w_in, b_gates, w_spatial, b_spatial_t, ln_g, ln_b, w_out_b):
    tm = TM_PROJ
    nt = SEQ // tm
    in_cols = w_in.shape[1]
    qkv_shapes = [jax.ShapeDtypeStruct((BATCH, 3, d, SEQ // d, GROUP_WIDTH), BF16) for d in DILATIONS]
    qkv_specs = [
        pl.BlockSpec((1, 3, d, tm // d, GROUP_WIDTH), lambda i: (i // nt, 0, 0, i % nt, 0))
        for d in DILATIONS
    ]
    tok_spec = pl.BlockSpec((tm, D_MODEL), lambda i: (i, 0))
    return pl.pallas_call(
        _inproj_kernel,
        grid=(TOKENS // tm,),
        in_specs=[
            tok_spec,
            pl.BlockSpec((tm, 1), lambda i: (i, 0)),
            _resident((1, LANES)),
            _resident((1, LANES)),
            _resident((1, D_MODEL)),
            _resident((D_MODEL, in_cols)),
            _resident((1, 2 * D_MODEL)),
            _resident((GMLP_GROUPS, GMLP_CHUNK, GMLP_CHUNK)),
            _resident((GMLP_CHUNK, GMLP_GROUPS)),
            _resident((1, GMLP_WIDTH)),
            _resident((1, GMLP_WIDTH)),
            _resident((GMLP_WIDTH, D_MODEL)),
        ],
        out_specs=qkv_specs + [tok_spec, tok_spec],
        out_shape=qkv_shapes + [jax.ShapeDtypeStruct((TOKENS, D_MODEL), BF16)] * 2,
        scratch_shapes=[
            pltpu.VMEM((6 * (GROUP_WIDTH // LANES) * (tm // PROJ_SUB), PROJ_SUB, LANES), F32),
            pltpu.VMEM((tm, GMLP_WIDTH), BF16),
        ],
        compiler_params=pltpu.CompilerParams(
            dimension_semantics=("parallel",), vmem_limit_bytes=VMEM_LIMIT),
        name="inproj",
    )(x2d, pos_col, invf, phase, g, w_in, b_gates, w_spatial, b_spatial_t, ln_g, ln_b, w_out_b)


def _attn_kernel(qkv0_ref, qkv1_ref, qkv2_ref, y_ref, acc_ref, m_ref, z_ref, bias_ref):
    blk = BAND_BLOCK
    lane_row = lax.broadcasted_iota(I32, (1, LANES), 1)
    head0_b = jnp.where(lane_row < HEAD_DIM, 1.0, 0.0).astype(BF16)
    head1_b = jnp.where(lane_row < HEAD_DIM, 0.0, 1.0).astype(BF16)
    head0 = lax.broadcasted_iota(I32, (blk, LANES), 1) < HEAD_DIM
    ones_b = jnp.ones((2 * blk, LANES), BF16)

    qi = lax.broadcasted_iota(I32, (2 * blk, 2 * blk), 0) & (blk - 1)
    kc = lax.broadcasted_iota(I32, (2 * blk, 2 * blk), 1)
    for slot, off in enumerate((0, blk)):
        dist = qi + off - kc
        bias_ref[slot] = jnp.where((dist >= 0) & (dist <= blk), 0.0, NEG_INF)

    for gi, (ref, dil) in enumerate(zip((qkv0_ref, qkv1_ref, qkv2_ref), DILATIONS)):
        seq_len = SEQ // dil
        nb = seq_len // blk
        nb_shift = nb.bit_length() - 1

        def body(i, carry, ref=ref, dil=dil, nb=nb, nb_shift=nb_shift, gi=gi):
            r = lax.shift_right_logical(i, nb_shift)
            n = i & (nb - 1)
            q0 = pl.multiple_of(n * blk, blk)
            w0 = pl.multiple_of(jnp.maximum(n - 1, 0) * blk, blk)
            q = ref[0, 0, r, pl.ds(q0, blk), :]
            k = ref[0, 1, r, pl.ds(w0, 2 * blk), :]
            v = ref[0, 2, r, pl.ds(w0, 2 * blk), :]
            q2 = jnp.concatenate([q * head0_b, q * head1_b], axis=0)
            s = lax.dot_general(q2, k, (((1,), (1,)), ((), ())), preferred_element_type=F32)
            s = s + bias_ref[jnp.minimum(n, 1)]
            m2 = jnp.max(s, axis=-1, keepdims=True)
            p = jnp.exp(s - m2)
            v_ext = jnp.concatenate([v, ones_b], axis=1)
            o2 = jnp.dot(p.astype(BF16), v_ext, preferred_element_type=F32)
            o = jnp.where(head0, o2[:blk, :LANES], o2[blk:, :LANES])
            den = jnp.where(head0, o2[:blk, LANES:], o2[blk:, LANES:])
            m = jnp.where(head0, m2[:blk], m2[blk:])
            if gi == 0:
                acc_ref[pl.ds(q0, blk), :] = o
                m_ref[pl.ds(q0, blk), :] = m
                z_ref[pl.ds(q0, blk), :] = den
            else:
                idx = pl.ds(n * (blk * dil) + r, blk, stride=dil)
                m_old = m_ref[idx, :]
                m_new = jnp.maximum(m_old, m)
                e_old = jnp.exp(m_old - m_new)
                e_new = jnp.exp(m - m_new)
                acc_ref[idx, :] = acc_ref[idx, :] * e_old + o * e_new
                z_ref[idx, :] = z_ref[idx, :] * e_old + den * e_new
                m_ref[idx, :] = m_new
            return carry

        lax.fori_loop(0, dil * nb, body, 0, unroll=ATTN_UNROLL[gi])

    y_ref[0] = (acc_ref[...] / z_ref[...]).astype(BF16)


def _attention(qkv):
    in_specs = [
        pl.BlockSpec((1, 3, d, SEQ // d, LANES), lambda b, h: (b, 0, 0, 0, h)) for d in DILATIONS
    ]
    return pl.pallas_call(
        _attn_kernel,
        grid=(BATCH, GROUP_WIDTH // LANES),
        in_specs=in_specs,
        out_specs=pl.BlockSpec((1, SEQ, LANES), lambda b, h: (b, 0, h)),
        out_shape=jax.ShapeDtypeStruct((BATCH, SEQ, GROUP_WIDTH), BF16),
        scratch_shapes=[
            pltpu.VMEM((SEQ, LANES), F32), pltpu.VMEM((SEQ, LANES), F32), pltpu.VMEM((SEQ, LANES), F32),
            pltpu.VMEM((2, 2 * BAND_BLOCK, 2 * BAND_BLOCK), F32),
        ],
        compiler_params=pltpu.CompilerParams(
            dimension_semantics=("parallel", "parallel"), vmem_limit_bytes=VMEM_LIMIT),
        name="attn",
    )(*qkv)


def _post_kernel(ya_ref, ga_ref, mb_ref, x_ref, k_ref, v_ref, woa_ref, wo_ref, xg_ref, wq_ref,
                 wox_ref, mg_ref, wr_ref, br_ref, h_ref, hn_ref, route_ref, route_t_ref, o_scr):
    for c in range(TM_POST // POST_SUB):
        rows = slice(c * POST_SUB, (c + 1) * POST_SUB)
        _post_rows(rows, ya_ref, ga_ref, mb_ref, x_ref, k_ref, v_ref, woa_ref, wo_ref, xg_ref, wq_ref,
                   wox_ref, mg_ref, wr_ref, br_ref, h_ref, hn_ref, route_ref, route_t_ref, o_scr)


def _post_rows(rows, ya_ref, ga_ref, mb_ref, x_ref, k_ref, v_ref, woa_ref, wo_ref, xg_ref, wq_ref,
               wox_ref, mg_ref, wr_ref, br_ref, h_ref, hn_ref, route_ref, route_t_ref, o_scr):
    tm = POST_SUB
    t = jnp.dot(ya_ref[rows, :], woa_ref[...], preferred_element_type=F32)
    merged = (ga_ref[rows, :].astype(F32) * t + mb_ref[rows, :].astype(F32)).astype(BF16)
    h1 = x_ref[rows, :] + jnp.dot(merged, wo_ref[...], preferred_element_type=F32)

    hn = _rms(h1, xg_ref[...]).astype(BF16)
    q = (jnp.dot(hn, wq_ref[...], preferred_element_type=F32) * (XATTN_HEAD_DIM ** -0.5)).astype(BF16)
    hd = XATTN_HEAD_DIM
    for h in range(XATTN_HEADS):
        s = lax.dot_general(q[:, h * hd:(h + 1) * hd], k_ref[:, h * hd:(h + 1) * hd],
                            (((1,), (1,)), ((), ())), preferred_element_type=F32)
        m = jnp.max(s, axis=-1, keepdims=True)
        p = jnp.exp(s - m)
        den = jnp.sum(p, axis=-1, keepdims=True)
        oh = jnp.dot(p.astype(BF16), v_ref[:, h * hd:(h + 1) * hd], preferred_element_type=F32) / den
        o_scr[rows, h * hd:(h + 1) * hd] = oh.astype(BF16)
    h2 = h1 + jnp.dot(o_scr[rows, :], wox_ref[...], preferred_element_type=F32)
    h_ref[rows, :] = h2

    hn2 = _rms(h2, mg_ref[...])
    hn_ref[rows, :] = _pack_row_halves(hn2)

    logits = jnp.dot(hn2.astype(BF16), wr_ref[...], preferred_element_type=F32) + br_ref[...]
    li = lax.broadcasted_iota(I32, (tm, LANES), 1)
    lif = li.astype(F32)
    grp_of_lane = lax.shift_right_logical(li, 3).astype(F32)
    is_grp = (li >= N_EXPERTS) & (li < N_EXPERTS + N_EXPERT_GROUPS)
    gl = jnp.where(is_grp, logits, -jnp.inf)
    gmax = jnp.max(gl, axis=-1, keepdims=True)
    grp = jnp.min(jnp.where(gl == gmax, lif - N_EXPERTS, float(LANES)), axis=-1, keepdims=True)
    gsum = jnp.sum(jnp.where(is_grp, jnp.exp(logits - gmax), 0.0), axis=-1, keepdims=True)
    grp_gate = 1.0 / gsum
    in_grp = grp_of_lane == grp
    el = jnp.where(in_grp, logits, -jnp.inf)
    v1 = jnp.max(el, axis=-1, keepdims=True)
    i1 = jnp.min(jnp.where(el == v1, lif, float(LANES)), axis=-1, keepdims=True)
    el2 = jnp.where(lif == i1, -jnp.inf, el)
    v2 = jnp.max(el2, axis=-1, keepdims=True)
    i2 = jnp.min(jnp.where(el2 == v2, lif, float(LANES)), axis=-1, keepdims=True)
    tt = jnp.exp(v2 - v1)
    w1 = grp_gate / (1.0 + tt)
    w2 = grp_gate * tt / (1.0 + tt)
    route = jnp.where(li == 0, i1,
                      jnp.where(li == 1, i2,
                                jnp.where(li == 2, w1, jnp.where(li == 3, w2, 0.0))))
    route_ref[rows, :] = route[:, :8]
    route_t_ref[:, rows] = route.T[:8, :]


def _post(ya, ga, mb, x2d, kv, w_out_a, w_out, xg, w_q, w_o, mg, w_r, b_r):
    tm = TM_POST
    nt = SEQ // tm
    tok = lambda w: pl.BlockSpec((tm, w), lambda i: (i, 0))
    return pl.pallas_call(
        _post_kernel,
        grid=(TOKENS // tm,),
        in_specs=[
            tok(GROUP_WIDTH), tok(D_MODEL), tok(D_MODEL), tok(D_MODEL),
            pl.BlockSpec((N_MEM, D_MODEL), lambda i: (i // nt, 0)),
            pl.BlockSpec((N_MEM, D_MODEL), lambda i: (i // nt, 1)),
            _resident((GROUP_WIDTH, D_MODEL)),
            _resident((D_MODEL, D_MODEL)),
            _resident((1, D_MODEL)),
            _resident((D_MODEL, D_MODEL)),
            _resident((D_MODEL, D_MODEL)),
            _resident((1, D_MODEL)),
            _resident((D_MODEL, LANES)),
            _resident((1, LANES)),
        ],
        out_specs=[tok(D_MODEL), tok(HALF_MODEL), pl.BlockSpec((tm, 8), lambda i: (i, 0)),
                   pl.BlockSpec((8, tm), lambda i: (0, i))],
        out_shape=[
            jax.ShapeDtypeStruct((TOKENS, D_MODEL), F32),
            jax.ShapeDtypeStruct((TOKENS, HALF_MODEL), I32),
            jax.ShapeDtypeStruct((TOKENS, 8), F32),
            jax.ShapeDtypeStruct((8, TOKENS), F32),
        ],
        scratch_shapes=[pltpu.VMEM((tm, D_MODEL), BF16)],
        compiler_params=pltpu.CompilerParams(
            dimension_semantics=("parallel",), vmem_limit_bytes=VMEM_LIMIT),
        name="post",
    )(ya, ga, mb, x2d, kv, kv, w_out_a, w_out, xg, w_q, w_o, mg, w_r, b_r)


def _route_kernel(rt_ref, dest_ref, meta_ref, rank_scr, carry_ref, pstart_ref):
    pss = pl.program_id(0)
    i = pl.program_id(1)
    tl = TL_ROUTE
    ch = SCAN_CHUNK
    ei = lax.broadcasted_iota(I32, (N_EXPERTS, ch), 0).astype(F32)

    @pl.when((pss == 0) & (i == 0))
    def _():
        carry_ref[...] = jnp.zeros_like(carry_ref)

    @pl.when(pss == 0)
    def _():
        ur = lax.broadcasted_iota(I32, (ch, ch), 0)
        uc = lax.broadcasted_iota(I32, (ch, ch), 1)
        upper = jnp.where(ur < uc, 1.0, 0.0).astype(BF16)
        for c in range(tl // ch):
            e1 = rt_ref[0:1, c * ch:(c + 1) * ch]
            e2 = rt_ref[1:2, c * ch:(c + 1) * ch]
            oh1 = e1 == ei
            oh2 = e2 == ei
            oh = jnp.where(oh1 | oh2, 1.0, 0.0)
            cnt = jnp.dot(oh.astype(BF16), upper, preferred_element_type=F32) + carry_ref[:, 0:1]
            rank1 = jnp.sum(jnp.where(oh1, cnt, 0.0), axis=0, keepdims=True)
            rank2 = jnp.sum(jnp.where(oh2, cnt, 0.0), axis=0, keepdims=True)
            col = pl.multiple_of(i * tl + c * ch, ch)
            rank_scr[0:1, pl.ds(col, ch)] = rank1
            rank_scr[1:2, pl.ds(col, ch)] = rank2
            carry_ref[...] = carry_ref[...] + jnp.sum(oh, axis=1, keepdims=True)

    @pl.when((pss == 1) & (i == 0))
    def _():
        counts = carry_ref[...].astype(I32)
        padded = lax.shift_left(lax.shift_right_logical(counts + (ROW_BLOCK - 1),
                                                        int(math.log2(ROW_BLOCK))),
                                int(math.log2(ROW_BLOCK)))
        row = lax.broadcasted_iota(I32, (N_EXPERTS, LANES), 0)
        lane = lax.broadcasted_iota(I32, (N_EXPERTS, LANES), 1)
        pend = padded
        sh = 1
        while sh < N_EXPERTS:
            pend = pend + jnp.where(row >= sh, pltpu.roll(pend, sh, 0), 0)
            sh *= 2
        pstart = pend - padded
        pstart_ref[...] = pstart
        diag = row == lane

        def as_row(x):
            return jnp.sum(jnp.where(diag, x, 0), axis=0, keepdims=True)

        blk_lane = lax.broadcasted_iota(I32, (N_EXPERTS, META_LANES), 1) * ROW_BLOCK
        blk_exp = jnp.sum(jnp.where(pend[:, 0:1] <= blk_lane, 1, 0), axis=0, keepdims=True)
        blk_exp = jnp.minimum(blk_exp, N_EXPERTS - 1)
        meta_ref[...] = jnp.zeros_like(meta_ref)
        meta_ref[0:1, :] = blk_exp
        meta_ref[3:4, 0:LANES] = as_row(pend)

    @pl.when(pss == 1)
    def _():
        ps = pstart_ref[:, 0:1].astype(F32)
        for c in range(tl // ch):
            e1 = rt_ref[0:1, c * ch:(c + 1) * ch]
            e2 = rt_ref[1:2, c * ch:(c + 1) * ch]
            col = pl.multiple_of(i * tl + c * ch, ch)
            d1 = rank_scr[0:1, pl.ds(col, ch)] + jnp.sum(jnp.where(e1 == ei, ps, 0.0), axis=0, keepdims=True)
            d2 = rank_scr[1:2, pl.ds(col, ch)] + jnp.sum(jnp.where(e2 == ei, ps, 0.0), axis=0, keepdims=True)
            dest_ref[0:1, c * ch:(c + 1) * ch] = d1.astype(I32)
            dest_ref[1:2, c * ch:(c + 1) * ch] = d2.astype(I32)


def _route(rt):
    tl = TL_ROUTE
    return pl.pallas_call(
        _route_kernel,
        grid=(2, TOKENS // tl),
        in_specs=[pl.BlockSpec((8, tl), lambda p, i: (0, i))],
        out_specs=[
            pl.BlockSpec((2, tl), lambda p, i: (0, i * p)),
            pl.BlockSpec((8, META_LANES), lambda p, i: (0, 0)),
        ],
        out_shape=[
            jax.ShapeDtypeStruct((2, TOKENS), I32),
            jax.ShapeDtypeStruct((8, META_LANES), I32),
        ],
        scratch_shapes=[
            pltpu.VMEM((2, TOKENS), F32),
            pltpu.VMEM((N_EXPERTS, LANES), F32),
            pltpu.VMEM((N_EXPERTS, LANES), I32),
        ],
        compiler_params=pltpu.CompilerParams(
            dimension_semantics=("arbitrary", "arbitrary"), vmem_limit_bytes=VMEM_LIMIT),
        name="route",
    )(rt)


SC_CORES = 2
SC_SUBCORES = 16
SC_WORKERS = SC_CORES * SC_SUBCORES
SC_LANES = 16
SC_WINDOW = 64
SC_SCAN_CHUNK = 32768
PART_UNIT = SC_WORKERS * 2 * SC_WINDOW
DISPATCH_SPLIT = (4, 10, 10, 12)
COMBINE_SPLIT = (2, 6, 8)


def _sc_move_rows(table_hbm, idx_v, out_hbm, out_base, n_windows, rows_v, gsem, wsems):
    assert n_windows % 2 == 0 and n_windows >= 2

    def gather(j, b):
        idx = idx_v.at[pl.ds(j * SC_WINDOW, SC_WINDOW)]
        pltpu.async_copy(table_hbm.at[idx], rows_v.at[b], gsem).wait()

    def write(j, b):
        dst = out_hbm.at[pl.ds(out_base + j * SC_WINDOW, SC_WINDOW)]
        return pltpu.make_async_copy(rows_v.at[b], dst, wsems.at[b])

    for b in range(2):
        gather(b, b)
        write(b, b).start()

    @pl.loop(2, n_windows, step=2)
    def _(j):
        for b in range(2):
            write(j - 2 + b, b).wait()
            gather(j + b, b)
            write(j + b, b).start()

    for b in range(2):
        write(n_windows - 2 + b, b).wait()


def _sc_inverse_map(dest0, dest1):
    n_tok = dest0.shape[0]
    rows_per_w = PADDED_ROWS // SC_WORKERS
    assert rows_per_w % SC_LANES == 0 and n_tok % SC_SCAN_CHUNK == 0
    mesh = plsc.VectorSubcoreMesh(core_axis_name="c", subcore_axis_name="s")

    @functools.partial(
        pl.kernel, mesh=mesh,
        out_type=jax.ShapeDtypeStruct((PADDED_ROWS,), I32),
        scratch_types=[pltpu.VMEM((rows_per_w,), I32), pltpu.VMEM((SC_SCAN_CHUNK,), I32)],
        compiler_params=pltpu.CompilerParams(needs_layout_passes=False),
        name="sc_inverse_map",
    )
    def inverse_map(d0_hbm, d1_hbm, tok_hbm, tok_v, dchunk_v):
        wid = lax.axis_index("s") * SC_CORES + lax.axis_index("c")
        lo = wid * rows_per_w
        lane = lax.iota(I32, SC_LANES)

        @pl.loop(0, rows_per_w // SC_LANES)
        def _(i):
            tok_v[pl.ds(i * SC_LANES, SC_LANES)] = (lo + i * SC_LANES + lane) & (n_tok - 1)

        for d_hbm in (d0_hbm, d1_hbm):
            @pl.loop(0, n_tok // SC_SCAN_CHUNK)
            def _(c, d_hbm=d_hbm):
                pltpu.sync_copy(d_hbm.at[pl.ds(c * SC_SCAN_CHUNK, SC_SCAN_CHUNK)], dchunk_v)

                @plsc.parallel_loop(0, SC_SCAN_CHUNK // SC_LANES, unroll=4)
                def _(v):
                    local = dchunk_v[pl.ds(v * SC_LANES, SC_LANES)] - lo
                    mine = (local >= 0) & (local < rows_per_w)
                    tok = c * SC_SCAN_CHUNK + v * SC_LANES + lane
                    plsc.store_scatter(tok_v, [jnp.where(mine, local, 0)], tok, mask=mine)

        pltpu.sync_copy(tok_v, tok_hbm.at[pl.ds(lo, rows_per_w)])

    return inverse_map(dest0, dest1)


def _sc_gather(table, idxs):
    n = idxs[0].shape[0]
    width = table.shape[1]
    per_w = n // SC_WORKERS
    assert per_w % (2 * SC_WINDOW) == 0
    mesh = plsc.VectorSubcoreMesh(core_axis_name="c", subcore_axis_name="s")
    out = jax.ShapeDtypeStruct((n, width), table.dtype)
    k = len(idxs)

    @functools.partial(
        pl.kernel, mesh=mesh, out_type=(out,) * k,
        scratch_types=[
            pltpu.VMEM((per_w,), I32),
            pltpu.VMEM((2, SC_WINDOW, width), table.dtype),
            pltpu.SemaphoreType.DMA,
            pltpu.SemaphoreType.DMA((2,)),
        ],
        name="sc_gather",
    )
    def gather(table_hbm, *refs):
        idx_hbms, out_hbms = refs[:k], refs[k:2 * k]
        idx_v, rows_v, gsem, wsems = refs[2 * k:]
        wid = lax.axis_index("s") * SC_CORES + lax.axis_index("c")
        base = wid * per_w
        for idx_hbm, out_hbm in zip(idx_hbms, out_hbms):
            pltpu.sync_copy(idx_hbm.at[pl.ds(base, per_w)], idx_v)
            _sc_move_rows(table_hbm, idx_v, out_hbm, base, per_w // SC_WINDOW, rows_v, gsem, wsems)

    return gather(table, *idxs)


def _experts_kernel(blk_start, be_ref, nv_ref, xs_ref, wg_ref, wu_ref, wd_ref, *rest):
    yb_ref = rest[-1]
    j = pl.program_id(0) + blk_start

    @pl.when(j < nv_ref[0])
    def _():
        lo, hi = _unpack_row_halves(xs_ref[...])
        xb = jnp.concatenate([lo.astype(BF16), hi.astype(BF16)], axis=1)
        a = jnp.dot(xb, wg_ref[0].astype(BF16), preferred_element_type=F32)
        b = jnp.dot(xb, wu_ref[0].astype(BF16), preferred_element_type=F32)
        hb = (jax.nn.silu(a) * b).astype(BF16)
        yb_ref[...] = _pack_row_halves(
            jnp.dot(hb, wd_ref[0].astype(BF16), preferred_element_type=F32))

    @pl.when(j >= nv_ref[0])
    def _():
        yb_ref[...] = jnp.zeros_like(yb_ref)


def _experts(blk_expert, n_valid, xs_part, blk_start, yb_prev, w_gate, w_up, w_down):
    n_blocks = xs_part.shape[0] // ROW_BLOCK

    def row_map(j, be, nv):
        return (jnp.clip(jnp.minimum(j + blk_start, nv[0] - 1) - blk_start, 0, n_blocks - 1), 0)

    def out_map(j, be, nv):
        return (j + blk_start, 0)

    def w_map(j, be, nv):
        return (be[jnp.minimum(j + blk_start, nv[0] - 1)], 0, 0)

    in_specs = [
        pl.BlockSpec((ROW_BLOCK, HALF_MODEL), row_map),
        pl.BlockSpec((1, D_MODEL, EXPERT_FF), w_map),
        pl.BlockSpec((1, D_MODEL, EXPERT_FF), w_map),
        pl.BlockSpec((1, EXPERT_FF, D_MODEL), w_map),
    ]
    operands = [blk_expert, n_valid, xs_part, w_gate, w_up, w_down]
    aliases = {}
    if yb_prev is not None:
        in_specs.append(pl.BlockSpec(memory_space=pl.ANY))
        aliases = {len(operands): 0}
        operands.append(yb_prev)
    grid_spec = pltpu.PrefetchScalarGridSpec(
        num_scalar_prefetch=2,
        grid=(n_blocks,),
        in_specs=in_specs,
        out_specs=pl.BlockSpec((ROW_BLOCK, HALF_MODEL), out_map),
    )
    return pl.pallas_call(
        functools.partial(_experts_kernel, blk_start),
        grid_spec=grid_spec,
        out_shape=jax.ShapeDtypeStruct((PADDED_ROWS, HALF_MODEL), I32),
        input_output_aliases=aliases,
        compiler_params=pltpu.CompilerParams(
            dimension_semantics=("arbitrary",), vmem_limit_bytes=VMEM_LIMIT),
        name="experts",
    )(*operands)


def _combine_kernel(route_ref, h_ref, g_ref, y0_ref, y1_ref, *rest):
    out_ref = rest[-1]
    w1 = route_ref[:, 2:3]
    w2 = route_ref[:, 3:4]
    lo0, hi0 = _unpack_row_halves(y0_ref[...])
    lo1, hi1 = _unpack_row_halves(y1_ref[...])
    y = jnp.concatenate([lo0 * w1 + lo1 * w2, hi0 * w1 + hi1 * w2], axis=1)
    out_ref[...] = _rms(h_ref[...] + y, g_ref[...])


def _combine(route, h2, g, y0, y1, tok_start, out_prev):
    tm = TM_COMBINE
    blk0 = tok_start // tm
    glob = lambda w: pl.BlockSpec((tm, w), lambda i: (i + blk0, 0))
    part = pl.BlockSpec((tm, HALF_MODEL), lambda i: (i, 0))
    in_specs = [glob(8), glob(D_MODEL), _resident((1, D_MODEL)), part, part]
    operands = [route, h2, g, y0, y1]
    aliases = {}
    if out_prev is not None:
        in_specs.append(pl.BlockSpec(memory_space=pl.ANY))
        aliases = {len(operands): 0}
        operands.append(out_prev)
    return pl.pallas_call(
        _combine_kernel,
        grid=(y0.shape[0] // tm,),
        in_specs=in_specs,
        out_specs=glob(D_MODEL),
        out_shape=jax.ShapeDtypeStruct((TOKENS, D_MODEL), F32),
        input_output_aliases=aliases,
        compiler_params=pltpu.CompilerParams(
            dimension_semantics=("parallel",), vmem_limit_bytes=VMEM_LIMIT),
        name="combine",
    )(*operands)


def kernel(x, mem, positions, mix_norm_g, w_in, b_gates, w_spatial, b_spatial, v_norm_g, v_norm_b,
           w_out_a, w_out_b, w_out, xattn_norm_g, mem_norm_g, w_q_x, w_kv_x, w_o_x, moe_norm_g,
           w_router_grp, b_router_grp, w_router_exp, b_router_exp, w_gate_e, w_up_e, w_down_e,
           final_norm_g):
    assert x.shape == (BATCH, SEQ, D_MODEL) and mem.shape == (BATCH, N_MEM, D_MODEL)
    assert mix_norm_g.shape[0] == 1, "single layer"
    x2d = x.reshape(TOKENS, D_MODEL)
    pos_col = positions.reshape(TOKENS, 1).astype(F32)
    half = HEAD_DIM // 2
    inv_freq = ROPE_THETA ** (-jnp.arange(half, dtype=F32) / half)
    invf = jnp.tile(inv_freq, LANES // half).reshape(1, LANES)
    phase = jnp.tile(jnp.concatenate([jnp.zeros((half,), F32), jnp.full((half,), math.pi / 2, F32)]),
                     LANES // HEAD_DIM).reshape(1, LANES)

    kv = _memkv(mem.reshape(BATCH * N_MEM, D_MODEL), mem_norm_g[0].reshape(1, D_MODEL),
                w_kv_x[0].astype(BF16))

    qkv0, qkv1, qkv2, ga, mb = _inproj(
        x2d, pos_col, invf, phase, mix_norm_g[0].reshape(1, D_MODEL), w_in[0].astype(BF16),
        b_gates[0].reshape(1, 2 * D_MODEL), w_spatial[0], b_spatial[0].T,
        v_norm_g[0].reshape(1, GMLP_WIDTH), v_norm_b[0].reshape(1, GMLP_WIDTH),
        w_out_b[0].astype(BF16))

    ya = _attention((qkv0, qkv1, qkv2)).reshape(TOKENS, GROUP_WIDTH)

    pad = LANES - N_EXPERTS - N_EXPERT_GROUPS
    w_r = jnp.concatenate([w_router_exp[0], w_router_grp[0], jnp.zeros((D_MODEL, pad), F32)], axis=1)
    b_r = jnp.concatenate([b_router_exp[0], b_router_grp[0], jnp.zeros((pad,), F32)]).reshape(1, LANES)
    h2, hn2, route, route_t = _post(
        ya, ga, mb, x2d, kv, w_out_a[0].astype(BF16), w_out[0].astype(BF16),
        xattn_norm_g[0].reshape(1, D_MODEL), w_q_x[0].astype(BF16), w_o_x[0].astype(BF16),
        moe_norm_g[0].reshape(1, D_MODEL), w_r.astype(BF16), b_r)

    dest, meta = _route(route_t)
    d0, d1 = dest[0], dest[1]
    tok_of_row = _sc_inverse_map(d0, d1)
    blk_expert = meta[0, :N_ROW_BLOCKS]
    n_valid = (meta[3, N_EXPERTS - 1:N_EXPERTS] // ROW_BLOCK).astype(I32)
    row_cuts = [PART_UNIT * c for c in itertools.accumulate((0,) + DISPATCH_SPLIT)]
    assert row_cuts[-1] == PADDED_ROWS
    xs_parts = [_sc_gather(hn2, [tok_of_row[a:b]])[0] for a, b in zip(row_cuts, row_cuts[1:])]
    yb = None
    for a, xs_part in zip(row_cuts, xs_parts):
        yb = _experts(blk_expert, n_valid, xs_part, a // ROW_BLOCK, yb, w_gate_e[0], w_up_e[0], w_down_e[0])
    g_fin = final_norm_g.reshape(1, D_MODEL)
    tok_cuts = [PART_UNIT * c for c in itertools.accumulate((0,) + COMBINE_SPLIT)]
    assert tok_cuts[-1] == TOKENS
    gathered = [_sc_gather(yb, [d0[a:b], d1[a:b]]) for a, b in zip(tok_cuts, tok_cuts[1:])]
    out = None
    for a, (y0, y1) in zip(tok_cuts, gathered):
        out = _combine(route, h2, g_fin, y0, y1, a, out)
    return out.reshape(BATCH, SEQ, D_MODEL)
```

```python
import functools
import itertools
import math

import jax
import jax.numpy as jnp
from jax import lax
from jax.experimental import pallas as pl
from jax.experimental.pallas import tpu as pltpu
from jax.experimental.pallas import tpu_sc as plsc

F32 = jnp.float32
BF16 = jnp.bfloat16
I32 = jnp.int32

D_MODEL = 1024
BATCH = 16
SEQ = 4096
TOKENS = BATCH * SEQ

HEAD_DIM = 64
DILATIONS = (1, 4, 16)
HEADS_PER_GROUP = 4
GROUP_WIDTH = HEADS_PER_GROUP * HEAD_DIM
ATT_WIDTH = len(DILATIONS) * GROUP_WIDTH
BAND_BLOCK = 128
ROPE_THETA = 10000.0

GMLP_CHUNK = 128
GMLP_GROUPS = 4
GMLP_WIDTH = 512

N_MEM = 256
XATTN_HEADS = 4
XATTN_HEAD_DIM = D_MODEL // XATTN_HEADS

N_EXPERT_GROUPS = 4
EXPERTS_PER_GROUP = 8
N_EXPERTS = 32
TOP_K = 2
EXPERT_FF = 512

RMS_EPS = 1e-6
LN_EPS = 1e-5
NEG_INF = -1e30

LANES = 128

COL_U = 3 * ATT_WIDTH
COL_V = COL_U + GMLP_WIDTH
COL_GA = COL_V + GMLP_WIDTH
COL_GB = COL_GA + D_MODEL

ROW_BLOCK = 512
ASSIGN = TOKENS * TOP_K
PADDED_ROWS = ASSIGN + N_EXPERTS * ROW_BLOCK
N_ROW_BLOCKS = PADDED_ROWS // ROW_BLOCK
META_LANES = ((N_ROW_BLOCKS + LANES - 1) // LANES) * LANES

TM_PROJ = 1024
PROJ_SUB = 512
TM_POST = 1024
POST_SUB = 512
TL_ROUTE = 8192
SCAN_CHUNK = 256
TM_COMBINE = 1024
ATTN_UNROLL = (32, 32, 32)

VMEM_LIMIT = 56 * 1024 * 1024


def _rms(x, g):
    return x * lax.rsqrt(jnp.mean(x * x, axis=-1, keepdims=True) + RMS_EPS) * g


HALF_MODEL = D_MODEL // 2


def _pack_row_halves(x):
    return pltpu.pack_elementwise([x[:, :HALF_MODEL], x[:, HALF_MODEL:]], packed_dtype=BF16)


def _unpack_row_halves(p):
    lo = pltpu.unpack_elementwise(p, index=0, packed_dtype=BF16, unpacked_dtype=F32)
    hi = pltpu.unpack_elementwise(p, index=1, packed_dtype=BF16, unpacked_dtype=F32)
    return lo, hi


def _resident(shape):
    nd = len(shape)
    return pl.BlockSpec(shape, lambda *_: (0,) * nd, pipeline_mode=pl.Buffered(1))


def _memkv_kernel(mem_ref, g_ref, w_ref, kv_ref):
    mn = _rms(mem_ref[...], g_ref[...]).astype(BF16)
    kv_ref[...] = jnp.dot(mn, w_ref[...], preferred_element_type=F32).astype(BF16)


def _memkv(mem2d, g, w_kv):
    rows = mem2d.shape[0]
    tm = 512
    return pl.pallas_call(
        _memkv_kernel,
        grid=(rows // tm,),
        in_specs=[
            pl.BlockSpec((tm, D_MODEL), lambda i: (i, 0)),
            _resident((1, D_MODEL)),
            _resident((D_MODEL, 2 * D_MODEL)),
        ],
        out_specs=pl.BlockSpec((tm, 2 * D_MODEL), lambda i: (i, 0)),
        out_shape=jax.ShapeDtypeStruct((rows, 2 * D_MODEL), BF16),
        compiler_params=pltpu.CompilerParams(
            dimension_semantics=("parallel",), vmem_limit_bytes=VMEM_LIMIT),
        name="memkv",
    )(mem2d, g, w_kv)


def _inproj_kernel(x_ref, pos_ref, invf_ref, phase_ref, g_ref, w_ref, bg_ref, wsp_ref,
                   bsp_ref, lng_ref, lnb_ref, wob_ref,
                   qkv0_ref, qkv1_ref, qkv2_ref, ga_ref, mb_ref, scr_ref, yb_ref):
    for sub in range(TM_PROJ // PROJ_SUB):
        _inproj_rows(sub, x_ref, pos_ref, invf_ref, phase_ref, g_ref, w_ref, bg_ref, wsp_ref,
                     bsp_ref, lng_ref, lnb_ref, wob_ref,
                     (qkv0_ref, qkv1_ref, qkv2_ref), ga_ref, mb_ref, scr_ref, yb_ref)


def _inproj_rows(sub, x_ref, pos_ref, invf_ref, phase_ref, g_ref, w_ref, bg_ref, wsp_ref,
                 bsp_ref, lng_ref, lnb_ref, wob_ref, out_refs, ga_ref, mb_ref, scr_ref, yb_ref):
    tm = PROJ_SUB
    rows = slice(sub * tm, (sub + 1) * tm)
    xn = _rms(x_ref[rows, :], g_ref[...]).astype(BF16)

    lane = lax.broadcasted_iota(I32, (tm, LANES), 1)
    upper = (lane & 32) != 0
    t1 = jnp.sin(pos_ref[rows, :] * invf_ref[...] + phase_ref[...])
    cosf = jnp.where(upper, t1, pltpu.roll(t1, 96, 1))
    sinf = jnp.where(upper, pltpu.roll(t1, 32, 1), -t1)

    def rope(res):
        outs = []
        for c in range(GROUP_WIDTH // LANES):
            xt = res[:, c * LANES:(c + 1) * LANES]
            rot = jnp.where(upper, pltpu.roll(xt, 32, 1), pltpu.roll(xt, 96, 1))
            outs.append(xt * cosf + rot * sinf)
        return jnp.concatenate(outs, axis=1)

    zu_raw = jnp.dot(xn, w_ref[:, COL_U:COL_V], preferred_element_type=F32)
    zv_raw = jnp.dot(xn, w_ref[:, COL_V:COL_GA], preferred_element_type=F32)

    slabs = GROUP_WIDTH // LANES

    def project_group(gi):
        dil = DILATIONS[gi]
        for which in range(3):
            c0 = which * ATT_WIDTH + gi * GROUP_WIDTH
            res = jnp.dot(xn, w_ref[:, c0:c0 + GROUP_WIDTH], preferred_element_type=F32)
            if which < 2:
                res = rope(res)
            if which == 0:
                res = res * (HEAD_DIM ** -0.5)
            if dil == 1:
                out_refs[gi][0, which, 0, rows, :] = res.astype(BF16)
            else:
                n = tm // dil
                for c in range(slabs):
                    slot = ((sub * 2 + gi - 1) * 3 + which) * slabs + c
                    scr_ref[slot] = res[:, c * LANES:(c + 1) * LANES]
                    for r in range(dil):
                        out_refs[gi][0, which, r, sub * n:(sub + 1) * n, c * LANES:(c + 1) * LANES] = (
                            scr_ref[slot, pl.ds(r, n, stride=dil), :].astype(BF16))

    for gi in range(len(DILATIONS)):
        project_group(gi)

    zu = jax.nn.gelu(zu_raw)
    zv = jax.nn.gelu(zv_raw)
    mu = jnp.mean(zv, axis=-1, keepdims=True)
    zc = zv - mu
    var = jnp.mean(zc * zc, axis=-1, keepdims=True)
    vn = (zc * lax.rsqrt(var + LN_EPS) * lng_ref[...] + lnb_ref[...]).astype(BF16)
    tri_r = lax.broadcasted_iota(I32, (GMLP_CHUNK, GMLP_CHUNK), 0)
    tri_c = lax.broadcasted_iota(I32, (GMLP_CHUNK, GMLP_CHUNK), 1)
    causal = tri_r >= tri_c
    n_chunks = tm // GMLP_CHUNK
    gw = GMLP_WIDTH // GMLP_GROUPS
    for g in range(GMLP_GROUPS):
        wsg = jnp.where(causal, wsp_ref[g], 0.0).astype(BF16)
        vcat = jnp.concatenate(
            [vn[c * GMLP_CHUNK:(c + 1) * GMLP_CHUNK, g * gw:(g + 1) * gw] for c in range(n_chunks)],
            axis=1)
        mixed = jnp.dot(wsg, vcat, preferred_element_type=F32) + bsp_ref[:, g:g + 1]
        for c in range(n_chunks):
            u_blk = zu[c * GMLP_CHUNK:(c + 1) * GMLP_CHUNK, g * gw:(g + 1) * gw]
            r0 = sub * tm + c * GMLP_CHUNK
            yb_ref[r0:r0 + GMLP_CHUNK, g * gw:(g + 1) * gw] = (
                u_blk * mixed[:, c * gw:(c + 1) * gw]).astype(BF16)

    gate_a = jax.nn.sigmoid(
        jnp.dot(xn, w_ref[:, COL_GA:COL_GB], preferred_element_type=F32) + bg_ref[:, :D_MODEL])
    ga_ref[rows, :] = gate_a.astype(BF16)
    gate_b = jax.nn.sigmoid(
        jnp.dot(xn, w_ref[:, COL_GB:COL_GB + D_MODEL], preferred_element_type=F32) + bg_ref[:, D_MODEL:])
    mb_ref[rows, :] = (gate_b * jnp.dot(yb_ref[rows, :], wob_ref[...], preferred_element_type=F32)).astype(BF16)


def _inproj(x2d, pos_col, invf, phase, g, w_in, b_gates, w_spatial, b_spatial_t, ln_g, ln_b, w_out_b):
    tm = TM_PROJ
    nt = SEQ // tm
    in_cols = w_in.shape[1]
    qkv_shapes = [jax.ShapeDtypeStruct((BATCH, 3, d, SEQ // d, GROUP_WIDTH), BF16) for d in DILATIONS]
    qkv_specs = [
        pl.BlockSpec((1, 3, d, tm // d, GROUP_WIDTH), lambda i: (i // nt, 0, 0, i % nt, 0))
        for d in DILATIONS
    ]
    tok_spec = pl.BlockSpec((tm, D_MODEL), lambda i: (i, 0))
    return pl.pallas_call(
        _inproj_kernel,
        grid=(TOKENS // tm,),
        in_specs=[
            tok_spec,
            pl.BlockSpec((tm, 1), lambda i: (i, 0)),
            _resident((1, LANES)),
            _resident((1, LANES)),
            _resident((1, D_MODEL)),
            _resident((D_MODEL, in_cols)),
            _resident((1, 2 * D_MODEL)),
            _resident((GMLP_GROUPS, GMLP_CHUNK, GMLP_CHUNK)),
            _resident((GMLP_CHUNK, GMLP_GROUPS)),
            _resident((1, GMLP_WIDTH)),
            _resident((1, GMLP_WIDTH)),
            _resident((GMLP_WIDTH, D_MODEL)),
        ],
        out_specs=qkv_specs + [tok_spec, tok_spec],
        out_shape=qkv_shapes + [jax.ShapeDtypeStruct((TOKENS, D_MODEL), BF16)] * 2,
        scratch_shapes=[
            pltpu.VMEM((6 * (GROUP_WIDTH // LANES) * (tm // PROJ_SUB), PROJ_SUB, LANES), F32),
            pltpu.VMEM((tm, GMLP_WIDTH), BF16),
        ],
        compiler_params=pltpu.CompilerParams(
            dimension_semantics=("parallel",), vmem_limit_bytes=VMEM_LIMIT),
        name="inproj",
    )(x2d, pos_col, invf, phase, g, w_in, b_gates, w_spatial, b_spatial_t, ln_g, ln_b, w_out_b)


def _attn_kernel(qkv0_ref, qkv1_ref, qkv2_ref, y_ref, acc_ref, m_ref, z_ref, bias_ref):
    blk = BAND_BLOCK
    lane_row = lax.broadcasted_iota(I32, (1, LANES), 1)
    head0_b = jnp.where(lane_row < HEAD_DIM, 1.0, 0.0).astype(BF16)
    head1_b = jnp.where(lane_row < HEAD_DIM, 0.0, 1.0).astype(BF16)
    head0 = lax.broadcasted_iota(I32, (blk, LANES), 1) < HEAD_DIM
    ones_b = jnp.ones((2 * blk, LANES), BF16)

    qi = lax.broadcasted_iota(I32, (2 * blk, 2 * blk), 0) & (blk - 1)
    kc = lax.broadcasted_iota(I32, (2 * blk, 2 * blk), 1)
    for slot, off in enumerate((0, blk)):
        dist = qi + off - kc
        bias_ref[slot] = jnp.where((dist >= 0) & (dist <= blk), 0.0, NEG_INF)

    for gi, (ref, dil) in enumerate(zip((qkv0_ref, qkv1_ref, qkv2_ref), DILATIONS)):
        seq_len = SEQ // dil
        nb = seq_len // blk
        nb_shift = nb.bit_length() - 1

        def body(i, carry, ref=ref, dil=dil, nb=nb, nb_shift=nb_shift, gi=gi):
            r = lax.shift_right_logical(i, nb_shift)
            n = i & (nb - 1)
            q0 = pl.multiple_of(n * blk, blk)
            w0 = pl.multiple_of(jnp.maximum(n - 1, 0) * blk, blk)
            q = ref[0, 0, r, pl.ds(q0, blk), :]
            k = ref[0, 1, r, pl.ds(w0, 2 * blk), :]
            v = ref[0, 2, r, pl.ds(w0, 2 * blk), :]
            q2 = jnp.concatenate([q * head0_b, q * head1_b], axis=0)
            s = lax.dot_general(q2, k, (((1,), (1,)), ((), ())), preferred_element_type=F32)
            s = s + bias_ref[jnp.minimum(n, 1)]
            m2 = jnp.max(s, axis=-1, keepdims=True)
            p = jnp.exp(s - m2)
            v_ext = jnp.concatenate([v, ones_b], axis=1)
            o2 = jnp.dot(p.astype(BF16), v_ext, preferred_element_type=F32)
            o = jnp.where(head0, o2[:blk, :LANES], o2[blk:, :LANES])
            den = jnp.where(head0, o2[:blk, LANES:], o2[blk:, LANES:])
            m = jnp.where(head0, m2[:blk], m2[blk:])
            if gi == 0:
                acc_ref[pl.ds(q0, blk), :] = o
                m_ref[pl.ds(q0, blk), :] = m
                z_ref[pl.ds(q0, blk), :] = den
            else:
                idx = pl.ds(n * (blk * dil) + r, blk, stride=dil)
                m_old = m_ref[idx, :]
                m_new = jnp.maximum(m_old, m)
                e_old = jnp.exp(m_old - m_new)
                e_new = jnp.exp(m - m_new)
                acc_ref[idx, :] = acc_ref[idx, :] * e_old + o * e_new
                z_ref[idx, :] = z_ref[idx, :] * e_old + den * e_new
                m_ref[idx, :] = m_new
            return carry

        lax.fori_loop(0, dil * nb, body, 0, unroll=ATTN_UNROLL[gi])

    y_ref[0] = (acc_ref[...] / z_ref[...]).astype(BF16)


def _attention(qkv):
    in_specs = [
        pl.BlockSpec((1, 3, d, SEQ // d, LANES), lambda b, h: (b, 0, 0, 0, h)) for d in DILATIONS
    ]
    return pl.pallas_call(
        _attn_kernel,
        grid=(BATCH, GROUP_WIDTH // LANES),
        in_specs=in_specs,
        out_specs=pl.BlockSpec((1, SEQ, LANES), lambda b, h: (b, 0, h)),
        out_shape=jax.ShapeDtypeStruct((BATCH, SEQ, GROUP_WIDTH), BF16),
        scratch_shapes=[
            pltpu.VMEM((SEQ, LANES), F32), pltpu.VMEM((SEQ, LANES), F32), pltpu.VMEM((SEQ, LANES), F32),
            pltpu.VMEM((2, 2 * BAND_BLOCK, 2 * BAND_BLOCK), F32),
        ],
        compiler_params=pltpu.CompilerParams(
            dimension_semantics=("parallel", "parallel"), vmem_limit_bytes=VMEM_LIMIT),
        name="attn",
    )(*qkv)


def _post_kernel(ya_ref, ga_ref, mb_ref, x_ref, k_ref, v_ref, woa_ref, wo_ref, xg_ref, wq_ref,
                 wox_ref, mg_ref, wr_ref, br_ref, h_ref, hn_ref, route_ref, route_t_ref, o_scr):
    for c in range(TM_POST // POST_SUB):
        rows = slice(c * POST_SUB, (c + 1) * POST_SUB)
        _post_rows(rows, ya_ref, ga_ref, mb_ref, x_ref, k_ref, v_ref, woa_ref, wo_ref, xg_ref, wq_ref,
                   wox_ref, mg_ref, wr_ref, br_ref, h_ref, hn_ref, route_ref, route_t_ref, o_scr)


def _post_rows(rows, ya_ref, ga_ref, mb_ref, x_ref, k_ref, v_ref, woa_ref, wo_ref, xg_ref, wq_ref,
               wox_ref, mg_ref, wr_ref, br_ref, h_ref, hn_ref, route_ref, route_t_ref, o_scr):
    tm = POST_SUB
    t = jnp.dot(ya_ref[rows, :], woa_ref[...], preferred_element_type=F32)
    merged = (ga_ref[rows, :].astype(F32) * t + mb_ref[rows, :].astype(F32)).astype(BF16)
    h1 = x_ref[rows, :] + jnp.dot(merged, wo_ref[...], preferred_element_type=F32)

    hn = _rms(h1, xg_ref[...]).astype(BF16)
    q = (jnp.dot(hn, wq_ref[...], preferred_element_type=F32) * (XATTN_HEAD_DIM ** -0.5)).astype(BF16)
    hd = XATTN_HEAD_DIM
    for h in range(XATTN_HEADS):
        s = lax.dot_general(q[:, h * hd:(h + 1) * hd], k_ref[:, h * hd:(h + 1) * hd],
                            (((1,), (1,)), ((), ())), preferred_element_type=F32)
        m = jnp.max(s, axis=-1, keepdims=True)
        p = jnp.exp(s - m)
        den = jnp.sum(p, axis=-1, keepdims=True)
        oh = jnp.dot(p.astype(BF16), v_ref[:, h * hd:(h + 1) * hd], preferred_element_type=F32) / den
        o_scr[rows, h * hd:(h + 1) * hd] = oh.astype(BF16)
    h2 = h1 + jnp.dot(o_scr[rows, :], wox_ref[...], preferred_element_type=F32)
    h_ref[rows, :] = h2

    hn2 = _rms(h2, mg_ref[...])
    hn_ref[rows, :] = _pack_row_halves(hn2)

    logits = jnp.dot(hn2.astype(BF16), wr_ref[...], preferred_element_type=F32) + br_ref[...]
    li = lax.broadcasted_iota(I32, (tm, LANES), 1)
    lif = li.astype(F32)
    grp_of_lane = lax.shift_right_logical(li, 3).astype(F32)
    is_grp = (li >= N_EXPERTS) & (li < N_EXPERTS + N_EXPERT_GROUPS)
    gl = jnp.where(is_grp, logits, -jnp.inf)
    gmax = jnp.max(gl, axis=-1, keepdims=True)
    grp = jnp.min(jnp.where(gl == gmax, lif - N_EXPERTS, float(LANES)), axis=-1, keepdims=True)
    gsum = jnp.sum(jnp.where(is_grp, jnp.exp(logits - gmax), 0.0), axis=-1, keepdims=True)
    grp_gate = 1.0 / gsum
    in_grp = grp_of_lane == grp
    el = jnp.where(in_grp, logits, -jnp.inf)
    v1 = jnp.max(el, axis=-1, keepdims=True)
    i1 = jnp.min(jnp.where(el == v1, lif, float(LANES)), axis=-1, keepdims=True)
    el2 = jnp.where(lif == i1, -jnp.inf, el)
    v2 = jnp.max(el2, axis=-1, keepdims=True)
    i2 = jnp.min(jnp.where(el2 == v2, lif, float(LANES)), axis=-1, keepdims=True)
    tt = jnp.exp(v2 - v1)
    w1 = grp_gate / (1.0 + tt)
    w2 = grp_gate * tt / (1.0 + tt)
    route = jnp.where(li == 0, i1,
                      jnp.where(li == 1, i2,
                                jnp.where(li == 2, w1, jnp.where(li == 3, w2, 0.0))))
    route_ref[rows, :] = route[:, :8]
    route_t_ref[:, rows] = route.T[:8, :]


def _post(ya, ga, mb, x2d, kv, w_out_a, w_out, xg, w_q, w_o, mg, w_r, b_r):
    tm = TM_POST
    nt = SEQ // tm
    tok = lambda w: pl.BlockSpec((tm, w), lambda i: (i, 0))
    return pl.pallas_call(
        _post_kernel,
        grid=(TOKENS // tm,),
        in_specs=[
            tok(GROUP_WIDTH), tok(D_MODEL), tok(D_MODEL), tok(D_MODEL),
            pl.BlockSpec((N_MEM, D_MODEL), lambda i: (i // nt, 0)),
            pl.BlockSpec((N_MEM, D_MODEL), lambda i: (i // nt, 1)),
            _resident((GROUP_WIDTH, D_MODEL)),
            _resident((D_MODEL, D_MODEL)),
            _resident((1, D_MODEL)),
            _resident((D_MODEL, D_MODEL)),
            _resident((D_MODEL, D_MODEL)),
            _resident((1, D_MODEL)),
            _resident((D_MODEL, LANES)),
            _resident((1, LANES)),
        ],
        out_specs=[tok(D_MODEL), tok(HALF_MODEL), pl.BlockSpec((tm, 8), lambda i: (i, 0)),
                   pl.BlockSpec((8, tm), lambda i: (0, i))],
        out_shape=[
            jax.ShapeDtypeStruct((TOKENS, D_MODEL), F32),
            jax.ShapeDtypeStruct((TOKENS, HALF_MODEL), I32),
            jax.ShapeDtypeStruct((TOKENS, 8), F32),
            jax.ShapeDtypeStruct((8, TOKENS), F32),
        ],
        scratch_shapes=[pltpu.VMEM((tm, D_MODEL), BF16)],
        compiler_params=pltpu.CompilerParams(
            dimension_semantics=("parallel",), vmem_limit_bytes=VMEM_LIMIT),
        name="post",
    )(ya, ga, mb, x2d, kv, kv, w_out_a, w_out, xg, w_q, w_o, mg, w_r, b_r)


def _route_kernel(rt_ref, dest_ref, meta_ref, rank_scr, carry_ref, pstart_ref):
    pss = pl.program_id(0)
    i = pl.program_id(1)
    tl = TL_ROUTE
    ch = SCAN_CHUNK
    ei = lax.broadcasted_iota(I32, (N_EXPERTS, ch), 0).astype(F32)

    @pl.when((pss == 0) & (i == 0))
    def _():
        carry_ref[...] = jnp.zeros_like(carry_ref)

    @pl.when(pss == 0)
    def _():
        ur = lax.broadcasted_iota(I32, (ch, ch), 0)
        uc = lax.broadcasted_iota(I32, (ch, ch), 1)
        upper = jnp.where(ur < uc, 1.0, 0.0).astype(BF16)
        for c in range(tl // ch):
            e1 = rt_ref[0:1, c * ch:(c + 1) * ch]
            e2 = rt_ref[1:2, c * ch:(c + 1) * ch]
            oh1 = e1 == ei
            oh2 = e2 == ei
            oh = jnp.where(oh1 | oh2, 1.0, 0.0)
            cnt = jnp.dot(oh.astype(BF16), upper, preferred_element_type=F32) + carry_ref[:, 0:1]
            rank1 = jnp.sum(jnp.where(oh1, cnt, 0.0), axis=0, keepdims=True)
            rank2 = jnp.sum(jnp.where(oh2, cnt, 0.0), axis=0, keepdims=True)
            col = pl.multiple_of(i * tl + c * ch, ch)
            rank_scr[0:1, pl.ds(col, ch)] = rank1
            rank_scr[1:2, pl.ds(col, ch)] = rank2
            carry_ref[...] = carry_ref[...] + jnp.sum(oh, axis=1, keepdims=True)

    @pl.when((pss == 1) & (i == 0))
    def _():
        counts = carry_ref[...].astype(I32)
        padded = lax.shift_left(lax.shift_right_logical(counts + (ROW_BLOCK - 1),
                                                        int(math.log2(ROW_BLOCK))),
                                int(math.log2(ROW_BLOCK)))
        row = lax.broadcasted_iota(I32, (N_EXPERTS, LANES), 0)
        lane = lax.broadcasted_iota(I32, (N_EXPERTS, LANES), 1)
        pend = padded
        sh = 1
        while sh < N_EXPERTS:
            pend = pend + jnp.where(row >= sh, pltpu.roll(pend, sh, 0), 0)
            sh *= 2
        pstart = pend - padded
        pstart_ref[...] = pstart
        diag = row == lane

        def as_row(x):
            return jnp.sum(jnp.where(diag, x, 0), axis=0, keepdims=True)

        blk_lane = lax.broadcasted_iota(I32, (N_EXPERTS, META_LANES), 1) * ROW_BLOCK
        blk_exp = jnp.sum(jnp.where(pend[:, 0:1] <= blk_lane, 1, 0), axis=0, keepdims=True)
        blk_exp = jnp.minimum(blk_exp, N_EXPERTS - 1)
        meta_ref[...] = jnp.zeros_like(meta_ref)
        meta_ref[0:1, :] = blk_exp
        meta_ref[3:4, 0:LANES] = as_row(pend)

    @pl.when(pss == 1)
    def _():
        ps = pstart_ref[:, 0:1].astype(F32)
        for c in range(tl // ch):
            e1 = rt_ref[0:1, c * ch:(c + 1) * ch]
            e2 = rt_ref[1:2, c * ch:(c + 1) * ch]
            col = pl.multiple_of(i * tl + c * ch, ch)
            d1 = rank_scr[0:1, pl.ds(col, ch)] + jnp.sum(jnp.where(e1 == ei, ps, 0.0), axis=0, keepdims=True)
            d2 = rank_scr[1:2, pl.ds(col, ch)] + jnp.sum(jnp.where(e2 == ei, ps, 0.0), axis=0, keepdims=True)
            dest_ref[0:1, c * ch:(c + 1) * ch] = d1.astype(I32)
            dest_ref[1:2, c * ch:(c + 1) * ch] = d2.astype(I32)


def _route(rt):
    tl = TL_ROUTE
    return pl.pallas_call(
        _route_kernel,
        grid=(2, TOKENS // tl),
        in_specs=[pl.BlockSpec((8, tl), lambda p, i: (0, i))],
        out_specs=[
            pl.BlockSpec((2, tl), lambda p, i: (0, i * p)),
            pl.BlockSpec((8, META_LANES), lambda p, i: (0, 0)),
        ],
        out_shape=[
            jax.ShapeDtypeStruct((2, TOKENS), I32),
            jax.ShapeDtypeStruct((8, META_LANES), I32),
        ],
        scratch_shapes=[
            pltpu.VMEM((2, TOKENS), F32),
            pltpu.VMEM((N_EXPERTS, LANES), F32),
            pltpu.VMEM((N_EXPERTS, LANES), I32),
        ],
        compiler_params=pltpu.CompilerParams(
            dimension_semantics=("arbitrary", "arbitrary"), vmem_limit_bytes=VMEM_LIMIT),
        name="route",
    )(rt)


SC_CORES = 2
SC_SUBCORES = 16
SC_WORKERS = SC_CORES * SC_SUBCORES
SC_LANES = 16
SC_WINDOW = 64
SC_SCAN_CHUNK = 32768
PART_UNIT = SC_WORKERS * 2 * SC_WINDOW
DISPATCH_SPLIT = (4, 10, 10, 12)
COMBINE_SPLIT = (2, 6, 8)


def _sc_move_rows(table_hbm, idx_v, out_hbm, out_base, n_windows, rows_v, gsem, wsems):
    assert n_windows % 2 == 0 and n_windows >= 2

    def gather(j, b):
        idx = idx_v.at[pl.ds(j * SC_WINDOW, SC_WINDOW)]
        pltpu.async_copy(table_hbm.at[idx], rows_v.at[b], gsem).wait()

    def write(j, b):
        dst = out_hbm.at[pl.ds(out_base + j * SC_WINDOW, SC_WINDOW)]
        return pltpu.make_async_copy(rows_v.at[b], dst, wsems.at[b])

    for b in range(2):
        gather(b, b)
        write(b, b).start()

    @pl.loop(2, n_windows, step=2)
    def _(j):
        for b in range(2):
            write(j - 2 + b, b).wait()
            gather(j + b, b)
            write(j + b, b).start()

    for b in range(2):
        write(n_windows - 2 + b, b).wait()


def _sc_inverse_map(dest0, dest1):
    n_tok = dest0.shape[0]
    rows_per_w = PADDED_ROWS // SC_WORKERS
    assert rows_per_w % SC_LANES == 0 and n_tok % SC_SCAN_CHUNK == 0
    mesh = plsc.VectorSubcoreMesh(core_axis_name="c", subcore_axis_name="s")

    @functools.partial(
        pl.kernel, mesh=mesh,
        out_type=jax.ShapeDtypeStruct((PADDED_ROWS,), I32),
        scratch_types=[pltpu.VMEM((rows_per_w,), I32), pltpu.VMEM((SC_SCAN_CHUNK,), I32)],
        compiler_params=pltpu.CompilerParams(needs_layout_passes=False),
        name="sc_inverse_map",
    )
    def inverse_map(d0_hbm, d1_hbm, tok_hbm, tok_v, dchunk_v):
        wid = lax.axis_index("s") * SC_CORES + lax.axis_index("c")
        lo = wid * rows_per_w
        lane = lax.iota(I32, SC_LANES)

        @pl.loop(0, rows_per_w // SC_LANES)
        def _(i):
            tok_v[pl.ds(i * SC_LANES, SC_LANES)] = (lo + i * SC_LANES + lane) & (n_tok - 1)

        for d_hbm in (d0_hbm, d1_hbm):
            @pl.loop(0, n_tok // SC_SCAN_CHUNK)
            def _(c, d_hbm=d_hbm):
                pltpu.sync_copy(d_hbm.at[pl.ds(c * SC_SCAN_CHUNK, SC_SCAN_CHUNK)], dchunk_v)

                @plsc.parallel_loop(0, SC_SCAN_CHUNK // SC_LANES, unroll=4)
                def _(v):
                    local = dchunk_v[pl.ds(v * SC_LANES, SC_LANES)] - lo
                    mine = (local >= 0) & (local < rows_per_w)
                    tok = c * SC_SCAN_CHUNK + v * SC_LANES + lane
                    plsc.store_scatter(tok_v, [jnp.where(mine, local, 0)], tok, mask=mine)

        pltpu.sync_copy(tok_v, tok_hbm.at[pl.ds(lo, rows_per_w)])

    return inverse_map(dest0, dest1)


def _sc_gather(table, idxs):
    n = idxs[0].shape[0]
    width = table.shape[1]
    per_w = n // SC_WORKERS
    assert per_w % (2 * SC_WINDOW) == 0
    mesh = plsc.VectorSubcoreMesh(core_axis_name="c", subcore_axis_name="s")
    out = jax.ShapeDtypeStruct((n, width), table.dtype)
    k = len(idxs)

    @functools.partial(
        pl.kernel, mesh=mesh, out_type=(out,) * k,
        scratch_types=[
            pltpu.VMEM((per_w,), I32),
            pltpu.VMEM((2, SC_WINDOW, width), table.dtype),
            pltpu.SemaphoreType.DMA,
            pltpu.SemaphoreType.DMA((2,)),
        ],
        name="sc_gather",
    )
    def gather(table_hbm, *refs):
        idx_hbms, out_hbms = refs[:k], refs[k:2 * k]
        idx_v, rows_v, gsem, wsems = refs[2 * k:]
        wid = lax.axis_index("s") * SC_CORES + lax.axis_index("c")
        base = wid * per_w
        for idx_hbm, out_hbm in zip(idx_hbms, out_hbms):
            pltpu.sync_copy(idx_hbm.at[pl.ds(base, per_w)], idx_v)
            _sc_move_rows(table_hbm, idx_v, out_hbm, base, per_w // SC_WINDOW, rows_v, gsem, wsems)

    return gather(table, *idxs)


def _experts_kernel(blk_start, be_ref, nv_ref, xs_ref, wg_ref, wu_ref, wd_ref, *rest):
    yb_ref = rest[-1]
    j = pl.program_id(0) + blk_start

    @pl.when(j < nv_ref[0])
    def _():
        lo, hi = _unpack_row_halves(xs_ref[...])
        xb = jnp.concatenate([lo.astype(BF16), hi.astype(BF16)], axis=1)
        a = jnp.dot(xb, wg_ref[0].astype(BF16), preferred_element_type=F32)
        b = jnp.dot(xb, wu_ref[0].astype(BF16), preferred_element_type=F32)
        hb = (jax.nn.silu(a) * b).astype(BF16)
        yb_ref[...] = _pack_row_halves(
            jnp.dot(hb, wd_ref[0].astype(BF16), preferred_element_type=F32))

    @pl.when(j >= nv_ref[0])
    def _():
        yb_ref[...] = jnp.zeros_like(yb_ref)


def _experts(blk_expert, n_valid, xs_part, blk_start, yb_prev, w_gate, w_up, w_down):
    n_blocks = xs_part.shape[0] // ROW_BLOCK

    def row_map(j, be, nv):
        return (jnp.clip(jnp.minimum(j + blk_start, nv[0] - 1) - blk_start, 0, n_blocks - 1), 0)

    def out_map(j, be, nv):
        return (j + blk_start, 0)

    def w_map(j, be, nv):
        return (be[jnp.minimum(j + blk_start, nv[0] - 1)], 0, 0)

    in_specs = [
        pl.BlockSpec((ROW_BLOCK, HALF_MODEL), row_map),
        pl.BlockSpec((1, D_MODEL, EXPERT_FF), w_map),
        pl.BlockSpec((1, D_MODEL, EXPERT_FF), w_map),
        pl.BlockSpec((1, EXPERT_FF, D_MODEL), w_map),
    ]
    operands = [blk_expert, n_valid, xs_part, w_gate, w_up, w_down]
    aliases = {}
    if yb_prev is not None:
        in_specs.append(pl.BlockSpec(memory_space=pl.ANY))
        aliases = {len(operands): 0}
        operands.append(yb_prev)
    grid_spec = pltpu.PrefetchScalarGridSpec(
        num_scalar_prefetch=2,
        grid=(n_blocks,),
        in_specs=in_specs,
        out_specs=pl.BlockSpec((ROW_BLOCK, HALF_MODEL), out_map),
    )
    return pl.pallas_call(
        functools.partial(_experts_kernel, blk_start),
        grid_spec=grid_spec,
        out_shape=jax.ShapeDtypeStruct((PADDED_ROWS, HALF_MODEL), I32),
        input_output_aliases=aliases,
        compiler_params=pltpu.CompilerParams(
            dimension_semantics=("arbitrary",), vmem_limit_bytes=VMEM_LIMIT),
        name="experts",
    )(*operands)


def _combine_kernel(route_ref, h_ref, g_ref, y0_ref, y1_ref, *rest):
    out_ref = rest[-1]
    w1 = route_ref[:, 2:3]
    w2 = route_ref[:, 3:4]
    lo0, hi0 = _unpack_row_halves(y0_ref[...])
    lo1, hi1 = _unpack_row_halves(y1_ref[...])
    y = jnp.concatenate([lo0 * w1 + lo1 * w2, hi0 * w1 + hi1 * w2], axis=1)
    out_ref[...] = _rms(h_ref[...] + y, g_ref[...])


def _combine(route, h2, g, y0, y1, tok_start, out_prev):
    tm = TM_COMBINE
    blk0 = tok_start // tm
    glob = lambda w: pl.BlockSpec((tm, w), lambda i: (i + blk0, 0))
    part = pl.BlockSpec((tm, HALF_MODEL), lambda i: (i, 0))
    in_specs = [glob(8), glob(D_MODEL), _resident((1, D_MODEL)), part, part]
    operands = [route, h2, g, y0, y1]
    aliases = {}
    if out_prev is not None:
        in_specs.append(pl.BlockSpec(memory_space=pl.ANY))
        aliases = {len(operands): 0}
        operands.append(out_prev)
    return pl.pallas_call(
        _combine_kernel,
        grid=(y0.shape[0] // tm,),
        in_specs=in_specs,
        out_specs=glob(D_MODEL),
        out_shape=jax.ShapeDtypeStruct((TOKENS, D_MODEL), F32),
        input_output_aliases=aliases,
        compiler_params=pltpu.CompilerParams(
            dimension_semantics=("parallel",), vmem_limit_bytes=VMEM_LIMIT),
        name="combine",
    )(*operands)


def kernel(x, mem, positions, mix_norm_g, w_in, b_gates, w_spatial, b_spatial, v_norm_g, v_norm_b,
           w_out_a, w_out_b, w_out, xattn_norm_g, mem_norm_g, w_q_x, w_kv_x, w_o_x, moe_norm_g,
           w_router_grp, b_router_grp, w_router_exp, b_router_exp, w_gate_e, w_up_e, w_down_e,
           final_norm_g):
    assert x.shape == (BATCH, SEQ, D_MODEL) and mem.shape == (BATCH, N_MEM, D_MODEL)
    assert mix_norm_g.shape[0] == 1, "single layer"
    x2d = x.reshape(TOKENS, D_MODEL)
    pos_col = positions.reshape(TOKENS, 1).astype(F32)
    half = HEAD_DIM // 2
    inv_freq = ROPE_THETA ** (-jnp.arange(half, dtype=F32) / half)
    invf = jnp.tile(inv_freq, LANES // half).reshape(1, LANES)
    phase = jnp.tile(jnp.concatenate([jnp.zeros((half,), F32), jnp.full((half,), math.pi / 2, F32)]),
                     LANES // HEAD_DIM).reshape(1, LANES)

    kv = _memkv(mem.reshape(BATCH * N_MEM, D_MODEL), mem_norm_g[0].reshape(1, D_MODEL),
                w_kv_x[0].astype(BF16))

    qkv0, qkv1, qkv2, ga, mb = _inproj(
        x2d, pos_col, invf, phase, mix_norm_g[0].reshape(1, D_MODEL), w_in[0].astype(BF16),
        b_gates[0].reshape(1, 2 * D_MODEL), w_spatial[0], b_spatial[0].T,
        v_norm_g[0].reshape(1, GMLP_WIDTH), v_norm_b[0].reshape(1, GMLP_WIDTH),
        w_out_b[0].astype(BF16))

    ya = _attention((qkv0, qkv1, qkv2)).reshape(TOKENS, GROUP_WIDTH)

    pad = LANES - N_EXPERTS - N_EXPERT_GROUPS
    w_r = jnp.concatenate([w_router_exp[0], w_router_grp[0], jnp.zeros((D_MODEL, pad), F32)], axis=1)
    b_r = jnp.concatenate([b_router_exp[0], b_router_grp[0], jnp.zeros((pad,), F32)]).reshape(1, LANES)
    h2, hn2, route, route_t = _post(
        ya, ga, mb, x2d, kv, w_out_a[0].astype(BF16), w_out[0].astype(BF16),
        xattn_norm_g[0].reshape(1, D_MODEL), w_q_x[0].astype(BF16), w_o_x[0].astype(BF16),
        moe_norm_g[0].reshape(1, D_MODEL), w_r.astype(BF16), b_r)

    dest, meta = _route(route_t)
    d0, d1 = dest[0], dest[1]
    tok_of_row = _sc_inverse_map(d0, d1)
    blk_expert = meta[0, :N_ROW_BLOCKS]
    n_valid = (meta[3, N_EXPERTS - 1:N_EXPERTS] // ROW_BLOCK).astype(I32)
    row_cuts = [PART_UNIT * c for c in itertools.accumulate((0,) + DISPATCH_SPLIT)]
    assert row_cuts[-1] == PADDED_ROWS
    xs_parts = [_sc_gather(hn2, [tok_of_row[a:b]])[0] for a, b in zip(row_cuts, row_cuts[1:])]
    yb = None
    for a, xs_part in zip(row_cuts, xs_parts):
        yb = _experts(blk_expert, n_valid, xs_part, a // ROW_BLOCK, yb, w_gate_e[0], w_up_e[0], w_down_e[0])
    g_fin = final_norm_g.reshape(1, D_MODEL)
    tok_cuts = [PART_UNIT * c for c in itertools.accumulate((0,) + COMBINE_SPLIT)]
    assert tok_cuts[-1] == TOKENS
    gathered = [_sc_gather(yb, [d0[a:b], d1[a:b]]) for a, b in zip(tok_cuts, tok_cuts[1:])]
    out = None
    for a, (y0, y1) in zip(tok_cuts, gathered):
        out = _combine(route, h2, g_fin, y0, y1, a, out)
    return out.reshape(BATCH, SEQ, D_MODEL)
```

```python
import functools
import itertools
import math

import jax
import jax.numpy as jnp
from jax import lax
from jax.experimental import pallas as pl
from jax.experimental.pallas import tpu as pltpu
from jax.experimental.pallas import tpu_sc as plsc

F32 = jnp.float32
BF16 = jnp.bfloat16
I32 = jnp.int32

D_MODEL = 1024
BATCH = 16
SEQ = 4096
TOKENS = BATCH * SEQ

HEAD_DIM = 64
DILATIONS = (1, 4, 16)
HEADS_PER_GROUP = 4
GROUP_WIDTH = HEADS_PER_GROUP * HEAD_DIM
ATT_WIDTH = len(DILATIONS) * GROUP_WIDTH
BAND_BLOCK = 128
ROPE_THETA = 10000.0

GMLP_CHUNK = 128
GMLP_GROUPS = 4
GMLP_WIDTH = 512

N_MEM = 256
XATTN_HEADS = 4
XATTN_HEAD_DIM = D_MODEL // XATTN_HEADS

N_EXPERT_GROUPS = 4
EXPERTS_PER_GROUP = 8
N_EXPERTS = 32
TOP_K = 2
EXPERT_FF = 512

RMS_EPS = 1e-6
LN_EPS = 1e-5
NEG_INF = -1e30

LANES = 128

COL_U = 3 * ATT_WIDTH
COL_V = COL_U + GMLP_WIDTH
COL_GA = COL_V + GMLP_WIDTH
COL_GB = COL_GA + D_MODEL

ROW_BLOCK = 512
ASSIGN = TOKENS * TOP_K
PADDED_ROWS = ASSIGN + N_EXPERTS * ROW_BLOCK
N_ROW_BLOCKS = PADDED_ROWS // ROW_BLOCK
META_LANES = ((N_ROW_BLOCKS + LANES - 1) // LANES) * LANES

TM_PROJ = 1024
PROJ_SUB = 512
TM_POST = 1024
POST_SUB = 512
TL_ROUTE = 8192
SCAN_CHUNK = 256
TM_COMBINE = 1024
ATTN_UNROLL = (32, 32, 32)

VMEM_LIMIT = 56 * 1024 * 1024


def _rms(x, g):
    return x * lax.rsqrt(jnp.mean(x * x, axis=-1, keepdims=True) + RMS_EPS) * g


HALF_MODEL = D_MODEL // 2


def _pack_row_halves(x):
    return pltpu.pack_elementwise([x[:, :HALF_MODEL], x[:, HALF_MODEL:]], packed_dtype=BF16)


def _unpack_row_halves(p):
    lo = pltpu.unpack_elementwise(p, index=0, packed_dtype=BF16, unpacked_dtype=F32)
    hi = pltpu.unpack_elementwise(p, index=1, packed_dtype=BF16, unpacked_dtype=F32)
    return lo, hi


def _resident(shape):
    nd = len(shape)
    return pl.BlockSpec(shape, lambda *_: (0,) * nd, pipeline_mode=pl.Buffered(1))


def _memkv_kernel(mem_ref, g_ref, w_ref, kv_ref):
    mn = _rms(mem_ref[...], g_ref[...]).astype(BF16)
    kv_ref[...] = jnp.dot(mn, w_ref[...], preferred_element_type=F32).astype(BF16)


def _memkv(mem2d, g, w_kv):
    rows = mem2d.shape[0]
    tm = 512
    return pl.pallas_call(
        _memkv_kernel,
        grid=(rows // tm,),
        in_specs=[
            pl.BlockSpec((tm, D_MODEL), lambda i: (i, 0)),
            _resident((1, D_MODEL)),
            _resident((D_MODEL, 2 * D_MODEL)),
        ],
        out_specs=pl.BlockSpec((tm, 2 * D_MODEL), lambda i: (i, 0)),
        out_shape=jax.ShapeDtypeStruct((rows, 2 * D_MODEL), BF16),
        compiler_params=pltpu.CompilerParams(
            dimension_semantics=("parallel",), vmem_limit_bytes=VMEM_LIMIT),
        name="memkv",
    )(mem2d, g, w_kv)


def _inproj_kernel(x_ref, pos_ref, invf_ref, phase_ref, g_ref, w_ref, bg_ref, wsp_ref,
                   bsp_ref, lng_ref, lnb_ref, wob_ref,
                   qkv0_ref, qkv1_ref, qkv2_ref, ga_ref, mb_ref, scr_ref, yb_ref):
    for sub in range(TM_PROJ // PROJ_SUB):
        _inproj_rows(sub, x_ref, pos_ref, invf_ref, phase_ref, g_ref, w_ref, bg_ref, wsp_ref,
                     bsp_ref, lng_ref, lnb_ref, wob_ref,
                     (qkv0_ref, qkv1_ref, qkv2_ref), ga_ref, mb_ref, scr_ref, yb_ref)


def _inproj_rows(sub, x_ref, pos_ref, invf_ref, phase_ref, g_ref, w_ref, bg_ref, wsp_ref,
                 bsp_ref, lng_ref, lnb_ref, wob_ref, out_refs, ga_ref, mb_ref, scr_ref, yb_ref):
    tm = PROJ_SUB
    rows = slice(sub * tm, (sub + 1) * tm)
    xn = _rms(x_ref[rows, :], g_ref[...]).astype(BF16)

    lane = lax.broadcasted_iota(I32, (tm, LANES), 1)
    upper = (lane & 32) != 0
    t1 = jnp.sin(pos_ref[rows, :] * invf_ref[...] + phase_ref[...])
    cosf = jnp.where(upper, t1, pltpu.roll(t1, 96, 1))
    sinf = jnp.where(upper, pltpu.roll(t1, 32, 1), -t1)

    def rope(res):
        outs = []
        for c in range(GROUP_WIDTH // LANES):
            xt = res[:, c * LANES:(c + 1) * LANES]
            rot = jnp.where(upper, pltpu.roll(xt, 32, 1), pltpu.roll(xt, 96, 1))
            outs.append(xt * cosf + rot * sinf)
        return jnp.concatenate(outs, axis=1)

    zu_raw = jnp.dot(xn, w_ref[:, COL_U:COL_V], preferred_element_type=F32)
    zv_raw = jnp.dot(xn, w_ref[:, COL_V:COL_GA], preferred_element_type=F32)

    slabs = GROUP_WIDTH // LANES

    def project_group(gi):
        dil = DILATIONS[gi]
        for which in range(3):
            c0 = which * ATT_WIDTH + gi * GROUP_WIDTH
            res = jnp.dot(xn, w_ref[:, c0:c0 + GROUP_WIDTH], preferred_element_type=F32)
            if which < 2:
                res = rope(res)
            if which == 0:
                res = res * (HEAD_DIM ** -0.5)
            if dil == 1:
                out_refs[gi][0, which, 0, rows, :] = res.astype(BF16)
            else:
                n = tm // dil
                for c in range(slabs):
                    slot = ((sub * 2 + gi - 1) * 3 + which) * slabs + c
                    scr_ref[slot] = res[:, c * LANES:(c + 1) * LANES]
                    for r in range(dil):
                        out_refs[gi][0, which, r, sub * n:(sub + 1) * n, c * LANES:(c + 1) * LANES] = (
                            scr_ref[slot, pl.ds(r, n, stride=dil), :].astype(BF16))

    for gi in range(len(DILATIONS)):
        project_group(gi)

    zu = jax.nn.gelu(zu_raw)
    zv = jax.nn.gelu(zv_raw)
    mu = jnp.mean(zv, axis=-1, keepdims=True)
    zc = zv - mu
    var = jnp.mean(zc * zc, axis=-1, keepdims=True)
    vn = (zc * lax.rsqrt(var + LN_EPS) * lng_ref[...] + lnb_ref[...]).astype(BF16)
    tri_r = lax.broadcasted_iota(I32, (GMLP_CHUNK, GMLP_CHUNK), 0)
    tri_c = lax.broadcasted_iota(I32, (GMLP_CHUNK, GMLP_CHUNK), 1)
    causal = tri_r >= tri_c
    n_chunks = tm // GMLP_CHUNK
    gw = GMLP_WIDTH // GMLP_GROUPS
    for g in range(GMLP_GROUPS):
        wsg = jnp.where(causal, wsp_ref[g], 0.0).astype(BF16)
        vcat = jnp.concatenate(
            [vn[c * GMLP_CHUNK:(c + 1) * GMLP_CHUNK, g * gw:(g + 1) * gw] for c in range(n_chunks)],
            axis=1)
        mixed = jnp.dot(wsg, vcat, preferred_element_type=F32) + bsp_ref[:, g:g + 1]
        for c in range(n_chunks):
            u_blk = zu[c * GMLP_CHUNK:(c + 1) * GMLP_CHUNK, g * gw:(g + 1) * gw]
            r0 = sub * tm + c * GMLP_CHUNK
            yb_ref[r0:r0 + GMLP_CHUNK, g * gw:(g + 1) * gw] = (
                u_blk * mixed[:, c * gw:(c + 1) * gw]).astype(BF16)

    gate_a = jax.nn.sigmoid(
        jnp.dot(xn, w_ref[:, COL_GA:COL_GB], preferred_element_type=F32) + bg_ref[:, :D_MODEL])
    ga_ref[rows, :] = gate_a.astype(BF16)
    gate_b = jax.nn.sigmoid(
        jnp.dot(xn, w_ref[:, COL_GB:COL_GB + D_MODEL], preferred_element_type=F32) + bg_ref[:, D_MODEL:])
    mb_ref[rows, :] = (gate_b * jnp.dot(yb_ref[rows, :], wob_ref[...], preferred_element_type=F32)).astype(BF16)


def _inproj(x2d, pos_col, invf, phase, g, w_in, b_gates, w_spatial, b_spatial_t, ln_g, ln_b, w_out_b):
    tm = TM_PROJ
    nt = SEQ // tm
    in_cols = w_in.shape[1]
    qkv_shapes = [jax.ShapeDtypeStruct((BATCH, 3, d, SEQ // d, GROUP_WIDTH), BF16) for d in DILATIONS]
    qkv_specs = [
        pl.BlockSpec((1, 3, d, tm // d, GROUP_WIDTH), lambda i: (i // nt, 0, 0, i % nt, 0))
        for d in DILATIONS
    ]
    tok_spec = pl.BlockSpec((tm, D_MODEL), lambda i: (i, 0))
    return pl.pallas_call(
        _inproj_kernel,
        grid=(TOKENS // tm,),
        in_specs=[
            tok_spec,
            pl.BlockSpec((tm, 1), lambda i: (i, 0)),
            _resident((1, LANES)),
            _resident((1, LANES)),
            _resident((1, D_MODEL)),
            _resident((D_MODEL, in_cols)),
            _resident((1, 2 * D_MODEL)),
            _resident((GMLP_GROUPS, GMLP_CHUNK, GMLP_CHUNK)),
            _resident((GMLP_CHUNK, GMLP_GROUPS)),
            _resident((1, GMLP_WIDTH)),
            _resident((1, GMLP_WIDTH)),
            _resident((GMLP_WIDTH, D_MODEL)),
        ],
        out_specs=qkv_specs + [tok_spec, tok_spec],
        out_shape=qkv_shapes + [jax.ShapeDtypeStruct((TOKENS, D_MODEL), BF16)] * 2,
        scratch_shapes=[
            pltpu.VMEM((6 * (GROUP_WIDTH // LANES) * (tm // PROJ_SUB), PROJ_SUB, LANES), F32),
            pltpu.VMEM((tm, GMLP_WIDTH), BF16),
        ],
        compiler_params=pltpu.CompilerParams(
            dimension_semantics=("parallel",), vmem_limit_bytes=VMEM_LIMIT),
        name="inproj",
    )(x2d, pos_col, invf, phase, g, w_in, b_gates, w_spatial, b_spatial_t, ln_g, ln_b, w_out_b)


def _attn_kernel(qkv0_ref, qkv1_ref, qkv2_ref, y_ref, acc_ref, m_ref, z_ref, bias_ref):
    blk = BAND_BLOCK
    lane_row = lax.broadcasted_iota(I32, (1, LANES), 1)
    head0_b = jnp.where(lane_row < HEAD_DIM, 1.0, 0.0).astype(BF16)
    head1_b = jnp.where(lane_row < HEAD_DIM, 0.0, 1.0).astype(BF16)
    head0 = lax.broadcasted_iota(I32, (blk, LANES), 1) < HEAD_DIM
    ones_b = jnp.ones((2 * blk, LANES), BF16)

    qi = lax.broadcasted_iota(I32, (2 * blk, 2 * blk), 0) & (blk - 1)
    kc = lax.broadcasted_iota(I32, (2 * blk, 2 * blk), 1)
    for slot, off in enumerate((0, blk)):
        dist = qi + off - kc
        bias_ref[slot] = jnp.where((dist >= 0) & (dist <= blk), 0.0, NEG_INF)

    for gi, (ref, dil) in enumerate(zip((qkv0_ref, qkv1_ref, qkv2_ref), DILATIONS)):
        seq_len = SEQ // dil
        nb = seq_len // blk
        nb_shift = nb.bit_length() - 1

        def body(i, carry, ref=ref, dil=dil, nb=nb, nb_shift=nb_shift, gi=gi):
            r = lax.shift_right_logical(i, nb_shift)
            n = i & (nb - 1)
            q0 = pl.multiple_of(n * blk, blk)
            w0 = pl.multiple_of(jnp.maximum(n - 1, 0) * blk, blk)
            q = ref[0, 0, r, pl.ds(q0, blk), :]
            k = ref[0, 1, r, pl.ds(w0, 2 * blk), :]
            v = ref[0, 2, r, pl.ds(w0, 2 * blk), :]
            q2 = jnp.concatenate([q * head0_b, q * head1_b], axis=0)
            s = lax.dot_general(q2, k, (((1,), (1,)), ((), ())), preferred_element_type=F32)
            s = s + bias_ref[jnp.minimum(n, 1)]
            m2 = jnp.max(s, axis=-1, keepdims=True)
            p = jnp.exp(s - m2)
            v_ext = jnp.concatenate([v, ones_b], axis=1)
            o2 = jnp.dot(p.astype(BF16), v_ext, preferred_element_type=F32)
            o = jnp.where(head0, o2[:blk, :LANES], o2[blk:, :LANES])
            den = jnp.where(head0, o2[:blk, LANES:], o2[blk:, LANES:])
            m = jnp.where(head0, m2[:blk], m2[blk:])
            if gi == 0:
                acc_ref[pl.ds(q0, blk), :] = o
                m_ref[pl.ds(q0, blk), :] = m
                z_ref[pl.ds(q0, blk), :] = den
            else:
                idx = pl.ds(n * (blk * dil) + r, blk, stride=dil)
                m_old = m_ref[idx, :]
                m_new = jnp.maximum(m_old, m)
                e_old = jnp.exp(m_old - m_new)
                e_new = jnp.exp(m - m_new)
                acc_ref[idx, :] = acc_ref[idx, :] * e_old + o * e_new
                z_ref[idx, :] = z_ref[idx, :] * e_old + den * e_new
                m_ref[idx, :] = m_new
            return carry

        lax.fori_loop(0, dil * nb, body, 0, unroll=ATTN_UNROLL[gi])

    y_ref[0] = (acc_ref[...] / z_ref[...]).astype(BF16)


def _attention(qkv):
    in_specs = [
        pl.BlockSpec((1, 3, d, SEQ // d, LANES), lambda b, h: (b, 0, 0, 0, h)) for d in DILATIONS
    ]
    return pl.pallas_call(
        _attn_kernel,
        grid=(BATCH, GROUP_WIDTH // LANES),
        in_specs=in_specs,
        out_specs=pl.BlockSpec((1, SEQ, LANES), lambda b, h: (b, 0, h)),
        out_shape=jax.ShapeDtypeStruct((BATCH, SEQ, GROUP_WIDTH), BF16),
        scratch_shapes=[
            pltpu.VMEM((SEQ, LANES), F32), pltpu.VMEM((SEQ, LANES), F32), pltpu.VMEM((SEQ, LANES), F32),
            pltpu.VMEM((2, 2 * BAND_BLOCK, 2 * BAND_BLOCK), F32),
        ],
        compiler_params=pltpu.CompilerParams(
            dimension_semantics=("parallel", "parallel"), vmem_limit_bytes=VMEM_LIMIT),
        name="attn",
    )(*qkv)


def _post_kernel(ya_ref, ga_ref, mb_ref, x_ref, mem_ref, memg_ref, wkv_ref, woa_ref, wo_ref, xg_ref,
                 wq_ref, wox_ref, mg_ref, wr_ref, br_ref, h_ref, hn_ref, route_ref, route_t_ref,
                 o_scr, kv_scr):
    @pl.when(pl.program_id(0) % (SEQ // TM_POST) == 0)
    def _():
        mn = _rms(mem_ref[...], memg_ref[...]).astype(BF16)
        kv_scr[...] = jnp.dot(mn, wkv_ref[...], preferred_element_type=F32).astype(BF16)

    k_ref = kv_scr.at[:, pl.ds(0, D_MODEL)]
    v_ref = kv_scr.at[:, pl.ds(D_MODEL, D_MODEL)]
    for c in range(TM_POST // POST_SUB):
        rows = slice(c * POST_SUB, (c + 1) * POST_SUB)
        _post_rows(rows, ya_ref, ga_ref, mb_ref, x_ref, k_ref, v_ref, woa_ref, wo_ref, xg_ref, wq_ref,
                   wox_ref, mg_ref, wr_ref, br_ref, h_ref, hn_ref, route_ref, route_t_ref, o_scr)


def _post_rows(rows, ya_ref, ga_ref, mb_ref, x_ref, k_ref, v_ref, woa_ref, wo_ref, xg_ref, wq_ref,
               wox_ref, mg_ref, wr_ref, br_ref, h_ref, hn_ref, route_ref, route_t_ref, o_scr):
    tm = POST_SUB
    t = jnp.dot(ya_ref[rows, :], woa_ref[...], preferred_element_type=F32)
    merged = (ga_ref[rows, :].astype(F32) * t + mb_ref[rows, :].astype(F32)).astype(BF16)
    h1 = x_ref[rows, :] + jnp.dot(merged, wo_ref[...], preferred_element_type=F32)

    hn = _rms(h1, xg_ref[...]).astype(BF16)
    q = (jnp.dot(hn, wq_ref[...], preferred_element_type=F32) * (XATTN_HEAD_DIM ** -0.5)).astype(BF16)
    hd = XATTN_HEAD_DIM
    for h in range(XATTN_HEADS):
        s = lax.dot_general(q[:, h * hd:(h + 1) * hd], k_ref[:, h * hd:(h + 1) * hd],
                            (((1,), (1,)), ((), ())), preferred_element_type=F32)
        m = jnp.max(s, axis=-1, keepdims=True)
        p = jnp.exp(s - m)
        den = jnp.sum(p, axis=-1, keepdims=True)
        oh = jnp.dot(p.astype(BF16), v_ref[:, h * hd:(h + 1) * hd], preferred_element_type=F32) / den
        o_scr[rows, h * hd:(h + 1) * hd] = oh.astype(BF16)
    h2 = h1 + jnp.dot(o_scr[rows, :], wox_ref[...], preferred_element_type=F32)
    h_ref[rows, :] = h2

    hn2 = _rms(h2, mg_ref[...])
    hn_ref[rows, :] = _pack_row_halves(hn2)

    logits = jnp.dot(hn2.astype(BF16), wr_ref[...], preferred_element_type=F32) + br_ref[...]
    li = lax.broadcasted_iota(I32, (tm, LANES), 1)
    lif = li.astype(F32)
    grp_of_lane = lax.shift_right_logical(li, 3).astype(F32)
    is_grp = (li >= N_EXPERTS) & (li < N_EXPERTS + N_EXPERT_GROUPS)
    gl = jnp.where(is_grp, logits, -jnp.inf)
    gmax = jnp.max(gl, axis=-1, keepdims=True)
    grp = jnp.min(jnp.where(gl == gmax, lif - N_EXPERTS, float(LANES)), axis=-1, keepdims=True)
    gsum = jnp.sum(jnp.where(is_grp, jnp.exp(logits - gmax), 0.0), axis=-1, keepdims=True)
    grp_gate = 1.0 / gsum
    in_grp = grp_of_lane == grp
    el = jnp.where(in_grp, logits, -jnp.inf)
    v1 = jnp.max(el, axis=-1, keepdims=True)
    i1 = jnp.min(jnp.where(el == v1, lif, float(LANES)), axis=-1, keepdims=True)
    el2 = jnp.where(lif == i1, -jnp.inf, el)
    v2 = jnp.max(el2, axis=-1, keepdims=True)
    i2 = jnp.min(jnp.where(el2 == v2, lif, float(LANES)), axis=-1, keepdims=True)
    tt = jnp.exp(v2 - v1)
    w1 = grp_gate / (1.0 + tt)
    w2 = grp_gate * tt / (1.0 + tt)
    route = jnp.where(li == 0, i1,
                      jnp.where(li == 1, i2,
                                jnp.where(li == 2, w1, jnp.where(li == 3, w2, 0.0))))
    route_ref[rows, :] = route[:, :8]
    route_t_ref[:, rows] = route.T[:8, :]


def _post(ya, ga, mb, x2d, mem2d, mem_g, w_kv, w_out_a, w_out, xg, w_q, w_o, mg, w_r, b_r):
    tm = TM_POST
    nt = SEQ // tm
    tok = lambda w: pl.BlockSpec((tm, w), lambda i: (i, 0))
    return pl.pallas_call(
        _post_kernel,
        grid=(TOKENS // tm,),
        in_specs=[
            tok(GROUP_WIDTH), tok(D_MODEL), tok(D_MODEL), tok(D_MODEL),
            pl.BlockSpec((N_MEM, D_MODEL), lambda i: (i // nt, 0)),
            _resident((1, D_MODEL)),
            _resident((D_MODEL, 2 * D_MODEL)),
            _resident((GROUP_WIDTH, D_MODEL)),
            _resident((D_MODEL, D_MODEL)),
            _resident((1, D_MODEL)),
            _resident((D_MODEL, D_MODEL)),
            _resident((D_MODEL, D_MODEL)),
            _resident((1, D_MODEL)),
            _resident((D_MODEL, LANES)),
            _resident((1, LANES)),
        ],
        out_specs=[tok(D_MODEL), tok(HALF_MODEL), pl.BlockSpec((tm, 8), lambda i: (i, 0)),
                   pl.BlockSpec((8, tm), lambda i: (0, i))],
        out_shape=[
            jax.ShapeDtypeStruct((TOKENS, D_MODEL), F32),
            jax.ShapeDtypeStruct((TOKENS, HALF_MODEL), I32),
            jax.ShapeDtypeStruct((TOKENS, 8), F32),
            jax.ShapeDtypeStruct((8, TOKENS), F32),
        ],
        scratch_shapes=[pltpu.VMEM((tm, D_MODEL), BF16), pltpu.VMEM((N_MEM, 2 * D_MODEL), BF16)],
        compiler_params=pltpu.CompilerParams(
            dimension_semantics=("arbitrary",), vmem_limit_bytes=VMEM_LIMIT),
        name="post",
    )(ya, ga, mb, x2d, mem2d, mem_g, w_kv, w_out_a, w_out, xg, w_q, w_o, mg, w_r, b_r)


def _route_kernel(rt_ref, dest_ref, meta_ref, rank_scr, carry_ref, pstart_ref):
    pss = pl.program_id(0)
    i = pl.program_id(1)
    tl = TL_ROUTE
    ch = SCAN_CHUNK
    ei = lax.broadcasted_iota(I32, (N_EXPERTS, ch), 0).astype(F32)

    @pl.when((pss == 0) & (i == 0))
    def _():
        carry_ref[...] = jnp.zeros_like(carry_ref)

    @pl.when(pss == 0)
    def _():
        ur = lax.broadcasted_iota(I32, (ch, ch), 0)
        uc = lax.broadcasted_iota(I32, (ch, ch), 1)
        upper = jnp.where(ur < uc, 1.0, 0.0).astype(BF16)
        for c in range(tl // ch):
            e1 = rt_ref[0:1, c * ch:(c + 1) * ch]
            e2 = rt_ref[1:2, c * ch:(c + 1) * ch]
            oh1 = e1 == ei
            oh2 = e2 == ei
            oh = jnp.where(oh1 | oh2, 1.0, 0.0)
            cnt = jnp.dot(oh.astype(BF16), upper, preferred_element_type=F32) + carry_ref[:, 0:1]
            rank1 = jnp.sum(jnp.where(oh1, cnt, 0.0), axis=0, keepdims=True)
            rank2 = jnp.sum(jnp.where(oh2, cnt, 0.0), axis=0, keepdims=True)
            col = pl.multiple_of(i * tl + c * ch, ch)
            rank_scr[0:1, pl.ds(col, ch)] = rank1
            rank_scr[1:2, pl.ds(col, ch)] = rank2
            carry_ref[...] = carry_ref[...] + jnp.sum(oh, axis=1, keepdims=True)

    @pl.when((pss == 1) & (i == 0))
    def _():
        counts = carry_ref[...].astype(I32)
        padded = lax.shift_left(lax.shift_right_logical(counts + (ROW_BLOCK - 1),
                                                        int(math.log2(ROW_BLOCK))),
                                int(math.log2(ROW_BLOCK)))
        row = lax.broadcasted_iota(I32, (N_EXPERTS, LANES), 0)
        lane = lax.broadcasted_iota(I32, (N_EXPERTS, LANES), 1)
        pend = padded
        sh = 1
        while sh < N_EXPERTS:
            pend = pend + jnp.where(row >= sh, pltpu.roll(pend, sh, 0), 0)
            sh *= 2
        pstart = pend - padded
        pstart_ref[...] = pstart
        diag = row == lane

        def as_row(x):
            return jnp.sum(jnp.where(diag, x, 0), axis=0, keepdims=True)

        blk_lane = lax.broadcasted_iota(I32, (N_EXPERTS, META_LANES), 1) * ROW_BLOCK
        blk_exp = jnp.sum(jnp.where(pend[:, 0:1] <= blk_lane, 1, 0), axis=0, keepdims=True)
        blk_exp = jnp.minimum(blk_exp, N_EXPERTS - 1)
        meta_ref[...] = jnp.zeros_like(meta_ref)
        meta_ref[0:1, :] = blk_exp
        meta_ref[3:4, 0:LANES] = as_row(pend)

    @pl.when(pss == 1)
    def _():
        ps = pstart_ref[:, 0:1].astype(F32)
        for c in range(tl // ch):
            e1 = rt_ref[0:1, c * ch:(c + 1) * ch]
            e2 = rt_ref[1:2, c * ch:(c + 1) * ch]
            col = pl.multiple_of(i * tl + c * ch, ch)
            d1 = rank_scr[0:1, pl.ds(col, ch)] + jnp.sum(jnp.where(e1 == ei, ps, 0.0), axis=0, keepdims=True)
            d2 = rank_scr[1:2, pl.ds(col, ch)] + jnp.sum(jnp.where(e2 == ei, ps, 0.0), axis=0, keepdims=True)
            dest_ref[0:1, c * ch:(c + 1) * ch] = d1.astype(I32)
            dest_ref[1:2, c * ch:(c + 1) * ch] = d2.astype(I32)


def _route(rt):
    tl = TL_ROUTE
    return pl.pallas_call(
        _route_kernel,
        grid=(2, TOKENS // tl),
        in_specs=[pl.BlockSpec((8, tl), lambda p, i: (0, i))],
        out_specs=[
            pl.BlockSpec((2, tl), lambda p, i: (0, i * p)),
            pl.BlockSpec((8, META_LANES), lambda p, i: (0, 0)),
        ],
        out_shape=[
            jax.ShapeDtypeStruct((2, TOKENS), I32),
            jax.ShapeDtypeStruct((8, META_LANES), I32),
        ],
        scratch_shapes=[
            pltpu.VMEM((2, TOKENS), F32),
            pltpu.VMEM((N_EXPERTS, LANES), F32),
            pltpu.VMEM((N_EXPERTS, LANES), I32),
        ],
        compiler_params=pltpu.CompilerParams(
            dimension_semantics=("arbitrary", "arbitrary"), vmem_limit_bytes=VMEM_LIMIT),
        name="route",
    )(rt)


SC_CORES = 2
SC_SUBCORES = 16
SC_WORKERS = SC_CORES * SC_SUBCORES
SC_LANES = 16
SC_WINDOW = 64
SC_SCAN_CHUNK = 32768
PART_UNIT = SC_WORKERS * 2 * SC_WINDOW
DISPATCH_SPLIT = (4, 10, 10, 12)
COMBINE_SPLIT = (2, 6, 8)


def _sc_move_rows(table_hbm, idx_v, out_hbm, out_base, n_windows, rows_v, gsem, wsems):
    assert n_windows % 2 == 0 and n_windows >= 2

    def gather(j, b):
        idx = idx_v.at[pl.ds(j * SC_WINDOW, SC_WINDOW)]
        pltpu.async_copy(table_hbm.at[idx], rows_v.at[b], gsem).wait()

    def write(j, b):
        dst = out_hbm.at[pl.ds(out_base + j * SC_WINDOW, SC_WINDOW)]
        return pltpu.make_async_copy(rows_v.at[b], dst, wsems.at[b])

    for b in range(2):
        gather(b, b)
        write(b, b).start()

    @pl.loop(2, n_windows, step=2)
    def _(j):
        for b in range(2):
            write(j - 2 + b, b).wait()
            gather(j + b, b)
            write(j + b, b).start()

    for b in range(2):
        write(n_windows - 2 + b, b).wait()


def _sc_inverse_map(dest0, dest1):
    n_tok = dest0.shape[0]
    rows_per_w = PADDED_ROWS // SC_WORKERS
    assert rows_per_w % SC_LANES == 0 and n_tok % SC_SCAN_CHUNK == 0
    mesh = plsc.VectorSubcoreMesh(core_axis_name="c", subcore_axis_name="s")

    @functools.partial(
        pl.kernel, mesh=mesh,
        out_type=jax.ShapeDtypeStruct((PADDED_ROWS,), I32),
        scratch_types=[pltpu.VMEM((rows_per_w,), I32), pltpu.VMEM((SC_SCAN_CHUNK,), I32)],
        compiler_params=pltpu.CompilerParams(needs_layout_passes=False),
        name="sc_inverse_map",
    )
    def inverse_map(d0_hbm, d1_hbm, tok_hbm, tok_v, dchunk_v):
        wid = lax.axis_index("s") * SC_CORES + lax.axis_index("c")
        lo = wid * rows_per_w
        lane = lax.iota(I32, SC_LANES)

        @pl.loop(0, rows_per_w // SC_LANES)
        def _(i):
            tok_v[pl.ds(i * SC_LANES, SC_LANES)] = (lo + i * SC_LANES + lane) & (n_tok - 1)

        for d_hbm in (d0_hbm, d1_hbm):
            @pl.loop(0, n_tok // SC_SCAN_CHUNK)
            def _(c, d_hbm=d_hbm):
                pltpu.sync_copy(d_hbm.at[pl.ds(c * SC_SCAN_CHUNK, SC_SCAN_CHUNK)], dchunk_v)

                @plsc.parallel_loop(0, SC_SCAN_CHUNK // SC_LANES, unroll=4)
                def _(v):
                    local = dchunk_v[pl.ds(v * SC_LANES, SC_LANES)] - lo
                    mine = (local >= 0) & (local < rows_per_w)
                    tok = c * SC_SCAN_CHUNK + v * SC_LANES + lane
                    plsc.store_scatter(tok_v, [jnp.where(mine, local, 0)], tok, mask=mine)

        pltpu.sync_copy(tok_v, tok_hbm.at[pl.ds(lo, rows_per_w)])

    return inverse_map(dest0, dest1)


def _sc_gather(table, idxs):
    n = idxs[0].shape[0]
    width = table.shape[1]
    per_w = n // SC_WORKERS
    assert per_w % (2 * SC_WINDOW) == 0
    mesh = plsc.VectorSubcoreMesh(core_axis_name="c", subcore_axis_name="s")
    out = jax.ShapeDtypeStruct((n, width), table.dtype)
    k = len(idxs)

    @functools.partial(
        pl.kernel, mesh=mesh, out_type=(out,) * k,
        scratch_types=[
            pltpu.VMEM((per_w,), I32),
            pltpu.VMEM((2, SC_WINDOW, width), table.dtype),
            pltpu.SemaphoreType.DMA,
            pltpu.SemaphoreType.DMA((2,)),
        ],
        name="sc_gather",
    )
    def gather(table_hbm, *refs):
        idx_hbms, out_hbms = refs[:k], refs[k:2 * k]
        idx_v, rows_v, gsem, wsems = refs[2 * k:]
        wid = lax.axis_index("s") * SC_CORES + lax.axis_index("c")
        base = wid * per_w
        for idx_hbm, out_hbm in zip(idx_hbms, out_hbms):
            pltpu.sync_copy(idx_hbm.at[pl.ds(base, per_w)], idx_v)
            _sc_move_rows(table_hbm, idx_v, out_hbm, base, per_w // SC_WINDOW, rows_v, gsem, wsems)

    return gather(table, *idxs)


def _experts_kernel(blk_start, be_ref, nv_ref, xs_ref, wg_ref, wu_ref, wd_ref, *rest):
    yb_ref = rest[-1]
    j = pl.program_id(0) + blk_start

    @pl.when(j < nv_ref[0])
    def _():
        lo, hi = _unpack_row_halves(xs_ref[...])
        xb = jnp.concatenate([lo.astype(BF16), hi.astype(BF16)], axis=1)
        a = jnp.dot(xb, wg_ref[0].astype(BF16), preferred_element_type=F32)
        b = jnp.dot(xb, wu_ref[0].astype(BF16), preferred_element_type=F32)
        hb = (jax.nn.silu(a) * b).astype(BF16)
        yb_ref[...] = _pack_row_halves(
            jnp.dot(hb, wd_ref[0].astype(BF16), preferred_element_type=F32))

    @pl.when(j >= nv_ref[0])
    def _():
        yb_ref[...] = jnp.zeros_like(yb_ref)


def _experts(blk_expert, n_valid, xs_part, blk_start, yb_prev, w_gate, w_up, w_down):
    n_blocks = xs_part.shape[0] // ROW_BLOCK

    def row_map(j, be, nv):
        return (jnp.clip(jnp.minimum(j + blk_start, nv[0] - 1) - blk_start, 0, n_blocks - 1), 0)

    def out_map(j, be, nv):
        return (j + blk_start, 0)

    def w_map(j, be, nv):
        return (be[jnp.minimum(j + blk_start, nv[0] - 1)], 0, 0)

    in_specs = [
        pl.BlockSpec((ROW_BLOCK, HALF_MODEL), row_map),
        pl.BlockSpec((1, D_MODEL, EXPERT_FF), w_map),
        pl.BlockSpec((1, D_MODEL, EXPERT_FF), w_map),
        pl.BlockSpec((1, EXPERT_FF, D_MODEL), w_map),
    ]
    operands = [blk_expert, n_valid, xs_part, w_gate, w_up, w_down]
    aliases = {}
    if yb_prev is not None:
        in_specs.append(pl.BlockSpec(memory_space=pl.ANY))
        aliases = {len(operands): 0}
        operands.append(yb_prev)
    grid_spec = pltpu.PrefetchScalarGridSpec(
        num_scalar_prefetch=2,
        grid=(n_blocks,),
        in_specs=in_specs,
        out_specs=pl.BlockSpec((ROW_BLOCK, HALF_MODEL), out_map),
    )
    return pl.pallas_call(
        functools.partial(_experts_kernel, blk_start),
        grid_spec=grid_spec,
        out_shape=jax.ShapeDtypeStruct((PADDED_ROWS, HALF_MODEL), I32),
        input_output_aliases=aliases,
        compiler_params=pltpu.CompilerParams(
            dimension_semantics=("arbitrary",), vmem_limit_bytes=VMEM_LIMIT),
        name="experts",
    )(*operands)


def _combine_kernel(route_ref, h_ref, g_ref, y0_ref, y1_ref, *rest):
    out_ref = rest[-1]
    w1 = route_ref[:, 2:3]
    w2 = route_ref[:, 3:4]
    lo0, hi0 = _unpack_row_halves(y0_ref[...])
    lo1, hi1 = _unpack_row_halves(y1_ref[...])
    y = jnp.concatenate([lo0 * w1 + lo1 * w2, hi0 * w1 + hi1 * w2], axis=1)
    out_ref[...] = _rms(h_ref[...] + y, g_ref[...])


def _combine(route, h2, g, y0, y1, tok_start, out_prev):
    tm = TM_COMBINE
    blk0 = tok_start // tm
    glob = lambda w: pl.BlockSpec((tm, w), lambda i: (i + blk0, 0))
    part = pl.BlockSpec((tm, HALF_MODEL), lambda i: (i, 0))
    in_specs = [glob(8), glob(D_MODEL), _resident((1, D_MODEL)), part, part]
    operands = [route, h2, g, y0, y1]
    aliases = {}
    if out_prev is not None:
        in_specs.append(pl.BlockSpec(memory_space=pl.ANY))
        aliases = {len(operands): 0}
        operands.append(out_prev)
    return pl.pallas_call(
        _combine_kernel,
        grid=(y0.shape[0] // tm,),
        in_specs=in_specs,
        out_specs=glob(D_MODEL),
        out_shape=jax.ShapeDtypeStruct((TOKENS, D_MODEL), F32),
        input_output_aliases=aliases,
        compiler_params=pltpu.CompilerParams(
            dimension_semantics=("parallel",), vmem_limit_bytes=VMEM_LIMIT),
        name="combine",
    )(*operands)


def kernel(x, mem, positions, mix_norm_g, w_in, b_gates, w_spatial, b_spatial, v_norm_g, v_norm_b,
           w_out_a, w_out_b, w_out, xattn_norm_g, mem_norm_g, w_q_x, w_kv_x, w_o_x, moe_norm_g,
           w_router_grp, b_router_grp, w_router_exp, b_router_exp, w_gate_e, w_up_e, w_down_e,
           final_norm_g):
    assert x.shape == (BATCH, SEQ, D_MODEL) and mem.shape == (BATCH, N_MEM, D_MODEL)
    assert mix_norm_g.shape[0] == 1, "single layer"
    x2d = x.reshape(TOKENS, D_MODEL)
    pos_col = positions.reshape(TOKENS, 1).astype(F32)
    half = HEAD_DIM // 2
    inv_freq = ROPE_THETA ** (-jnp.arange(half, dtype=F32) / half)
    invf = jnp.tile(inv_freq, LANES // half).reshape(1, LANES)
    phase = jnp.tile(jnp.concatenate([jnp.zeros((half,), F32), jnp.full((half,), math.pi / 2, F32)]),
                     LANES // HEAD_DIM).reshape(1, LANES)

    qkv0, qkv1, qkv2, ga, mb = _inproj(
        x2d, pos_col, invf, phase, mix_norm_g[0].reshape(1, D_MODEL), w_in[0].astype(BF16),
        b_gates[0].reshape(1, 2 * D_MODEL), w_spatial[0], b_spatial[0].T,
        v_norm_g[0].reshape(1, GMLP_WIDTH), v_norm_b[0].reshape(1, GMLP_WIDTH),
        w_out_b[0].astype(BF16))

    ya = _attention((qkv0, qkv1, qkv2)).reshape(TOKENS, GROUP_WIDTH)

    pad = LANES - N_EXPERTS - N_EXPERT_GROUPS
    w_r = jnp.concatenate([w_router_exp[0], w_router_grp[0], jnp.zeros((D_MODEL, pad), F32)], axis=1)
    b_r = jnp.concatenate([b_router_exp[0], b_router_grp[0], jnp.zeros((pad,), F32)]).reshape(1, LANES)
    h2, hn2, route, route_t = _post(
        ya, ga, mb, x2d, mem.reshape(BATCH * N_MEM, D_MODEL), mem_norm_g[0].reshape(1, D_MODEL),
        w_kv_x[0].astype(BF16), w_out_a[0].astype(BF16), w_out[0].astype(BF16),
        xattn_norm_g[0].reshape(1, D_MODEL), w_q_x[0].astype(BF16), w_o_x[0].astype(BF16),
        moe_norm_g[0].reshape(1, D_MODEL), w_r.astype(BF16), b_r)

    dest, meta = _route(route_t)
    d0, d1 = dest[0], dest[1]
    tok_of_row = _sc_inverse_map(d0, d1)
    blk_expert = meta[0, :N_ROW_BLOCKS]
    n_valid = (meta[3, N_EXPERTS - 1:N_EXPERTS] // ROW_BLOCK).astype(I32)
    row_cuts = [PART_UNIT * c for c in itertools.accumulate((0,) + DISPATCH_SPLIT)]
    assert row_cuts[-1] == PADDED_ROWS
    xs_parts = [_sc_gather(hn2, [tok_of_row[a:b]])[0] for a, b in zip(row_cuts, row_cuts[1:])]
    yb = None
    for a, xs_part in zip(row_cuts, xs_parts):
        yb = _experts(blk_expert, n_valid, xs_part, a // ROW_BLOCK, yb, w_gate_e[0], w_up_e[0], w_down_e[0])
    g_fin = final_norm_g.reshape(1, D_MODEL)
    tok_cuts = [PART_UNIT * c for c in itertools.accumulate((0,) + COMBINE_SPLIT)]
    assert tok_cuts[-1] == TOKENS
    gathered = [_sc_gather(yb, [d0[a:b], d1[a:b]]) for a, b in zip(tok_cuts, tok_cuts[1:])]
    out = None
    for a, (y0, y1) in zip(tok_cuts, gathered):
        out = _combine(route, h2, g_fin, y0, y1, a, out)
    return out.reshape(BATCH, SEQ, D_MODEL)
```
